```python
import math
import jax, jax.numpy as jnp
from jax import lax
import numpy as np

D_MODEL = 1024
BATCH = 2
SEQ = 8192
DEPTH = 2
DEC_BATCH = 128
DEC_SEQ = 1
PAST_LEN = 16384
PAGE_SIZE = 128

N_HEADS = 16
N_KV_HEADS = 4
HEAD_DIM = 64
GROUP = N_HEADS // N_KV_HEADS
WINDOW = 128
ATTN_BLOCK = 128
N_BUCKETS = 32
MAX_DISTANCE = 128
CONV_DIM = 1024
CONV_K = 3
N_EXPERTS = 64
TOP_K = 8
N_GROUPS = 8
TOPK_GROUPS = 4
EXPERT_FF = 256
SHARED_FF = 256
ROUTED_SCALE = 2.5
MOE_BLOCK = 256
LN_EPS = 1e-5
ALPHA = (2 * DEPTH) ** 0.25
BETA = (8 * DEPTH) ** -0.25
NEG = -1e30

Q_DIM = N_HEADS * HEAD_DIM
KV_DIM = N_KV_HEADS * HEAD_DIM
SPLIT_SIZES = (Q_DIM, KV_DIM, KV_DIM, CONV_DIM, CONV_DIM, CONV_DIM, D_MODEL, D_MODEL)
IN_DIM = sum(SPLIT_SIZES)
SPLIT_POINTS = tuple(int(v) for v in np.cumsum(SPLIT_SIZES)[:-1])

kernel_name = "hybrid_swa_sink_shortconv_moe_deepnorm_step"


def layer_norm(x, g, b):
    xf = x.astype(jnp.float32)
    mu = jnp.mean(xf, axis=-1, keepdims=True)
    var = jnp.mean(jnp.square(xf - mu), axis=-1, keepdims=True)
    return ((xf - mu) * lax.rsqrt(var + LN_EPS)).astype(x.dtype) * g + b


def t5_bucket(dist):
    n = jnp.maximum(dist, 0)
    max_exact = N_BUCKETS // 2
    large = max_exact + (jnp.log(jnp.maximum(n, 1).astype(jnp.float32) / max_exact)
                         / math.log(MAX_DISTANCE / max_exact) * (N_BUCKETS - max_exact)).astype(jnp.int32)
    large = jnp.minimum(large, N_BUCKETS - 1)
    return jnp.where(n < max_exact, n, large)


def rel_bias_for(dist, rel_bias):
    b = rel_bias[t5_bucket(dist)].astype(jnp.float32)
    return jnp.moveaxis(b, -1, 0).reshape((N_KV_HEADS, GROUP) + dist.shape)


def sink_softmax(s, sink):
    sk = jnp.broadcast_to(sink.astype(jnp.float32).reshape(N_KV_HEADS, GROUP, 1, 1), s.shape[:-1] + (1,))
    p = jax.nn.softmax(jnp.concatenate([s, sk], axis=-1), axis=-1)
    return p[..., :-1]


def swa_prompt(q, k, v, sink, rel_bias):
    B, S = q.shape[0], q.shape[1]
    nb = S // ATTN_BLOCK
    qb = q.reshape(B, nb, ATTN_BLOCK, N_KV_HEADS, GROUP, HEAD_DIM)
    kb = k.reshape(B, nb, ATTN_BLOCK, N_KV_HEADS, HEAD_DIM)
    vb = v.reshape(B, nb, ATTN_BLOCK, N_KV_HEADS, HEAD_DIM)

    def with_prev(t):
        prev = jnp.pad(t, ((0, 0), (1, 0), (0, 0), (0, 0), (0, 0)))[:, :-1]
        return jnp.concatenate([prev, t], axis=2)

    kk, vv = with_prev(kb), with_prev(vb)
    s = jnp.einsum('bnqhgd,bnkhd->bnhgqk', qb, kk,
                   preferred_element_type=jnp.float32) * (HEAD_DIM ** -0.5)
    qi = jnp.arange(ATTN_BLOCK)[:, None]
    ki = jnp.arange(2 * ATTN_BLOCK)[None, :]
    dist = qi + ATTN_BLOCK - ki
    bias = rel_bias_for(dist, rel_bias)
    k_pos = jnp.arange(nb)[:, None, None] * ATTN_BLOCK - ATTN_BLOCK + ki[None]
    valid = (dist >= 0)[None] & (dist < WINDOW)[None] & (k_pos >= 0)
    s = jnp.where(valid[None, :, None, None], s + bias, NEG)
    p = sink_softmax(s, sink)
    o = jnp.einsum('bnhgqk,bnkhd->bnqhgd', p.astype(v.dtype), vv)
    return o.reshape(B, S, Q_DIM)


def swa_sample(q, k_all, v_all, sink, rel_bias):
    Bd, Ld = q.shape[0], q.shape[1]
    Lk = k_all.shape[1]
    Wb = Lk - Ld
    s = jnp.einsum('bqhgd,bkhd->bhgqk', q, k_all,
                   preferred_element_type=jnp.float32) * (HEAD_DIM ** -0.5)
    dist = jnp.arange(Ld)[:, None] + Wb - jnp.arange(Lk)[None, :]
    bias = rel_bias_for(dist, rel_bias)
    valid = (dist >= 0) & (dist < WINDOW)
    s = jnp.where(valid, s + bias, NEG)
    p = sink_softmax(s, sink)
    o = jnp.einsum('bhgqk,bkhd->bqhgd', p.astype(v_all.dtype), v_all)
    return o.reshape(Bd, Ld, Q_DIM)


def short_conv(u, buf, conv_w):
    L = u.shape[1]
    up = jnp.concatenate([buf, u], axis=1)
    y = sum(conv_w[j] * up[:, j:j + L] for j in range(CONV_K))
    return y, up[:, -(CONV_K - 1):]


def token_mixer(x, win_k, win_v, conv_buf, rel_bias, w_in, sink, conv_w, w_attn_out, w_conv_out, w_out):
    Bx, L = x.shape[0], x.shape[1]
    proj = x @ w_in
    q, k, v, b_g, c_g, h, g_a, g_b = jnp.split(proj, SPLIT_POINTS, axis=-1)
    q = q.reshape(Bx, L, N_KV_HEADS, GROUP, HEAD_DIM)
    k = k.reshape(Bx, L, N_KV_HEADS, HEAD_DIM)
    v = v.reshape(Bx, L, N_KV_HEADS, HEAD_DIM)
    if win_k is None:
        att = swa_prompt(q, k, v, sink, rel_bias)
        start = max(L - WINDOW, 0)
        new_k, new_v = k[:, start:], v[:, start:]
        conv_buf = jnp.zeros((Bx, CONV_K - 1, CONV_DIM), x.dtype)
    else:
        wb = win_k.shape[1]
        k_all = jnp.concatenate([win_k, k], axis=1)
        v_all = jnp.concatenate([win_v, v], axis=1)
        att = swa_sample(q, k_all, v_all, sink, rel_bias)
        new_k, new_v = k_all[:, -wb:], v_all[:, -wb:]
    y_conv, new_buf = short_conv(c_g * h, conv_buf, conv_w)
    y_conv = b_g * y_conv
    merged = jax.nn.sigmoid(g_a) * (att @ w_attn_out) + jax.nn.sigmoid(g_b) * (y_conv @ w_conv_out)
    return merged @ w_out, new_k, new_v, new_buf


def moe(xt, router_w, router_bias, w_gate, w_up, w_down, s_gate, s_up, s_down):
    T = xt.shape[0]
    scores = jax.nn.sigmoid((xt @ router_w).astype(jnp.float32))
    sel = scores + router_bias.astype(jnp.float32)
    grp = sel.reshape(T, N_GROUPS, N_EXPERTS // N_GROUPS)
    gscore = lax.top_k(grp, 2)[0].sum(-1)
    _, gidx = lax.top_k(gscore, TOPK_GROUPS)
    gmask = jax.nn.one_hot(gidx, N_GROUPS, dtype=jnp.float32).sum(-2)
    emask = jnp.repeat(gmask, N_EXPERTS // N_GROUPS, axis=-1)
    _, eidx = lax.top_k(jnp.where(emask > 0, sel, -jnp.inf), TOP_K)
    w = jnp.take_along_axis(scores, eidx, axis=-1)
    w = w / jnp.sum(w, axis=-1, keepdims=True) * ROUTED_SCALE
    gates = jnp.einsum('tk,tke->te', w, jax.nn.one_hot(eidx, N_EXPERTS, dtype=jnp.float32)).astype(xt.dtype)
    pad = (-T) % MOE_BLOCK
    xb = jnp.pad(xt, ((0, pad), (0, 0))).reshape(-1, MOE_BLOCK, D_MODEL)
    gb = jnp.pad(gates, ((0, pad), (0, 0))).reshape(-1, MOE_BLOCK, N_EXPERTS)

    def expert_block(args):
        xs, gs = args
        hid = jax.nn.silu(jnp.einsum('td,edf->tef', xs, w_gate)) * jnp.einsum('td,edf->tef', xs, w_up)
        return jnp.einsum('tef,te,efd->td', hid, gs, w_down)

    routed = lax.map(expert_block, (xb, gb)).reshape(-1, D_MODEL)[:T]
    shared = (jax.nn.silu(xt @ s_gate) * (xt @ s_up)) @ s_down
    return routed + shared


def setup_inputs(seed: int = 0) -> dict:
    key = jax.random.key(seed)
    ks = jax.random.split(key, 24)
    f32 = jnp.float32
    nrm = lambda k, shape, scale: jax.random.normal(k, shape, f32) * scale
    win_buf = min(WINDOW, PAST_LEN)
    col_scale = jnp.concatenate([
        jnp.ones((Q_DIM + KV_DIM,), f32), jnp.full((KV_DIM,), BETA, f32),
        jnp.ones((2 * CONV_DIM,), f32), jnp.full((CONV_DIM,), BETA, f32),
        jnp.ones((2 * D_MODEL,), f32)])
    return {
        "x_prompt": nrm(ks[0], (BATCH, SEQ, D_MODEL), 1.0),
        "x_sample": nrm(ks[1], (DEC_BATCH, DEC_SEQ, D_MODEL), 1.0),
        "cache_k_win": nrm(ks[2], (DEPTH, DEC_BATCH, win_buf, N_KV_HEADS, HEAD_DIM), 1.0),
        "cache_v_win": nrm(ks[3], (DEPTH, DEC_BATCH, win_buf, N_KV_HEADS, HEAD_DIM), BETA),
        "state_conv": nrm(ks[4], (DEPTH, DEC_BATCH, CONV_K - 1, CONV_DIM), BETA),
        "rel_bias": nrm(ks[5], (N_BUCKETS, N_HEADS), 0.1),
        "w_in": nrm(ks[6], (DEPTH, D_MODEL, IN_DIM), D_MODEL ** -0.5) * col_scale,
        "attn_sink": nrm(ks[7], (DEPTH, N_HEADS), 0.5),
        "conv_w": nrm(ks[8], (DEPTH, CONV_K, CONV_DIM), CONV_K ** -0.5),
        "w_attn_out": nrm(ks[9], (DEPTH, Q_DIM, D_MODEL), Q_DIM ** -0.5 * BETA),
        "w_conv_out": nrm(ks[10], (DEPTH, CONV_DIM, D_MODEL), CONV_DIM ** -0.5 * BETA),
        "w_out": nrm(ks[11], (DEPTH, D_MODEL, D_MODEL), D_MODEL ** -0.5 * BETA),
        "ln1_g": 1.0 + nrm(ks[12], (DEPTH, D_MODEL), 0.01),
        "ln1_b": nrm(ks[13], (DEPTH, D_MODEL), 0.01),
        "router_w": nrm(ks[14], (DEPTH, D_MODEL, N_EXPERTS), D_MODEL ** -0.5),
        "router_bias": nrm(ks[15], (DEPTH, N_EXPERTS), 0.01),
        "exp_w_gate": nrm(ks[16], (DEPTH, N_EXPERTS, D_MODEL, EXPERT_FF), D_MODEL ** -0.5),
        "exp_w_up": nrm(ks[17], (DEPTH, N_EXPERTS, D_MODEL, EXPERT_FF), D_MODEL ** -0.5),
        "exp_w_down": nrm(ks[18], (DEPTH, N_EXPERTS, EXPERT_FF, D_MODEL), EXPERT_FF ** -0.5 * BETA),
        "shared_w_gate": nrm(ks[19], (DEPTH, D_MODEL, SHARED_FF), D_MODEL ** -0.5),
        "shared_w_up": nrm(ks[20], (DEPTH, D_MODEL, SHARED_FF), D_MODEL ** -0.5),
        "shared_w_down": nrm(ks[21], (DEPTH, SHARED_FF, D_MODEL), SHARED_FF ** -0.5 * BETA),
        "ln2_g": 1.0 + nrm(ks[22], (DEPTH, D_MODEL), 0.01),
        "ln2_b": nrm(ks[23], (DEPTH, D_MODEL), 0.01),
    }


def reference(x_prompt, x_sample, cache_k_win, cache_v_win, state_conv, rel_bias, w_in, attn_sink,
              conv_w, w_attn_out, w_conv_out, w_out, ln1_g, ln1_b, router_w, router_bias,
              exp_w_gate, exp_w_up, exp_w_down, shared_w_gate, shared_w_up, shared_w_down,
              ln2_g, ln2_b):
    yp, ys = x_prompt, x_sample
    n_prompt = yp.shape[0] * yp.shape[1]
    kp_l, vp_l, cp_l, ks_l, vs_l, cs_l = [], [], [], [], [], []
    for l in range(DEPTH):
        mix_w = (rel_bias, w_in[l], attn_sink[l], conv_w[l], w_attn_out[l], w_conv_out[l], w_out[l])
        mp, kp, vp, cp = token_mixer(yp, None, None, None, *mix_w)
        ms, ksn, vsn, csn = token_mixer(ys, cache_k_win[l], cache_v_win[l], state_conv[l], *mix_w)
        kp_l.append(kp); vp_l.append(vp); cp_l.append(cp)
        ks_l.append(ksn); vs_l.append(vsn); cs_l.append(csn)
        yp = layer_norm(ALPHA * yp + mp, ln1_g[l], ln1_b[l])
        ys = layer_norm(ALPHA * ys + ms, ln1_g[l], ln1_b[l])
        xt = jnp.concatenate([yp.reshape(-1, D_MODEL), ys.reshape(-1, D_MODEL)], axis=0)
        ff = moe(xt, router_w[l], router_bias[l], exp_w_gate[l], exp_w_up[l], exp_w_down[l],
                 shared_w_gate[l], shared_w_up[l], shared_w_down[l])
        yp = layer_norm(ALPHA * yp + ff[:n_prompt].reshape(yp.shape), ln2_g[l], ln2_b[l])
        ys = layer_norm(ALPHA * ys + ff[n_prompt:].reshape(ys.shape), ln2_g[l], ln2_b[l])
    return (yp, ys, jnp.stack(kp_l), jnp.stack(vp_l), jnp.stack(cp_l),
            jnp.stack(ks_l), jnp.stack(vs_l), jnp.stack(cs_l))
```

```python
import functools
import math

import jax
import jax.numpy as jnp
from jax import lax
from jax.experimental import pallas as pl
from jax.experimental.pallas import tpu as pltpu

D_MODEL = 1024
N_HEADS = 16
N_KV_HEADS = 4
HEAD_DIM = 64
GROUP = N_HEADS // N_KV_HEADS
WINDOW = 128
ATTN_BLOCK = 128
N_BUCKETS = 32
MAX_DISTANCE = 128
CONV_DIM = 1024
CONV_K = 3
N_EXPERTS = 64
TOP_K = 8
N_GROUPS = 8
TOPK_GROUPS = 4
GROUP_SIZE = N_EXPERTS // N_GROUPS
EXPERT_FF = 256
ROUTED_SCALE = 2.5
LN_EPS = 1e-5
NEG = -1e30

Q_DIM = N_HEADS * HEAD_DIM
KV_DIM = N_KV_HEADS * HEAD_DIM
OFF_K = Q_DIM
OFF_V = OFF_K + KV_DIM
OFF_B = OFF_V + KV_DIM
OFF_C = OFF_B + CONV_DIM
OFF_H = OFF_C + CONV_DIM
OFF_GA = OFF_H + CONV_DIM
OFF_GB = OFF_GA + D_MODEL
IN_DIM = OFF_GB + D_MODEL

LANES = 128
CONV_PAD = 8
VMEM_LIMIT = 60 * 1024 * 1024

BF16 = jnp.bfloat16
F32 = jnp.float32


def _dot(a, b):
    return jnp.dot(a, b, preferred_element_type=F32)


def _dot_nt(a, b):
    return lax.dot_general(a, b, (((1,), (1,)), ((), ())), preferred_element_type=F32)


def _layer_norm(z, g, b):
    mu = jnp.mean(z, axis=-1, keepdims=True)
    d = z - mu
    var = jnp.mean(d * d, axis=-1, keepdims=True)
    return d * lax.rsqrt(var + LN_EPS) * g + b


def _sink_softmax(s, sink_col):
    m = jnp.maximum(jnp.max(s, axis=-1, keepdims=True), sink_col)
    e = jnp.exp(s - m)
    den = jnp.sum(e, axis=-1, keepdims=True) + jnp.exp(sink_col - m)
    return e * (1.0 / den)


def _merge_project(x, attn_o, y_conv, g_a, g_b, w_co_ref, w_o_ref, lng_ref, lnb_ref, alpha):
    merged = jax.nn.sigmoid(g_a) * attn_o + jax.nn.sigmoid(g_b) * _dot(y_conv.astype(BF16), w_co_ref[...])
    out = _dot(merged.astype(BF16), w_o_ref[...])
    return _layer_norm(alpha * x + out, lng_ref[...], lnb_ref[...])


def _mixer_prompt_kernel(x_ref, w_in_ref, w_ao_ref, w_co_ref, w_o_ref, convw_ref, bias_ref, sink_ref,
                         lng_ref, lnb_ref,
                         x1_ref, kwin_ref, vwin_ref, conv_ref,
                         ka_ref, kb_ref, va_ref, vb_ref, att_ref, ubuf_ref, *, alpha):
    i = pl.program_id(1)
    tq = x_ref.shape[1]
    nblk = tq // ATTN_BLOCK
    half = LANES // 2
    scale = HEAD_DIM ** -0.5

    @pl.when(i == 0)
    def _init():
        for ref in (ka_ref, kb_ref, va_ref, vb_ref):
            ref[:, ATTN_BLOCK:, :] = jnp.zeros((N_KV_HEADS, ATTN_BLOCK, LANES), BF16)
        ubuf_ref[0:CONV_PAD, :] = jnp.zeros((CONV_PAD, CONV_DIM), F32)

    x = x_ref[0]
    xb = x.astype(BF16)
    qkv = _dot(xb, w_in_ref[:, 0:OFF_B])

    lane = lax.broadcasted_iota(jnp.int32, (ATTN_BLOCK, LANES), 1)
    lo = lane < half
    key_col = lax.broadcasted_iota(jnp.int32, (2 * ATTN_BLOCK, 2 * ATTN_BLOCK), 1)

    for j in range(nblk):
        r0 = j * ATTN_BLOCK
        rows = slice(r0, r0 + ATTN_BLOCK)
        for ref in (ka_ref, kb_ref, va_ref, vb_ref):
            ref[:, 0:ATTN_BLOCK, :] = ref[:, ATTN_BLOCK:, :]
        for c in range(N_KV_HEADS // 2):
            for off, a_ref, b_ref in ((OFF_K, ka_ref, kb_ref), (OFF_V, va_ref, vb_ref)):
                chunk = qkv[rows, off + c * LANES: off + (c + 1) * LANES]
                c_lo = jnp.where(lo, chunk, 0.0)
                c_hi = jnp.where(lo, 0.0, chunk)
                a_ref[2 * c, ATTN_BLOCK:, :] = c_lo.astype(BF16)
                b_ref[2 * c, ATTN_BLOCK:, :] = pltpu.roll(c_lo, half, 1).astype(BF16)
                b_ref[2 * c + 1, ATTN_BLOCK:, :] = c_hi.astype(BF16)
                a_ref[2 * c + 1, ATTN_BLOCK:, :] = pltpu.roll(c_hi, half, 1).astype(BF16)
        for h in range(N_KV_HEADS):
            q0 = h * GROUP * HEAD_DIM
            q2 = jnp.concatenate([qkv[rows, q0:q0 + LANES], qkv[rows, q0 + LANES:q0 + 2 * LANES]],
                                 axis=0).astype(BF16)
            probs = []
            for t, k_ref in enumerate((ka_ref, kb_ref)):
                s = _dot_nt(q2, k_ref[h]) * scale + bias_ref[h, t]
                if j == 0:
                    s = jnp.where(jnp.logical_and(i == 0, key_col < ATTN_BLOCK), NEG, s)
                probs.append(_sink_softmax(s, sink_ref[h, t]).astype(BF16))
            o = _dot(probs[0], va_ref[h]) + _dot(probs[1], vb_ref[h])
            att_ref[rows, q0:q0 + LANES] = o[0:ATTN_BLOCK].astype(BF16)
            att_ref[rows, q0 + LANES:q0 + 2 * LANES] = o[ATTN_BLOCK:].astype(BF16)

    kwin_ref[0] = qkv[tq - WINDOW:tq, OFF_K:OFF_V]
    vwin_ref[0] = qkv[tq - WINDOW:tq, OFF_V:OFF_B]

    attn_o = _dot(att_ref[...], w_ao_ref[...])

    bch = _dot(xb, w_in_ref[:, OFF_B:OFF_GA])
    u = bch[:, CONV_DIM:2 * CONV_DIM] * bch[:, 2 * CONV_DIM:3 * CONV_DIM]
    ubuf_ref[CONV_PAD:CONV_PAD + tq, :] = u
    cw = convw_ref[...]
    y = (cw[0:1] * ubuf_ref[CONV_PAD - 2:CONV_PAD - 2 + tq, :]
         + cw[1:2] * ubuf_ref[CONV_PAD - 1:CONV_PAD - 1 + tq, :]
         + cw[2:3] * u)
    conv_ref[0] = ubuf_ref[CONV_PAD + tq - (CONV_K - 1):CONV_PAD + tq, :]
    ubuf_ref[0:CONV_PAD, :] = ubuf_ref[tq:tq + CONV_PAD, :]
    y_conv = bch[:, 0:CONV_DIM] * y

    gab = _dot(xb, w_in_ref[:, OFF_GA:IN_DIM])
    x1_ref[0] = _merge_project(x, attn_o, y_conv, gab[:, 0:D_MODEL], gab[:, D_MODEL:], w_co_ref, w_o_ref,
                               lng_ref, lnb_ref, alpha)


def _const_spec(shape):
    nd = len(shape)
    return pl.BlockSpec(shape, lambda *_: (0,) * nd, pipeline_mode=pl.Buffered(1))


def _mixer_prompt(x, w_in, w_ao, w_co, w_o, conv_w, bias_ab, sink_ab, ln_g, ln_b, *, alpha, tq):
    b, s, d = x.shape
    kernel = functools.partial(_mixer_prompt_kernel, alpha=alpha)
    return pl.pallas_call(
        kernel,
        grid=(b, s // tq),
        in_specs=[
            pl.BlockSpec((1, tq, d), lambda bi, i: (bi, i, 0)),
            _const_spec(w_in.shape), _const_spec(w_ao.shape), _const_spec(w_co.shape), _const_spec(w_o.shape),
            _const_spec(conv_w.shape), _const_spec(bias_ab.shape), _const_spec(sink_ab.shape),
            _const_spec(ln_g.shape), _const_spec(ln_b.shape),
        ],
        out_specs=[
            pl.BlockSpec((1, tq, d), lambda bi, i: (bi, i, 0)),
            pl.BlockSpec((1, WINDOW, KV_DIM), lambda bi, i: (bi, 0, 0)),
            pl.BlockSpec((1, WINDOW, KV_DIM), lambda bi, i: (bi, 0, 0)),
            pl.BlockSpec((1, CONV_K - 1, CONV_DIM), lambda bi, i: (bi, 0, 0)),
        ],
        out_shape=[
            jax.ShapeDtypeStruct((b, s, d), F32),
            jax.ShapeDtypeStruct((b, WINDOW, KV_DIM), F32),
            jax.ShapeDtypeStruct((b, WINDOW, KV_DIM), F32),
            jax.ShapeDtypeStruct((b, CONV_K - 1, CONV_DIM), F32),
        ],
        scratch_shapes=[
            pltpu.VMEM((N_KV_HEADS, 2 * ATTN_BLOCK, LANES), BF16),
            pltpu.VMEM((N_KV_HEADS, 2 * ATTN_BLOCK, LANES), BF16),
            pltpu.VMEM((N_KV_HEADS, 2 * ATTN_BLOCK, LANES), BF16),
            pltpu.VMEM((N_KV_HEADS, 2 * ATTN_BLOCK, LANES), BF16),
            pltpu.VMEM((tq, Q_DIM), BF16),
            pltpu.VMEM((tq + CONV_PAD, CONV_DIM), F32),
        ],
        compiler_params=pltpu.CompilerParams(
            dimension_semantics=("arbitrary", "arbitrary"), vmem_limit_bytes=VMEM_LIMIT),
        name="mixer_prompt",
    )(x, w_in, w_ao, w_co, w_o, conv_w, bias_ab, sink_ab, ln_g, ln_b)


def _proj_kernel(x_ref, w_ref, o_ref):
    o_ref[...] = _dot(x_ref[...].astype(BF16), w_ref[...])


def _sample_proj(x, w_in, *, tn):
    m, d = x.shape
    n = w_in.shape[1]
    return pl.pallas_call(
        _proj_kernel,
        grid=(n // tn,),
        in_specs=[pl.BlockSpec((m, d), lambda j: (0, 0)), pl.BlockSpec((d, tn), lambda j: (0, j))],
        out_specs=pl.BlockSpec((m, tn), lambda j: (0, j)),
        out_shape=jax.ShapeDtypeStruct((m, n), F32),
        compiler_params=pltpu.CompilerParams(dimension_semantics=("arbitrary",), vmem_limit_bytes=VMEM_LIMIT),
        name="sample_proj",
    )(x, w_in)


def _sample_attn_kernel(q4_ref, knew_ref, vnew_ref, ck_ref, cv_ref, bias_ref, sink_ref, hmask_ref,
                        nk_ref, nv_ref, ag_ref):
    bt = ck_ref.shape[0]
    win = ck_ref.shape[1]
    scale = HEAD_DIM ** -0.5
    row = lax.broadcasted_iota(jnp.int32, (win, KV_DIM), 0)
    last = row == win - 1
    hmask = hmask_ref[...]
    for b in range(bt):
        kb = jnp.where(last, knew_ref[b:b + 1, :], pltpu.roll(ck_ref[b], win - 1, 0))
        vb = jnp.where(last, vnew_ref[b:b + 1, :], pltpu.roll(cv_ref[b], win - 1, 0))
        nk_ref[b] = kb
        nv_ref[b] = vb
        q4 = q4_ref[b]
        qm = (jnp.concatenate([q4] * N_KV_HEADS, axis=0) * hmask).astype(BF16)
        s = _dot_nt(qm, kb.astype(BF16)) * scale + bias_ref[...]
        p = _sink_softmax(s, sink_ref[...]).astype(BF16)
        o = _dot(p, vb.astype(BF16)) * hmask
        o4 = o[0:GROUP]
        for h in range(1, N_KV_HEADS):
            o4 = o4 + o[h * GROUP:(h + 1) * GROUP]
        ag_ref[b] = o4


def _sample_attn(q4, k_new, v_new, cache_k, cache_v, bias_s, sink_col, hmask, *, bt):
    nb, win, kvd = cache_k.shape
    return pl.pallas_call(
        _sample_attn_kernel,
        grid=(nb // bt,),
        in_specs=[
            pl.BlockSpec((bt, GROUP, kvd), lambda i: (i, 0, 0)),
            pl.BlockSpec((bt, kvd), lambda i: (i, 0)),
            pl.BlockSpec((bt, kvd), lambda i: (i, 0)),
            pl.BlockSpec((bt, win, kvd), lambda i: (i, 0, 0)),
            pl.BlockSpec((bt, win, kvd), lambda i: (i, 0, 0)),
            pl.BlockSpec(bias_s.shape, lambda i: (0, 0)),
            pl.BlockSpec(sink_col.shape, lambda i: (0, 0)),
            pl.BlockSpec(hmask.shape, lambda i: (0, 0)),
        ],
        out_specs=[
            pl.BlockSpec((bt, win, kvd), lambda i: (i, 0, 0)),
            pl.BlockSpec((bt, win, kvd), lambda i: (i, 0, 0)),
            pl.BlockSpec((bt, GROUP, kvd), lambda i: (i, 0, 0)),
        ],
        out_shape=[
            jax.ShapeDtypeStruct((nb, win, kvd), F32),
            jax.ShapeDtypeStruct((nb, win, kvd), F32),
            jax.ShapeDtypeStruct((nb, GROUP, kvd), F32),
        ],
        compiler_params=pltpu.CompilerParams(dimension_semantics=("arbitrary",), vmem_limit_bytes=VMEM_LIMIT),
        name="sample_attn",
    )(q4, k_new, v_new, cache_k, cache_v, bias_s, sink_col, hmask)


def _sample_post_kernel(x_ref, att_ref, proj_ref, st_ref, convw_ref, w_ao_ref, w_co_ref, w_o_ref,
                        lng_ref, lnb_ref, x1_ref, u_ref, *, alpha):
    attn_o = _dot(att_ref[...].astype(BF16), w_ao_ref[...])
    u = proj_ref[:, OFF_C:OFF_H] * proj_ref[:, OFF_H:OFF_GA]
    cw = convw_ref[...]
    y = cw[0:1] * st_ref[0] + cw[1:2] * st_ref[1] + cw[2:3] * u
    u_ref[...] = u
    y_conv = proj_ref[:, OFF_B:OFF_C] * y
    x1_ref[...] = _merge_project(x_ref[...], attn_o, y_conv, proj_ref[:, OFF_GA:OFF_GB], proj_ref[:, OFF_GB:IN_DIM],
                                 w_co_ref, w_o_ref, lng_ref, lnb_ref, alpha)


def _sample_post(x, att, proj, state, conv_w, w_ao, w_co, w_o, ln_g, ln_b, *, alpha):
    m, d = x.shape
    kernel = functools.partial(_sample_post_kernel, alpha=alpha)
    return pl.pallas_call(
        kernel,
        out_shape=[jax.ShapeDtypeStruct((m, d), F32), jax.ShapeDtypeStruct((m, CONV_DIM), F32)],
        compiler_params=pltpu.CompilerParams(vmem_limit_bytes=VMEM_LIMIT),
        name="sample_post",
    )(x, att, proj, state, conv_w, w_ao, w_co, w_o, ln_g, ln_b)


def _first_max(cur, ids, axes, big):
    m = cur
    for ax in axes:
        m = jnp.max(m, axis=ax, keepdims=True)
    idx = jnp.where(cur == m, ids, big)
    for ax in axes:
        idx = jnp.min(idx, axis=ax, keepdims=True)
    return m, idx


def _router_kernel(x_ref, rwt_ref, rb_ref, gates_ref):
    tm = x_ref.shape[0]
    logits_t = _dot_nt(rwt_ref[...], x_ref[...].astype(BF16))
    scores = jax.nn.sigmoid(logits_t)
    sel = scores + rb_ref[...]
    shape3 = (N_GROUPS, GROUP_SIZE, tm)
    sel3 = sel.reshape(shape3)
    scores3 = scores.reshape(shape3)
    member = lax.broadcasted_iota(jnp.int32, shape3, 1)
    m1, i1 = _first_max(sel3, member, (1,), GROUP_SIZE)
    m2 = jnp.max(jnp.where(member == i1, -jnp.inf, sel3), axis=1, keepdims=True)
    gscore = m1 + m2
    gid = lax.broadcasted_iota(jnp.int32, gscore.shape, 0)
    gsel = jnp.zeros(gscore.shape, jnp.bool_)
    for _ in range(TOPK_GROUPS):
        _, gi = _first_max(gscore, gid, (0,), N_GROUPS)
        hit = gid == gi
        gsel = jnp.logical_or(gsel, hit)
        gscore = jnp.where(hit, -jnp.inf, gscore)
    eid = lax.broadcasted_iota(jnp.int32, shape3, 0) * GROUP_SIZE + member
    cur = jnp.where(gsel, sel3, -jnp.inf)
    chosen = jnp.zeros(shape3, jnp.bool_)
    for _ in range(TOP_K):
        _, ei = _first_max(cur, eid, (1, 0), N_EXPERTS)
        hit = eid == ei
        chosen = jnp.logical_or(chosen, hit)
        cur = jnp.where(hit, -jnp.inf, cur)
    w = jnp.where(chosen, scores3, 0.0)
    tot = jnp.sum(jnp.sum(w, axis=1, keepdims=True), axis=0, keepdims=True)
    gates_t = (w / tot * ROUTED_SCALE).reshape(N_EXPERTS, tm)
    gates_ref[...] = gates_t.T


def _router(x, rw_t, rb_col, *, tm):
    t, d = x.shape
    return pl.pallas_call(
        _router_kernel,
        grid=(t // tm,),
        in_specs=[
            pl.BlockSpec((tm, d), lambda i: (i, 0)),
            pl.BlockSpec(rw_t.shape, lambda i: (0, 0)),
            pl.BlockSpec(rb_col.shape, lambda i: (0, 0)),
        ],
        out_specs=pl.BlockSpec((tm, N_EXPERTS), lambda i: (i, 0)),
        out_shape=jax.ShapeDtypeStruct((t, N_EXPERTS), F32),
        compiler_params=pltpu.CompilerParams(dimension_semantics=("arbitrary",), vmem_limit_bytes=VMEM_LIMIT),
        name="router",
    )(x, rw_t, rb_col)


def _moe_kernel(x_ref, gates_ref, wg_ref, wu_ref, wd_ref, sg_ref, su_ref, sd_ref, lng_ref, lnb_ref,
                o_ref, xb_ref, acc_ref, *, alpha):
    e = pl.program_id(1)

    @pl.when(e == 0)
    def _shared():
        xb = x_ref[...].astype(BF16)
        xb_ref[...] = xb
        hs = jax.nn.silu(_dot(xb, sg_ref[...])) * _dot(xb, su_ref[...])
        acc_ref[...] = _dot(hs.astype(BF16), sd_ref[...])

    xb = xb_ref[...]
    lane = lax.broadcasted_iota(jnp.int32, gates_ref.shape, 1)
    gate = jnp.sum(jnp.where(lane == e, gates_ref[...], 0.0), axis=1, keepdims=True)
    hid = jax.nn.silu(_dot(xb, wg_ref[0].astype(BF16))) * _dot(xb, wu_ref[0].astype(BF16)) * gate
    acc_ref[...] += _dot(hid.astype(BF16), wd_ref[0].astype(BF16))

    @pl.when(e == pl.num_programs(1) - 1)
    def _finish():
        o_ref[...] = _layer_norm(alpha * x_ref[...] + acc_ref[...], lng_ref[...], lnb_ref[...])


def _moe(x, gates, wg, wu, wd, sg, su, sd, ln_g, ln_b, *, alpha, tm):
    t, d = x.shape
    ne, _, ff = wg.shape
    kernel = functools.partial(_moe_kernel, alpha=alpha)
    return pl.pallas_call(
        kernel,
        grid=(t // tm, ne),
        in_specs=[
            pl.BlockSpec((tm, d), lambda i, e: (i, 0)),
            pl.BlockSpec((tm, ne), lambda i, e: (i, 0)),
            pl.BlockSpec((1, d, ff), lambda i, e: (e, 0, 0)),
            pl.BlockSpec((1, d, ff), lambda i, e: (e, 0, 0)),
            pl.BlockSpec((1, ff, d), lambda i, e: (e, 0, 0)),
            pl.BlockSpec(sg.shape, lambda i, e: (0, 0)),
            pl.BlockSpec(su.shape, lambda i, e: (0, 0)),
            pl.BlockSpec(sd.shape, lambda i, e: (0, 0)),
            pl.BlockSpec(ln_g.shape, lambda i, e: (0, 0)),
            pl.BlockSpec(ln_b.shape, lambda i, e: (0, 0)),
        ],
        out_specs=pl.BlockSpec((tm, d), lambda i, e: (i, 0)),
        out_shape=jax.ShapeDtypeStruct((t, d), F32),
        scratch_shapes=[pltpu.VMEM((tm, d), BF16), pltpu.VMEM((tm, d), F32)],
        compiler_params=pltpu.CompilerParams(
            dimension_semantics=("arbitrary", "arbitrary"), vmem_limit_bytes=VMEM_LIMIT),
        name="moe",
    )(x, gates, wg, wu, wd, sg, su, sd, ln_g, ln_b)


def _t5_bucket(dist):
    n = jnp.maximum(dist, 0)
    max_exact = N_BUCKETS // 2
    large = max_exact + (jnp.log(jnp.maximum(n, 1).astype(F32) / max_exact)
                         / math.log(MAX_DISTANCE / max_exact) * (N_BUCKETS - max_exact)).astype(jnp.int32)
    large = jnp.minimum(large, N_BUCKETS - 1)
    return jnp.where(n < max_exact, n, large)


def _bias_tables(rel_bias, win):
    qi = jnp.arange(ATTN_BLOCK)[:, None]
    ki = jnp.arange(2 * ATTN_BLOCK)[None, :]
    dist = qi + ATTN_BLOCK - ki
    valid = (dist >= 0) & (dist < WINDOW)
    bias = jnp.moveaxis(rel_bias[_t5_bucket(dist)].astype(F32), -1, 0)
    bias = jnp.where(valid[None], bias, NEG).reshape(N_KV_HEADS, GROUP, ATTN_BLOCK, 2 * ATTN_BLOCK)
    bias_ab = jnp.stack([jnp.concatenate([bias[:, t], bias[:, t + 2]], axis=1) for t in range(2)], axis=1)
    dist_s = (win - 1) - jnp.arange(win)
    bias_s = rel_bias[_t5_bucket(dist_s)].astype(F32).T
    return bias_ab, bias_s


def _sink_tables(sink):
    s = sink.astype(F32).reshape(N_KV_HEADS, GROUP)
    rows = [jnp.concatenate([jnp.broadcast_to(s[:, t, None], (N_KV_HEADS, ATTN_BLOCK)),
                             jnp.broadcast_to(s[:, t + 2, None], (N_KV_HEADS, ATTN_BLOCK))], axis=1)
            for t in range(2)]
    return jnp.stack(rows, axis=1)[..., None], sink.astype(F32)[:, None]


def kernel(x_prompt, x_sample, cache_k_win, cache_v_win, state_conv, rel_bias, w_in, attn_sink, conv_w,
           w_attn_out, w_conv_out, w_out, ln1_g, ln1_b, router_w, router_bias, exp_w_gate, exp_w_up,
           exp_w_down, shared_w_gate, shared_w_up, shared_w_down, ln2_g, ln2_b):
    depth = w_in.shape[0]
    alpha = (2 * depth) ** 0.25
    nb, seq, d = x_prompt.shape
    nd = x_sample.shape[0]
    win = cache_k_win.shape[2]
    assert x_sample.shape[1] == 1 and win == WINDOW and seq % 512 == 0

    bias_ab, bias_s = _bias_tables(rel_bias, win)
    hmask = (jnp.arange(KV_DIM)[None, :] // HEAD_DIM == jnp.arange(N_HEADS)[:, None] // GROUP).astype(F32)

    yp = x_prompt
    ys = x_sample.reshape(nd, d)
    outs = [[] for _ in range(6)]
    for l in range(depth):
        w_in_b = w_in[l].astype(BF16)
        w_ao_b = w_attn_out[l].astype(BF16)
        w_co_b = w_conv_out[l].astype(BF16)
        w_o_b = w_out[l].astype(BF16)
        g1, b1 = ln1_g[l][None, :], ln1_b[l][None, :]
        g2, b2 = ln2_g[l][None, :], ln2_b[l][None, :]
        sink_ab, sink_col = _sink_tables(attn_sink[l])

        yp, kp, vp, cp = _mixer_prompt(yp, w_in_b, w_ao_b, w_co_b, w_o_b, conv_w[l], bias_ab, sink_ab, g1, b1,
                                       alpha=alpha, tq=512)

        proj = _sample_proj(ys, w_in_b, tn=IN_DIM // 4)
        q4 = proj[:, :Q_DIM].reshape(nd, N_KV_HEADS, GROUP, HEAD_DIM).transpose(0, 2, 1, 3).reshape(nd, GROUP, KV_DIM)
        ksn, vsn, ag = _sample_attn(q4, proj[:, OFF_K:OFF_V], proj[:, OFF_V:OFF_B],
                                    cache_k_win[l].reshape(nd, win, KV_DIM), cache_v_win[l].reshape(nd, win, KV_DIM),
                                    bias_s, sink_col, hmask, bt=8)
        att = ag.reshape(nd, GROUP, N_KV_HEADS, HEAD_DIM).transpose(0, 2, 1, 3).reshape(nd, Q_DIM)
        state_t = jnp.swapaxes(state_conv[l], 0, 1)
        ys, us = _sample_post(ys, att, proj, state_t, conv_w[l], w_ao_b, w_co_b, w_o_b, g1, b1, alpha=alpha)

        outs[0].append(kp.reshape(nb, WINDOW, N_KV_HEADS, HEAD_DIM))
        outs[1].append(vp.reshape(nb, WINDOW, N_KV_HEADS, HEAD_DIM))
        outs[2].append(cp)
        outs[3].append(ksn.reshape(nd, win, N_KV_HEADS, HEAD_DIM))
        outs[4].append(vsn.reshape(nd, win, N_KV_HEADS, HEAD_DIM))
        outs[5].append(jnp.concatenate([state_conv[l][:, 1:], us[:, None, :]], axis=1))

        rw_t = router_w[l].T.astype(BF16)
        rb_col = router_bias[l].astype(F32)[:, None]
        sg, su, sd = (shared_w_gate[l].astype(BF16), shared_w_up[l].astype(BF16), shared_w_down[l].astype(BF16))
        xp = yp.reshape(nb * seq, d)
        gp = _router(xp, rw_t, rb_col, tm=512)
        gs = _router(ys, rw_t, rb_col, tm=nd)
        yp = _moe(xp, gp, exp_w_gate[l], exp_w_up[l], exp_w_down[l], sg, su, sd, g2, b2,
                  alpha=alpha, tm=1024).reshape(nb, seq, d)
        ys = _moe(ys, gs, exp_w_gate[l], exp_w_up[l], exp_w_down[l], sg, su, sd, g2, b2, alpha=alpha, tm=nd)

    return (yp, ys.reshape(nd, 1, d)) + tuple(jnp.stack(o) for o in outs)
```

```python
import functools
import math

import jax
import jax.numpy as jnp
from jax import lax
from jax.experimental import pallas as pl
from jax.experimental.pallas import tpu as pltpu

D_MODEL = 1024
N_HEADS = 16
N_KV_HEADS = 4
HEAD_DIM = 64
GROUP = N_HEADS // N_KV_HEADS
WINDOW = 128
ATTN_BLOCK = 128
N_BUCKETS = 32
MAX_DISTANCE = 128
CONV_DIM = 1024
CONV_K = 3
N_EXPERTS = 64
TOP_K = 8
N_GROUPS = 8
TOPK_GROUPS = 4
GROUP_SIZE = N_EXPERTS // N_GROUPS
EXPERT_FF = 256
ROUTED_SCALE = 2.5
LN_EPS = 1e-5
NEG = -1e30

Q_DIM = N_HEADS * HEAD_DIM
KV_DIM = N_KV_HEADS * HEAD_DIM
OFF_K = Q_DIM
OFF_V = OFF_K + KV_DIM
OFF_B = OFF_V + KV_DIM
OFF_C = OFF_B + CONV_DIM
OFF_H = OFF_C + CONV_DIM
OFF_GA = OFF_H + CONV_DIM
OFF_GB = OFF_GA + D_MODEL
IN_DIM = OFF_GB + D_MODEL

LANES = 128
CONV_PAD = 8
VMEM_LIMIT = 60 * 1024 * 1024

BF16 = jnp.bfloat16
F32 = jnp.float32


def _dot(a, b):
    return jnp.dot(a, b, preferred_element_type=F32)


def _dot_nt(a, b):
    return lax.dot_general(a, b, (((1,), (1,)), ((), ())), preferred_element_type=F32)


def _layer_norm(z, g, b):
    mu = jnp.mean(z, axis=-1, keepdims=True)
    d = z - mu
    var = jnp.mean(d * d, axis=-1, keepdims=True)
    return d * lax.rsqrt(var + LN_EPS) * g + b


def _sink_softmax(s, sink_col):
    m = jnp.maximum(jnp.max(s, axis=-1, keepdims=True), sink_col)
    e = jnp.exp(s - m)
    den = jnp.sum(e, axis=-1, keepdims=True) + jnp.exp(sink_col - m)
    return e * (1.0 / den)


def _merge_project(x, attn_o, y_conv, g_a, g_b, w_co_ref, w_o_ref, lng_ref, lnb_ref, alpha):
    merged = jax.nn.sigmoid(g_a) * attn_o + jax.nn.sigmoid(g_b) * _dot(y_conv.astype(BF16), w_co_ref[...])
    out = _dot(merged.astype(BF16), w_o_ref[...])
    return _layer_norm(alpha * x + out, lng_ref[...], lnb_ref[...])


def _mixer_prompt_kernel(x_ref, w_in_ref, w_ao_ref, w_co_ref, w_o_ref, convw_ref, bias_ref, sink_ref,
                         lng_ref, lnb_ref,
                         x1_ref, kwin_ref, vwin_ref, conv_ref,
                         ka_ref, kb_ref, va_ref, vb_ref, att_ref, ubuf_ref, *, alpha):
    i = pl.program_id(1)
    tq = x_ref.shape[1]
    nblk = tq // ATTN_BLOCK
    half = LANES // 2
    scale = HEAD_DIM ** -0.5

    @pl.when(i == 0)
    def _init():
        for ref in (ka_ref, kb_ref, va_ref, vb_ref):
            ref[:, ATTN_BLOCK:, :] = jnp.zeros((N_KV_HEADS, ATTN_BLOCK, LANES), BF16)
        ubuf_ref[0:CONV_PAD, :] = jnp.zeros((CONV_PAD, CONV_DIM), F32)

    x = x_ref[0]
    xb = x.astype(BF16)
    qkv = _dot(xb, w_in_ref[:, 0:OFF_B])

    lane = lax.broadcasted_iota(jnp.int32, (ATTN_BLOCK, LANES), 1)
    lo = lane < half
    key_col = lax.broadcasted_iota(jnp.int32, (2 * ATTN_BLOCK, 2 * ATTN_BLOCK), 1)

    for j in range(nblk):
        r0 = j * ATTN_BLOCK
        rows = slice(r0, r0 + ATTN_BLOCK)
        for ref in (ka_ref, kb_ref, va_ref, vb_ref):
            ref[:, 0:ATTN_BLOCK, :] = ref[:, ATTN_BLOCK:, :]
        for c in range(N_KV_HEADS // 2):
            for off, a_ref, b_ref in ((OFF_K, ka_ref, kb_ref), (OFF_V, va_ref, vb_ref)):
                chunk = qkv[rows, off + c * LANES: off + (c + 1) * LANES]
                c_lo = jnp.where(lo, chunk, 0.0)
                c_hi = jnp.where(lo, 0.0, chunk)
                a_ref[2 * c, ATTN_BLOCK:, :] = c_lo.astype(BF16)
                b_ref[2 * c, ATTN_BLOCK:, :] = pltpu.roll(c_lo, half, 1).astype(BF16)
                b_ref[2 * c + 1, ATTN_BLOCK:, :] = c_hi.astype(BF16)
                a_ref[2 * c + 1, ATTN_BLOCK:, :] = pltpu.roll(c_hi, half, 1).astype(BF16)
        for h in range(N_KV_HEADS):
            q0 = h * GROUP * HEAD_DIM
            q2 = jnp.concatenate([qkv[rows, q0:q0 + LANES], qkv[rows, q0 + LANES:q0 + 2 * LANES]],
                                 axis=0).astype(BF16)
            probs = []
            for t, k_ref in enumerate((ka_ref, kb_ref)):
                s = _dot_nt(q2, k_ref[h]) * scale + bias_ref[h, t]
                if j == 0:
                    s = jnp.where(jnp.logical_and(i == 0, key_col < ATTN_BLOCK), NEG, s)
                probs.append(_sink_softmax(s, sink_ref[h, t]).astype(BF16))
            o = _dot(probs[0], va_ref[h]) + _dot(probs[1], vb_ref[h])
            att_ref[rows, q0:q0 + LANES] = o[0:ATTN_BLOCK].astype(BF16)
            att_ref[rows, q0 + LANES:q0 + 2 * LANES] = o[ATTN_BLOCK:].astype(BF16)

    kwin_ref[0] = qkv[tq - WINDOW:tq, OFF_K:OFF_V]
    vwin_ref[0] = qkv[tq - WINDOW:tq, OFF_V:OFF_B]

    attn_o = _dot(att_ref[...], w_ao_ref[...])

    bch = _dot(xb, w_in_ref[:, OFF_B:OFF_GA])
    u = bch[:, CONV_DIM:2 * CONV_DIM] * bch[:, 2 * CONV_DIM:3 * CONV_DIM]
    ubuf_ref[CONV_PAD:CONV_PAD + tq, :] = u
    cw = convw_ref[...]
    y = (cw[0:1] * ubuf_ref[CONV_PAD - 2:CONV_PAD - 2 + tq, :]
         + cw[1:2] * ubuf_ref[CONV_PAD - 1:CONV_PAD - 1 + tq, :]
         + cw[2:3] * u)
    conv_ref[0] = ubuf_ref[CONV_PAD + tq - (CONV_K - 1):CONV_PAD + tq, :]
    ubuf_ref[0:CONV_PAD, :] = ubuf_ref[tq:tq + CONV_PAD, :]
    y_conv = bch[:, 0:CONV_DIM] * y

    gab = _dot(xb, w_in_ref[:, OFF_GA:IN_DIM])
    x1_ref[0] = _merge_project(x, attn_o, y_conv, gab[:, 0:D_MODEL], gab[:, D_MODEL:], w_co_ref, w_o_ref,
                               lng_ref, lnb_ref, alpha)


def _const_spec(shape):
    nd = len(shape)
    return pl.BlockSpec(shape, lambda *_: (0,) * nd, pipeline_mode=pl.Buffered(1))


def _mixer_prompt(x, w_in, w_ao, w_co, w_o, conv_w, bias_ab, sink_ab, ln_g, ln_b, *, alpha, tq):
    b, s, d = x.shape
    kernel = functools.partial(_mixer_prompt_kernel, alpha=alpha)
    return pl.pallas_call(
        kernel,
        grid=(b, s // tq),
        in_specs=[
            pl.BlockSpec((1, tq, d), lambda bi, i: (bi, i, 0)),
            _const_spec(w_in.shape), _const_spec(w_ao.shape), _const_spec(w_co.shape), _const_spec(w_o.shape),
            _const_spec(conv_w.shape), _const_spec(bias_ab.shape), _const_spec(sink_ab.shape),
            _const_spec(ln_g.shape), _const_spec(ln_b.shape),
        ],
        out_specs=[
            pl.BlockSpec((1, tq, d), lambda bi, i: (bi, i, 0)),
            pl.BlockSpec((1, WINDOW, KV_DIM), lambda bi, i: (bi, 0, 0)),
            pl.BlockSpec((1, WINDOW, KV_DIM), lambda bi, i: (bi, 0, 0)),
            pl.BlockSpec((1, CONV_K - 1, CONV_DIM), lambda bi, i: (bi, 0, 0)),
        ],
        out_shape=[
            jax.ShapeDtypeStruct((b, s, d), F32),
            jax.ShapeDtypeStruct((b, WINDOW, KV_DIM), F32),
            jax.ShapeDtypeStruct((b, WINDOW, KV_DIM), F32),
            jax.ShapeDtypeStruct((b, CONV_K - 1, CONV_DIM), F32),
        ],
        scratch_shapes=[
            pltpu.VMEM((N_KV_HEADS, 2 * ATTN_BLOCK, LANES), BF16),
            pltpu.VMEM((N_KV_HEADS, 2 * ATTN_BLOCK, LANES), BF16),
            pltpu.VMEM((N_KV_HEADS, 2 * ATTN_BLOCK, LANES), BF16),
            pltpu.VMEM((N_KV_HEADS, 2 * ATTN_BLOCK, LANES), BF16),
            pltpu.VMEM((tq, Q_DIM), BF16),
            pltpu.VMEM((tq + CONV_PAD, CONV_DIM), F32),
        ],
        compiler_params=pltpu.CompilerParams(
            dimension_semantics=("arbitrary", "arbitrary"), vmem_limit_bytes=VMEM_LIMIT),
        name="mixer_prompt",
    )(x, w_in, w_ao, w_co, w_o, conv_w, bias_ab, sink_ab, ln_g, ln_b)


def _proj_kernel(x_ref, w_ref, o_ref):
    o_ref[...] = _dot(x_ref[...].astype(BF16), w_ref[...])


def _sample_proj(x, w_in, *, tn):
    m, d = x.shape
    n = w_in.shape[1]
    return pl.pallas_call(
        _proj_kernel,
        grid=(n // tn,),
        in_specs=[pl.BlockSpec((m, d), lambda j: (0, 0)), pl.BlockSpec((d, tn), lambda j: (0, j))],
        out_specs=pl.BlockSpec((m, tn), lambda j: (0, j)),
        out_shape=jax.ShapeDtypeStruct((m, n), F32),
        compiler_params=pltpu.CompilerParams(dimension_semantics=("arbitrary",), vmem_limit_bytes=VMEM_LIMIT),
        name="sample_proj",
    )(x, w_in)


def _sample_attn_kernel(q4_ref, knew_ref, vnew_ref, ck_ref, cv_ref, bias_ref, sink_ref, hmask_ref,
                        nk_ref, nv_ref, ag_ref):
    bt = ck_ref.shape[0]
    win = ck_ref.shape[1]
    scale = HEAD_DIM ** -0.5
    row = lax.broadcasted_iota(jnp.int32, (win, KV_DIM), 0)
    last = row == win - 1
    hmask = hmask_ref[...]
    for b in range(bt):
        kb = jnp.where(last, knew_ref[b:b + 1, :], pltpu.roll(ck_ref[b], win - 1, 0))
        vb = jnp.where(last, vnew_ref[b:b + 1, :], pltpu.roll(cv_ref[b], win - 1, 0))
        nk_ref[b] = kb
        nv_ref[b] = vb
        q4 = q4_ref[b]
        qm = (jnp.concatenate([q4] * N_KV_HEADS, axis=0) * hmask).astype(BF16)
        s = _dot_nt(qm, kb.astype(BF16)) * scale + bias_ref[...]
        p = _sink_softmax(s, sink_ref[...]).astype(BF16)
        o = _dot(p, vb.astype(BF16)) * hmask
        o4 = o[0:GROUP]
        for h in range(1, N_KV_HEADS):
            o4 = o4 + o[h * GROUP:(h + 1) * GROUP]
        ag_ref[b] = o4


def _sample_attn(q4, k_new, v_new, cache_k, cache_v, bias_s, sink_col, hmask, *, bt):
    nb, win, kvd = cache_k.shape
    return pl.pallas_call(
        _sample_attn_kernel,
        grid=(nb // bt,),
        in_specs=[
            pl.BlockSpec((bt, GROUP, kvd), lambda i: (i, 0, 0)),
            pl.BlockSpec((bt, kvd), lambda i: (i, 0)),
            pl.BlockSpec((bt, kvd), lambda i: (i, 0)),
            pl.BlockSpec((bt, win, kvd), lambda i: (i, 0, 0)),
            pl.BlockSpec((bt, win, kvd), lambda i: (i, 0, 0)),
            pl.BlockSpec(bias_s.shape, lambda i: (0, 0)),
            pl.BlockSpec(sink_col.shape, lambda i: (0, 0)),
            pl.BlockSpec(hmask.shape, lambda i: (0, 0)),
        ],
        out_specs=[
            pl.BlockSpec((bt, win, kvd), lambda i: (i, 0, 0)),
            pl.BlockSpec((bt, win, kvd), lambda i: (i, 0, 0)),
            pl.BlockSpec((bt, GROUP, kvd), lambda i: (i, 0, 0)),
        ],
        out_shape=[
            jax.ShapeDtypeStruct((nb, win, kvd), F32),
            jax.ShapeDtypeStruct((nb, win, kvd), F32),
            jax.ShapeDtypeStruct((nb, GROUP, kvd), F32),
        ],
        compiler_params=pltpu.CompilerParams(dimension_semantics=("arbitrary",), vmem_limit_bytes=VMEM_LIMIT),
        name="sample_attn",
    )(q4, k_new, v_new, cache_k, cache_v, bias_s, sink_col, hmask)


def _sample_post_kernel(x_ref, att_ref, proj_ref, st_ref, convw_ref, w_ao_ref, w_co_ref, w_o_ref,
                        lng_ref, lnb_ref, x1_ref, u_ref, *, alpha):
    attn_o = _dot(att_ref[...].astype(BF16), w_ao_ref[...])
    u = proj_ref[:, OFF_C:OFF_H] * proj_ref[:, OFF_H:OFF_GA]
    cw = convw_ref[...]
    y = cw[0:1] * st_ref[0] + cw[1:2] * st_ref[1] + cw[2:3] * u
    u_ref[...] = u
    y_conv = proj_ref[:, OFF_B:OFF_C] * y
    x1_ref[...] = _merge_project(x_ref[...], attn_o, y_conv, proj_ref[:, OFF_GA:OFF_GB], proj_ref[:, OFF_GB:IN_DIM],
                                 w_co_ref, w_o_ref, lng_ref, lnb_ref, alpha)


def _sample_post(x, att, proj, state, conv_w, w_ao, w_co, w_o, ln_g, ln_b, *, alpha):
    m, d = x.shape
    kernel = functools.partial(_sample_post_kernel, alpha=alpha)
    return pl.pallas_call(
        kernel,
        out_shape=[jax.ShapeDtypeStruct((m, d), F32), jax.ShapeDtypeStruct((m, CONV_DIM), F32)],
        compiler_params=pltpu.CompilerParams(vmem_limit_bytes=VMEM_LIMIT),
        name="sample_post",
    )(x, att, proj, state, conv_w, w_ao, w_co, w_o, ln_g, ln_b)


def _first_max(cur, ids, axes, big):
    m = cur
    for ax in axes:
        m = jnp.max(m, axis=ax, keepdims=True)
    idx = jnp.where(cur == m, ids, big)
    for ax in axes:
        idx = jnp.min(idx, axis=ax, keepdims=True)
    return m, idx


def _router_kernel(x_ref, rwt_ref, rb_ref, eidx_ref, gate_ref):
    tm = x_ref.shape[0]
    logits_t = _dot_nt(rwt_ref[...], x_ref[...].astype(BF16))
    scores = jax.nn.sigmoid(logits_t)
    sel = scores + rb_ref[...]
    shape3 = (N_GROUPS, GROUP_SIZE, tm)
    sel3 = sel.reshape(shape3)
    scores3 = scores.reshape(shape3)
    member = lax.broadcasted_iota(jnp.int32, shape3, 1)
    m1, i1 = _first_max(sel3, member, (1,), GROUP_SIZE)
    m2 = jnp.max(jnp.where(member == i1, -jnp.inf, sel3), axis=1, keepdims=True)
    gscore = m1 + m2
    gid = lax.broadcasted_iota(jnp.int32, gscore.shape, 0)
    gsel = jnp.zeros(gscore.shape, jnp.bool_)
    for _ in range(TOPK_GROUPS):
        _, gi = _first_max(gscore, gid, (0,), N_GROUPS)
        hit = gid == gi
        gsel = jnp.logical_or(gsel, hit)
        gscore = jnp.where(hit, -jnp.inf, gscore)
    eid = lax.broadcasted_iota(jnp.int32, shape3, 0) * GROUP_SIZE + member
    cur = jnp.where(gsel, sel3, -jnp.inf)
    ids, ws = [], []
    for _ in range(TOP_K):
        _, ei = _first_max(cur, eid, (1, 0), N_EXPERTS)
        hit = eid == ei
        sc = jnp.sum(jnp.sum(jnp.where(hit, scores3, 0.0), axis=1, keepdims=True), axis=0, keepdims=True)
        ids.append(ei[0])
        ws.append(sc[0])
        cur = jnp.where(hit, -jnp.inf, cur)
    w = jnp.concatenate(ws, axis=0)
    tot = jnp.sum(w, axis=0, keepdims=True)
    eidx_ref[...] = jnp.concatenate(ids, axis=0)
    gate_ref[...] = w / tot * ROUTED_SCALE


def _router(x, rw_t, rb_col, *, tm):
    t, d = x.shape
    return pl.pallas_call(
        _router_kernel,
        grid=(t // tm,),
        in_specs=[
            pl.BlockSpec((tm, d), lambda i: (i, 0)),
            pl.BlockSpec(rw_t.shape, lambda i: (0, 0)),
            pl.BlockSpec(rb_col.shape, lambda i: (0, 0)),
        ],
        out_specs=[pl.BlockSpec((TOP_K, tm), lambda i: (0, i)), pl.BlockSpec((TOP_K, tm), lambda i: (0, i))],
        out_shape=[jax.ShapeDtypeStruct((TOP_K, t), jnp.int32), jax.ShapeDtypeStruct((TOP_K, t), F32)],
        compiler_params=pltpu.CompilerParams(dimension_semantics=("arbitrary",), vmem_limit_bytes=VMEM_LIMIT),
        name="router",
    )(x, rw_t, rb_col)


ROW_TILE = 256
TILE_PITCH = ROW_TILE + 1
SPARE_TOKENS = 8
KEY_SPAN = 8192
FLAG_FIRST, FLAG_LAST, FLAG_NEW_EXPERT, FLAG_VALID = 1, 2, 4, 8
SCATTER_BATCH = 8


def _moe_routed_kernel(ce_ref, flags_ref, idx_ref, gate_ref, x_ref, wg_ref, wu_ref, wd_ref, sg_ref, su_ref,
                       sd_ref, lng_ref, lnb_ref, o_ref, wgu_ref, wdb_ref, gat_ref, res_ref, *,
                       alpha, chunk_tokens, sub_tokens):
    del ce_ref
    flags = flags_ref[pl.program_id(0)]
    nchunk = D_MODEL // LANES
    ff = wg_ref.shape[2]

    @pl.when((flags & FLAG_FIRST) != 0)
    def _zero():
        o_ref[...] = jnp.zeros(o_ref.shape, F32)

    @pl.when((flags & FLAG_NEW_EXPERT) != 0)
    def _cast_weights():
        wgu_ref[:, 0:ff] = wg_ref[0].astype(BF16)
        wgu_ref[:, ff:2 * ff] = wu_ref[0].astype(BF16)
        wdb_ref[...] = wd_ref[0].astype(BF16)

    @pl.when((flags & FLAG_VALID) != 0)
    def _tile():
        for r in range(ROW_TILE):
            src = pl.multiple_of(idx_ref[0, 0, r], 8)
            gat_ref[pl.ds(r, nchunk, stride=TILE_PITCH), :] = x_ref[pl.ds(src, 8), :]
        lhs = jnp.concatenate([gat_ref[j * TILE_PITCH:j * TILE_PITCH + ROW_TILE, :] for j in range(nchunk)],
                              axis=1).astype(BF16)
        h = _dot(lhs, wgu_ref[...])
        gate_col = jnp.broadcast_to(gate_ref[0], (LANES, ROW_TILE)).T
        hid = jax.nn.silu(h[:, 0:ff]) * h[:, ff:2 * ff] * jnp.concatenate([gate_col] * (ff // LANES), axis=1)
        y = _dot(hid.astype(BF16), wdb_ref[...])
        for j in range(nchunk):
            res_ref[j * TILE_PITCH:j * TILE_PITCH + ROW_TILE, :] = y[:, j * LANES:(j + 1) * LANES]
        for r0 in range(0, ROW_TILE, SCATTER_BATCH):
            dst = [pl.multiple_of(idx_ref[0, 0, r0 + u], 8) for u in range(SCATTER_BATCH)]
            acc = [o_ref[pl.ds(dst[u], 8), :] + res_ref[pl.ds(r0 + u, nchunk, stride=TILE_PITCH), :]
                   for u in range(SCATTER_BATCH)]
            for u in range(SCATTER_BATCH):
                o_ref[pl.ds(dst[u], 8), :] = acc[u]

    @pl.when((flags & FLAG_LAST) != 0)
    def _finish():
        def body(s, carry):
            base = pl.multiple_of(s * (sub_tokens * 8), 8)

            def rows_2d(ref):
                return jnp.concatenate([ref[pl.ds(base + j, sub_tokens, stride=8), :] for j in range(nchunk)],
                                       axis=1)

            x2 = rows_2d(x_ref)
            xb = x2.astype(BF16)
            hs = jax.nn.silu(_dot(xb, sg_ref[...])) * _dot(xb, su_ref[...])
            ffn = rows_2d(o_ref) + _dot(hs.astype(BF16), sd_ref[...])
            res = _layer_norm(alpha * x2 + ffn, lng_ref[...], lnb_ref[...])
            for j in range(nchunk):
                o_ref[pl.ds(base + j, sub_tokens, stride=8), :] = res[:, j * LANES:(j + 1) * LANES]
            return carry

        lax.fori_loop(0, chunk_tokens // sub_tokens, body, 0)


def _moe_routed(tile_ce, tile_flags, idx8, gates, x_tm, wg, wu, wd, sg, su, sd, ln_g, ln_b, *,
                alpha, chunk_tokens, sub_tokens):
    n_chunks, chunk_rows, _ = x_tm.shape
    n_tiles = idx8.shape[0]
    ne, d, ff = wg.shape
    kernel = functools.partial(_moe_routed_kernel, alpha=alpha, chunk_tokens=chunk_tokens, sub_tokens=sub_tokens)

    def chunk_map(i, ce, fl):
        return (ce[i] // ne, 0, 0)

    def expert_map(i, ce, fl):
        return (ce[i] % ne, 0, 0)

    def const2(i, ce, fl):
        return (0, 0)

    grid_spec = pltpu.PrefetchScalarGridSpec(
        num_scalar_prefetch=2,
        grid=(n_tiles,),
        in_specs=[
            pl.BlockSpec((1, 1, ROW_TILE), lambda i, ce, fl: (i, 0, 0), memory_space=pltpu.SMEM),
            pl.BlockSpec((1, 1, ROW_TILE), lambda i, ce, fl: (i, 0, 0)),
            pl.BlockSpec((None, chunk_rows, LANES), chunk_map, pipeline_mode=pl.Buffered(1)),
            pl.BlockSpec((1, d, ff), expert_map),
            pl.BlockSpec((1, d, ff), expert_map),
            pl.BlockSpec((1, ff, d), expert_map),
            pl.BlockSpec(sg.shape, const2, pipeline_mode=pl.Buffered(1)),
            pl.BlockSpec(su.shape, const2, pipeline_mode=pl.Buffered(1)),
            pl.BlockSpec(sd.shape, const2, pipeline_mode=pl.Buffered(1)),
            pl.BlockSpec(ln_g.shape, const2),
            pl.BlockSpec(ln_b.shape, const2),
        ],
        out_specs=pl.BlockSpec((None, chunk_rows, LANES), chunk_map, pipeline_mode=pl.Buffered(1)),
        scratch_shapes=[
            pltpu.VMEM((d, 2 * ff), BF16),
            pltpu.VMEM((ff, d), BF16),
            pltpu.VMEM((8 * TILE_PITCH, LANES), F32),
            pltpu.VMEM((8 * TILE_PITCH, LANES), F32),
        ],
    )
    return pl.pallas_call(
        kernel,
        grid_spec=grid_spec,
        out_shape=jax.ShapeDtypeStruct(x_tm.shape, F32),
        compiler_params=pltpu.CompilerParams(dimension_semantics=("arbitrary",), vmem_limit_bytes=VMEM_LIMIT),
        name="moe_routed",
    )(tile_ce, tile_flags, idx8, gates, x_tm, wg, wu, wd, sg, su, sd, ln_g, ln_b)


MOE_CHUNKS = 4
MOE_SUB_MAX = 512


def _moe_tiling(total_tokens):
    assert total_tokens % (MOE_CHUNKS * 8) == 0
    chunk_tokens = total_tokens // MOE_CHUNKS
    assert chunk_tokens < KEY_SPAN - 1
    sub_tokens = max(s for s in range(8, MOE_SUB_MAX + 1, 8) if chunk_tokens % s == 0)
    return MOE_CHUNKS, chunk_tokens, sub_tokens


def _route_plan(eidx, gate, *, chunk_tokens, n_chunks):
    t = eidx.shape[1]
    n_seg = n_chunks * N_EXPERTS
    tok = jnp.arange(t, dtype=jnp.int32)
    seg = (tok // chunk_tokens)[None, :] * N_EXPERTS + eidx
    keys_real = (seg * KEY_SPAN + (tok % chunk_tokens)[None, :]).reshape(-1)
    counts = jnp.sum(seg.reshape(-1)[:, None] == jnp.arange(n_seg, dtype=jnp.int32)[None, :], axis=0,
                     dtype=jnp.int32)
    n_pad = (-counts) % ROW_TILE
    slot = jnp.arange(ROW_TILE - 1, dtype=jnp.int32)
    int_max = jnp.iinfo(jnp.int32).max
    keys_pad = jnp.where(slot[None, :] < n_pad[:, None],
                         jnp.arange(n_seg, dtype=jnp.int32)[:, None] * KEY_SPAN + (KEY_SPAN - 1), int_max)
    keys = jnp.concatenate([keys_real, keys_pad.reshape(-1)])
    vals = jnp.concatenate([gate.reshape(-1), jnp.zeros((keys_pad.size,), F32)])
    assert keys.size % ROW_TILE == 0
    keys, vals = lax.sort((keys, vals), num_keys=1)
    n_tiles = keys.size // ROW_TILE
    keys = keys.reshape(n_tiles, ROW_TILE)
    head = keys[:, 0]
    valid = head != int_max
    tile_ce = jnp.minimum(head // KEY_SPAN, n_seg - 1)
    tile_c = tile_ce // N_EXPERTS
    prev_ce = jnp.concatenate([jnp.full((1,), -N_EXPERTS, jnp.int32), tile_ce[:-1]])
    next_c = jnp.concatenate([tile_c[1:], jnp.full((1,), -1, jnp.int32)])
    next_valid = jnp.concatenate([valid[1:], jnp.zeros((1,), jnp.bool_)])
    first = valid & (tile_c != prev_ce // N_EXPERTS)
    last = valid & (~next_valid | (next_c != tile_c))
    new_expert = valid & (tile_ce != prev_ce)
    flags = (first * FLAG_FIRST + last * FLAG_LAST + new_expert * FLAG_NEW_EXPERT + valid * FLAG_VALID)
    idx8 = jnp.minimum(keys & (KEY_SPAN - 1), chunk_tokens) * 8
    return (tile_ce, flags.astype(jnp.int32), idx8.reshape(n_tiles, 1, ROW_TILE),
            vals.reshape(n_tiles, 1, ROW_TILE))


def _t5_bucket(dist):
    n = jnp.maximum(dist, 0)
    max_exact = N_BUCKETS // 2
    large = max_exact + (jnp.log(jnp.maximum(n, 1).astype(F32) / max_exact)
                         / math.log(MAX_DISTANCE / max_exact) * (N_BUCKETS - max_exact)).astype(jnp.int32)
    large = jnp.minimum(large, N_BUCKETS - 1)
    return jnp.where(n < max_exact, n, large)


def _bias_tables(rel_bias, win):
    qi = jnp.arange(ATTN_BLOCK)[:, None]
    ki = jnp.arange(2 * ATTN_BLOCK)[None, :]
    dist = qi + ATTN_BLOCK - ki
    valid = (dist >= 0) & (dist < WINDOW)
    bias = jnp.moveaxis(rel_bias[_t5_bucket(dist)].astype(F32), -1, 0)
    bias = jnp.where(valid[None], bias, NEG).reshape(N_KV_HEADS, GROUP, ATTN_BLOCK, 2 * ATTN_BLOCK)
    bias_ab = jnp.stack([jnp.concatenate([bias[:, t], bias[:, t + 2]], axis=1) for t in range(2)], axis=1)
    dist_s = (win - 1) - jnp.arange(win)
    bias_s = rel_bias[_t5_bucket(dist_s)].astype(F32).T
    return bias_ab, bias_s


def _sink_tables(sink):
    s = sink.astype(F32).reshape(N_KV_HEADS, GROUP)
    rows = [jnp.concatenate([jnp.broadcast_to(s[:, t, None], (N_KV_HEADS, ATTN_BLOCK)),
                             jnp.broadcast_to(s[:, t + 2, None], (N_KV_HEADS, ATTN_BLOCK))], axis=1)
            for t in range(2)]
    return jnp.stack(rows, axis=1)[..., None], sink.astype(F32)[:, None]


def kernel(x_prompt, x_sample, cache_k_win, cache_v_win, state_conv, rel_bias, w_in, attn_sink, conv_w,
           w_attn_out, w_conv_out, w_out, ln1_g, ln1_b, router_w, router_bias, exp_w_gate, exp_w_up,
           exp_w_down, shared_w_gate, shared_w_up, shared_w_down, ln2_g, ln2_b):
    depth = w_in.shape[0]
    alpha = (2 * depth) ** 0.25
    nb, seq, d = x_prompt.shape
    nd = x_sample.shape[0]
    win = cache_k_win.shape[2]
    assert x_sample.shape[1] == 1 and win == WINDOW and seq % 512 == 0

    n_prompt = nb * seq
    n_chunks, chunk_tokens, sub_tokens = _moe_tiling(n_prompt + nd)

    bias_ab, bias_s = _bias_tables(rel_bias, win)
    hmask = (jnp.arange(KV_DIM)[None, :] // HEAD_DIM == jnp.arange(N_HEADS)[:, None] // GROUP).astype(F32)

    yp = x_prompt
    ys = x_sample.reshape(nd, d)
    outs = [[] for _ in range(6)]
    for l in range(depth):
        w_in_b = w_in[l].astype(BF16)
        w_ao_b = w_attn_out[l].astype(BF16)
        w_co_b = w_conv_out[l].astype(BF16)
        w_o_b = w_out[l].astype(BF16)
        g1, b1 = ln1_g[l][None, :], ln1_b[l][None, :]
        g2, b2 = ln2_g[l][None, :], ln2_b[l][None, :]
        sink_ab, sink_col = _sink_tables(attn_sink[l])

        yp, kp, vp, cp = _mixer_prompt(yp, w_in_b, w_ao_b, w_co_b, w_o_b, conv_w[l], bias_ab, sink_ab, g1, b1,
                                       alpha=alpha, tq=512)

        proj = _sample_proj(ys, w_in_b, tn=IN_DIM // 4)
        q4 = proj[:, :Q_DIM].reshape(nd, N_KV_HEADS, GROUP, HEAD_DIM).transpose(0, 2, 1, 3).reshape(nd, GROUP, KV_DIM)
        ksn, vsn, ag = _sample_attn(q4, proj[:, OFF_K:OFF_V], proj[:, OFF_V:OFF_B],
                                    cache_k_win[l].reshape(nd, win, KV_DIM), cache_v_win[l].reshape(nd, win, KV_DIM),
                                    bias_s, sink_col, hmask, bt=8)
        att = ag.reshape(nd, GROUP, N_KV_HEADS, HEAD_DIM).transpose(0, 2, 1, 3).reshape(nd, Q_DIM)
        state_t = jnp.swapaxes(state_conv[l], 0, 1)
        ys, us = _sample_post(ys, att, proj, state_t, conv_w[l], w_ao_b, w_co_b, w_o_b, g1, b1, alpha=alpha)

        outs[0].append(kp.reshape(nb, WINDOW, N_KV_HEADS, HEAD_DIM))
        outs[1].append(vp.reshape(nb, WINDOW, N_KV_HEADS, HEAD_DIM))
        outs[2].append(cp)
        outs[3].append(ksn.reshape(nd, win, N_KV_HEADS, HEAD_DIM))
        outs[4].append(vsn.reshape(nd, win, N_KV_HEADS, HEAD_DIM))
        outs[5].append(jnp.concatenate([state_conv[l][:, 1:], us[:, None, :]], axis=1))

        rw_t = router_w[l].T.astype(BF16)
        rb_col = router_bias[l].astype(F32)[:, None]
        sg, su, sd = (shared_w_gate[l].astype(BF16), shared_w_up[l].astype(BF16), shared_w_down[l].astype(BF16))
        xp = yp.reshape(n_prompt, d)
        ep, wp = _router(xp, rw_t, rb_col, tm=512)
        es, ws = _router(ys, rw_t, rb_col, tm=nd)
        plan = _route_plan(jnp.concatenate([ep, es], axis=1), jnp.concatenate([wp, ws], axis=1),
                           chunk_tokens=chunk_tokens, n_chunks=n_chunks)
        xt = jnp.concatenate([xp, ys], axis=0).reshape(n_chunks, chunk_tokens, d)
        xt = jnp.pad(xt, ((0, 0), (0, SPARE_TOKENS), (0, 0))).reshape(n_chunks, -1, LANES)
        out = _moe_routed(*plan, xt, exp_w_gate[l], exp_w_up[l], exp_w_down[l], sg, su, sd, g2, b2,
                          alpha=alpha, chunk_tokens=chunk_tokens, sub_tokens=sub_tokens)
        out = out.reshape(n_chunks, chunk_tokens + SPARE_TOKENS, d)[:, :chunk_tokens].reshape(-1, d)
        yp = out[:n_prompt].reshape(nb, seq, d)
        ys = out[n_prompt:]

    return (yp, ys.reshape(nd, 1, d)) + tuple(jnp.stack(o) for o in outs)
```

```python
import functools
import math

import jax
import jax.numpy as jnp
from jax import lax
from jax.experimental import pallas as pl
from jax.experimental.pallas import tpu as pltpu

D_MODEL = 1024
N_HEADS = 16
N_KV_HEADS = 4
HEAD_DIM = 64
GROUP = N_HEADS // N_KV_HEADS
WINDOW = 128
ATTN_BLOCK = 128
N_BUCKETS = 32
MAX_DISTANCE = 128
CONV_DIM = 1024
CONV_K = 3
N_EXPERTS = 64
TOP_K = 8
N_GROUPS = 8
TOPK_GROUPS = 4
GROUP_SIZE = N_EXPERTS // N_GROUPS
EXPERT_FF = 256
ROUTED_SCALE = 2.5
LN_EPS = 1e-5
NEG = -1e30

Q_DIM = N_HEADS * HEAD_DIM
KV_DIM = N_KV_HEADS * HEAD_DIM
OFF_K = Q_DIM
OFF_V = OFF_K + KV_DIM
OFF_B = OFF_V + KV_DIM
OFF_C = OFF_B + CONV_DIM
OFF_H = OFF_C + CONV_DIM
OFF_GA = OFF_H + CONV_DIM
OFF_GB = OFF_GA + D_MODEL
IN_DIM = OFF_GB + D_MODEL

LANES = 128
CONV_PAD = 8
VMEM_LIMIT = 60 * 1024 * 1024

BF16 = jnp.bfloat16
F32 = jnp.float32


def _dot(a, b):
    return jnp.dot(a, b, preferred_element_type=F32)


def _dot_nt(a, b):
    return lax.dot_general(a, b, (((1,), (1,)), ((), ())), preferred_element_type=F32)


def _layer_norm(z, g, b):
    mu = jnp.mean(z, axis=-1, keepdims=True)
    d = z - mu
    var = jnp.mean(d * d, axis=-1, keepdims=True)
    return d * lax.rsqrt(var + LN_EPS) * g + b


def _sink_softmax(s, sink_col):
    m = jnp.maximum(jnp.max(s, axis=-1, keepdims=True), sink_col)
    e = jnp.exp(s - m)
    den = jnp.sum(e, axis=-1, keepdims=True) + jnp.exp(sink_col - m)
    return e * (1.0 / den)


def _merge_project(x, attn_o, y_conv, g_a, g_b, w_co_ref, w_o_ref, lng_ref, lnb_ref, alpha):
    merged = jax.nn.sigmoid(g_a) * attn_o + jax.nn.sigmoid(g_b) * _dot(y_conv.astype(BF16), w_co_ref[...])
    out = _dot(merged.astype(BF16), w_o_ref[...])
    return _layer_norm(alpha * x + out, lng_ref[...], lnb_ref[...])


def _mixer_prompt_kernel(x_ref, w_in_ref, w_ao_ref, w_co_ref, w_o_ref, convw_ref, bias_ref, sink_ref,
                         lng_ref, lnb_ref,
                         x1_ref, kwin_ref, vwin_ref, conv_ref,
                         ka_ref, kb_ref, va_ref, vb_ref, att_ref, ubuf_ref, *, alpha):
    i = pl.program_id(1)
    tq = x_ref.shape[1]
    nblk = tq // ATTN_BLOCK
    half = LANES // 2
    scale = HEAD_DIM ** -0.5

    @pl.when(i == 0)
    def _init():
        for ref in (ka_ref, kb_ref, va_ref, vb_ref):
            ref[:, ATTN_BLOCK:, :] = jnp.zeros((N_KV_HEADS, ATTN_BLOCK, LANES), BF16)
        ubuf_ref[0:CONV_PAD, :] = jnp.zeros((CONV_PAD, CONV_DIM), F32)

    x = x_ref[0]
    xb = x.astype(BF16)
    qkv = _dot(xb, w_in_ref[:, 0:OFF_B])

    lane = lax.broadcasted_iota(jnp.int32, (ATTN_BLOCK, LANES), 1)
    lo = lane < half
    key_col = lax.broadcasted_iota(jnp.int32, (2 * ATTN_BLOCK, 2 * ATTN_BLOCK), 1)

    for j in range(nblk):
        r0 = j * ATTN_BLOCK
        rows = slice(r0, r0 + ATTN_BLOCK)
        for ref in (ka_ref, kb_ref, va_ref, vb_ref):
            ref[:, 0:ATTN_BLOCK, :] = ref[:, ATTN_BLOCK:, :]
        for c in range(N_KV_HEADS // 2):
            for off, a_ref, b_ref in ((OFF_K, ka_ref, kb_ref), (OFF_V, va_ref, vb_ref)):
                chunk = qkv[rows, off + c * LANES: off + (c + 1) * LANES]
                c_lo = jnp.where(lo, chunk, 0.0)
                c_hi = jnp.where(lo, 0.0, chunk)
                a_ref[2 * c, ATTN_BLOCK:, :] = c_lo.astype(BF16)
                b_ref[2 * c, ATTN_BLOCK:, :] = pltpu.roll(c_lo, half, 1).astype(BF16)
                b_ref[2 * c + 1, ATTN_BLOCK:, :] = c_hi.astype(BF16)
                a_ref[2 * c + 1, ATTN_BLOCK:, :] = pltpu.roll(c_hi, half, 1).astype(BF16)
        for h in range(N_KV_HEADS):
            q0 = h * GROUP * HEAD_DIM
            q2 = jnp.concatenate([qkv[rows, q0:q0 + LANES], qkv[rows, q0 + LANES:q0 + 2 * LANES]],
                                 axis=0).astype(BF16)
            probs = []
            for t, k_ref in enumerate((ka_ref, kb_ref)):
                s = _dot_nt(q2, k_ref[h]) * scale + bias_ref[h, t]
                if j == 0:
                    s = jnp.where(jnp.logical_and(i == 0, key_col < ATTN_BLOCK), NEG, s)
                probs.append(_sink_softmax(s, sink_ref[h, t]).astype(BF16))
            o = _dot(probs[0], va_ref[h]) + _dot(probs[1], vb_ref[h])
            att_ref[rows, q0:q0 + LANES] = o[0:ATTN_BLOCK].astype(BF16)
            att_ref[rows, q0 + LANES:q0 + 2 * LANES] = o[ATTN_BLOCK:].astype(BF16)

    kwin_ref[0] = qkv[tq - WINDOW:tq, OFF_K:OFF_V]
    vwin_ref[0] = qkv[tq - WINDOW:tq, OFF_V:OFF_B]

    attn_o = _dot(att_ref[...], w_ao_ref[...])

    bch = _dot(xb, w_in_ref[:, OFF_B:OFF_GA])
    u = bch[:, CONV_DIM:2 * CONV_DIM] * bch[:, 2 * CONV_DIM:3 * CONV_DIM]
    ubuf_ref[CONV_PAD:CONV_PAD + tq, :] = u
    cw = convw_ref[...]
    y = (cw[0:1] * ubuf_ref[CONV_PAD - 2:CONV_PAD - 2 + tq, :]
         + cw[1:2] * ubuf_ref[CONV_PAD - 1:CONV_PAD - 1 + tq, :]
         + cw[2:3] * u)
    conv_ref[0] = ubuf_ref[CONV_PAD + tq - (CONV_K - 1):CONV_PAD + tq, :]
    ubuf_ref[0:CONV_PAD, :] = ubuf_ref[tq:tq + CONV_PAD, :]
    y_conv = bch[:, 0:CONV_DIM] * y

    gab = _dot(xb, w_in_ref[:, OFF_GA:IN_DIM])
    x1_ref[0] = _merge_project(x, attn_o, y_conv, gab[:, 0:D_MODEL], gab[:, D_MODEL:], w_co_ref, w_o_ref,
                               lng_ref, lnb_ref, alpha)


def _const_spec(shape):
    nd = len(shape)
    return pl.BlockSpec(shape, lambda *_: (0,) * nd, pipeline_mode=pl.Buffered(1))


def _mixer_prompt(x, w_in, w_ao, w_co, w_o, conv_w, bias_ab, sink_ab, ln_g, ln_b, *, alpha, tq):
    b, s, d = x.shape
    kernel = functools.partial(_mixer_prompt_kernel, alpha=alpha)
    return pl.pallas_call(
        kernel,
        grid=(b, s // tq),
        in_specs=[
            pl.BlockSpec((1, tq, d), lambda bi, i: (bi, i, 0)),
            _const_spec(w_in.shape), _const_spec(w_ao.shape), _const_spec(w_co.shape), _const_spec(w_o.shape),
            _const_spec(conv_w.shape), _const_spec(bias_ab.shape), _const_spec(sink_ab.shape),
            _const_spec(ln_g.shape), _const_spec(ln_b.shape),
        ],
        out_specs=[
            pl.BlockSpec((1, tq, d), lambda bi, i: (bi, i, 0)),
            pl.BlockSpec((1, WINDOW, KV_DIM), lambda bi, i: (bi, 0, 0)),
            pl.BlockSpec((1, WINDOW, KV_DIM), lambda bi, i: (bi, 0, 0)),
            pl.BlockSpec((1, CONV_K - 1, CONV_DIM), lambda bi, i: (bi, 0, 0)),
        ],
        out_shape=[
            jax.ShapeDtypeStruct((b, s, d), F32),
            jax.ShapeDtypeStruct((b, WINDOW, KV_DIM), F32),
            jax.ShapeDtypeStruct((b, WINDOW, KV_DIM), F32),
            jax.ShapeDtypeStruct((b, CONV_K - 1, CONV_DIM), F32),
        ],
        scratch_shapes=[
            pltpu.VMEM((N_KV_HEADS, 2 * ATTN_BLOCK, LANES), BF16),
            pltpu.VMEM((N_KV_HEADS, 2 * ATTN_BLOCK, LANES), BF16),
            pltpu.VMEM((N_KV_HEADS, 2 * ATTN_BLOCK, LANES), BF16),
            pltpu.VMEM((N_KV_HEADS, 2 * ATTN_BLOCK, LANES), BF16),
            pltpu.VMEM((tq, Q_DIM), BF16),
            pltpu.VMEM((tq + CONV_PAD, CONV_DIM), F32),
        ],
        compiler_params=pltpu.CompilerParams(
            dimension_semantics=("arbitrary", "arbitrary"), vmem_limit_bytes=VMEM_LIMIT),
        name="mixer_prompt",
    )(x, w_in, w_ao, w_co, w_o, conv_w, bias_ab, sink_ab, ln_g, ln_b)


def _proj_kernel(x_ref, w_ref, o_ref):
    o_ref[...] = _dot(x_ref[...].astype(BF16), w_ref[...])


def _sample_proj(x, w_in, *, tn):
    m, d = x.shape
    n = w_in.shape[1]
    return pl.pallas_call(
        _proj_kernel,
        grid=(n // tn,),
        in_specs=[pl.BlockSpec((m, d), lambda j: (0, 0)), pl.BlockSpec((d, tn), lambda j: (0, j))],
        out_specs=pl.BlockSpec((m, tn), lambda j: (0, j)),
        out_shape=jax.ShapeDtypeStruct((m, n), F32),
        compiler_params=pltpu.CompilerParams(dimension_semantics=("arbitrary",), vmem_limit_bytes=VMEM_LIMIT),
        name="sample_proj",
    )(x, w_in)


def _sample_attn_kernel(q4_ref, knew_ref, vnew_ref, ck_ref, cv_ref, bias_ref, sink_ref, hmask_ref,
                        nk_ref, nv_ref, ag_ref):
    bt = ck_ref.shape[0]
    win = ck_ref.shape[1]
    scale = HEAD_DIM ** -0.5
    row = lax.broadcasted_iota(jnp.int32, (win, KV_DIM), 0)
    last = row == win - 1
    hmask = hmask_ref[...]
    for b in range(bt):
        kb = jnp.where(last, knew_ref[b:b + 1, :], pltpu.roll(ck_ref[b], win - 1, 0))
        vb = jnp.where(last, vnew_ref[b:b + 1, :], pltpu.roll(cv_ref[b], win - 1, 0))
        nk_ref[b] = kb
        nv_ref[b] = vb
        q4 = q4_ref[b]
        qm = (jnp.concatenate([q4] * N_KV_HEADS, axis=0) * hmask).astype(BF16)
        s = _dot_nt(qm, kb.astype(BF16)) * scale + bias_ref[...]
        p = _sink_softmax(s, sink_ref[...]).astype(BF16)
        o = _dot(p, vb.astype(BF16)) * hmask
        o4 = o[0:GROUP]
        for h in range(1, N_KV_HEADS):
            o4 = o4 + o[h * GROUP:(h + 1) * GROUP]
        ag_ref[b] = o4


def _sample_attn(q4, k_new, v_new, cache_k, cache_v, bias_s, sink_col, hmask, *, layer, bt):
    _, nb, win, kvd = cache_k.shape
    return pl.pallas_call(
        _sample_attn_kernel,
        grid=(nb // bt,),
        in_specs=[
            pl.BlockSpec((bt, GROUP, kvd), lambda i: (i, 0, 0)),
            pl.BlockSpec((bt, kvd), lambda i: (i, 0)),
            pl.BlockSpec((bt, kvd), lambda i: (i, 0)),
            pl.BlockSpec((None, bt, win, kvd), lambda i: (layer, i, 0, 0)),
            pl.BlockSpec((None, bt, win, kvd), lambda i: (layer, i, 0, 0)),
            pl.BlockSpec(bias_s.shape, lambda i: (0, 0)),
            pl.BlockSpec(sink_col.shape, lambda i: (0, 0)),
            pl.BlockSpec(hmask.shape, lambda i: (0, 0)),
        ],
        out_specs=[
            pl.BlockSpec((bt, win, kvd), lambda i: (i, 0, 0)),
            pl.BlockSpec((bt, win, kvd), lambda i: (i, 0, 0)),
            pl.BlockSpec((bt, GROUP, kvd), lambda i: (i, 0, 0)),
        ],
        out_shape=[
            jax.ShapeDtypeStruct((nb, win, kvd), F32),
            jax.ShapeDtypeStruct((nb, win, kvd), F32),
            jax.ShapeDtypeStruct((nb, GROUP, kvd), F32),
        ],
        compiler_params=pltpu.CompilerParams(dimension_semantics=("arbitrary",), vmem_limit_bytes=VMEM_LIMIT),
        name="sample_attn",
    )(q4, k_new, v_new, cache_k, cache_v, bias_s, sink_col, hmask)


def _sample_post_kernel(x_ref, att_ref, proj_ref, st_ref, convw_ref, w_ao_ref, w_co_ref, w_o_ref,
                        lng_ref, lnb_ref, x1_ref, u_ref, *, alpha):
    attn_o = _dot(att_ref[...].astype(BF16), w_ao_ref[...])
    u = proj_ref[:, OFF_C:OFF_H] * proj_ref[:, OFF_H:OFF_GA]
    cw = convw_ref[...]
    y = cw[0:1] * st_ref[0] + cw[1:2] * st_ref[1] + cw[2:3] * u
    u_ref[...] = u
    y_conv = proj_ref[:, OFF_B:OFF_C] * y
    x1_ref[...] = _merge_project(x_ref[...], attn_o, y_conv, proj_ref[:, OFF_GA:OFF_GB], proj_ref[:, OFF_GB:IN_DIM],
                                 w_co_ref, w_o_ref, lng_ref, lnb_ref, alpha)


def _sample_post(x, att, proj, state, conv_w, w_ao, w_co, w_o, ln_g, ln_b, *, alpha):
    m, d = x.shape
    kernel = functools.partial(_sample_post_kernel, alpha=alpha)
    return pl.pallas_call(
        kernel,
        out_shape=[jax.ShapeDtypeStruct((m, d), F32), jax.ShapeDtypeStruct((m, CONV_DIM), F32)],
        compiler_params=pltpu.CompilerParams(vmem_limit_bytes=VMEM_LIMIT),
        name="sample_post",
    )(x, att, proj, state, conv_w, w_ao, w_co, w_o, ln_g, ln_b)


def _first_max(cur, ids, axes, big):
    m = cur
    for ax in axes:
        m = jnp.max(m, axis=ax, keepdims=True)
    idx = jnp.where(cur == m, ids, big)
    for ax in axes:
        idx = jnp.min(idx, axis=ax, keepdims=True)
    return m, idx


def _router_kernel(x_ref, rwt_ref, rb_ref, eidx_ref, gate_ref):
    tm = x_ref.shape[0]
    logits_t = _dot_nt(rwt_ref[...], x_ref[...].astype(BF16))
    scores = jax.nn.sigmoid(logits_t)
    sel = scores + rb_ref[...]
    shape3 = (N_GROUPS, GROUP_SIZE, tm)
    sel3 = sel.reshape(shape3)
    scores3 = scores.reshape(shape3)
    member = lax.broadcasted_iota(jnp.int32, shape3, 1)
    m1, i1 = _first_max(sel3, member, (1,), GROUP_SIZE)
    m2 = jnp.max(jnp.where(member == i1, -jnp.inf, sel3), axis=1, keepdims=True)
    gscore = m1 + m2
    gid = lax.broadcasted_iota(jnp.int32, gscore.shape, 0)
    gsel = jnp.zeros(gscore.shape, jnp.bool_)
    for _ in range(TOPK_GROUPS):
        _, gi = _first_max(gscore, gid, (0,), N_GROUPS)
        hit = gid == gi
        gsel = jnp.logical_or(gsel, hit)
        gscore = jnp.where(hit, -jnp.inf, gscore)
    eid = lax.broadcasted_iota(jnp.int32, shape3, 0) * GROUP_SIZE + member
    cur = jnp.where(gsel, sel3, -jnp.inf)
    ids, ws = [], []
    for _ in range(TOP_K):
        _, ei = _first_max(cur, eid, (1, 0), N_EXPERTS)
        hit = eid == ei
        sc = jnp.sum(jnp.sum(jnp.where(hit, scores3, 0.0), axis=1, keepdims=True), axis=0, keepdims=True)
        ids.append(ei[0])
        ws.append(sc[0])
        cur = jnp.where(hit, -jnp.inf, cur)
    w = jnp.concatenate(ws, axis=0)
    tot = jnp.sum(w, axis=0, keepdims=True)
    eidx_ref[...] = jnp.concatenate(ids, axis=0)
    gate_ref[...] = w / tot * ROUTED_SCALE


def _router(x, rw_t, rb_col, *, tm):
    t, d = x.shape
    return pl.pallas_call(
        _router_kernel,
        grid=(t // tm,),
        in_specs=[
            pl.BlockSpec((tm, d), lambda i: (i, 0)),
            pl.BlockSpec(rw_t.shape, lambda i: (0, 0)),
            pl.BlockSpec(rb_col.shape, lambda i: (0, 0)),
        ],
        out_specs=[pl.BlockSpec((TOP_K, tm), lambda i: (0, i)), pl.BlockSpec((TOP_K, tm), lambda i: (0, i))],
        out_shape=[jax.ShapeDtypeStruct((TOP_K, t), jnp.int32), jax.ShapeDtypeStruct((TOP_K, t), F32)],
        compiler_params=pltpu.CompilerParams(dimension_semantics=("arbitrary",), vmem_limit_bytes=VMEM_LIMIT),
        name="router",
    )(x, rw_t, rb_col)


ROW_TILE = 256
TILE_PITCH = ROW_TILE + 1
SPARE_TOKENS = 8
PAIR_BITS = 16
FLAG_FIRST, FLAG_LAST, FLAG_NEW_EXPERT, FLAG_VALID = 1, 2, 4, 8
SCATTER_BATCH = 8
assert TOP_K == 8


def _moe_routed_kernel(ce_ref, flags_ref, idx_ref, gtab_ref, x_ref, wg_ref, wu_ref, wd_ref, sg_ref, su_ref,
                       sd_ref, lng_ref, lnb_ref, o_ref, wgu_ref, wdb_ref, gat_ref, res_ref, *,
                       alpha, chunk_tokens, sub_tokens):
    del ce_ref
    flags = flags_ref[pl.program_id(0)]
    nchunk = D_MODEL // LANES
    ff = wg_ref.shape[2]

    def slab_row(r):
        return pl.multiple_of((idx_ref[0, 0, r] >> 3) << 3, 8)

    @pl.when((flags & FLAG_FIRST) != 0)
    def _zero():
        o_ref[...] = jnp.zeros(o_ref.shape, F32)

    @pl.when((flags & FLAG_NEW_EXPERT) != 0)
    def _cast_weights():
        wgu_ref[:, 0:ff] = wg_ref[0].astype(BF16)
        wgu_ref[:, ff:2 * ff] = wu_ref[0].astype(BF16)
        wdb_ref[...] = wd_ref[0].astype(BF16)

    @pl.when((flags & FLAG_VALID) != 0)
    def _tile():
        for r in range(ROW_TILE):
            gat_ref[pl.ds(r, nchunk, stride=TILE_PITCH), :] = x_ref[pl.ds(slab_row(r), 8), :]
        lhs = jnp.concatenate([gat_ref[j * TILE_PITCH:j * TILE_PITCH + ROW_TILE, :] for j in range(nchunk)],
                              axis=1).astype(BF16)
        h = _dot(lhs, wgu_ref[...])
        hid = jax.nn.silu(h[:, 0:ff]) * h[:, ff:2 * ff]
        y = _dot(hid.astype(BF16), wdb_ref[...])
        for j in range(nchunk):
            res_ref[j * TILE_PITCH:j * TILE_PITCH + ROW_TILE, :] = y[:, j * LANES:(j + 1) * LANES]
        for r0 in range(0, ROW_TILE, SCATTER_BATCH):
            dst = [slab_row(r0 + u) for u in range(SCATTER_BATCH)]
            acc = [o_ref[pl.ds(dst[u], 8), :]
                   + gtab_ref[0, 0, idx_ref[0, 0, r0 + u]] * res_ref[pl.ds(r0 + u, nchunk, stride=TILE_PITCH), :]
                   for u in range(SCATTER_BATCH)]
            for u in range(SCATTER_BATCH):
                o_ref[pl.ds(dst[u], 8), :] = acc[u]

    @pl.when((flags & FLAG_LAST) != 0)
    def _finish():
        def body(s, carry):
            base = pl.multiple_of(s * (sub_tokens * 8), 8)

            def rows_2d(ref):
                return jnp.concatenate([ref[pl.ds(base + j, sub_tokens, stride=8), :] for j in range(nchunk)],
                                       axis=1)

            x2 = rows_2d(x_ref)
            xb = x2.astype(BF16)
            hs = jax.nn.silu(_dot(xb, sg_ref[...])) * _dot(xb, su_ref[...])
            ffn = rows_2d(o_ref) + _dot(hs.astype(BF16), sd_ref[...])
            res = _layer_norm(alpha * x2 + ffn, lng_ref[...], lnb_ref[...])
            for j in range(nchunk):
                o_ref[pl.ds(base + j, sub_tokens, stride=8), :] = res[:, j * LANES:(j + 1) * LANES]
            return carry

        lax.fori_loop(0, chunk_tokens // sub_tokens, body, 0)


def _moe_routed(tile_ce, tile_flags, pair_idx, gate_tab, x_tm, wg, wu, wd, sg, su, sd, ln_g, ln_b, *,
                layer, alpha, chunk_tokens, sub_tokens):
    n_chunks, chunk_rows, _ = x_tm.shape
    n_tiles = pair_idx.shape[0]
    _, ne, d, ff = wg.shape
    kernel = functools.partial(_moe_routed_kernel, alpha=alpha, chunk_tokens=chunk_tokens, sub_tokens=sub_tokens)

    def chunk_map(i, ce, fl):
        return (ce[i] // ne, 0, 0)

    def expert_map(i, ce, fl):
        return (layer, ce[i] % ne, 0, 0)

    def const2(i, ce, fl):
        return (0, 0)

    grid_spec = pltpu.PrefetchScalarGridSpec(
        num_scalar_prefetch=2,
        grid=(n_tiles,),
        in_specs=[
            pl.BlockSpec((1, 1, ROW_TILE), lambda i, ce, fl: (i, 0, 0), memory_space=pltpu.SMEM),
            pl.BlockSpec((1, 1, gate_tab.shape[2]), lambda i, ce, fl: (ce[i] // ne, 0, 0),
                         memory_space=pltpu.SMEM, pipeline_mode=pl.Buffered(1)),
            pl.BlockSpec((None, chunk_rows, LANES), chunk_map, pipeline_mode=pl.Buffered(1)),
            pl.BlockSpec((None, 1, d, ff), expert_map),
            pl.BlockSpec((None, 1, d, ff), expert_map),
            pl.BlockSpec((None, 1, ff, d), expert_map),
            pl.BlockSpec(sg.shape, const2, pipeline_mode=pl.Buffered(1)),
            pl.BlockSpec(su.shape, const2, pipeline_mode=pl.Buffered(1)),
            pl.BlockSpec(sd.shape, const2, pipeline_mode=pl.Buffered(1)),
            pl.BlockSpec(ln_g.shape, const2),
            pl.BlockSpec(ln_b.shape, const2),
        ],
        out_specs=pl.BlockSpec((None, chunk_rows, LANES), chunk_map, pipeline_mode=pl.Buffered(1)),
        scratch_shapes=[
            pltpu.VMEM((d, 2 * ff), BF16),
            pltpu.VMEM((ff, d), BF16),
            pltpu.VMEM((8 * TILE_PITCH, LANES), F32),
            pltpu.VMEM((8 * TILE_PITCH, LANES), F32),
        ],
    )
    return pl.pallas_call(
        kernel,
        grid_spec=grid_spec,
        out_shape=jax.ShapeDtypeStruct(x_tm.shape, F32),
        compiler_params=pltpu.CompilerParams(dimension_semantics=("arbitrary",), vmem_limit_bytes=VMEM_LIMIT),
        name="moe_routed",
    )(tile_ce, tile_flags, pair_idx, gate_tab, x_tm, wg, wu, wd, sg, su, sd, ln_g, ln_b)


MOE_CHUNKS = 4
MOE_SUB_MAX = 512


def _moe_tiling(n_prompt, n_sample):
    assert n_prompt % (MOE_CHUNKS * 8) == 0 and n_sample % (MOE_CHUNKS * 8) == 0
    chunk_tokens = (n_prompt + n_sample) // MOE_CHUNKS
    assert (chunk_tokens + 1) * TOP_K <= 1 << PAIR_BITS
    sub_tokens = max(s for s in range(8, MOE_SUB_MAX + 1, 8) if chunk_tokens % s == 0)
    return MOE_CHUNKS, chunk_tokens, sub_tokens


def _chunked(prompt, sample, n_chunks):
    return jnp.concatenate([prompt.reshape((n_chunks, -1) + prompt.shape[1:]),
                            sample.reshape((n_chunks, -1) + sample.shape[1:])], axis=1)


def _route_plan(eidx_p, eidx_s, gate_p, gate_s, *, chunk_tokens, n_chunks):
    n_seg = n_chunks * N_EXPERTS
    eidx = _chunked(eidx_p.T, eidx_s.T, n_chunks)
    gate = _chunked(gate_p.T, gate_s.T, n_chunks)
    seg = jnp.arange(n_chunks, dtype=jnp.int32)[:, None, None] * N_EXPERTS + eidx
    pair = jnp.arange(chunk_tokens * TOP_K, dtype=jnp.int32).reshape(1, chunk_tokens, TOP_K)
    keys_real = ((seg << PAIR_BITS) | pair).reshape(-1)
    counts = jnp.sum(eidx[..., None] == jnp.arange(N_EXPERTS, dtype=jnp.int32), axis=(1, 2),
                     dtype=jnp.int32).reshape(n_seg)
    n_pad = (-counts) % ROW_TILE
    slot = jnp.arange(ROW_TILE - 1, dtype=jnp.int32)
    int_max = jnp.iinfo(jnp.int32).max
    pad_pair = (1 << PAIR_BITS) - 1
    keys_pad = jnp.where(slot[None, :] < n_pad[:, None],
                         (jnp.arange(n_seg, dtype=jnp.int32)[:, None] << PAIR_BITS) | pad_pair, int_max)
    keys = jnp.concatenate([keys_real, keys_pad.reshape(-1)])
    assert keys.size % ROW_TILE == 0
    n_tiles = keys.size // ROW_TILE
    keys = jnp.sort(keys).reshape(n_tiles, ROW_TILE)
    head = keys[:, 0]
    valid = head != int_max
    tile_ce = jnp.minimum(head >> PAIR_BITS, n_seg - 1)
    tile_c = tile_ce // N_EXPERTS
    prev_ce = jnp.concatenate([jnp.full((1,), -N_EXPERTS, jnp.int32), tile_ce[:-1]])
    next_c = jnp.concatenate([tile_c[1:], jnp.full((1,), -1, jnp.int32)])
    next_valid = jnp.concatenate([valid[1:], jnp.zeros((1,), jnp.bool_)])
    first = valid & (tile_c != prev_ce // N_EXPERTS)
    last = valid & (~next_valid | (next_c != tile_c))
    new_expert = valid & (tile_ce != prev_ce)
    flags = (first * FLAG_FIRST + last * FLAG_LAST + new_expert * FLAG_NEW_EXPERT + valid * FLAG_VALID)
    pair_idx = jnp.minimum(keys & pad_pair, chunk_tokens * TOP_K)
    gate_tab = jnp.concatenate([gate.reshape(n_chunks, 1, -1), jnp.zeros((n_chunks, 1, TOP_K), F32)], axis=2)
    return tile_ce, flags.astype(jnp.int32), pair_idx.reshape(n_tiles, 1, ROW_TILE), gate_tab


def _t5_bucket(dist):
    n = jnp.maximum(dist, 0)
    max_exact = N_BUCKETS // 2
    large = max_exact + (jnp.log(jnp.maximum(n, 1).astype(F32) / max_exact)
                         / math.log(MAX_DISTANCE / max_exact) * (N_BUCKETS - max_exact)).astype(jnp.int32)
    large = jnp.minimum(large, N_BUCKETS - 1)
    return jnp.where(n < max_exact, n, large)


def _bias_lookup(rel_bias, bucket):
    onehot = (bucket[..., None] == jnp.arange(N_BUCKETS, dtype=bucket.dtype)).astype(F32)
    return jnp.einsum("...b,bh->h...", onehot, rel_bias.astype(F32), precision=lax.Precision.HIGHEST)


def _bias_tables(rel_bias, win):
    qi = jnp.arange(ATTN_BLOCK)[:, None]
    ki = jnp.arange(2 * ATTN_BLOCK)[None, :]
    dist = qi + ATTN_BLOCK - ki
    valid = (dist >= 0) & (dist < WINDOW)
    bias = _bias_lookup(rel_bias, _t5_bucket(dist))
    bias = jnp.where(valid[None], bias, NEG).reshape(N_KV_HEADS, GROUP, ATTN_BLOCK, 2 * ATTN_BLOCK)
    bias_ab = jnp.stack([jnp.concatenate([bias[:, t], bias[:, t + 2]], axis=1) for t in range(2)], axis=1)
    dist_s = (win - 1) - jnp.arange(win)
    bias_s = _bias_lookup(rel_bias, _t5_bucket(dist_s))
    return bias_ab, bias_s


def _sink_tables(sink):
    s = sink.astype(F32).reshape(N_KV_HEADS, GROUP)
    rows = [jnp.concatenate([jnp.broadcast_to(s[:, t, None], (N_KV_HEADS, ATTN_BLOCK)),
                             jnp.broadcast_to(s[:, t + 2, None], (N_KV_HEADS, ATTN_BLOCK))], axis=1)
            for t in range(2)]
    return jnp.stack(rows, axis=1)[..., None], sink.astype(F32)[:, None]


def kernel(x_prompt, x_sample, cache_k_win, cache_v_win, state_conv, rel_bias, w_in, attn_sink, conv_w,
           w_attn_out, w_conv_out, w_out, ln1_g, ln1_b, router_w, router_bias, exp_w_gate, exp_w_up,
           exp_w_down, shared_w_gate, shared_w_up, shared_w_down, ln2_g, ln2_b):
    depth = w_in.shape[0]
    alpha = (2 * depth) ** 0.25
    nb, seq, d = x_prompt.shape
    nd = x_sample.shape[0]
    win = cache_k_win.shape[2]
    assert x_sample.shape[1] == 1 and win == WINDOW and seq % 512 == 0

    n_prompt = nb * seq
    n_chunks, chunk_tokens, sub_tokens = _moe_tiling(n_prompt, nd)

    bias_ab, bias_s = _bias_tables(rel_bias, win)
    hmask = (jnp.arange(KV_DIM)[None, :] // HEAD_DIM == jnp.arange(N_HEADS)[:, None] // GROUP).astype(F32)

    yp = x_prompt
    ys = x_sample.reshape(nd, d)
    outs = [[] for _ in range(6)]
    for l in range(depth):
        w_in_b = w_in[l].astype(BF16)
        w_ao_b = w_attn_out[l].astype(BF16)
        w_co_b = w_conv_out[l].astype(BF16)
        w_o_b = w_out[l].astype(BF16)
        g1, b1 = ln1_g[l][None, :], ln1_b[l][None, :]
        g2, b2 = ln2_g[l][None, :], ln2_b[l][None, :]
        sink_ab, sink_col = _sink_tables(attn_sink[l])

        yp, kp, vp, cp = _mixer_prompt(yp, w_in_b, w_ao_b, w_co_b, w_o_b, conv_w[l], bias_ab, sink_ab, g1, b1,
                                       alpha=alpha, tq=512)

        proj = _sample_proj(ys, w_in_b, tn=IN_DIM // 4)
        q4 = proj[:, :Q_DIM].reshape(nd, N_KV_HEADS, GROUP, HEAD_DIM).transpose(0, 2, 1, 3).reshape(nd, GROUP, KV_DIM)
        ksn, vsn, ag = _sample_attn(q4, proj[:, OFF_K:OFF_V], proj[:, OFF_V:OFF_B],
                                    cache_k_win.reshape(depth, nd, win, KV_DIM),
                                    cache_v_win.reshape(depth, nd, win, KV_DIM),
                                    bias_s, sink_col, hmask, layer=l, bt=8)
        att = ag.reshape(nd, GROUP, N_KV_HEADS, HEAD_DIM).transpose(0, 2, 1, 3).reshape(nd, Q_DIM)
        state_t = jnp.swapaxes(state_conv[l], 0, 1)
        ys, us = _sample_post(ys, att, proj, state_t, conv_w[l], w_ao_b, w_co_b, w_o_b, g1, b1, alpha=alpha)

        outs[0].append(kp.reshape(nb, WINDOW, N_KV_HEADS, HEAD_DIM))
        outs[1].append(vp.reshape(nb, WINDOW, N_KV_HEADS, HEAD_DIM))
        outs[2].append(cp)
        outs[3].append(ksn.reshape(nd, win, N_KV_HEADS, HEAD_DIM))
        outs[4].append(vsn.reshape(nd, win, N_KV_HEADS, HEAD_DIM))
        outs[5].append(jnp.concatenate([state_conv[l][:, 1:], us[:, None, :]], axis=1))

        rw_t = router_w[l].T.astype(BF16)
        rb_col = router_bias[l].astype(F32)[:, None]
        sg, su, sd = (shared_w_gate[l].astype(BF16), shared_w_up[l].astype(BF16), shared_w_down[l].astype(BF16))
        xp = yp.reshape(n_prompt, d)
        ep, wp = _router(xp, rw_t, rb_col, tm=512)
        es, ws = _router(ys, rw_t, rb_col, tm=nd)
        plan = _route_plan(ep, es, wp, ws, chunk_tokens=chunk_tokens, n_chunks=n_chunks)
        xt = jnp.concatenate([_chunked(xp, ys, n_chunks), jnp.zeros((n_chunks, SPARE_TOKENS, d), F32)], axis=1)
        out = _moe_routed(*plan, xt.reshape(n_chunks, -1, LANES), exp_w_gate, exp_w_up, exp_w_down, sg, su, sd,
                          g2, b2, layer=l, alpha=alpha, chunk_tokens=chunk_tokens, sub_tokens=sub_tokens)
        out = out.reshape(n_chunks, chunk_tokens + SPARE_TOKENS, d)
        yp = out[:, :n_prompt // n_chunks].reshape(nb, seq, d)
        ys = out[:, n_prompt // n_chunks:chunk_tokens].reshape(nd, d)

    return (yp, ys.reshape(nd, 1, d)) + tuple(jnp.stack(o) for o in outs)
```

```python
import functools
import math

import jax
import jax.numpy as jnp
from jax import lax
from jax.experimental import pallas as pl
from jax.experimental.pallas import tpu as pltpu

D_MODEL = 1024
N_HEADS = 16
N_KV_HEADS = 4
HEAD_DIM = 64
GROUP = N_HEADS // N_KV_HEADS
WINDOW = 128
ATTN_BLOCK = 128
N_BUCKETS = 32
MAX_DISTANCE = 128
CONV_DIM = 1024
CONV_K = 3
N_EXPERTS = 64
TOP_K = 8
N_GROUPS = 8
TOPK_GROUPS = 4
GROUP_SIZE = N_EXPERTS // N_GROUPS
EXPERT_FF = 256
ROUTED_SCALE = 2.5
LN_EPS = 1e-5
NEG = -1e30

Q_DIM = N_HEADS * HEAD_DIM
KV_DIM = N_KV_HEADS * HEAD_DIM
OFF_K = Q_DIM
OFF_V = OFF_K + KV_DIM
OFF_B = OFF_V + KV_DIM
OFF_C = OFF_B + CONV_DIM
OFF_H = OFF_C + CONV_DIM
OFF_GA = OFF_H + CONV_DIM
OFF_GB = OFF_GA + D_MODEL
IN_DIM = OFF_GB + D_MODEL

LANES = 128
CONV_PAD = 8
VMEM_LIMIT = 60 * 1024 * 1024

BF16 = jnp.bfloat16
F32 = jnp.float32


def _dot(a, b):
    return jnp.dot(a, b, preferred_element_type=F32)


def _dot_nt(a, b):
    return lax.dot_general(a, b, (((1,), (1,)), ((), ())), preferred_element_type=F32)


def _layer_norm(z, g, b):
    mu = jnp.mean(z, axis=-1, keepdims=True)
    d = z - mu
    var = jnp.mean(d * d, axis=-1, keepdims=True)
    return d * lax.rsqrt(var + LN_EPS) * g + b


def _sink_softmax(s, sink_col):
    m = jnp.maximum(jnp.max(s, axis=-1, keepdims=True), sink_col)
    e = jnp.exp(s - m)
    den = jnp.sum(e, axis=-1, keepdims=True) + jnp.exp(sink_col - m)
    return e * (1.0 / den)


def _merge_project(x, attn_o, y_conv, g_a, g_b, w_co_ref, w_o_ref, lng_ref, lnb_ref, alpha):
    merged = jax.nn.sigmoid(g_a) * attn_o + jax.nn.sigmoid(g_b) * _dot(y_conv.astype(BF16), w_co_ref[...])
    out = _dot(merged.astype(BF16), w_o_ref[...])
    return _layer_norm(alpha * x + out, lng_ref[...], lnb_ref[...])


def _mixer_prompt_kernel(x_ref, w_in_ref, w_ao_ref, w_co_ref, w_o_ref, convw_ref, bias_ref, sink_ref,
                         lng_ref, lnb_ref,
                         x1_ref, kwin_ref, vwin_ref, conv_ref,
                         ka_ref, kb_ref, va_ref, vb_ref, att_ref, ubuf_ref, *, alpha):
    i = pl.program_id(1)
    tq = x_ref.shape[1]
    nblk = tq // ATTN_BLOCK
    half = LANES // 2
    scale = HEAD_DIM ** -0.5

    @pl.when(i == 0)
    def _init():
        for ref in (ka_ref, kb_ref, va_ref, vb_ref):
            ref[:, ATTN_BLOCK:, :] = jnp.zeros((N_KV_HEADS, ATTN_BLOCK, LANES), BF16)
        ubuf_ref[0:CONV_PAD, :] = jnp.zeros((CONV_PAD, CONV_DIM), F32)

    x = x_ref[0]
    xb = x.astype(BF16)
    qkv = _dot(xb, w_in_ref[:, 0:OFF_B])

    lane = lax.broadcasted_iota(jnp.int32, (ATTN_BLOCK, LANES), 1)
    lo = lane < half
    key_col = lax.broadcasted_iota(jnp.int32, (2 * ATTN_BLOCK, 2 * ATTN_BLOCK), 1)

    for j in range(nblk):
        r0 = j * ATTN_BLOCK
        rows = slice(r0, r0 + ATTN_BLOCK)
        for ref in (ka_ref, kb_ref, va_ref, vb_ref):
            ref[:, 0:ATTN_BLOCK, :] = ref[:, ATTN_BLOCK:, :]
        for c in range(N_KV_HEADS // 2):
            for off, a_ref, b_ref in ((OFF_K, ka_ref, kb_ref), (OFF_V, va_ref, vb_ref)):
                chunk = qkv[rows, off + c * LANES: off + (c + 1) * LANES]
                c_lo = jnp.where(lo, chunk, 0.0)
                c_hi = jnp.where(lo, 0.0, chunk)
                a_ref[2 * c, ATTN_BLOCK:, :] = c_lo.astype(BF16)
                b_ref[2 * c, ATTN_BLOCK:, :] = pltpu.roll(c_lo, half, 1).astype(BF16)
                b_ref[2 * c + 1, ATTN_BLOCK:, :] = c_hi.astype(BF16)
                a_ref[2 * c + 1, ATTN_BLOCK:, :] = pltpu.roll(c_hi, half, 1).astype(BF16)
        for h in range(N_KV_HEADS):
            q0 = h * GROUP * HEAD_DIM
            q2 = jnp.concatenate([qkv[rows, q0:q0 + LANES], qkv[rows, q0 + LANES:q0 + 2 * LANES]],
                                 axis=0).astype(BF16)
            probs = []
            for t, k_ref in enumerate((ka_ref, kb_ref)):
                s = _dot_nt(q2, k_ref[h]) * scale + bias_ref[h, t]
                if j == 0:
                    s = jnp.where(jnp.logical_and(i == 0, key_col < ATTN_BLOCK), NEG, s)
                probs.append(_sink_softmax(s, sink_ref[h, t]).astype(BF16))
            o = _dot(probs[0], va_ref[h]) + _dot(probs[1], vb_ref[h])
            att_ref[rows, q0:q0 + LANES] = o[0:ATTN_BLOCK].astype(BF16)
            att_ref[rows, q0 + LANES:q0 + 2 * LANES] = o[ATTN_BLOCK:].astype(BF16)

    kwin_ref[0] = qkv[tq - WINDOW:tq, OFF_K:OFF_V]
    vwin_ref[0] = qkv[tq - WINDOW:tq, OFF_V:OFF_B]

    attn_o = _dot(att_ref[...], w_ao_ref[...])

    bch = _dot(xb, w_in_ref[:, OFF_B:OFF_GA])
    u = bch[:, CONV_DIM:2 * CONV_DIM] * bch[:, 2 * CONV_DIM:3 * CONV_DIM]
    ubuf_ref[CONV_PAD:CONV_PAD + tq, :] = u
    cw = convw_ref[...]
    y = (cw[0:1] * ubuf_ref[CONV_PAD - 2:CONV_PAD - 2 + tq, :]
         + cw[1:2] * ubuf_ref[CONV_PAD - 1:CONV_PAD - 1 + tq, :]
         + cw[2:3] * u)
    conv_ref[0] = ubuf_ref[CONV_PAD + tq - (CONV_K - 1):CONV_PAD + tq, :]
    ubuf_ref[0:CONV_PAD, :] = ubuf_ref[tq:tq + CONV_PAD, :]
    y_conv = bch[:, 0:CONV_DIM] * y

    gab = _dot(xb, w_in_ref[:, OFF_GA:IN_DIM])
    x1_ref[0] = _merge_project(x, attn_o, y_conv, gab[:, 0:D_MODEL], gab[:, D_MODEL:], w_co_ref, w_o_ref,
                               lng_ref, lnb_ref, alpha)


def _const_spec(shape):
    nd = len(shape)
    return pl.BlockSpec(shape, lambda *_: (0,) * nd, pipeline_mode=pl.Buffered(1))


def _mixer_prompt(x, w_in, w_ao, w_co, w_o, conv_w, bias_ab, sink_ab, ln_g, ln_b, *, alpha, tq):
    b, s, d = x.shape
    kernel = functools.partial(_mixer_prompt_kernel, alpha=alpha)
    return pl.pallas_call(
        kernel,
        grid=(b, s // tq),
        in_specs=[
            pl.BlockSpec((1, tq, d), lambda bi, i: (bi, i, 0)),
            _const_spec(w_in.shape), _const_spec(w_ao.shape), _const_spec(w_co.shape), _const_spec(w_o.shape),
            _const_spec(conv_w.shape), _const_spec(bias_ab.shape), _const_spec(sink_ab.shape),
            _const_spec(ln_g.shape), _const_spec(ln_b.shape),
        ],
        out_specs=[
            pl.BlockSpec((1, tq, d), lambda bi, i: (bi, i, 0)),
            pl.BlockSpec((1, WINDOW, KV_DIM), lambda bi, i: (bi, 0, 0)),
            pl.BlockSpec((1, WINDOW, KV_DIM), lambda bi, i: (bi, 0, 0)),
            pl.BlockSpec((1, CONV_K - 1, CONV_DIM), lambda bi, i: (bi, 0, 0)),
        ],
        out_shape=[
            jax.ShapeDtypeStruct((b, s, d), F32),
            jax.ShapeDtypeStruct((b, WINDOW, KV_DIM), F32),
            jax.ShapeDtypeStruct((b, WINDOW, KV_DIM), F32),
            jax.ShapeDtypeStruct((b, CONV_K - 1, CONV_DIM), F32),
        ],
        scratch_shapes=[
            pltpu.VMEM((N_KV_HEADS, 2 * ATTN_BLOCK, LANES), BF16),
            pltpu.VMEM((N_KV_HEADS, 2 * ATTN_BLOCK, LANES), BF16),
            pltpu.VMEM((N_KV_HEADS, 2 * ATTN_BLOCK, LANES), BF16),
            pltpu.VMEM((N_KV_HEADS, 2 * ATTN_BLOCK, LANES), BF16),
            pltpu.VMEM((tq, Q_DIM), BF16),
            pltpu.VMEM((tq + CONV_PAD, CONV_DIM), F32),
        ],
        compiler_params=pltpu.CompilerParams(
            dimension_semantics=("arbitrary", "arbitrary"), vmem_limit_bytes=VMEM_LIMIT),
        name="mixer_prompt",
    )(x, w_in, w_ao, w_co, w_o, conv_w, bias_ab, sink_ab, ln_g, ln_b)


def _proj_kernel(x_ref, w_ref, o_ref):
    o_ref[...] = _dot(x_ref[...].astype(BF16), w_ref[...])


def _sample_proj(x, w_in, *, tn):
    m, d = x.shape
    n = w_in.shape[1]
    return pl.pallas_call(
        _proj_kernel,
        grid=(n // tn,),
        in_specs=[pl.BlockSpec((m, d), lambda j: (0, 0)), pl.BlockSpec((d, tn), lambda j: (0, j))],
        out_specs=pl.BlockSpec((m, tn), lambda j: (0, j)),
        out_shape=jax.ShapeDtypeStruct((m, n), F32),
        compiler_params=pltpu.CompilerParams(dimension_semantics=("arbitrary",), vmem_limit_bytes=VMEM_LIMIT),
        name="sample_proj",
    )(x, w_in)


def _sample_attn_kernel(q4_ref, knew_ref, vnew_ref, ck_ref, cv_ref, bias_ref, sink_ref, hmask_ref,
                        nk_ref, nv_ref, ag_ref):
    bt = ck_ref.shape[0]
    win = ck_ref.shape[1]
    scale = HEAD_DIM ** -0.5
    row = lax.broadcasted_iota(jnp.int32, (win, KV_DIM), 0)
    last = row == win - 1
    hmask = hmask_ref[...]
    for b in range(bt):
        kb = jnp.where(last, knew_ref[b:b + 1, :], pltpu.roll(ck_ref[b], win - 1, 0))
        vb = jnp.where(last, vnew_ref[b:b + 1, :], pltpu.roll(cv_ref[b], win - 1, 0))
        nk_ref[b] = kb
        nv_ref[b] = vb
        q4 = q4_ref[b]
        qm = (jnp.concatenate([q4] * N_KV_HEADS, axis=0) * hmask).astype(BF16)
        s = _dot_nt(qm, kb.astype(BF16)) * scale + bias_ref[...]
        p = _sink_softmax(s, sink_ref[...]).astype(BF16)
        o = _dot(p, vb.astype(BF16)) * hmask
        o4 = o[0:GROUP]
        for h in range(1, N_KV_HEADS):
            o4 = o4 + o[h * GROUP:(h + 1) * GROUP]
        ag_ref[b] = o4


def _sample_attn(q4, k_new, v_new, cache_k, cache_v, bias_s, sink_col, hmask, *, layer, bt):
    _, nb, win, kvd = cache_k.shape
    return pl.pallas_call(
        _sample_attn_kernel,
        grid=(nb // bt,),
        in_specs=[
            pl.BlockSpec((bt, GROUP, kvd), lambda i: (i, 0, 0)),
            pl.BlockSpec((bt, kvd), lambda i: (i, 0)),
            pl.BlockSpec((bt, kvd), lambda i: (i, 0)),
            pl.BlockSpec((None, bt, win, kvd), lambda i: (layer, i, 0, 0)),
            pl.BlockSpec((None, bt, win, kvd), lambda i: (layer, i, 0, 0)),
            pl.BlockSpec(bias_s.shape, lambda i: (0, 0)),
            pl.BlockSpec(sink_col.shape, lambda i: (0, 0)),
            pl.BlockSpec(hmask.shape, lambda i: (0, 0)),
        ],
        out_specs=[
            pl.BlockSpec((bt, win, kvd), lambda i: (i, 0, 0)),
            pl.BlockSpec((bt, win, kvd), lambda i: (i, 0, 0)),
            pl.BlockSpec((bt, GROUP, kvd), lambda i: (i, 0, 0)),
        ],
        out_shape=[
            jax.ShapeDtypeStruct((nb, win, kvd), F32),
            jax.ShapeDtypeStruct((nb, win, kvd), F32),
            jax.ShapeDtypeStruct((nb, GROUP, kvd), F32),
        ],
        compiler_params=pltpu.CompilerParams(dimension_semantics=("arbitrary",), vmem_limit_bytes=VMEM_LIMIT),
        name="sample_attn",
    )(q4, k_new, v_new, cache_k, cache_v, bias_s, sink_col, hmask)


def _sample_post_kernel(x_ref, att_ref, proj_ref, st_ref, convw_ref, w_ao_ref, w_co_ref, w_o_ref,
                        lng_ref, lnb_ref, x1_ref, u_ref, *, alpha):
    attn_o = _dot(att_ref[...].astype(BF16), w_ao_ref[...])
    u = proj_ref[:, OFF_C:OFF_H] * proj_ref[:, OFF_H:OFF_GA]
    cw = convw_ref[...]
    y = cw[0:1] * st_ref[0] + cw[1:2] * st_ref[1] + cw[2:3] * u
    u_ref[...] = u
    y_conv = proj_ref[:, OFF_B:OFF_C] * y
    x1_ref[...] = _merge_project(x_ref[...], attn_o, y_conv, proj_ref[:, OFF_GA:OFF_GB], proj_ref[:, OFF_GB:IN_DIM],
                                 w_co_ref, w_o_ref, lng_ref, lnb_ref, alpha)


def _sample_post(x, att, proj, state, conv_w, w_ao, w_co, w_o, ln_g, ln_b, *, alpha):
    m, d = x.shape
    kernel = functools.partial(_sample_post_kernel, alpha=alpha)
    return pl.pallas_call(
        kernel,
        out_shape=[jax.ShapeDtypeStruct((m, d), F32), jax.ShapeDtypeStruct((m, CONV_DIM), F32)],
        compiler_params=pltpu.CompilerParams(vmem_limit_bytes=VMEM_LIMIT),
        name="sample_post",
    )(x, att, proj, state, conv_w, w_ao, w_co, w_o, ln_g, ln_b)


def _first_max(cur, ids, axes, big):
    m = cur
    for ax in axes:
        m = jnp.max(m, axis=ax, keepdims=True)
    idx = jnp.where(cur == m, ids, big)
    for ax in axes:
        idx = jnp.min(idx, axis=ax, keepdims=True)
    return m, idx


def _router_kernel(x_ref, rwt_ref, rb_ref, eidx_ref, gate_ref):
    tm = x_ref.shape[0]
    logits_t = _dot_nt(rwt_ref[...], x_ref[...].astype(BF16))
    scores = jax.nn.sigmoid(logits_t)
    sel = scores + rb_ref[...]
    shape3 = (N_GROUPS, GROUP_SIZE, tm)
    sel3 = sel.reshape(shape3)
    scores3 = scores.reshape(shape3)
    member = lax.broadcasted_iota(jnp.int32, shape3, 1)
    m1, i1 = _first_max(sel3, member, (1,), GROUP_SIZE)
    m2 = jnp.max(jnp.where(member == i1, -jnp.inf, sel3), axis=1, keepdims=True)
    gscore = m1 + m2
    gid = lax.broadcasted_iota(jnp.int32, gscore.shape, 0)
    gsel = jnp.zeros(gscore.shape, jnp.bool_)
    for _ in range(TOPK_GROUPS):
        _, gi = _first_max(gscore, gid, (0,), N_GROUPS)
        hit = gid == gi
        gsel = jnp.logical_or(gsel, hit)
        gscore = jnp.where(hit, -jnp.inf, gscore)
    eid = lax.broadcasted_iota(jnp.int32, shape3, 0) * GROUP_SIZE + member
    cur = jnp.where(gsel, sel3, -jnp.inf)
    ids, ws = [], []
    for _ in range(TOP_K):
        _, ei = _first_max(cur, eid, (1, 0), N_EXPERTS)
        hit = eid == ei
        sc = jnp.sum(jnp.sum(jnp.where(hit, scores3, 0.0), axis=1, keepdims=True), axis=0, keepdims=True)
        ids.append(ei[0])
        ws.append(sc[0])
        cur = jnp.where(hit, -jnp.inf, cur)
    w = jnp.concatenate(ws, axis=0)
    tot = jnp.sum(w, axis=0, keepdims=True)
    eidx_ref[...] = jnp.concatenate(ids, axis=0)
    gate_ref[...] = w / tot * ROUTED_SCALE


def _router(x, rw_t, rb_col, *, tm):
    t, d = x.shape
    return pl.pallas_call(
        _router_kernel,
        grid=(t // tm,),
        in_specs=[
            pl.BlockSpec((tm, d), lambda i: (i, 0)),
            pl.BlockSpec(rw_t.shape, lambda i: (0, 0)),
            pl.BlockSpec(rb_col.shape, lambda i: (0, 0)),
        ],
        out_specs=[pl.BlockSpec((TOP_K, tm), lambda i: (0, i)), pl.BlockSpec((TOP_K, tm), lambda i: (0, i))],
        out_shape=[jax.ShapeDtypeStruct((TOP_K, t), jnp.int32), jax.ShapeDtypeStruct((TOP_K, t), F32)],
        compiler_params=pltpu.CompilerParams(dimension_semantics=("arbitrary",), vmem_limit_bytes=VMEM_LIMIT),
        name="router",
    )(x, rw_t, rb_col)


ROW_TILE = 256
TILE_PITCH = ROW_TILE + 1
SPARE_TOKENS = 8
PAIR_BITS = 16
FLAG_FIRST, FLAG_LAST, FLAG_NEW_EXPERT, FLAG_VALID = 1, 2, 4, 8
SCATTER_BATCH = 8


def _moe_routed_kernel(ce_ref, flags_ref, idx_prev_ref, idx_ref, idx_next_ref, gate_ref, x_ref, wg_ref, wu_ref,
                       wd_ref, sg_ref, su_ref, sd_ref, lng_ref, lnb_ref, o_ref, wgu_ref, wdb_ref,
                       gat0_ref, gat1_ref, res0_ref, res1_ref, *, alpha, chunk_tokens, sub_tokens):
    del ce_ref
    step = pl.program_id(0)
    flags = flags_ref[step]
    odd = (step & 1) == 1
    nchunk = D_MODEL // LANES
    ff = wg_ref.shape[2]

    def slab_row(ref, r):
        return pl.multiple_of(ref[0, 0, r], 8)

    def gather_row(ref, gat_ref, r):
        gat_ref[pl.ds(r, nchunk, stride=TILE_PITCH), :] = x_ref[pl.ds(slab_row(ref, r), 8), :]

    def scatter_rows(ref, res_ref, rows):
        dst = [slab_row(ref, r) for r in rows]
        acc = [o_ref[pl.ds(d, 8), :] + res_ref[pl.ds(r, nchunk, stride=TILE_PITCH), :] for d, r in zip(dst, rows)]
        for d, a in zip(dst, acc):
            o_ref[pl.ds(d, 8), :] = a

    def expert_mlp(gat_ref, res_ref):
        lhs = jnp.concatenate([gat_ref[j * TILE_PITCH:j * TILE_PITCH + ROW_TILE, :] for j in range(nchunk)],
                              axis=1).astype(BF16)
        h = _dot(lhs, wgu_ref[...])
        gate_col = jnp.broadcast_to(gate_ref[0], (LANES, ROW_TILE)).T
        hid = jax.nn.silu(h[:, 0:ff]) * h[:, ff:2 * ff] * jnp.concatenate([gate_col] * (ff // LANES), axis=1)
        y = _dot(hid.astype(BF16), wdb_ref[...])
        for j in range(nchunk):
            res_ref[j * TILE_PITCH:j * TILE_PITCH + ROW_TILE, :] = y[:, j * LANES:(j + 1) * LANES]

    def by_parity(fn):
        @pl.when(jnp.logical_not(odd))
        def _even():
            fn(gat0_ref, gat1_ref, res0_ref, res1_ref)

        @pl.when(odd)
        def _odd():
            fn(gat1_ref, gat0_ref, res1_ref, res0_ref)

    @pl.when((flags & FLAG_FIRST) != 0)
    def _start_chunk():
        o_ref[...] = jnp.zeros(o_ref.shape, F32)

        def start(gat_cur, gat_other, res_cur, res_other):
            res_other[...] = jnp.zeros(res_other.shape, F32)

            def body(r, carry):
                gather_row(idx_ref, gat_cur, r)
                return carry

            lax.fori_loop(0, ROW_TILE, body, 0)

        by_parity(start)

    @pl.when((flags & FLAG_NEW_EXPERT) != 0)
    def _cast_weights():
        wgu_ref[:, 0:ff] = wg_ref[0].astype(BF16)
        wgu_ref[:, ff:2 * ff] = wu_ref[0].astype(BF16)
        wdb_ref[...] = wd_ref[0].astype(BF16)

    @pl.when((flags & FLAG_VALID) != 0)
    def _tile():
        def main(gat_cur, gat_other, res_cur, res_other):
            for r0 in range(0, ROW_TILE, SCATTER_BATCH):
                scatter_rows(idx_prev_ref, res_other, range(r0, r0 + SCATTER_BATCH))
            expert_mlp(gat_cur, res_cur)
            for r in range(ROW_TILE):
                gather_row(idx_next_ref, gat_other, r)

        by_parity(main)

    @pl.when((flags & FLAG_LAST) != 0)
    def _finish():
        def flush(gat_cur, gat_other, res_cur, res_other):
            def body(r, carry):
                scatter_rows(idx_ref, res_cur, [r])
                return carry

            lax.fori_loop(0, ROW_TILE, body, 0)

        by_parity(flush)

        def body(s, carry):
            base = pl.multiple_of(s * (sub_tokens * 8), 8)

            def rows_2d(ref):
                return jnp.concatenate([ref[pl.ds(base + j, sub_tokens, stride=8), :] for j in range(nchunk)],
                                       axis=1)

            x2 = rows_2d(x_ref)
            xb = x2.astype(BF16)
            hs = jax.nn.silu(_dot(xb, sg_ref[...])) * _dot(xb, su_ref[...])
            ffn = rows_2d(o_ref) + _dot(hs.astype(BF16), sd_ref[...])
            res = _layer_norm(alpha * x2 + ffn, lng_ref[...], lnb_ref[...])
            for j in range(nchunk):
                o_ref[pl.ds(base + j, sub_tokens, stride=8), :] = res[:, j * LANES:(j + 1) * LANES]
            return carry

        lax.fori_loop(0, chunk_tokens // sub_tokens, body, 0)


def _moe_routed(tile_ce, tile_flags, row_idx, gates, x_tm, wg, wu, wd, sg, su, sd, ln_g, ln_b, *,
                layer, alpha, chunk_tokens, sub_tokens):
    n_chunks, chunk_rows, _ = x_tm.shape
    n_tiles = row_idx.shape[0]
    _, ne, d, ff = wg.shape

    def idx_spec(shift):
        return pl.BlockSpec((1, 1, ROW_TILE),
                            lambda i, ce, fl: (jnp.clip(i + shift, 0, n_tiles - 1), 0, 0),
                            memory_space=pltpu.SMEM)
    kernel = functools.partial(_moe_routed_kernel, alpha=alpha, chunk_tokens=chunk_tokens, sub_tokens=sub_tokens)

    def chunk_map(i, ce, fl):
        return (ce[i] // ne, 0, 0)

    def expert_map(i, ce, fl):
        return (layer, ce[i] % ne, 0, 0)

    def const2(i, ce, fl):
        return (0, 0)

    grid_spec = pltpu.PrefetchScalarGridSpec(
        num_scalar_prefetch=2,
        grid=(n_tiles,),
        in_specs=[
            idx_spec(-1), idx_spec(0), idx_spec(1),
            pl.BlockSpec((1, 1, ROW_TILE), lambda i, ce, fl: (i, 0, 0)),
            pl.BlockSpec((None, chunk_rows, LANES), chunk_map, pipeline_mode=pl.Buffered(1)),
            pl.BlockSpec((None, 1, d, ff), expert_map),
            pl.BlockSpec((None, 1, d, ff), expert_map),
            pl.BlockSpec((None, 1, ff, d), expert_map),
            pl.BlockSpec(sg.shape, const2, pipeline_mode=pl.Buffered(1)),
            pl.BlockSpec(su.shape, const2, pipeline_mode=pl.Buffered(1)),
            pl.BlockSpec(sd.shape, const2, pipeline_mode=pl.Buffered(1)),
            pl.BlockSpec(ln_g.shape, const2),
            pl.BlockSpec(ln_b.shape, const2),
        ],
        out_specs=pl.BlockSpec((None, chunk_rows, LANES), chunk_map, pipeline_mode=pl.Buffered(1)),
        scratch_shapes=[
            pltpu.VMEM((d, 2 * ff), BF16),
            pltpu.VMEM((ff, d), BF16),
            pltpu.VMEM((8 * TILE_PITCH, LANES), F32),
            pltpu.VMEM((8 * TILE_PITCH, LANES), F32),
            pltpu.VMEM((8 * TILE_PITCH, LANES), F32),
            pltpu.VMEM((8 * TILE_PITCH, LANES), F32),
        ],
    )
    return pl.pallas_call(
        kernel,
        grid_spec=grid_spec,
        out_shape=jax.ShapeDtypeStruct(x_tm.shape, F32),
        compiler_params=pltpu.CompilerParams(dimension_semantics=("arbitrary",), vmem_limit_bytes=VMEM_LIMIT),
        name="moe_routed",
    )(tile_ce, tile_flags, row_idx, row_idx, row_idx, gates, x_tm, wg, wu, wd, sg, su, sd, ln_g, ln_b)


MOE_CHUNKS = 4
MOE_SUB_MAX = 512


def _moe_tiling(n_prompt, n_sample):
    assert n_prompt % (MOE_CHUNKS * 8) == 0 and n_sample % (MOE_CHUNKS * 8) == 0
    chunk_tokens = (n_prompt + n_sample) // MOE_CHUNKS
    assert (chunk_tokens + 1) * TOP_K <= 1 << PAIR_BITS
    sub_tokens = max(s for s in range(8, MOE_SUB_MAX + 1, 8) if chunk_tokens % s == 0)
    return MOE_CHUNKS, chunk_tokens, sub_tokens


def _chunked(prompt, sample, n_chunks):
    return jnp.concatenate([prompt.reshape((n_chunks, -1) + prompt.shape[1:]),
                            sample.reshape((n_chunks, -1) + sample.shape[1:])], axis=1)


def _route_plan(eidx_p, eidx_s, gate_p, gate_s, *, chunk_tokens, n_chunks):
    n_seg = n_chunks * N_EXPERTS
    eidx = _chunked(eidx_p.T, eidx_s.T, n_chunks)
    gate = _chunked(gate_p.T, gate_s.T, n_chunks)
    seg = jnp.arange(n_chunks, dtype=jnp.int32)[:, None, None] * N_EXPERTS + eidx
    pair = jnp.arange(chunk_tokens * TOP_K, dtype=jnp.int32).reshape(1, chunk_tokens, TOP_K)
    keys_real = ((seg << PAIR_BITS) | pair).reshape(-1)
    counts = jnp.sum(eidx[..., None] == jnp.arange(N_EXPERTS, dtype=jnp.int32), axis=(1, 2),
                     dtype=jnp.int32).reshape(n_seg)
    n_pad = (-counts) % ROW_TILE
    slot = jnp.arange(ROW_TILE - 1, dtype=jnp.int32)
    int_max = jnp.iinfo(jnp.int32).max
    pad_pair = (1 << PAIR_BITS) - 1
    keys_pad = jnp.where(slot[None, :] < n_pad[:, None],
                         (jnp.arange(n_seg, dtype=jnp.int32)[:, None] << PAIR_BITS) | pad_pair, int_max)
    keys = jnp.concatenate([keys_real, keys_pad.reshape(-1)])
    vals = jnp.concatenate([gate.reshape(-1), jnp.zeros((keys_pad.size,), F32)])
    assert keys.size % ROW_TILE == 0
    n_tiles = keys.size // ROW_TILE
    keys, vals = lax.sort((keys, vals), num_keys=1)
    keys = keys.reshape(n_tiles, ROW_TILE)
    head = keys[:, 0]
    valid = head != int_max
    tile_ce = jnp.minimum(head >> PAIR_BITS, n_seg - 1)
    tile_c = tile_ce // N_EXPERTS
    prev_ce = jnp.concatenate([jnp.full((1,), -N_EXPERTS, jnp.int32), tile_ce[:-1]])
    next_c = jnp.concatenate([tile_c[1:], jnp.full((1,), -1, jnp.int32)])
    next_valid = jnp.concatenate([valid[1:], jnp.zeros((1,), jnp.bool_)])
    first = valid & (tile_c != prev_ce // N_EXPERTS)
    last = valid & (~next_valid | (next_c != tile_c))
    new_expert = valid & (tile_ce != prev_ce)
    flags = (first * FLAG_FIRST + last * FLAG_LAST + new_expert * FLAG_NEW_EXPERT + valid * FLAG_VALID)
    row_idx = jnp.minimum((keys & pad_pair) // TOP_K, chunk_tokens) * 8
    return (tile_ce, flags.astype(jnp.int32), row_idx.reshape(n_tiles, 1, ROW_TILE),
            vals.reshape(n_tiles, 1, ROW_TILE))


def _t5_bucket(dist):
    n = jnp.maximum(dist, 0)
    max_exact = N_BUCKETS // 2
    large = max_exact + (jnp.log(jnp.maximum(n, 1).astype(F32) / max_exact)
                         / math.log(MAX_DISTANCE / max_exact) * (N_BUCKETS - max_exact)).astype(jnp.int32)
    large = jnp.minimum(large, N_BUCKETS - 1)
    return jnp.where(n < max_exact, n, large)


def _bias_lookup(rel_bias, bucket):
    onehot = (bucket[..., None] == jnp.arange(N_BUCKETS, dtype=bucket.dtype)).astype(F32)
    return jnp.einsum("...b,bh->h...", onehot, rel_bias.astype(F32), precision=lax.Precision.HIGHEST)


def _bias_tables(rel_bias, win):
    qi = jnp.arange(ATTN_BLOCK)[:, None]
    ki = jnp.arange(2 * ATTN_BLOCK)[None, :]
    dist = qi + ATTN_BLOCK - ki
    valid = (dist >= 0) & (dist < WINDOW)
    bias = _bias_lookup(rel_bias, _t5_bucket(dist))
    bias = jnp.where(valid[None], bias, NEG).reshape(N_KV_HEADS, GROUP, ATTN_BLOCK, 2 * ATTN_BLOCK)
    bias_ab = jnp.stack([jnp.concatenate([bias[:, t], bias[:, t + 2]], axis=1) for t in range(2)], axis=1)
    dist_s = (win - 1) - jnp.arange(win)
    bias_s = _bias_lookup(rel_bias, _t5_bucket(dist_s))
    return bias_ab, bias_s


def _sink_tables(sink):
    s = sink.astype(F32).reshape(N_KV_HEADS, GROUP)
    rows = [jnp.concatenate([jnp.broadcast_to(s[:, t, None], (N_KV_HEADS, ATTN_BLOCK)),
                             jnp.broadcast_to(s[:, t + 2, None], (N_KV_HEADS, ATTN_BLOCK))], axis=1)
            for t in range(2)]
    return jnp.stack(rows, axis=1)[..., None], sink.astype(F32)[:, None]


def kernel(x_prompt, x_sample, cache_k_win, cache_v_win, state_conv, rel_bias, w_in, attn_sink, conv_w,
           w_attn_out, w_conv_out, w_out, ln1_g, ln1_b, router_w, router_bias, exp_w_gate, exp_w_up,
           exp_w_down, shared_w_gate, shared_w_up, shared_w_down, ln2_g, ln2_b):
    depth = w_in.shape[0]
    alpha = (2 * depth) ** 0.25
    nb, seq, d = x_prompt.shape
    nd = x_sample.shape[0]
    win = cache_k_win.shape[2]
    assert x_sample.shape[1] == 1 and win == WINDOW and seq % 512 == 0

    n_prompt = nb * seq
    n_chunks, chunk_tokens, sub_tokens = _moe_tiling(n_prompt, nd)

    bias_ab, bias_s = _bias_tables(rel_bias, win)
    hmask = (jnp.arange(KV_DIM)[None, :] // HEAD_DIM == jnp.arange(N_HEADS)[:, None] // GROUP).astype(F32)

    yp = x_prompt
    ys = x_sample.reshape(nd, d)
    outs = [[] for _ in range(6)]
    for l in range(depth):
        w_in_b = w_in[l].astype(BF16)
        w_ao_b = w_attn_out[l].astype(BF16)
        w_co_b = w_conv_out[l].astype(BF16)
        w_o_b = w_out[l].astype(BF16)
        g1, b1 = ln1_g[l][None, :], ln1_b[l][None, :]
        g2, b2 = ln2_g[l][None, :], ln2_b[l][None, :]
        sink_ab, sink_col = _sink_tables(attn_sink[l])

        yp, kp, vp, cp = _mixer_prompt(yp, w_in_b, w_ao_b, w_co_b, w_o_b, conv_w[l], bias_ab, sink_ab, g1, b1,
                                       alpha=alpha, tq=512)

        proj = _sample_proj(ys, w_in_b, tn=IN_DIM // 4)
        q4 = proj[:, :Q_DIM].reshape(nd, N_KV_HEADS, GROUP, HEAD_DIM).transpose(0, 2, 1, 3).reshape(nd, GROUP, KV_DIM)
        ksn, vsn, ag = _sample_attn(q4, proj[:, OFF_K:OFF_V], proj[:, OFF_V:OFF_B],
                                    cache_k_win.reshape(depth, nd, win, KV_DIM),
                                    cache_v_win.reshape(depth, nd, win, KV_DIM),
                                    bias_s, sink_col, hmask, layer=l, bt=8)
        att = ag.reshape(nd, GROUP, N_KV_HEADS, HEAD_DIM).transpose(0, 2, 1, 3).reshape(nd, Q_DIM)
        state_t = jnp.swapaxes(state_conv[l], 0, 1)
        ys, us = _sample_post(ys, att, proj, state_t, conv_w[l], w_ao_b, w_co_b, w_o_b, g1, b1, alpha=alpha)

        outs[0].append(kp.reshape(nb, WINDOW, N_KV_HEADS, HEAD_DIM))
        outs[1].append(vp.reshape(nb, WINDOW, N_KV_HEADS, HEAD_DIM))
        outs[2].append(cp)
        outs[3].append(ksn.reshape(nd, win, N_KV_HEADS, HEAD_DIM))
        outs[4].append(vsn.reshape(nd, win, N_KV_HEADS, HEAD_DIM))
        outs[5].append(jnp.concatenate([state_conv[l][:, 1:], us[:, None, :]], axis=1))

        rw_t = router_w[l].T.astype(BF16)
        rb_col = router_bias[l].astype(F32)[:, None]
        sg, su, sd = (shared_w_gate[l].astype(BF16), shared_w_up[l].astype(BF16), shared_w_down[l].astype(BF16))
        xp = yp.reshape(n_prompt, d)
        ep, wp = _router(xp, rw_t, rb_col, tm=512)
        es, ws = _router(ys, rw_t, rb_col, tm=nd)
        plan = _route_plan(ep, es, wp, ws, chunk_tokens=chunk_tokens, n_chunks=n_chunks)
        xt = jnp.concatenate([_chunked(xp, ys, n_chunks), jnp.zeros((n_chunks, SPARE_TOKENS, d), F32)], axis=1)
        out = _moe_routed(*plan, xt.reshape(n_chunks, -1, LANES), exp_w_gate, exp_w_up, exp_w_down, sg, su, sd,
                          g2, b2, layer=l, alpha=alpha, chunk_tokens=chunk_tokens, sub_tokens=sub_tokens)
        out = out.reshape(n_chunks, chunk_tokens + SPARE_TOKENS, d)
        yp = out[:, :n_prompt // n_chunks].reshape(nb, seq, d)
        ys = out[:, n_prompt // n_chunks:chunk_tokens].reshape(nd, d)

    return (yp, ys.reshape(nd, 1, d)) + tuple(jnp.stack(o) for o in outs)
```

```python
import functools
import math

import jax
import jax.numpy as jnp
from jax import lax
from jax.experimental import pallas as pl
from jax.experimental.pallas import tpu as pltpu

D_MODEL = 1024
N_HEADS = 16
N_KV_HEADS = 4
HEAD_DIM = 64
GROUP = N_HEADS // N_KV_HEADS
WINDOW = 128
ATTN_BLOCK = 128
N_BUCKETS = 32
MAX_DISTANCE = 128
CONV_DIM = 1024
CONV_K = 3
N_EXPERTS = 64
TOP_K = 8
N_GROUPS = 8
TOPK_GROUPS = 4
GROUP_SIZE = N_EXPERTS // N_GROUPS
EXPERT_FF = 256
ROUTED_SCALE = 2.5
LN_EPS = 1e-5
NEG = -1e30

Q_DIM = N_HEADS * HEAD_DIM
KV_DIM = N_KV_HEADS * HEAD_DIM
OFF_K = Q_DIM
OFF_V = OFF_K + KV_DIM
OFF_B = OFF_V + KV_DIM
OFF_C = OFF_B + CONV_DIM
OFF_H = OFF_C + CONV_DIM
OFF_GA = OFF_H + CONV_DIM
OFF_GB = OFF_GA + D_MODEL
IN_DIM = OFF_GB + D_MODEL

LANES = 128
CONV_PAD = 8
VMEM_LIMIT = 60 * 1024 * 1024

BF16 = jnp.bfloat16
F32 = jnp.float32


def _dot(a, b):
    return jnp.dot(a, b, preferred_element_type=F32)


def _dot_nt(a, b):
    return lax.dot_general(a, b, (((1,), (1,)), ((), ())), preferred_element_type=F32)


def _layer_norm(z, g, b):
    mu = jnp.mean(z, axis=-1, keepdims=True)
    d = z - mu
    var = jnp.mean(d * d, axis=-1, keepdims=True)
    return d * lax.rsqrt(var + LN_EPS) * g + b


def _sink_softmax(s, sink_col):
    m = jnp.maximum(jnp.max(s, axis=-1, keepdims=True), sink_col)
    e = jnp.exp(s - m)
    den = jnp.sum(e, axis=-1, keepdims=True) + jnp.exp(sink_col - m)
    return e * (1.0 / den)


def _merge_project(x, attn_o, y_conv, g_a, g_b, w_co_ref, w_o_ref, lng_ref, lnb_ref, alpha):
    merged = jax.nn.sigmoid(g_a) * attn_o + jax.nn.sigmoid(g_b) * _dot(y_conv.astype(BF16), w_co_ref[...])
    out = _dot(merged.astype(BF16), w_o_ref[...])
    return _layer_norm(alpha * x + out, lng_ref[...], lnb_ref[...])


def _mixer_prompt_kernel(x_ref, w_in_ref, w_ao_ref, w_co_ref, w_o_ref, convw_ref, bias_ref, sink_ref,
                         lng_ref, lnb_ref,
                         x1_ref, kwin_ref, vwin_ref, conv_ref,
                         ka_ref, kb_ref, va_ref, vb_ref, s_ref, p_ref, att_ref, ubuf_ref, *, alpha):
    i = pl.program_id(1)
    tq = x_ref.shape[1]
    nblk = tq // ATTN_BLOCK
    half = LANES // 2
    scale = HEAD_DIM ** -0.5
    assert math.frexp(scale)[0] == 0.5

    @pl.when(i == 0)
    def _init():
        for ref in (ka_ref, kb_ref, va_ref, vb_ref):
            ref[:, 0:ATTN_BLOCK, :] = jnp.zeros((N_KV_HEADS, ATTN_BLOCK, LANES), BF16)
        ubuf_ref[0:CONV_PAD, :] = jnp.zeros((CONV_PAD, CONV_DIM), F32)

    x = x_ref[0]
    xb = x.astype(BF16)
    qkv = _dot(xb, w_in_ref[:, 0:OFF_B])

    lo = lax.broadcasted_iota(jnp.int32, (tq, LANES), 1) < half
    for c in range(N_KV_HEADS // 2):
        for off, a_ref, b_ref in ((OFF_K, ka_ref, kb_ref), (OFF_V, va_ref, vb_ref)):
            chunk = qkv[:, off + c * LANES: off + (c + 1) * LANES]
            c_lo = jnp.where(lo, chunk, 0.0)
            c_hi = jnp.where(lo, 0.0, chunk)
            a_ref[2 * c, ATTN_BLOCK:, :] = c_lo.astype(BF16)
            b_ref[2 * c, ATTN_BLOCK:, :] = pltpu.roll(c_lo, half, 1).astype(BF16)
            b_ref[2 * c + 1, ATTN_BLOCK:, :] = c_hi.astype(BF16)
            a_ref[2 * c + 1, ATTN_BLOCK:, :] = pltpu.roll(c_hi, half, 1).astype(BF16)

    key_col = lax.broadcasted_iota(jnp.int32, (2 * ATTN_BLOCK, 2 * ATTN_BLOCK), 1)
    for j in range(nblk):
        rows = slice(j * ATTN_BLOCK, (j + 1) * ATTN_BLOCK)
        keys = slice(j * ATTN_BLOCK, (j + 2) * ATTN_BLOCK)
        for h in range(N_KV_HEADS):
            q0 = h * GROUP * HEAD_DIM
            q2 = (jnp.concatenate([qkv[rows, q0:q0 + LANES], qkv[rows, q0 + LANES:q0 + 2 * LANES]], axis=0)
                  * scale).astype(BF16)
            for t, k_ref in enumerate((ka_ref, kb_ref)):
                s = _dot_nt(q2, k_ref[h, keys, :])
                if j == 0:
                    s = jnp.where(jnp.logical_and(i == 0, key_col < ATTN_BLOCK), NEG, s)
                s_ref[j, h, t] = s
    for h in range(N_KV_HEADS):
        for t in range(2):
            s = s_ref[:, h, t] + bias_ref[h, t][None]
            p_ref[:, h, t] = _sink_softmax(s, sink_ref[h, t][None]).astype(BF16)
    for j in range(nblk):
        rows = slice(j * ATTN_BLOCK, (j + 1) * ATTN_BLOCK)
        keys = slice(j * ATTN_BLOCK, (j + 2) * ATTN_BLOCK)
        for h in range(N_KV_HEADS):
            q0 = h * GROUP * HEAD_DIM
            o = _dot(p_ref[j, h, 0], va_ref[h, keys, :]) + _dot(p_ref[j, h, 1], vb_ref[h, keys, :])
            att_ref[rows, q0:q0 + LANES] = o[0:ATTN_BLOCK].astype(BF16)
            att_ref[rows, q0 + LANES:q0 + 2 * LANES] = o[ATTN_BLOCK:].astype(BF16)
    for ref in (ka_ref, kb_ref, va_ref, vb_ref):
        ref[:, 0:ATTN_BLOCK, :] = ref[:, tq:tq + ATTN_BLOCK, :]

    kwin_ref[0] = qkv[tq - WINDOW:tq, OFF_K:OFF_V]
    vwin_ref[0] = qkv[tq - WINDOW:tq, OFF_V:OFF_B]

    attn_o = _dot(att_ref[...], w_ao_ref[...])

    bch = _dot(xb, w_in_ref[:, OFF_B:OFF_GA])
    u = bch[:, CONV_DIM:2 * CONV_DIM] * bch[:, 2 * CONV_DIM:3 * CONV_DIM]
    ubuf_ref[CONV_PAD:CONV_PAD + tq, :] = u
    cw = convw_ref[...]
    y = (cw[0:1] * ubuf_ref[CONV_PAD - 2:CONV_PAD - 2 + tq, :]
         + cw[1:2] * ubuf_ref[CONV_PAD - 1:CONV_PAD - 1 + tq, :]
         + cw[2:3] * u)
    conv_ref[0] = ubuf_ref[CONV_PAD + tq - (CONV_K - 1):CONV_PAD + tq, :]
    ubuf_ref[0:CONV_PAD, :] = ubuf_ref[tq:tq + CONV_PAD, :]
    y_conv = bch[:, 0:CONV_DIM] * y

    gab = _dot(xb, w_in_ref[:, OFF_GA:IN_DIM])
    x1_ref[0] = _merge_project(x, attn_o, y_conv, gab[:, 0:D_MODEL], gab[:, D_MODEL:], w_co_ref, w_o_ref,
                               lng_ref, lnb_ref, alpha)


def _const_spec(shape):
    nd = len(shape)
    return pl.BlockSpec(shape, lambda *_: (0,) * nd, pipeline_mode=pl.Buffered(1))


def _mixer_prompt(x, w_in, w_ao, w_co, w_o, conv_w, bias_ab, sink_ab, ln_g, ln_b, *, alpha, tq):
    b, s, d = x.shape
    kernel = functools.partial(_mixer_prompt_kernel, alpha=alpha)
    return pl.pallas_call(
        kernel,
        grid=(b, s // tq),
        in_specs=[
            pl.BlockSpec((1, tq, d), lambda bi, i: (bi, i, 0)),
            _const_spec(w_in.shape), _const_spec(w_ao.shape), _const_spec(w_co.shape), _const_spec(w_o.shape),
            _const_spec(conv_w.shape), _const_spec(bias_ab.shape), _const_spec(sink_ab.shape),
            _const_spec(ln_g.shape), _const_spec(ln_b.shape),
        ],
        out_specs=[
            pl.BlockSpec((1, tq, d), lambda bi, i: (bi, i, 0)),
            pl.BlockSpec((1, WINDOW, KV_DIM), lambda bi, i: (bi, 0, 0)),
            pl.BlockSpec((1, WINDOW, KV_DIM), lambda bi, i: (bi, 0, 0)),
            pl.BlockSpec((1, CONV_K - 1, CONV_DIM), lambda bi, i: (bi, 0, 0)),
        ],
        out_shape=[
            jax.ShapeDtypeStruct((b, s, d), F32),
            jax.ShapeDtypeStruct((b, WINDOW, KV_DIM), F32),
            jax.ShapeDtypeStruct((b, WINDOW, KV_DIM), F32),
            jax.ShapeDtypeStruct((b, CONV_K - 1, CONV_DIM), F32),
        ],
        scratch_shapes=[
            pltpu.VMEM((N_KV_HEADS, ATTN_BLOCK + tq, LANES), BF16),
            pltpu.VMEM((N_KV_HEADS, ATTN_BLOCK + tq, LANES), BF16),
            pltpu.VMEM((N_KV_HEADS, ATTN_BLOCK + tq, LANES), BF16),
            pltpu.VMEM((N_KV_HEADS, ATTN_BLOCK + tq, LANES), BF16),
            pltpu.VMEM((tq // ATTN_BLOCK, N_KV_HEADS, 2, 2 * ATTN_BLOCK, 2 * ATTN_BLOCK), F32),
            pltpu.VMEM((tq // ATTN_BLOCK, N_KV_HEADS, 2, 2 * ATTN_BLOCK, 2 * ATTN_BLOCK), BF16),
            pltpu.VMEM((tq, Q_DIM), BF16),
            pltpu.VMEM((tq + CONV_PAD, CONV_DIM), F32),
        ],
        compiler_params=pltpu.CompilerParams(
            dimension_semantics=("arbitrary", "arbitrary"), vmem_limit_bytes=VMEM_LIMIT),
        name="mixer_prompt",
    )(x, w_in, w_ao, w_co, w_o, conv_w, bias_ab, sink_ab, ln_g, ln_b)


def _proj_kernel(x_ref, w_ref, o_ref):
    o_ref[...] = _dot(x_ref[...].astype(BF16), w_ref[...])


def _sample_proj(x, w_in, *, tn):
    m, d = x.shape
    n = w_in.shape[1]
    return pl.pallas_call(
        _proj_kernel,
        grid=(n // tn,),
        in_specs=[pl.BlockSpec((m, d), lambda j: (0, 0)), pl.BlockSpec((d, tn), lambda j: (0, j))],
        out_specs=pl.BlockSpec((m, tn), lambda j: (0, j)),
        out_shape=jax.ShapeDtypeStruct((m, n), F32),
        compiler_params=pltpu.CompilerParams(dimension_semantics=("arbitrary",), vmem_limit_bytes=VMEM_LIMIT),
        name="sample_proj",
    )(x, w_in)


def _sample_attn_kernel(q4_ref, knew_ref, vnew_ref, ck_ref, cv_ref, bias_ref, sink_ref, hmask_ref,
                        nk_ref, nv_ref, ag_ref):
    bt = ck_ref.shape[0]
    win = ck_ref.shape[1]
    scale = HEAD_DIM ** -0.5
    row = lax.broadcasted_iota(jnp.int32, (win, KV_DIM), 0)
    last = row == win - 1
    hmask = hmask_ref[...]
    for b in range(bt):
        kb = jnp.where(last, knew_ref[b:b + 1, :], pltpu.roll(ck_ref[b], win - 1, 0))
        vb = jnp.where(last, vnew_ref[b:b + 1, :], pltpu.roll(cv_ref[b], win - 1, 0))
        nk_ref[b] = kb
        nv_ref[b] = vb
        q4 = q4_ref[b]
        qm = (jnp.concatenate([q4] * N_KV_HEADS, axis=0) * hmask).astype(BF16)
        s = _dot_nt(qm, kb.astype(BF16)) * scale + bias_ref[...]
        p = _sink_softmax(s, sink_ref[...]).astype(BF16)
        o = _dot(p, vb.astype(BF16)) * hmask
        o4 = o[0:GROUP]
        for h in range(1, N_KV_HEADS):
            o4 = o4 + o[h * GROUP:(h + 1) * GROUP]
        ag_ref[b] = o4


def _sample_attn(q4, k_new, v_new, cache_k, cache_v, bias_s, sink_col, hmask, *, layer, bt):
    _, nb, win, kvd = cache_k.shape
    return pl.pallas_call(
        _sample_attn_kernel,
        grid=(nb // bt,),
        in_specs=[
            pl.BlockSpec((bt, GROUP, kvd), lambda i: (i, 0, 0)),
            pl.BlockSpec((bt, kvd), lambda i: (i, 0)),
            pl.BlockSpec((bt, kvd), lambda i: (i, 0)),
            pl.BlockSpec((None, bt, win, kvd), lambda i: (layer, i, 0, 0)),
            pl.BlockSpec((None, bt, win, kvd), lambda i: (layer, i, 0, 0)),
            pl.BlockSpec(bias_s.shape, lambda i: (0, 0)),
            pl.BlockSpec(sink_col.shape, lambda i: (0, 0)),
            pl.BlockSpec(hmask.shape, lambda i: (0, 0)),
        ],
        out_specs=[
            pl.BlockSpec((bt, win, kvd), lambda i: (i, 0, 0)),
            pl.BlockSpec((bt, win, kvd), lambda i: (i, 0, 0)),
            pl.BlockSpec((bt, GROUP, kvd), lambda i: (i, 0, 0)),
        ],
        out_shape=[
            jax.ShapeDtypeStruct((nb, win, kvd), F32),
            jax.ShapeDtypeStruct((nb, win, kvd), F32),
            jax.ShapeDtypeStruct((nb, GROUP, kvd), F32),
        ],
        compiler_params=pltpu.CompilerParams(dimension_semantics=("arbitrary",), vmem_limit_bytes=VMEM_LIMIT),
        name="sample_attn",
    )(q4, k_new, v_new, cache_k, cache_v, bias_s, sink_col, hmask)


def _sample_post_kernel(x_ref, att_ref, proj_ref, st_ref, convw_ref, w_ao_ref, w_co_ref, w_o_ref,
                        lng_ref, lnb_ref, x1_ref, u_ref, *, alpha):
    attn_o = _dot(att_ref[...].astype(BF16), w_ao_ref[...])
    u = proj_ref[:, OFF_C:OFF_H] * proj_ref[:, OFF_H:OFF_GA]
    cw = convw_ref[...]
    y = cw[0:1] * st_ref[0] + cw[1:2] * st_ref[1] + cw[2:3] * u
    u_ref[...] = u
    y_conv = proj_ref[:, OFF_B:OFF_C] * y
    x1_ref[...] = _merge_project(x_ref[...], attn_o, y_conv, proj_ref[:, OFF_GA:OFF_GB], proj_ref[:, OFF_GB:IN_DIM],
                                 w_co_ref, w_o_ref, lng_ref, lnb_ref, alpha)


def _sample_post(x, att, proj, state, conv_w, w_ao, w_co, w_o, ln_g, ln_b, *, alpha):
    m, d = x.shape
    kernel = functools.partial(_sample_post_kernel, alpha=alpha)
    return pl.pallas_call(
        kernel,
        out_shape=[jax.ShapeDtypeStruct((m, d), F32), jax.ShapeDtypeStruct((m, CONV_DIM), F32)],
        compiler_params=pltpu.CompilerParams(vmem_limit_bytes=VMEM_LIMIT),
        name="sample_post",
    )(x, att, proj, state, conv_w, w_ao, w_co, w_o, ln_g, ln_b)


def _first_max(cur, ids, axes, big):
    m = cur
    for ax in axes:
        m = jnp.max(m, axis=ax, keepdims=True)
    idx = jnp.where(cur == m, ids, big)
    for ax in axes:
        idx = jnp.min(idx, axis=ax, keepdims=True)
    return m, idx


def _router_kernel(x_ref, rwt_ref, rb_ref, eidx_ref, gate_ref):
    tm = x_ref.shape[0]
    logits_t = _dot_nt(rwt_ref[...], x_ref[...].astype(BF16))
    scores = jax.nn.sigmoid(logits_t)
    sel = scores + rb_ref[...]
    shape3 = (N_GROUPS, GROUP_SIZE, tm)
    sel3 = sel.reshape(shape3)
    scores3 = scores.reshape(shape3)
    member = lax.broadcasted_iota(jnp.int32, shape3, 1)
    m1, i1 = _first_max(sel3, member, (1,), GROUP_SIZE)
    m2 = jnp.max(jnp.where(member == i1, -jnp.inf, sel3), axis=1, keepdims=True)
    gscore = m1 + m2
    gid = lax.broadcasted_iota(jnp.int32, gscore.shape, 0)
    gsel = jnp.zeros(gscore.shape, jnp.bool_)
    for _ in range(TOPK_GROUPS):
        _, gi = _first_max(gscore, gid, (0,), N_GROUPS)
        hit = gid == gi
        gsel = jnp.logical_or(gsel, hit)
        gscore = jnp.where(hit, -jnp.inf, gscore)
    eid = lax.broadcasted_iota(jnp.int32, shape3, 0) * GROUP_SIZE + member
    cur = jnp.where(gsel, sel3, -jnp.inf)
    ids, ws = [], []
    for _ in range(TOP_K):
        _, ei = _first_max(cur, eid, (1, 0), N_EXPERTS)
        hit = eid == ei
        sc = jnp.sum(jnp.sum(jnp.where(hit, scores3, 0.0), axis=1, keepdims=True), axis=0, keepdims=True)
        ids.append(ei[0])
        ws.append(sc[0])
        cur = jnp.where(hit, -jnp.inf, cur)
    w = jnp.concatenate(ws, axis=0)
    tot = jnp.sum(w, axis=0, keepdims=True)
    eidx_ref[...] = jnp.concatenate(ids, axis=0)
    gate_ref[...] = w / tot * ROUTED_SCALE


def _router(x, rw_t, rb_col, *, tm):
    t, d = x.shape
    return pl.pallas_call(
        _router_kernel,
        grid=(t // tm,),
        in_specs=[
            pl.BlockSpec((tm, d), lambda i: (i, 0)),
            pl.BlockSpec(rw_t.shape, lambda i: (0, 0)),
            pl.BlockSpec(rb_col.shape, lambda i: (0, 0)),
        ],
        out_specs=[pl.BlockSpec((TOP_K, tm), lambda i: (0, i)), pl.BlockSpec((TOP_K, tm), lambda i: (0, i))],
        out_shape=[jax.ShapeDtypeStruct((TOP_K, t), jnp.int32), jax.ShapeDtypeStruct((TOP_K, t), F32)],
        compiler_params=pltpu.CompilerParams(dimension_semantics=("arbitrary",), vmem_limit_bytes=VMEM_LIMIT),
        name="router",
    )(x, rw_t, rb_col)


ROW_TILE = 256
TILE_PITCH = ROW_TILE + 1
SPARE_TOKENS = 8
PAIR_BITS = 16
FLAG_FIRST, FLAG_LAST, FLAG_NEW_EXPERT, FLAG_VALID = 1, 2, 4, 8
SCATTER_BATCH = 8


def _moe_routed_kernel(ce_ref, flags_ref, idx_prev_ref, idx_ref, idx_next_ref, gate_ref, x_ref, wg_ref, wu_ref,
                       wd_ref, sg_ref, su_ref, sd_ref, lng_ref, lnb_ref, o_ref, wgu_ref, wdb_ref,
                       gat0_ref, gat1_ref, res0_ref, res1_ref, *, alpha, chunk_tokens, sub_tokens):
    del ce_ref
    step = pl.program_id(0)
    flags = flags_ref[step]
    odd = (step & 1) == 1
    nchunk = D_MODEL // LANES
    ff = wg_ref.shape[2]

    def slab_row(ref, r):
        return pl.multiple_of(ref[0, 0, r], 8)

    def gather_row(ref, gat_ref, r):
        gat_ref[pl.ds(r, nchunk, stride=TILE_PITCH), :] = x_ref[pl.ds(slab_row(ref, r), 8), :]

    def scatter_rows(ref, res_ref, rows):
        dst = [slab_row(ref, r) for r in rows]
        acc = [o_ref[pl.ds(d, 8), :] + res_ref[pl.ds(r, nchunk, stride=TILE_PITCH), :] for d, r in zip(dst, rows)]
        for d, a in zip(dst, acc):
            o_ref[pl.ds(d, 8), :] = a

    def expert_mlp(gat_ref, res_ref):
        lhs = jnp.concatenate([gat_ref[j * TILE_PITCH:j * TILE_PITCH + ROW_TILE, :] for j in range(nchunk)],
                              axis=1).astype(BF16)
        h = _dot(lhs, wgu_ref[...])
        gate_col = jnp.broadcast_to(gate_ref[0], (LANES, ROW_TILE)).T
        hid = jax.nn.silu(h[:, 0:ff]) * h[:, ff:2 * ff] * jnp.concatenate([gate_col] * (ff // LANES), axis=1)
        y = _dot(hid.astype(BF16), wdb_ref[...])
        for j in range(nchunk):
            res_ref[j * TILE_PITCH:j * TILE_PITCH + ROW_TILE, :] = y[:, j * LANES:(j + 1) * LANES]

    def by_parity(fn):
        @pl.when(jnp.logical_not(odd))
        def _even():
            fn(gat0_ref, gat1_ref, res0_ref, res1_ref)

        @pl.when(odd)
        def _odd():
            fn(gat1_ref, gat0_ref, res1_ref, res0_ref)

    @pl.when((flags & FLAG_FIRST) != 0)
    def _start_chunk():
        o_ref[...] = jnp.zeros(o_ref.shape, F32)

        def start(gat_cur, gat_other, res_cur, res_other):
            res_other[...] = jnp.zeros(res_other.shape, F32)

            def body(r, carry):
                gather_row(idx_ref, gat_cur, r)
                return carry

            lax.fori_loop(0, ROW_TILE, body, 0)

        by_parity(start)

    @pl.when((flags & FLAG_NEW_EXPERT) != 0)
    def _cast_weights():
        wgu_ref[:, 0:ff] = wg_ref[0].astype(BF16)
        wgu_ref[:, ff:2 * ff] = wu_ref[0].astype(BF16)
        wdb_ref[...] = wd_ref[0].astype(BF16)

    @pl.when((flags & FLAG_VALID) != 0)
    def _tile():
        def main(gat_cur, gat_other, res_cur, res_other):
            for r0 in range(0, ROW_TILE, SCATTER_BATCH):
                scatter_rows(idx_prev_ref, res_other, range(r0, r0 + SCATTER_BATCH))
            expert_mlp(gat_cur, res_cur)
            for r in range(ROW_TILE):
                gather_row(idx_next_ref, gat_other, r)

        by_parity(main)

    @pl.when((flags & FLAG_LAST) != 0)
    def _finish():
        def flush(gat_cur, gat_other, res_cur, res_other):
            def body(r, carry):
                scatter_rows(idx_ref, res_cur, [r])
                return carry

            lax.fori_loop(0, ROW_TILE, body, 0)

        by_parity(flush)

        def body(s, carry):
            base = pl.multiple_of(s * (sub_tokens * 8), 8)

            def rows_2d(ref):
                return jnp.concatenate([ref[pl.ds(base + j, sub_tokens, stride=8), :] for j in range(nchunk)],
                                       axis=1)

            x2 = rows_2d(x_ref)
            xb = x2.astype(BF16)
            hs = jax.nn.silu(_dot(xb, sg_ref[...])) * _dot(xb, su_ref[...])
            ffn = rows_2d(o_ref) + _dot(hs.astype(BF16), sd_ref[...])
            res = _layer_norm(alpha * x2 + ffn, lng_ref[...], lnb_ref[...])
            for j in range(nchunk):
                o_ref[pl.ds(base + j, sub_tokens, stride=8), :] = res[:, j * LANES:(j + 1) * LANES]
            return carry

        lax.fori_loop(0, chunk_tokens // sub_tokens, body, 0)


def _moe_routed(tile_ce, tile_flags, row_idx, gates, x_tm, wg, wu, wd, sg, su, sd, ln_g, ln_b, *,
                layer, alpha, chunk_tokens, sub_tokens):
    n_chunks, chunk_rows, _ = x_tm.shape
    n_tiles = row_idx.shape[0]
    _, ne, d, ff = wg.shape

    def idx_spec(shift):
        return pl.BlockSpec((1, 1, ROW_TILE),
                            lambda i, ce, fl: (jnp.clip(i + shift, 0, n_tiles - 1), 0, 0),
                            memory_space=pltpu.SMEM)
    kernel = functools.partial(_moe_routed_kernel, alpha=alpha, chunk_tokens=chunk_tokens, sub_tokens=sub_tokens)

    def chunk_map(i, ce, fl):
        return (ce[i] // ne, 0, 0)

    def expert_map(i, ce, fl):
        return (layer, ce[i] % ne, 0, 0)

    def const2(i, ce, fl):
        return (0, 0)

    grid_spec = pltpu.PrefetchScalarGridSpec(
        num_scalar_prefetch=2,
        grid=(n_tiles,),
        in_specs=[
            idx_spec(-1), idx_spec(0), idx_spec(1),
            pl.BlockSpec((1, 1, ROW_TILE), lambda i, ce, fl: (i, 0, 0)),
            pl.BlockSpec((None, chunk_rows, LANES), chunk_map, pipeline_mode=pl.Buffered(1)),
            pl.BlockSpec((None, 1, d, ff), expert_map),
            pl.BlockSpec((None, 1, d, ff), expert_map),
            pl.BlockSpec((None, 1, ff, d), expert_map),
            pl.BlockSpec(sg.shape, const2, pipeline_mode=pl.Buffered(1)),
            pl.BlockSpec(su.shape, const2, pipeline_mode=pl.Buffered(1)),
            pl.BlockSpec(sd.shape, const2, pipeline_mode=pl.Buffered(1)),
            pl.BlockSpec(ln_g.shape, const2),
            pl.BlockSpec(ln_b.shape, const2),
        ],
        out_specs=pl.BlockSpec((None, chunk_rows, LANES), chunk_map, pipeline_mode=pl.Buffered(1)),
        scratch_shapes=[
            pltpu.VMEM((d, 2 * ff), BF16),
            pltpu.VMEM((ff, d), BF16),
            pltpu.VMEM((8 * TILE_PITCH, LANES), F32),
            pltpu.VMEM((8 * TILE_PITCH, LANES), F32),
            pltpu.VMEM((8 * TILE_PITCH, LANES), F32),
            pltpu.VMEM((8 * TILE_PITCH, LANES), F32),
        ],
    )
    return pl.pallas_call(
        kernel,
        grid_spec=grid_spec,
        out_shape=jax.ShapeDtypeStruct(x_tm.shape, F32),
        compiler_params=pltpu.CompilerParams(dimension_semantics=("arbitrary",), vmem_limit_bytes=VMEM_LIMIT),
        name="moe_routed",
    )(tile_ce, tile_flags, row_idx, row_idx, row_idx, gates, x_tm, wg, wu, wd, sg, su, sd, ln_g, ln_b)


MOE_CHUNKS = 4
MOE_SUB_MAX = 512


def _moe_tiling(n_prompt, n_sample):
    assert n_prompt % (MOE_CHUNKS * 8) == 0 and n_sample % (MOE_CHUNKS * 8) == 0
    chunk_tokens = (n_prompt + n_sample) // MOE_CHUNKS
    assert (chunk_tokens + 1) * TOP_K <= 1 << PAIR_BITS
    sub_tokens = max(s for s in range(8, MOE_SUB_MAX + 1, 8) if chunk_tokens % s == 0)
    return MOE_CHUNKS, chunk_tokens, sub_tokens


def _chunked(prompt, sample, n_chunks):
    return jnp.concatenate([prompt.reshape((n_chunks, -1) + prompt.shape[1:]),
                            sample.reshape((n_chunks, -1) + sample.shape[1:])], axis=1)


def _route_plan(eidx_p, eidx_s, gate_p, gate_s, *, chunk_tokens, n_chunks):
    n_seg = n_chunks * N_EXPERTS
    eidx = _chunked(eidx_p.T, eidx_s.T, n_chunks)
    gate = _chunked(gate_p.T, gate_s.T, n_chunks)
    seg = jnp.arange(n_chunks, dtype=jnp.int32)[:, None, None] * N_EXPERTS + eidx
    pair = jnp.arange(chunk_tokens * TOP_K, dtype=jnp.int32).reshape(1, chunk_tokens, TOP_K)
    keys_real = ((seg << PAIR_BITS) | pair).reshape(-1)
    counts = jnp.sum(eidx[..., None] == jnp.arange(N_EXPERTS, dtype=jnp.int32), axis=(1, 2),
                     dtype=jnp.int32).reshape(n_seg)
    n_pad = (-counts) % ROW_TILE
    slot = jnp.arange(ROW_TILE - 1, dtype=jnp.int32)
    int_max = jnp.iinfo(jnp.int32).max
    pad_pair = (1 << PAIR_BITS) - 1
    keys_pad = jnp.where(slot[None, :] < n_pad[:, None],
                         (jnp.arange(n_seg, dtype=jnp.int32)[:, None] << PAIR_BITS) | pad_pair, int_max)
    keys = jnp.concatenate([keys_real, keys_pad.reshape(-1)])
    vals = jnp.concatenate([gate.reshape(-1), jnp.zeros((keys_pad.size,), F32)])
    assert keys.size % ROW_TILE == 0
    n_tiles = keys.size // ROW_TILE
    keys, vals = lax.sort((keys, vals), num_keys=1)
    keys = keys.reshape(n_tiles, ROW_TILE)
    head = keys[:, 0]
    valid = head != int_max
    tile_ce = jnp.minimum(head >> PAIR_BITS, n_seg - 1)
    tile_c = tile_ce // N_EXPERTS
    prev_ce = jnp.concatenate([jnp.full((1,), -N_EXPERTS, jnp.int32), tile_ce[:-1]])
    next_c = jnp.concatenate([tile_c[1:], jnp.full((1,), -1, jnp.int32)])
    next_valid = jnp.concatenate([valid[1:], jnp.zeros((1,), jnp.bool_)])
    first = valid & (tile_c != prev_ce // N_EXPERTS)
    last = valid & (~next_valid | (next_c != tile_c))
    new_expert = valid & (tile_ce != prev_ce)
    flags = (first * FLAG_FIRST + last * FLAG_LAST + new_expert * FLAG_NEW_EXPERT + valid * FLAG_VALID)
    row_idx = jnp.minimum((keys & pad_pair) // TOP_K, chunk_tokens) * 8
    return (tile_ce, flags.astype(jnp.int32), row_idx.reshape(n_tiles, 1, ROW_TILE),
            vals.reshape(n_tiles, 1, ROW_TILE))


def _t5_bucket(dist):
    n = jnp.maximum(dist, 0)
    max_exact = N_BUCKETS // 2
    large = max_exact + (jnp.log(jnp.maximum(n, 1).astype(F32) / max_exact)
                         / math.log(MAX_DISTANCE / max_exact) * (N_BUCKETS - max_exact)).astype(jnp.int32)
    large = jnp.minimum(large, N_BUCKETS - 1)
    return jnp.where(n < max_exact, n, large)


def _bias_lookup(rel_bias, bucket):
    onehot = (bucket[..., None] == jnp.arange(N_BUCKETS, dtype=bucket.dtype)).astype(F32)
    return jnp.einsum("...b,bh->h...", onehot, rel_bias.astype(F32), precision=lax.Precision.HIGHEST)


def _bias_tables(rel_bias, win):
    qi = jnp.arange(ATTN_BLOCK)[:, None]
    ki = jnp.arange(2 * ATTN_BLOCK)[None, :]
    dist = qi + ATTN_BLOCK - ki
    valid = (dist >= 0) & (dist < WINDOW)
    bias = _bias_lookup(rel_bias, _t5_bucket(dist))
    bias = jnp.where(valid[None], bias, NEG).reshape(N_KV_HEADS, GROUP, ATTN_BLOCK, 2 * ATTN_BLOCK)
    bias_ab = jnp.stack([jnp.concatenate([bias[:, t], bias[:, t + 2]], axis=1) for t in range(2)], axis=1)
    dist_s = (win - 1) - jnp.arange(win)
    bias_s = _bias_lookup(rel_bias, _t5_bucket(dist_s))
    return bias_ab, bias_s


def _sink_tables(sink):
    s = sink.astype(F32).reshape(N_KV_HEADS, GROUP)
    rows = [jnp.concatenate([jnp.broadcast_to(s[:, t, None], (N_KV_HEADS, ATTN_BLOCK)),
                             jnp.broadcast_to(s[:, t + 2, None], (N_KV_HEADS, ATTN_BLOCK))], axis=1)
            for t in range(2)]
    return jnp.stack(rows, axis=1)[..., None], sink.astype(F32)[:, None]


def kernel(x_prompt, x_sample, cache_k_win, cache_v_win, state_conv, rel_bias, w_in, attn_sink, conv_w,
           w_attn_out, w_conv_out, w_out, ln1_g, ln1_b, router_w, router_bias, exp_w_gate, exp_w_up,
           exp_w_down, shared_w_gate, shared_w_up, shared_w_down, ln2_g, ln2_b):
    depth = w_in.shape[0]
    alpha = (2 * depth) ** 0.25
    nb, seq, d = x_prompt.shape
    nd = x_sample.shape[0]
    win = cache_k_win.shape[2]
    assert x_sample.shape[1] == 1 and win == WINDOW and seq % 512 == 0

    n_prompt = nb * seq
    n_chunks, chunk_tokens, sub_tokens = _moe_tiling(n_prompt, nd)

    bias_ab, bias_s = _bias_tables(rel_bias, win)
    hmask = (jnp.arange(KV_DIM)[None, :] // HEAD_DIM == jnp.arange(N_HEADS)[:, None] // GROUP).astype(F32)

    yp = x_prompt
    ys = x_sample.reshape(nd, d)
    outs = [[] for _ in range(6)]
    for l in range(depth):
        w_in_b = w_in[l].astype(BF16)
        w_ao_b = w_attn_out[l].astype(BF16)
        w_co_b = w_conv_out[l].astype(BF16)
        w_o_b = w_out[l].astype(BF16)
        g1, b1 = ln1_g[l][None, :], ln1_b[l][None, :]
        g2, b2 = ln2_g[l][None, :], ln2_b[l][None, :]
        sink_ab, sink_col = _sink_tables(attn_sink[l])

        yp, kp, vp, cp = _mixer_prompt(yp, w_in_b, w_ao_b, w_co_b, w_o_b, conv_w[l], bias_ab, sink_ab, g1, b1,
                                       alpha=alpha, tq=512)

        proj = _sample_proj(ys, w_in_b, tn=IN_DIM // 4)
        q4 = proj[:, :Q_DIM].reshape(nd, N_KV_HEADS, GROUP, HEAD_DIM).transpose(0, 2, 1, 3).reshape(nd, GROUP, KV_DIM)
        ksn, vsn, ag = _sample_attn(q4, proj[:, OFF_K:OFF_V], proj[:, OFF_V:OFF_B],
                                    cache_k_win.reshape(depth, nd, win, KV_DIM),
                                    cache_v_win.reshape(depth, nd, win, KV_DIM),
                                    bias_s, sink_col, hmask, layer=l, bt=8)
        att = ag.reshape(nd, GROUP, N_KV_HEADS, HEAD_DIM).transpose(0, 2, 1, 3).reshape(nd, Q_DIM)
        state_t = jnp.swapaxes(state_conv[l], 0, 1)
        ys, us = _sample_post(ys, att, proj, state_t, conv_w[l], w_ao_b, w_co_b, w_o_b, g1, b1, alpha=alpha)

        outs[0].append(kp.reshape(nb, WINDOW, N_KV_HEADS, HEAD_DIM))
        outs[1].append(vp.reshape(nb, WINDOW, N_KV_HEADS, HEAD_DIM))
        outs[2].append(cp)
        outs[3].append(ksn.reshape(nd, win, N_KV_HEADS, HEAD_DIM))
        outs[4].append(vsn.reshape(nd, win, N_KV_HEADS, HEAD_DIM))
        outs[5].append(jnp.concatenate([state_conv[l][:, 1:], us[:, None, :]], axis=1))

        rw_t = router_w[l].T.astype(BF16)
        rb_col = router_bias[l].astype(F32)[:, None]
        sg, su, sd = (shared_w_gate[l].astype(BF16), shared_w_up[l].astype(BF16), shared_w_down[l].astype(BF16))
        xp = yp.reshape(n_prompt, d)
        ep, wp = _router(xp, rw_t, rb_col, tm=512)
        es, ws = _router(ys, rw_t, rb_col, tm=nd)
        plan = _route_plan(ep, es, wp, ws, chunk_tokens=chunk_tokens, n_chunks=n_chunks)
        xt = jnp.concatenate([_chunked(xp, ys, n_chunks), jnp.zeros((n_chunks, SPARE_TOKENS, d), F32)], axis=1)
        out = _moe_routed(*plan, xt.reshape(n_chunks, -1, LANES), exp_w_gate, exp_w_up, exp_w_down, sg, su, sd,
                          g2, b2, layer=l, alpha=alpha, chunk_tokens=chunk_tokens, sub_tokens=sub_tokens)
        out = out.reshape(n_chunks, chunk_tokens + SPARE_TOKENS, d)
        yp = out[:, :n_prompt // n_chunks].reshape(nb, seq, d)
        ys = out[:, n_prompt // n_chunks:chunk_tokens].reshape(nd, d)

    return (yp, ys.reshape(nd, 1, d)) + tuple(jnp.stack(o) for o in outs)
```

```python
import functools
import math

import jax
import jax.numpy as jnp
from jax import lax
from jax.experimental import pallas as pl
from jax.experimental.pallas import tpu as pltpu

D_MODEL = 1024
N_HEADS = 16
N_KV_HEADS = 4
HEAD_DIM = 64
GROUP = N_HEADS // N_KV_HEADS
WINDOW = 128
ATTN_BLOCK = 128
N_BUCKETS = 32
MAX_DISTANCE = 128
CONV_DIM = 1024
CONV_K = 3
N_EXPERTS = 64
TOP_K = 8
N_GROUPS = 8
TOPK_GROUPS = 4
GROUP_SIZE = N_EXPERTS // N_GROUPS
EXPERT_FF = 256
ROUTED_SCALE = 2.5
LN_EPS = 1e-5
NEG = -1e30

Q_DIM = N_HEADS * HEAD_DIM
KV_DIM = N_KV_HEADS * HEAD_DIM
OFF_K = Q_DIM
OFF_V = OFF_K + KV_DIM
OFF_B = OFF_V + KV_DIM
OFF_C = OFF_B + CONV_DIM
OFF_H = OFF_C + CONV_DIM
OFF_GA = OFF_H + CONV_DIM
OFF_GB = OFF_GA + D_MODEL
IN_DIM = OFF_GB + D_MODEL

LANES = 128
CONV_PAD = 8
VMEM_LIMIT = 60 * 1024 * 1024

BF16 = jnp.bfloat16
F32 = jnp.float32


def _dot(a, b):
    return jnp.dot(a, b, preferred_element_type=F32)


def _dot_nt(a, b):
    return lax.dot_general(a, b, (((1,), (1,)), ((), ())), preferred_element_type=F32)


def _layer_norm(z, g, b):
    mu = jnp.mean(z, axis=-1, keepdims=True)
    d = z - mu
    var = jnp.mean(d * d, axis=-1, keepdims=True)
    return d * lax.rsqrt(var + LN_EPS) * g + b


def _sink_softmax(s, sink, axis=-1):
    m = jnp.maximum(jnp.max(s, axis=axis, keepdims=True), sink)
    e = jnp.exp(s - m)
    den = jnp.sum(e, axis=axis, keepdims=True) + jnp.exp(sink - m)
    return e * (1.0 / den)


def _merge_project(x, attn_o, y_conv, g_a, g_b, w_co_ref, w_o_ref, lng_ref, lnb_ref, alpha):
    merged = jax.nn.sigmoid(g_a) * attn_o + jax.nn.sigmoid(g_b) * _dot(y_conv.astype(BF16), w_co_ref[...])
    out = _dot(merged.astype(BF16), w_o_ref[...])
    return _layer_norm(alpha * x + out, lng_ref[...], lnb_ref[...])


def _mixer_prompt_kernel(x_ref, w_in_ref, w_ao_ref, w_co_ref, w_o_ref, convw_ref, bias_ref, sink_ref,
                         lng_ref, lnb_ref,
                         x1_ref, kwin_ref, vwin_ref, conv_ref,
                         ka_ref, kb_ref, vat_ref, vbt_ref, s_ref, p_ref, att_ref, ubuf_ref, *, alpha):
    i = pl.program_id(1)
    tq = x_ref.shape[1]
    nblk = tq // ATTN_BLOCK
    half = LANES // 2
    scale = HEAD_DIM ** -0.5
    assert math.frexp(scale)[0] == 0.5

    @pl.when(i == 0)
    def _init():
        for ref in (ka_ref, kb_ref):
            ref[:, 0:ATTN_BLOCK, :] = jnp.zeros((N_KV_HEADS, ATTN_BLOCK, LANES), BF16)
        for ref in (vat_ref, vbt_ref):
            ref[:, :, 0:ATTN_BLOCK] = jnp.zeros((N_KV_HEADS, LANES, ATTN_BLOCK), BF16)
        ubuf_ref[0:CONV_PAD, :] = jnp.zeros((CONV_PAD, CONV_DIM), F32)

    x = x_ref[0]
    xb = x.astype(BF16)
    qkv = _dot(xb, w_in_ref[:, 0:OFF_B])

    lo = lax.broadcasted_iota(jnp.int32, (tq, LANES), 1) < half
    zeros_t = jnp.zeros((half, tq), BF16)
    for c in range(N_KV_HEADS // 2):
        chunk = qkv[:, OFF_K + c * LANES: OFF_K + (c + 1) * LANES]
        c_lo = jnp.where(lo, chunk, 0.0)
        c_hi = jnp.where(lo, 0.0, chunk)
        ka_ref[2 * c, ATTN_BLOCK:, :] = c_lo.astype(BF16)
        kb_ref[2 * c, ATTN_BLOCK:, :] = pltpu.roll(c_lo, half, 1).astype(BF16)
        kb_ref[2 * c + 1, ATTN_BLOCK:, :] = c_hi.astype(BF16)
        ka_ref[2 * c + 1, ATTN_BLOCK:, :] = pltpu.roll(c_hi, half, 1).astype(BF16)
        vt = qkv[:, OFF_V + c * LANES: OFF_V + (c + 1) * LANES].T.astype(BF16)
        vat_ref[2 * c, :, ATTN_BLOCK:] = jnp.concatenate([vt[0:half], zeros_t], axis=0)
        vbt_ref[2 * c, :, ATTN_BLOCK:] = jnp.concatenate([zeros_t, vt[0:half]], axis=0)
        vbt_ref[2 * c + 1, :, ATTN_BLOCK:] = jnp.concatenate([zeros_t, vt[half:]], axis=0)
        vat_ref[2 * c + 1, :, ATTN_BLOCK:] = jnp.concatenate([vt[half:], zeros_t], axis=0)

    key_row = lax.broadcasted_iota(jnp.int32, (2 * ATTN_BLOCK, 2 * ATTN_BLOCK), 0)
    for j in range(nblk):
        rows = slice(j * ATTN_BLOCK, (j + 1) * ATTN_BLOCK)
        keys = slice(j * ATTN_BLOCK, (j + 2) * ATTN_BLOCK)
        for h in range(N_KV_HEADS):
            q0 = h * GROUP * HEAD_DIM
            q2 = (jnp.concatenate([qkv[rows, q0:q0 + LANES], qkv[rows, q0 + LANES:q0 + 2 * LANES]], axis=0)
                  * scale).astype(BF16)
            for t, k_ref in enumerate((ka_ref, kb_ref)):
                s = _dot_nt(k_ref[h, keys, :], q2)
                if j == 0:
                    s = jnp.where(jnp.logical_and(i == 0, key_row < ATTN_BLOCK), NEG, s)
                s_ref[j, h, t] = s
    for h in range(N_KV_HEADS):
        for t in range(2):
            s = s_ref[:, h, t] + bias_ref[h, t][None]
            p_ref[:, h, t] = _sink_softmax(s, sink_ref[h, t][None], axis=1).astype(BF16)
    for j in range(nblk):
        rows = slice(j * ATTN_BLOCK, (j + 1) * ATTN_BLOCK)
        keys = slice(j * ATTN_BLOCK, (j + 2) * ATTN_BLOCK)
        for h in range(N_KV_HEADS):
            q0 = h * GROUP * HEAD_DIM
            o_t = _dot(vat_ref[h, :, keys], p_ref[j, h, 0]) + _dot(vbt_ref[h, :, keys], p_ref[j, h, 1])
            att_ref[q0:q0 + LANES, rows] = o_t[:, 0:ATTN_BLOCK].astype(BF16)
            att_ref[q0 + LANES:q0 + 2 * LANES, rows] = o_t[:, ATTN_BLOCK:].astype(BF16)
    for ref in (ka_ref, kb_ref):
        ref[:, 0:ATTN_BLOCK, :] = ref[:, tq:tq + ATTN_BLOCK, :]
    for ref in (vat_ref, vbt_ref):
        ref[:, :, 0:ATTN_BLOCK] = ref[:, :, tq:tq + ATTN_BLOCK]

    kwin_ref[0] = qkv[tq - WINDOW:tq, OFF_K:OFF_V]
    vwin_ref[0] = qkv[tq - WINDOW:tq, OFF_V:OFF_B]

    attn_o = _dot(w_ao_ref[...], att_ref[...]).T

    bch = _dot(xb, w_in_ref[:, OFF_B:OFF_GA])
    u = bch[:, CONV_DIM:2 * CONV_DIM] * bch[:, 2 * CONV_DIM:3 * CONV_DIM]
    ubuf_ref[CONV_PAD:CONV_PAD + tq, :] = u
    cw = convw_ref[...]
    y = (cw[0:1] * ubuf_ref[CONV_PAD - 2:CONV_PAD - 2 + tq, :]
         + cw[1:2] * ubuf_ref[CONV_PAD - 1:CONV_PAD - 1 + tq, :]
         + cw[2:3] * u)
    conv_ref[0] = ubuf_ref[CONV_PAD + tq - (CONV_K - 1):CONV_PAD + tq, :]
    ubuf_ref[0:CONV_PAD, :] = ubuf_ref[tq:tq + CONV_PAD, :]
    y_conv = bch[:, 0:CONV_DIM] * y

    gab = _dot(xb, w_in_ref[:, OFF_GA:IN_DIM])
    x1_ref[0] = _merge_project(x, attn_o, y_conv, gab[:, 0:D_MODEL], gab[:, D_MODEL:], w_co_ref, w_o_ref,
                               lng_ref, lnb_ref, alpha)


def _const_spec(shape):
    nd = len(shape)
    return pl.BlockSpec(shape, lambda *_: (0,) * nd, pipeline_mode=pl.Buffered(1))


def _mixer_prompt(x, w_in, w_ao, w_co, w_o, conv_w, bias_ab, sink_ab, ln_g, ln_b, *, alpha, tq):
    b, s, d = x.shape
    kernel = functools.partial(_mixer_prompt_kernel, alpha=alpha)
    return pl.pallas_call(
        kernel,
        grid=(b, s // tq),
        in_specs=[
            pl.BlockSpec((1, tq, d), lambda bi, i: (bi, i, 0)),
            _const_spec(w_in.shape), _const_spec(w_ao.shape), _const_spec(w_co.shape), _const_spec(w_o.shape),
            _const_spec(conv_w.shape), _const_spec(bias_ab.shape), _const_spec(sink_ab.shape),
            _const_spec(ln_g.shape), _const_spec(ln_b.shape),
        ],
        out_specs=[
            pl.BlockSpec((1, tq, d), lambda bi, i: (bi, i, 0)),
            pl.BlockSpec((1, WINDOW, KV_DIM), lambda bi, i: (bi, 0, 0)),
            pl.BlockSpec((1, WINDOW, KV_DIM), lambda bi, i: (bi, 0, 0)),
            pl.BlockSpec((1, CONV_K - 1, CONV_DIM), lambda bi, i: (bi, 0, 0)),
        ],
        out_shape=[
            jax.ShapeDtypeStruct((b, s, d), F32),
            jax.ShapeDtypeStruct((b, WINDOW, KV_DIM), F32),
            jax.ShapeDtypeStruct((b, WINDOW, KV_DIM), F32),
            jax.ShapeDtypeStruct((b, CONV_K - 1, CONV_DIM), F32),
        ],
        scratch_shapes=[
            pltpu.VMEM((N_KV_HEADS, ATTN_BLOCK + tq, LANES), BF16),
            pltpu.VMEM((N_KV_HEADS, ATTN_BLOCK + tq, LANES), BF16),
            pltpu.VMEM((N_KV_HEADS, LANES, ATTN_BLOCK + tq), BF16),
            pltpu.VMEM((N_KV_HEADS, LANES, ATTN_BLOCK + tq), BF16),
            pltpu.VMEM((tq // ATTN_BLOCK, N_KV_HEADS, 2, 2 * ATTN_BLOCK, 2 * ATTN_BLOCK), F32),
            pltpu.VMEM((tq // ATTN_BLOCK, N_KV_HEADS, 2, 2 * ATTN_BLOCK, 2 * ATTN_BLOCK), BF16),
            pltpu.VMEM((Q_DIM, tq), BF16),
            pltpu.VMEM((tq + CONV_PAD, CONV_DIM), F32),
        ],
        compiler_params=pltpu.CompilerParams(
            dimension_semantics=("arbitrary", "arbitrary"), vmem_limit_bytes=VMEM_LIMIT),
        name="mixer_prompt",
    )(x, w_in, w_ao, w_co, w_o, conv_w, bias_ab, sink_ab, ln_g, ln_b)


def _proj_kernel(x_ref, w_ref, o_ref):
    o_ref[...] = _dot(x_ref[...].astype(BF16), w_ref[...])


def _sample_proj(x, w_in, *, tn):
    m, d = x.shape
    n = w_in.shape[1]
    return pl.pallas_call(
        _proj_kernel,
        grid=(n // tn,),
        in_specs=[pl.BlockSpec((m, d), lambda j: (0, 0)), pl.BlockSpec((d, tn), lambda j: (0, j))],
        out_specs=pl.BlockSpec((m, tn), lambda j: (0, j)),
        out_shape=jax.ShapeDtypeStruct((m, n), F32),
        compiler_params=pltpu.CompilerParams(dimension_semantics=("arbitrary",), vmem_limit_bytes=VMEM_LIMIT),
        name="sample_proj",
    )(x, w_in)


def _sample_attn_kernel(q4_ref, knew_ref, vnew_ref, ck_ref, cv_ref, bias_ref, sink_ref, hmask_ref,
                        nk_ref, nv_ref, ag_ref):
    bt = ck_ref.shape[0]
    win = ck_ref.shape[1]
    scale = HEAD_DIM ** -0.5
    row = lax.broadcasted_iota(jnp.int32, (win, KV_DIM), 0)
    last = row == win - 1
    hmask = hmask_ref[...]
    for b in range(bt):
        kb = jnp.where(last, knew_ref[b:b + 1, :], pltpu.roll(ck_ref[b], win - 1, 0))
        vb = jnp.where(last, vnew_ref[b:b + 1, :], pltpu.roll(cv_ref[b], win - 1, 0))
        nk_ref[b] = kb
        nv_ref[b] = vb
        q4 = q4_ref[b]
        qm = (jnp.concatenate([q4] * N_KV_HEADS, axis=0) * hmask).astype(BF16)
        s = _dot_nt(qm, kb.astype(BF16)) * scale + bias_ref[...]
        p = _sink_softmax(s, sink_ref[...]).astype(BF16)
        o = _dot(p, vb.astype(BF16)) * hmask
        o4 = o[0:GROUP]
        for h in range(1, N_KV_HEADS):
            o4 = o4 + o[h * GROUP:(h + 1) * GROUP]
        ag_ref[b] = o4


def _sample_attn(q4, k_new, v_new, cache_k, cache_v, bias_s, sink_col, hmask, *, layer, bt):
    _, nb, win, kvd = cache_k.shape
    return pl.pallas_call(
        _sample_attn_kernel,
        grid=(nb // bt,),
        in_specs=[
            pl.BlockSpec((bt, GROUP, kvd), lambda i: (i, 0, 0)),
            pl.BlockSpec((bt, kvd), lambda i: (i, 0)),
            pl.BlockSpec((bt, kvd), lambda i: (i, 0)),
            pl.BlockSpec((None, bt, win, kvd), lambda i: (layer, i, 0, 0)),
            pl.BlockSpec((None, bt, win, kvd), lambda i: (layer, i, 0, 0)),
            pl.BlockSpec(bias_s.shape, lambda i: (0, 0)),
            pl.BlockSpec(sink_col.shape, lambda i: (0, 0)),
            pl.BlockSpec(hmask.shape, lambda i: (0, 0)),
        ],
        out_specs=[
            pl.BlockSpec((bt, win, kvd), lambda i: (i, 0, 0)),
            pl.BlockSpec((bt, win, kvd), lambda i: (i, 0, 0)),
            pl.BlockSpec((bt, GROUP, kvd), lambda i: (i, 0, 0)),
        ],
        out_shape=[
            jax.ShapeDtypeStruct((nb, win, kvd), F32),
            jax.ShapeDtypeStruct((nb, win, kvd), F32),
            jax.ShapeDtypeStruct((nb, GROUP, kvd), F32),
        ],
        compiler_params=pltpu.CompilerParams(dimension_semantics=("arbitrary",), vmem_limit_bytes=VMEM_LIMIT),
        name="sample_attn",
    )(q4, k_new, v_new, cache_k, cache_v, bias_s, sink_col, hmask)


def _sample_post_kernel(x_ref, att_ref, proj_ref, st_ref, convw_ref, w_ao_ref, w_co_ref, w_o_ref,
                        lng_ref, lnb_ref, x1_ref, u_ref, *, alpha):
    attn_o = _dot(att_ref[...].astype(BF16), w_ao_ref[...])
    u = proj_ref[:, OFF_C:OFF_H] * proj_ref[:, OFF_H:OFF_GA]
    cw = convw_ref[...]
    y = cw[0:1] * st_ref[0] + cw[1:2] * st_ref[1] + cw[2:3] * u
    u_ref[...] = u
    y_conv = proj_ref[:, OFF_B:OFF_C] * y
    x1_ref[...] = _merge_project(x_ref[...], attn_o, y_conv, proj_ref[:, OFF_GA:OFF_GB], proj_ref[:, OFF_GB:IN_DIM],
                                 w_co_ref, w_o_ref, lng_ref, lnb_ref, alpha)


def _sample_post(x, att, proj, state, conv_w, w_ao, w_co, w_o, ln_g, ln_b, *, alpha):
    m, d = x.shape
    kernel = functools.partial(_sample_post_kernel, alpha=alpha)
    return pl.pallas_call(
        kernel,
        out_shape=[jax.ShapeDtypeStruct((m, d), F32), jax.ShapeDtypeStruct((m, CONV_DIM), F32)],
        compiler_params=pltpu.CompilerParams(vmem_limit_bytes=VMEM_LIMIT),
        name="sample_post",
    )(x, att, proj, state, conv_w, w_ao, w_co, w_o, ln_g, ln_b)


def _first_max(cur, ids, axes, big):
    m = cur
    for ax in axes:
        m = jnp.max(m, axis=ax, keepdims=True)
    idx = jnp.where(cur == m, ids, big)
    for ax in axes:
        idx = jnp.min(idx, axis=ax, keepdims=True)
    return m, idx


def _router_kernel(x_ref, rwt_ref, rb_ref, eidx_ref, gate_ref):
    tm = x_ref.shape[0]
    logits_t = _dot_nt(rwt_ref[...], x_ref[...].astype(BF16))
    scores = jax.nn.sigmoid(logits_t)
    sel = scores + rb_ref[...]
    shape3 = (N_GROUPS, GROUP_SIZE, tm)
    sel3 = sel.reshape(shape3)
    scores3 = scores.reshape(shape3)
    member = lax.broadcasted_iota(jnp.int32, shape3, 1)
    m1, i1 = _first_max(sel3, member, (1,), GROUP_SIZE)
    m2 = jnp.max(jnp.where(member == i1, -jnp.inf, sel3), axis=1, keepdims=True)
    gscore = m1 + m2
    gid = lax.broadcasted_iota(jnp.int32, gscore.shape, 0)
    gsel = jnp.zeros(gscore.shape, jnp.bool_)
    for _ in range(TOPK_GROUPS):
        _, gi = _first_max(gscore, gid, (0,), N_GROUPS)
        hit = gid == gi
        gsel = jnp.logical_or(gsel, hit)
        gscore = jnp.where(hit, -jnp.inf, gscore)
    eid = lax.broadcasted_iota(jnp.int32, shape3, 0) * GROUP_SIZE + member
    cur = jnp.where(gsel, sel3, -jnp.inf)
    ids, ws = [], []
    for _ in range(TOP_K):
        _, ei = _first_max(cur, eid, (1, 0), N_EXPERTS)
        hit = eid == ei
        sc = jnp.sum(jnp.sum(jnp.where(hit, scores3, 0.0), axis=1, keepdims=True), axis=0, keepdims=True)
        ids.append(ei[0])
        ws.append(sc[0])
        cur = jnp.where(hit, -jnp.inf, cur)
    w = jnp.concatenate(ws, axis=0)
    tot = jnp.sum(w, axis=0, keepdims=True)
    eidx_ref[...] = jnp.concatenate(ids, axis=0)
    gate_ref[...] = w / tot * ROUTED_SCALE


def _router(x, rw_t, rb_col, *, tm):
    t, d = x.shape
    return pl.pallas_call(
        _router_kernel,
        grid=(t // tm,),
        in_specs=[
            pl.BlockSpec((tm, d), lambda i: (i, 0)),
            pl.BlockSpec(rw_t.shape, lambda i: (0, 0)),
            pl.BlockSpec(rb_col.shape, lambda i: (0, 0)),
        ],
        out_specs=[pl.BlockSpec((TOP_K, tm), lambda i: (0, i)), pl.BlockSpec((TOP_K, tm), lambda i: (0, i))],
        out_shape=[jax.ShapeDtypeStruct((TOP_K, t), jnp.int32), jax.ShapeDtypeStruct((TOP_K, t), F32)],
        compiler_params=pltpu.CompilerParams(dimension_semantics=("arbitrary",), vmem_limit_bytes=VMEM_LIMIT),
        name="router",
    )(x, rw_t, rb_col)


ROW_TILE = 256
TILE_PITCH = ROW_TILE + 1
SPARE_TOKENS = 8
PAIR_BITS = 16
FLAG_FIRST, FLAG_LAST, FLAG_NEW_EXPERT, FLAG_VALID = 1, 2, 4, 8
SCATTER_BATCH = 8


PREV, CUR, NEXT = 0, 1, 2


def _moe_routed_kernel(ce_ref, flags_ref, used_ref, idx_ref, gate_ref, x_ref, wg_ref, wu_ref,
                       wd_ref, sg_ref, su_ref, sd_ref, lng_ref, lnb_ref, o_ref, wgu_ref, wdb_ref,
                       gat0_ref, gat1_ref, res0_ref, res1_ref, *, alpha, chunk_tokens, sub_tokens):
    del ce_ref, used_ref
    step = pl.program_id(0)
    flags = flags_ref[step]
    odd = (step & 1) == 1
    nchunk = D_MODEL // LANES
    ff = wg_ref.shape[2]

    def slab_row(which, r):
        return pl.multiple_of(idx_ref[0, which, r], 8)

    def gather_row(which, gat_ref, r):
        gat_ref[pl.ds(r, nchunk, stride=TILE_PITCH), :] = x_ref[pl.ds(slab_row(which, r), 8), :]

    def scatter_rows(which, res_ref, rows):
        dst = [slab_row(which, r) for r in rows]
        acc = [o_ref[pl.ds(d, 8), :] + res_ref[pl.ds(r, nchunk, stride=TILE_PITCH), :] for d, r in zip(dst, rows)]
        for d, a in zip(dst, acc):
            o_ref[pl.ds(d, 8), :] = a

    def expert_mlp(gat_ref, res_ref):
        lhs = jnp.concatenate([gat_ref[j * TILE_PITCH:j * TILE_PITCH + ROW_TILE, :] for j in range(nchunk)],
                              axis=1).astype(BF16)
        h = _dot(lhs, wgu_ref[...])
        gate_col = jnp.broadcast_to(gate_ref[0], (LANES, ROW_TILE)).T
        hid = jax.nn.silu(h[:, 0:ff]) * h[:, ff:2 * ff] * jnp.concatenate([gate_col] * (ff // LANES), axis=1)
        y = _dot(hid.astype(BF16), wdb_ref[...])
        for j in range(nchunk):
            res_ref[j * TILE_PITCH:j * TILE_PITCH + ROW_TILE, :] = y[:, j * LANES:(j + 1) * LANES]

    def by_parity(fn):
        @pl.when(jnp.logical_not(odd))
        def _even():
            fn(gat0_ref, gat1_ref, res0_ref, res1_ref)

        @pl.when(odd)
        def _odd():
            fn(gat1_ref, gat0_ref, res1_ref, res0_ref)

    @pl.when((flags & FLAG_FIRST) != 0)
    def _start_chunk():
        o_ref[...] = jnp.zeros(o_ref.shape, F32)

        def start(gat_cur, gat_other, res_cur, res_other):
            res_other[...] = jnp.zeros(res_other.shape, F32)

            def body(r, carry):
                gather_row(CUR, gat_cur, r)
                return carry

            lax.fori_loop(0, ROW_TILE, body, 0)

        by_parity(start)

    @pl.when((flags & FLAG_NEW_EXPERT) != 0)
    def _cast_weights():
        wgu_ref[:, 0:ff] = wg_ref[0].astype(BF16)
        wgu_ref[:, ff:2 * ff] = wu_ref[0].astype(BF16)
        wdb_ref[...] = wd_ref[0].astype(BF16)

    @pl.when((flags & FLAG_VALID) != 0)
    def _tile():
        def main(gat_cur, gat_other, res_cur, res_other):
            for r0 in range(0, ROW_TILE, SCATTER_BATCH):
                scatter_rows(PREV, res_other, range(r0, r0 + SCATTER_BATCH))
            expert_mlp(gat_cur, res_cur)
            for r in range(ROW_TILE):
                gather_row(NEXT, gat_other, r)

        by_parity(main)

    @pl.when((flags & FLAG_LAST) != 0)
    def _finish():
        def flush(gat_cur, gat_other, res_cur, res_other):
            def body(r, carry):
                scatter_rows(CUR, res_cur, [r])
                return carry

            lax.fori_loop(0, ROW_TILE, body, 0)

        by_parity(flush)

        def body(s, carry):
            base = pl.multiple_of(s * (sub_tokens * 8), 8)

            def rows_2d(ref):
                return jnp.concatenate([ref[pl.ds(base + j, sub_tokens, stride=8), :] for j in range(nchunk)],
                                       axis=1)

            x2 = rows_2d(x_ref)
            xb = x2.astype(BF16)
            hs = jax.nn.silu(_dot(xb, sg_ref[...])) * _dot(xb, su_ref[...])
            ffn = rows_2d(o_ref) + _dot(hs.astype(BF16), sd_ref[...])
            res = _layer_norm(alpha * x2 + ffn, lng_ref[...], lnb_ref[...])
            for j in range(nchunk):
                o_ref[pl.ds(base + j, sub_tokens, stride=8), :] = res[:, j * LANES:(j + 1) * LANES]
            return carry

        lax.fori_loop(0, chunk_tokens // sub_tokens, body, 0)


def _moe_routed(tile_ce, tile_flags, n_used, row_idx, gates, x_tm, wg, wu, wd, sg, su, sd, ln_g, ln_b, *,
                layer, alpha, chunk_tokens, sub_tokens):
    n_chunks, chunk_rows, _ = x_tm.shape
    n_tiles = row_idx.shape[0]
    _, ne, d, ff = wg.shape
    kernel = functools.partial(_moe_routed_kernel, alpha=alpha, chunk_tokens=chunk_tokens, sub_tokens=sub_tokens)

    def tile_map(i, ce, fl, used):
        return (jnp.minimum(i, used[0] - 1), 0, 0)

    def chunk_map(i, ce, fl, used):
        return (ce[i] // ne, 0, 0)

    def expert_map(i, ce, fl, used):
        return (layer, ce[i] % ne, 0, 0)

    def const2(i, ce, fl, used):
        return (0, 0)

    grid_spec = pltpu.PrefetchScalarGridSpec(
        num_scalar_prefetch=3,
        grid=(n_tiles,),
        in_specs=[
            pl.BlockSpec((1, 3, ROW_TILE), tile_map, memory_space=pltpu.SMEM),
            pl.BlockSpec((1, 1, ROW_TILE), tile_map),
            pl.BlockSpec((None, chunk_rows, LANES), chunk_map, pipeline_mode=pl.Buffered(1)),
            pl.BlockSpec((None, 1, d, ff), expert_map),
            pl.BlockSpec((None, 1, d, ff), expert_map),
            pl.BlockSpec((None, 1, ff, d), expert_map),
            pl.BlockSpec(sg.shape, const2, pipeline_mode=pl.Buffered(1)),
            pl.BlockSpec(su.shape, const2, pipeline_mode=pl.Buffered(1)),
            pl.BlockSpec(sd.shape, const2, pipeline_mode=pl.Buffered(1)),
            pl.BlockSpec(ln_g.shape, const2),
            pl.BlockSpec(ln_b.shape, const2),
        ],
        out_specs=pl.BlockSpec((None, chunk_rows, LANES), chunk_map, pipeline_mode=pl.Buffered(1)),
        scratch_shapes=[
            pltpu.VMEM((d, 2 * ff), BF16),
            pltpu.VMEM((ff, d), BF16),
            pltpu.VMEM((8 * TILE_PITCH, LANES), F32),
            pltpu.VMEM((8 * TILE_PITCH, LANES), F32),
            pltpu.VMEM((8 * TILE_PITCH, LANES), F32),
            pltpu.VMEM((8 * TILE_PITCH, LANES), F32),
        ],
    )
    return pl.pallas_call(
        kernel,
        grid_spec=grid_spec,
        out_shape=jax.ShapeDtypeStruct(x_tm.shape, F32),
        compiler_params=pltpu.CompilerParams(dimension_semantics=("arbitrary",), vmem_limit_bytes=VMEM_LIMIT),
        name="moe_routed",
    )(tile_ce, tile_flags, n_used, row_idx, gates, x_tm, wg, wu, wd, sg, su, sd, ln_g, ln_b)


MOE_CHUNKS = 4
MOE_SUB_MAX = 512


def _moe_tiling(n_prompt, n_sample):
    assert n_prompt % (MOE_CHUNKS * 8) == 0 and n_sample % (MOE_CHUNKS * 8) == 0
    chunk_tokens = (n_prompt + n_sample) // MOE_CHUNKS
    assert (chunk_tokens + 1) * TOP_K <= 1 << PAIR_BITS
    sub_tokens = max(s for s in range(8, MOE_SUB_MAX + 1, 8) if chunk_tokens % s == 0)
    return MOE_CHUNKS, chunk_tokens, sub_tokens


def _chunked(prompt, sample, n_chunks):
    return jnp.concatenate([prompt.reshape((n_chunks, -1) + prompt.shape[1:]),
                            sample.reshape((n_chunks, -1) + sample.shape[1:])], axis=1)


def _route_plan(eidx_p, eidx_s, gate_p, gate_s, *, chunk_tokens, n_chunks):
    n_seg = n_chunks * N_EXPERTS
    eidx = _chunked(eidx_p.T, eidx_s.T, n_chunks)
    gate = _chunked(gate_p.T, gate_s.T, n_chunks)
    seg = jnp.arange(n_chunks, dtype=jnp.int32)[:, None, None] * N_EXPERTS + eidx
    pair = jnp.arange(chunk_tokens * TOP_K, dtype=jnp.int32).reshape(1, chunk_tokens, TOP_K)
    keys_real = ((seg << PAIR_BITS) | pair).reshape(-1)
    counts = jnp.sum(eidx[..., None] == jnp.arange(N_EXPERTS, dtype=jnp.int32), axis=(1, 2),
                     dtype=jnp.int32).reshape(n_seg)
    n_pad = (-counts) % ROW_TILE
    slot = jnp.arange(ROW_TILE - 1, dtype=jnp.int32)
    int_max = jnp.iinfo(jnp.int32).max
    pad_pair = (1 << PAIR_BITS) - 1
    keys_pad = jnp.where(slot[None, :] < n_pad[:, None],
                         (jnp.arange(n_seg, dtype=jnp.int32)[:, None] << PAIR_BITS) | pad_pair, int_max)
    keys = jnp.concatenate([keys_real, keys_pad.reshape(-1)])
    vals = jnp.concatenate([gate.reshape(-1), jnp.zeros((keys_pad.size,), F32)])
    assert keys.size % ROW_TILE == 0
    n_tiles = keys.size // ROW_TILE
    keys, vals = lax.sort((keys, vals), num_keys=1)
    keys = keys.reshape(n_tiles, ROW_TILE)
    head = keys[:, 0]
    valid = head != int_max
    tile_ce = jnp.minimum(head >> PAIR_BITS, n_seg - 1)
    tile_c = tile_ce // N_EXPERTS
    prev_ce = jnp.concatenate([jnp.full((1,), -N_EXPERTS, jnp.int32), tile_ce[:-1]])
    next_c = jnp.concatenate([tile_c[1:], jnp.full((1,), -1, jnp.int32)])
    next_valid = jnp.concatenate([valid[1:], jnp.zeros((1,), jnp.bool_)])
    first = valid & (tile_c != prev_ce // N_EXPERTS)
    last = valid & (~next_valid | (next_c != tile_c))
    new_expert = valid & (tile_ce != prev_ce)
    flags = (first * FLAG_FIRST + last * FLAG_LAST + new_expert * FLAG_NEW_EXPERT + valid * FLAG_VALID)
    row_idx = jnp.minimum((keys & pad_pair) // TOP_K, chunk_tokens) * 8
    row_idx3 = jnp.stack([jnp.concatenate([row_idx[:1], row_idx[:-1]]), row_idx,
                          jnp.concatenate([row_idx[1:], row_idx[-1:]])], axis=1)
    n_used = jnp.sum(valid, dtype=jnp.int32).reshape(1)
    return tile_ce, flags.astype(jnp.int32), n_used, row_idx3, vals.reshape(n_tiles, 1, ROW_TILE)


def _t5_bucket(dist):
    n = jnp.maximum(dist, 0)
    max_exact = N_BUCKETS // 2
    large = max_exact + (jnp.log(jnp.maximum(n, 1).astype(F32) / max_exact)
                         / math.log(MAX_DISTANCE / max_exact) * (N_BUCKETS - max_exact)).astype(jnp.int32)
    large = jnp.minimum(large, N_BUCKETS - 1)
    return jnp.where(n < max_exact, n, large)


def _bias_lookup(rel_bias, bucket):
    onehot = (bucket[..., None] == jnp.arange(N_BUCKETS, dtype=bucket.dtype)).astype(F32)
    return jnp.einsum("...b,bh->h...", onehot, rel_bias.astype(F32), precision=lax.Precision.HIGHEST)


def _bias_tables(rel_bias, win):
    qi = jnp.arange(ATTN_BLOCK)[:, None]
    ki = jnp.arange(2 * ATTN_BLOCK)[None, :]
    dist = qi + ATTN_BLOCK - ki
    valid = (dist >= 0) & (dist < WINDOW)
    bias = _bias_lookup(rel_bias, _t5_bucket(dist))
    bias = jnp.where(valid[None], bias, NEG).reshape(N_KV_HEADS, GROUP, ATTN_BLOCK, 2 * ATTN_BLOCK)
    bias_ab = jnp.stack([jnp.concatenate([bias[:, t], bias[:, t + 2]], axis=1) for t in range(2)], axis=1)
    dist_s = (win - 1) - jnp.arange(win)
    bias_s = _bias_lookup(rel_bias, _t5_bucket(dist_s))
    return bias_ab, bias_s


def _sink_tables(sink):
    s = sink.astype(F32).reshape(N_KV_HEADS, GROUP)
    rows = [jnp.concatenate([jnp.broadcast_to(s[:, t, None], (N_KV_HEADS, ATTN_BLOCK)),
                             jnp.broadcast_to(s[:, t + 2, None], (N_KV_HEADS, ATTN_BLOCK))], axis=1)
            for t in range(2)]
    return jnp.stack(rows, axis=1)[..., None], sink.astype(F32)[:, None]


def kernel(x_prompt, x_sample, cache_k_win, cache_v_win, state_conv, rel_bias, w_in, attn_sink, conv_w,
           w_attn_out, w_conv_out, w_out, ln1_g, ln1_b, router_w, router_bias, exp_w_gate, exp_w_up,
           exp_w_down, shared_w_gate, shared_w_up, shared_w_down, ln2_g, ln2_b):
    depth = w_in.shape[0]
    alpha = (2 * depth) ** 0.25
    nb, seq, d = x_prompt.shape
    nd = x_sample.shape[0]
    win = cache_k_win.shape[2]
    assert x_sample.shape[1] == 1 and win == WINDOW and seq % 512 == 0

    n_prompt = nb * seq
    n_chunks, chunk_tokens, sub_tokens = _moe_tiling(n_prompt, nd)

    bias_ab, bias_s = _bias_tables(rel_bias, win)
    hmask = (jnp.arange(KV_DIM)[None, :] // HEAD_DIM == jnp.arange(N_HEADS)[:, None] // GROUP).astype(F32)

    yp = x_prompt
    ys = x_sample.reshape(nd, d)
    outs = [[] for _ in range(6)]
    for l in range(depth):
        w_in_b = w_in[l].astype(BF16)
        w_ao_b = w_attn_out[l].astype(BF16)
        w_co_b = w_conv_out[l].astype(BF16)
        w_o_b = w_out[l].astype(BF16)
        g1, b1 = ln1_g[l][None, :], ln1_b[l][None, :]
        g2, b2 = ln2_g[l][None, :], ln2_b[l][None, :]
        sink_ab, sink_col = _sink_tables(attn_sink[l])

        yp, kp, vp, cp = _mixer_prompt(yp, w_in_b, w_attn_out[l].T.astype(BF16), w_co_b, w_o_b, conv_w[l],
                                       jnp.swapaxes(bias_ab, -1, -2), jnp.swapaxes(sink_ab, -1, -2), g1, b1,
                                       alpha=alpha, tq=512)

        proj = _sample_proj(ys, w_in_b, tn=IN_DIM // 4)
        q4 = proj[:, :Q_DIM].reshape(nd, N_KV_HEADS, GROUP, HEAD_DIM).transpose(0, 2, 1, 3).reshape(nd, GROUP, KV_DIM)
        ksn, vsn, ag = _sample_attn(q4, proj[:, OFF_K:OFF_V], proj[:, OFF_V:OFF_B],
                                    cache_k_win.reshape(depth, nd, win, KV_DIM),
                                    cache_v_win.reshape(depth, nd, win, KV_DIM),
                                    bias_s, sink_col, hmask, layer=l, bt=8)
        att = ag.reshape(nd, GROUP, N_KV_HEADS, HEAD_DIM).transpose(0, 2, 1, 3).reshape(nd, Q_DIM)
        state_t = jnp.swapaxes(state_conv[l], 0, 1)
        ys, us = _sample_post(ys, att, proj, state_t, conv_w[l], w_ao_b, w_co_b, w_o_b, g1, b1, alpha=alpha)

        outs[0].append(kp.reshape(nb, WINDOW, N_KV_HEADS, HEAD_DIM))
        outs[1].append(vp.reshape(nb, WINDOW, N_KV_HEADS, HEAD_DIM))
        outs[2].append(cp)
        outs[3].append(ksn.reshape(nd, win, N_KV_HEADS, HEAD_DIM))
        outs[4].append(vsn.reshape(nd, win, N_KV_HEADS, HEAD_DIM))
        outs[5].append(jnp.concatenate([state_conv[l][:, 1:], us[:, None, :]], axis=1))

        rw_t = router_w[l].T.astype(BF16)
        rb_col = router_bias[l].astype(F32)[:, None]
        sg, su, sd = (shared_w_gate[l].astype(BF16), shared_w_up[l].astype(BF16), shared_w_down[l].astype(BF16))
        xp = yp.reshape(n_prompt, d)
        ep, wp = _router(xp, rw_t, rb_col, tm=512)
        es, ws = _router(ys, rw_t, rb_col, tm=nd)
        plan = _route_plan(ep, es, wp, ws, chunk_tokens=chunk_tokens, n_chunks=n_chunks)
        xt = jnp.concatenate([_chunked(xp, ys, n_chunks), jnp.zeros((n_chunks, SPARE_TOKENS, d), F32)], axis=1)
        out = _moe_routed(*plan, xt.reshape(n_chunks, -1, LANES), exp_w_gate, exp_w_up, exp_w_down, sg, su, sd,
                          g2, b2, layer=l, alpha=alpha, chunk_tokens=chunk_tokens, sub_tokens=sub_tokens)
        out = out.reshape(n_chunks, chunk_tokens + SPARE_TOKENS, d)
        yp = out[:, :n_prompt // n_chunks].reshape(nb, seq, d)
        ys = out[:, n_prompt // n_chunks:chunk_tokens].reshape(nd, d)

    return (yp, ys.reshape(nd, 1, d)) + tuple(jnp.stack(o) for o in outs)
```

```python
import functools
import math

import jax
import jax.numpy as jnp
from jax import lax
from jax.experimental import pallas as pl
from jax.experimental.pallas import tpu as pltpu

D_MODEL = 1024
N_HEADS = 16
N_KV_HEADS = 4
HEAD_DIM = 64
GROUP = N_HEADS // N_KV_HEADS
WINDOW = 128
ATTN_BLOCK = 128
N_BUCKETS = 32
MAX_DISTANCE = 128
CONV_DIM = 1024
CONV_K = 3
N_EXPERTS = 64
TOP_K = 8
N_GROUPS = 8
TOPK_GROUPS = 4
GROUP_SIZE = N_EXPERTS // N_GROUPS
EXPERT_FF = 256
ROUTED_SCALE = 2.5
LN_EPS = 1e-5
NEG = -1e30

Q_DIM = N_HEADS * HEAD_DIM
KV_DIM = N_KV_HEADS * HEAD_DIM
OFF_K = Q_DIM
OFF_V = OFF_K + KV_DIM
OFF_B = OFF_V + KV_DIM
OFF_C = OFF_B + CONV_DIM
OFF_H = OFF_C + CONV_DIM
OFF_GA = OFF_H + CONV_DIM
OFF_GB = OFF_GA + D_MODEL
IN_DIM = OFF_GB + D_MODEL

LANES = 128
CONV_PAD = 8
VMEM_LIMIT = 60 * 1024 * 1024

BF16 = jnp.bfloat16
F32 = jnp.float32


def _dot(a, b):
    return jnp.dot(a, b, preferred_element_type=F32)


def _dot_nt(a, b):
    return lax.dot_general(a, b, (((1,), (1,)), ((), ())), preferred_element_type=F32)


def _layer_norm(z, g, b):
    mu = jnp.mean(z, axis=-1, keepdims=True)
    d = z - mu
    var = jnp.mean(d * d, axis=-1, keepdims=True)
    return d * lax.rsqrt(var + LN_EPS) * g + b


def _sink_softmax(s, sink, axis=-1):
    m = jnp.maximum(jnp.max(s, axis=axis, keepdims=True), sink)
    e = jnp.exp(s - m)
    den = jnp.sum(e, axis=axis, keepdims=True) + jnp.exp(sink - m)
    return e * (1.0 / den)


SLAB_ROWS = D_MODEL // LANES


def _slab_load(ref, tokens, base=0):
    return jnp.concatenate([ref[pl.ds(base + j, tokens, stride=SLAB_ROWS), :] for j in range(SLAB_ROWS)], axis=1)


def _slab_store(ref, val, base=0):
    tokens = val.shape[0]
    for j in range(SLAB_ROWS):
        ref[pl.ds(base + j, tokens, stride=SLAB_ROWS), :] = val[:, j * LANES:(j + 1) * LANES]


def _merge_project(x, attn_o, y_conv, g_a, g_b, w_co_ref, w_o_ref, lng_ref, lnb_ref, alpha):
    merged = jax.nn.sigmoid(g_a) * attn_o + jax.nn.sigmoid(g_b) * _dot(y_conv.astype(BF16), w_co_ref[...])
    out = _dot(merged.astype(BF16), w_o_ref[...])
    return _layer_norm(alpha * x + out, lng_ref[...], lnb_ref[...])


def _mixer_prompt_kernel(x_ref, w_in_ref, w_ao_ref, w_co_ref, w_o_ref, convw_ref, bias_ref, sink_ref,
                         lng_ref, lnb_ref,
                         x1_ref, kwin_ref, vwin_ref, conv_ref,
                         ka_ref, kb_ref, vat_ref, vbt_ref, s_ref, p_ref, att_ref, ubuf_ref, *, alpha, tq, slab_in):
    i = pl.program_id(1)
    nblk = tq // ATTN_BLOCK
    half = LANES // 2
    scale = HEAD_DIM ** -0.5
    assert math.frexp(scale)[0] == 0.5

    @pl.when(i == 0)
    def _init():
        for ref in (ka_ref, kb_ref):
            ref[:, 0:ATTN_BLOCK, :] = jnp.zeros((N_KV_HEADS, ATTN_BLOCK, LANES), BF16)
        for ref in (vat_ref, vbt_ref):
            ref[:, :, 0:ATTN_BLOCK] = jnp.zeros((N_KV_HEADS, LANES, ATTN_BLOCK), BF16)
        ubuf_ref[0:CONV_PAD, :] = jnp.zeros((CONV_PAD, CONV_DIM), F32)

    x = _slab_load(x_ref, tq) if slab_in else x_ref[0]
    xb = x.astype(BF16)
    qkv = _dot(xb, w_in_ref[:, 0:OFF_B])

    lo = lax.broadcasted_iota(jnp.int32, (tq, LANES), 1) < half
    zeros_t = jnp.zeros((half, tq), BF16)
    for c in range(N_KV_HEADS // 2):
        chunk = qkv[:, OFF_K + c * LANES: OFF_K + (c + 1) * LANES]
        c_lo = jnp.where(lo, chunk, 0.0)
        c_hi = jnp.where(lo, 0.0, chunk)
        ka_ref[2 * c, ATTN_BLOCK:, :] = c_lo.astype(BF16)
        kb_ref[2 * c, ATTN_BLOCK:, :] = pltpu.roll(c_lo, half, 1).astype(BF16)
        kb_ref[2 * c + 1, ATTN_BLOCK:, :] = c_hi.astype(BF16)
        ka_ref[2 * c + 1, ATTN_BLOCK:, :] = pltpu.roll(c_hi, half, 1).astype(BF16)
        vt = qkv[:, OFF_V + c * LANES: OFF_V + (c + 1) * LANES].T.astype(BF16)
        vat_ref[2 * c, :, ATTN_BLOCK:] = jnp.concatenate([vt[0:half], zeros_t], axis=0)
        vbt_ref[2 * c, :, ATTN_BLOCK:] = jnp.concatenate([zeros_t, vt[0:half]], axis=0)
        vbt_ref[2 * c + 1, :, ATTN_BLOCK:] = jnp.concatenate([zeros_t, vt[half:]], axis=0)
        vat_ref[2 * c + 1, :, ATTN_BLOCK:] = jnp.concatenate([vt[half:], zeros_t], axis=0)

    key_row = lax.broadcasted_iota(jnp.int32, (2 * ATTN_BLOCK, 2 * ATTN_BLOCK), 0)
    for j in range(nblk):
        rows = slice(j * ATTN_BLOCK, (j + 1) * ATTN_BLOCK)
        keys = slice(j * ATTN_BLOCK, (j + 2) * ATTN_BLOCK)
        for h in range(N_KV_HEADS):
            q0 = h * GROUP * HEAD_DIM
            q2 = (jnp.concatenate([qkv[rows, q0:q0 + LANES], qkv[rows, q0 + LANES:q0 + 2 * LANES]], axis=0)
                  * scale).astype(BF16)
            for t, k_ref in enumerate((ka_ref, kb_ref)):
                s = _dot_nt(k_ref[h, keys, :], q2)
                if j == 0:
                    s = jnp.where(jnp.logical_and(i == 0, key_row < ATTN_BLOCK), NEG, s)
                s_ref[j, h, t] = s
    for h in range(N_KV_HEADS):
        for t in range(2):
            s = s_ref[:, h, t] + bias_ref[h, t][None]
            p_ref[:, h, t] = _sink_softmax(s, sink_ref[h, t][None], axis=1).astype(BF16)
    for j in range(nblk):
        rows = slice(j * ATTN_BLOCK, (j + 1) * ATTN_BLOCK)
        keys = slice(j * ATTN_BLOCK, (j + 2) * ATTN_BLOCK)
        for h in range(N_KV_HEADS):
            q0 = h * GROUP * HEAD_DIM
            o_t = _dot(vat_ref[h, :, keys], p_ref[j, h, 0]) + _dot(vbt_ref[h, :, keys], p_ref[j, h, 1])
            att_ref[q0:q0 + LANES, rows] = o_t[:, 0:ATTN_BLOCK].astype(BF16)
            att_ref[q0 + LANES:q0 + 2 * LANES, rows] = o_t[:, ATTN_BLOCK:].astype(BF16)
    for ref in (ka_ref, kb_ref):
        ref[:, 0:ATTN_BLOCK, :] = ref[:, tq:tq + ATTN_BLOCK, :]
    for ref in (vat_ref, vbt_ref):
        ref[:, :, 0:ATTN_BLOCK] = ref[:, :, tq:tq + ATTN_BLOCK]

    kwin_ref[0] = qkv[tq - WINDOW:tq, OFF_K:OFF_V]
    vwin_ref[0] = qkv[tq - WINDOW:tq, OFF_V:OFF_B]

    attn_o = _dot(w_ao_ref[...], att_ref[...]).T

    bch = _dot(xb, w_in_ref[:, OFF_B:OFF_GA])
    u = bch[:, CONV_DIM:2 * CONV_DIM] * bch[:, 2 * CONV_DIM:3 * CONV_DIM]
    ubuf_ref[CONV_PAD:CONV_PAD + tq, :] = u
    cw = convw_ref[...]
    y = (cw[0:1] * ubuf_ref[CONV_PAD - 2:CONV_PAD - 2 + tq, :]
         + cw[1:2] * ubuf_ref[CONV_PAD - 1:CONV_PAD - 1 + tq, :]
         + cw[2:3] * u)
    conv_ref[0] = ubuf_ref[CONV_PAD + tq - (CONV_K - 1):CONV_PAD + tq, :]
    ubuf_ref[0:CONV_PAD, :] = ubuf_ref[tq:tq + CONV_PAD, :]
    y_conv = bch[:, 0:CONV_DIM] * y

    gab = _dot(xb, w_in_ref[:, OFF_GA:IN_DIM])
    _slab_store(x1_ref, _merge_project(x, attn_o, y_conv, gab[:, 0:D_MODEL], gab[:, D_MODEL:], w_co_ref, w_o_ref,
                                       lng_ref, lnb_ref, alpha))


def _const_spec(shape):
    nd = len(shape)
    return pl.BlockSpec(shape, lambda *_: (0,) * nd, pipeline_mode=pl.Buffered(1))


def _mixer_prompt(x, w_in, w_ao, w_co, w_o, conv_w, bias_ab, sink_ab, ln_g, ln_b, *, alpha, tq, batch, seq,
                  slab_shape):
    b, s = batch, seq
    n_chunks, chunk_rows, _ = slab_shape
    slab_in = x.ndim == 3 and x.shape == slab_shape
    steps_per_chunk = (b * s // n_chunks) // tq
    assert (b * s) % (n_chunks * tq) == 0 and s % tq == 0

    def slab_map(bi, i):
        g = bi * (s // tq) + i
        return (g // steps_per_chunk, g % steps_per_chunk, 0)

    slab_spec = pl.BlockSpec((None, tq * SLAB_ROWS, LANES), slab_map)
    kernel = functools.partial(_mixer_prompt_kernel, alpha=alpha, tq=tq, slab_in=slab_in)
    return pl.pallas_call(
        kernel,
        grid=(b, s // tq),
        in_specs=[
            slab_spec if slab_in else pl.BlockSpec((1, tq, D_MODEL), lambda bi, i: (bi, i, 0)),
            _const_spec(w_in.shape), _const_spec(w_ao.shape), _const_spec(w_co.shape), _const_spec(w_o.shape),
            _const_spec(conv_w.shape), _const_spec(bias_ab.shape), _const_spec(sink_ab.shape),
            _const_spec(ln_g.shape), _const_spec(ln_b.shape),
        ],
        out_specs=[
            slab_spec,
            pl.BlockSpec((1, WINDOW, KV_DIM), lambda bi, i: (bi, 0, 0)),
            pl.BlockSpec((1, WINDOW, KV_DIM), lambda bi, i: (bi, 0, 0)),
            pl.BlockSpec((1, CONV_K - 1, CONV_DIM), lambda bi, i: (bi, 0, 0)),
        ],
        out_shape=[
            jax.ShapeDtypeStruct(slab_shape, F32),
            jax.ShapeDtypeStruct((b, WINDOW, KV_DIM), F32),
            jax.ShapeDtypeStruct((b, WINDOW, KV_DIM), F32),
            jax.ShapeDtypeStruct((b, CONV_K - 1, CONV_DIM), F32),
        ],
        scratch_shapes=[
            pltpu.VMEM((N_KV_HEADS, ATTN_BLOCK + tq, LANES), BF16),
            pltpu.VMEM((N_KV_HEADS, ATTN_BLOCK + tq, LANES), BF16),
            pltpu.VMEM((N_KV_HEADS, LANES, ATTN_BLOCK + tq), BF16),
            pltpu.VMEM((N_KV_HEADS, LANES, ATTN_BLOCK + tq), BF16),
            pltpu.VMEM((tq // ATTN_BLOCK, N_KV_HEADS, 2, 2 * ATTN_BLOCK, 2 * ATTN_BLOCK), F32),
            pltpu.VMEM((tq // ATTN_BLOCK, N_KV_HEADS, 2, 2 * ATTN_BLOCK, 2 * ATTN_BLOCK), BF16),
            pltpu.VMEM((Q_DIM, tq), BF16),
            pltpu.VMEM((tq + CONV_PAD, CONV_DIM), F32),
        ],
        compiler_params=pltpu.CompilerParams(
            dimension_semantics=("arbitrary", "arbitrary"), vmem_limit_bytes=VMEM_LIMIT),
        name="mixer_prompt",
    )(x, w_in, w_ao, w_co, w_o, conv_w, bias_ab, sink_ab, ln_g, ln_b)


def _proj_kernel(x_ref, w_ref, o_ref):
    o_ref[...] = _dot(x_ref[...].astype(BF16), w_ref[...])


def _sample_proj(x, w_in, *, tn):
    m, d = x.shape
    n = w_in.shape[1]
    return pl.pallas_call(
        _proj_kernel,
        grid=(n // tn,),
        in_specs=[pl.BlockSpec((m, d), lambda j: (0, 0)), pl.BlockSpec((d, tn), lambda j: (0, j))],
        out_specs=pl.BlockSpec((m, tn), lambda j: (0, j)),
        out_shape=jax.ShapeDtypeStruct((m, n), F32),
        compiler_params=pltpu.CompilerParams(dimension_semantics=("arbitrary",), vmem_limit_bytes=VMEM_LIMIT),
        name="sample_proj",
    )(x, w_in)


def _sample_attn_kernel(q4_ref, knew_ref, vnew_ref, ck_ref, cv_ref, bias_ref, sink_ref, hmask_ref,
                        nk_ref, nv_ref, ag_ref):
    bt = ck_ref.shape[0]
    win = ck_ref.shape[1]
    scale = HEAD_DIM ** -0.5
    row = lax.broadcasted_iota(jnp.int32, (win, KV_DIM), 0)
    last = row == win - 1
    hmask = hmask_ref[...]
    for b in range(bt):
        kb = jnp.where(last, knew_ref[b:b + 1, :], pltpu.roll(ck_ref[b], win - 1, 0))
        vb = jnp.where(last, vnew_ref[b:b + 1, :], pltpu.roll(cv_ref[b], win - 1, 0))
        nk_ref[b] = kb
        nv_ref[b] = vb
        q4 = q4_ref[b]
        qm = (jnp.concatenate([q4] * N_KV_HEADS, axis=0) * hmask).astype(BF16)
        s = _dot_nt(qm, kb.astype(BF16)) * scale + bias_ref[...]
        p = _sink_softmax(s, sink_ref[...]).astype(BF16)
        o = _dot(p, vb.astype(BF16)) * hmask
        o4 = o[0:GROUP]
        for h in range(1, N_KV_HEADS):
            o4 = o4 + o[h * GROUP:(h + 1) * GROUP]
        ag_ref[b] = o4


def _sample_attn(q4, k_new, v_new, cache_k, cache_v, bias_s, sink_col, hmask, *, layer, bt):
    _, nb, win, kvd = cache_k.shape
    return pl.pallas_call(
        _sample_attn_kernel,
        grid=(nb // bt,),
        in_specs=[
            pl.BlockSpec((bt, GROUP, kvd), lambda i: (i, 0, 0)),
            pl.BlockSpec((bt, kvd), lambda i: (i, 0)),
            pl.BlockSpec((bt, kvd), lambda i: (i, 0)),
            pl.BlockSpec((None, bt, win, kvd), lambda i: (layer, i, 0, 0)),
            pl.BlockSpec((None, bt, win, kvd), lambda i: (layer, i, 0, 0)),
            pl.BlockSpec(bias_s.shape, lambda i: (0, 0)),
            pl.BlockSpec(sink_col.shape, lambda i: (0, 0)),
            pl.BlockSpec(hmask.shape, lambda i: (0, 0)),
        ],
        out_specs=[
            pl.BlockSpec((bt, win, kvd), lambda i: (i, 0, 0)),
            pl.BlockSpec((bt, win, kvd), lambda i: (i, 0, 0)),
            pl.BlockSpec((bt, GROUP, kvd), lambda i: (i, 0, 0)),
        ],
        out_shape=[
            jax.ShapeDtypeStruct((nb, win, kvd), F32),
            jax.ShapeDtypeStruct((nb, win, kvd), F32),
            jax.ShapeDtypeStruct((nb, GROUP, kvd), F32),
        ],
        compiler_params=pltpu.CompilerParams(dimension_semantics=("arbitrary",), vmem_limit_bytes=VMEM_LIMIT),
        name="sample_attn",
    )(q4, k_new, v_new, cache_k, cache_v, bias_s, sink_col, hmask)


def _sample_post_kernel(x_ref, att_ref, proj_ref, st_ref, convw_ref, w_ao_ref, w_co_ref, w_o_ref,
                        lng_ref, lnb_ref, x1_ref, u_ref, *, alpha):
    attn_o = _dot(att_ref[...].astype(BF16), w_ao_ref[...])
    u = proj_ref[:, OFF_C:OFF_H] * proj_ref[:, OFF_H:OFF_GA]
    cw = convw_ref[...]
    y = cw[0:1] * st_ref[0] + cw[1:2] * st_ref[1] + cw[2:3] * u
    u_ref[...] = u
    y_conv = proj_ref[:, OFF_B:OFF_C] * y
    x1_ref[...] = _merge_project(x_ref[...], attn_o, y_conv, proj_ref[:, OFF_GA:OFF_GB], proj_ref[:, OFF_GB:IN_DIM],
                                 w_co_ref, w_o_ref, lng_ref, lnb_ref, alpha)


def _sample_post(x, att, proj, state, conv_w, w_ao, w_co, w_o, ln_g, ln_b, *, alpha):
    m, d = x.shape
    kernel = functools.partial(_sample_post_kernel, alpha=alpha)
    return pl.pallas_call(
        kernel,
        out_shape=[jax.ShapeDtypeStruct((m, d), F32), jax.ShapeDtypeStruct((m, CONV_DIM), F32)],
        compiler_params=pltpu.CompilerParams(vmem_limit_bytes=VMEM_LIMIT),
        name="sample_post",
    )(x, att, proj, state, conv_w, w_ao, w_co, w_o, ln_g, ln_b)


def _first_max(cur, ids, axes, big):
    m = cur
    for ax in axes:
        m = jnp.max(m, axis=ax, keepdims=True)
    idx = jnp.where(cur == m, ids, big)
    for ax in axes:
        idx = jnp.min(idx, axis=ax, keepdims=True)
    return m, idx


def _router_kernel(x_ref, rwt_ref, rb_ref, eidx_ref, gate_ref, *, tm, slab_in):
    x = _slab_load(x_ref, tm) if slab_in else x_ref[...]
    logits_t = _dot_nt(rwt_ref[...], x.astype(BF16))
    scores = jax.nn.sigmoid(logits_t)
    sel = scores + rb_ref[...]
    shape3 = (N_GROUPS, GROUP_SIZE, tm)
    sel3 = sel.reshape(shape3)
    scores3 = scores.reshape(shape3)
    member = lax.broadcasted_iota(jnp.int32, shape3, 1)
    m1, i1 = _first_max(sel3, member, (1,), GROUP_SIZE)
    m2 = jnp.max(jnp.where(member == i1, -jnp.inf, sel3), axis=1, keepdims=True)
    gscore = m1 + m2
    gid = lax.broadcasted_iota(jnp.int32, gscore.shape, 0)
    gsel = jnp.zeros(gscore.shape, jnp.bool_)
    for _ in range(TOPK_GROUPS):
        _, gi = _first_max(gscore, gid, (0,), N_GROUPS)
        hit = gid == gi
        gsel = jnp.logical_or(gsel, hit)
        gscore = jnp.where(hit, -jnp.inf, gscore)
    eid = lax.broadcasted_iota(jnp.int32, shape3, 0) * GROUP_SIZE + member
    cur = jnp.where(gsel, sel3, -jnp.inf)
    ids, ws = [], []
    for _ in range(TOP_K):
        _, ei = _first_max(cur, eid, (1, 0), N_EXPERTS)
        hit = eid == ei
        sc = jnp.sum(jnp.sum(jnp.where(hit, scores3, 0.0), axis=1, keepdims=True), axis=0, keepdims=True)
        ids.append(ei[0])
        ws.append(sc[0])
        cur = jnp.where(hit, -jnp.inf, cur)
    w = jnp.concatenate(ws, axis=0)
    tot = jnp.sum(w, axis=0, keepdims=True)
    eidx_ref[...] = jnp.concatenate(ids, axis=0)
    gate_ref[...] = w / tot * ROUTED_SCALE


def _router(x, rw_t, rb_col, *, tm, tokens=None):
    slab_in = x.ndim == 3
    if slab_in:
        t = tokens
        steps_per_chunk = (t // x.shape[0]) // tm
        assert t % (x.shape[0] * tm) == 0
        x_spec = pl.BlockSpec((None, tm * SLAB_ROWS, LANES), lambda i: (i // steps_per_chunk, i % steps_per_chunk, 0))
    else:
        t = x.shape[0]
        x_spec = pl.BlockSpec((tm, D_MODEL), lambda i: (i, 0))
    return pl.pallas_call(
        functools.partial(_router_kernel, tm=tm, slab_in=slab_in),
        grid=(t // tm,),
        in_specs=[
            x_spec,
            pl.BlockSpec(rw_t.shape, lambda i: (0, 0)),
            pl.BlockSpec(rb_col.shape, lambda i: (0, 0)),
        ],
        out_specs=[pl.BlockSpec((TOP_K, tm), lambda i: (0, i)), pl.BlockSpec((TOP_K, tm), lambda i: (0, i))],
        out_shape=[jax.ShapeDtypeStruct((TOP_K, t), jnp.int32), jax.ShapeDtypeStruct((TOP_K, t), F32)],
        compiler_params=pltpu.CompilerParams(dimension_semantics=("arbitrary",), vmem_limit_bytes=VMEM_LIMIT),
        name="router",
    )(x, rw_t, rb_col)


ROW_TILE = 256
TILE_PITCH = ROW_TILE + 1
SPARE_TOKENS = 8
PAIR_BITS = 16
FLAG_FIRST, FLAG_LAST, FLAG_NEW_EXPERT, FLAG_VALID = 1, 2, 4, 8
SCATTER_BATCH = 8


PREV, CUR, NEXT = 0, 1, 2


def _moe_routed_kernel(ce_ref, flags_ref, used_ref, idx_ref, gate_ref, x_ref, wg_ref, wu_ref,
                       wd_ref, sg_ref, su_ref, sd_ref, lng_ref, lnb_ref, o_ref, wgu_ref, wdb_ref,
                       gat0_ref, gat1_ref, res0_ref, res1_ref, *, alpha, chunk_tokens, sub_tokens):
    del ce_ref, used_ref
    step = pl.program_id(0)
    flags = flags_ref[step]
    odd = (step & 1) == 1
    nchunk = D_MODEL // LANES
    ff = wg_ref.shape[2]

    def slab_row(which, r):
        return pl.multiple_of(idx_ref[0, which, r], 8)

    def gather_row(which, gat_ref, r):
        gat_ref[pl.ds(r, nchunk, stride=TILE_PITCH), :] = x_ref[pl.ds(slab_row(which, r), 8), :]

    def scatter_rows(which, res_ref, rows):
        dst = [slab_row(which, r) for r in rows]
        acc = [o_ref[pl.ds(d, 8), :] + res_ref[pl.ds(r, nchunk, stride=TILE_PITCH), :] for d, r in zip(dst, rows)]
        for d, a in zip(dst, acc):
            o_ref[pl.ds(d, 8), :] = a

    def expert_mlp(gat_ref, res_ref):
        lhs = jnp.concatenate([gat_ref[j * TILE_PITCH:j * TILE_PITCH + ROW_TILE, :] for j in range(nchunk)],
                              axis=1).astype(BF16)
        h = _dot(lhs, wgu_ref[...])
        gate_col = jnp.broadcast_to(gate_ref[0], (LANES, ROW_TILE)).T
        hid = jax.nn.silu(h[:, 0:ff]) * h[:, ff:2 * ff] * jnp.concatenate([gate_col] * (ff // LANES), axis=1)
        y = _dot(hid.astype(BF16), wdb_ref[...])
        for j in range(nchunk):
            res_ref[j * TILE_PITCH:j * TILE_PITCH + ROW_TILE, :] = y[:, j * LANES:(j + 1) * LANES]

    def by_parity(fn):
        @pl.when(jnp.logical_not(odd))
        def _even():
            fn(gat0_ref, gat1_ref, res0_ref, res1_ref)

        @pl.when(odd)
        def _odd():
            fn(gat1_ref, gat0_ref, res1_ref, res0_ref)

    @pl.when((flags & FLAG_FIRST) != 0)
    def _start_chunk():
        o_ref[...] = jnp.zeros(o_ref.shape, F32)

        def start(gat_cur, gat_other, res_cur, res_other):
            res_other[...] = jnp.zeros(res_other.shape, F32)

            def body(r, carry):
                gather_row(CUR, gat_cur, r)
                return carry

            lax.fori_loop(0, ROW_TILE, body, 0)

        by_parity(start)

    @pl.when((flags & FLAG_NEW_EXPERT) != 0)
    def _cast_weights():
        wgu_ref[:, 0:ff] = wg_ref[0].astype(BF16)
        wgu_ref[:, ff:2 * ff] = wu_ref[0].astype(BF16)
        wdb_ref[...] = wd_ref[0].astype(BF16)

    @pl.when((flags & FLAG_VALID) != 0)
    def _tile():
        def main(gat_cur, gat_other, res_cur, res_other):
            for r0 in range(0, ROW_TILE, SCATTER_BATCH):
                scatter_rows(PREV, res_other, range(r0, r0 + SCATTER_BATCH))
            expert_mlp(gat_cur, res_cur)
            for r in range(ROW_TILE):
                gather_row(NEXT, gat_other, r)

        by_parity(main)

    @pl.when((flags & FLAG_LAST) != 0)
    def _finish():
        def flush(gat_cur, gat_other, res_cur, res_other):
            def body(r, carry):
                scatter_rows(CUR, res_cur, [r])
                return carry

            lax.fori_loop(0, ROW_TILE, body, 0)

        by_parity(flush)

        def body(s, carry):
            base = pl.multiple_of(s * (sub_tokens * 8), 8)

            def rows_2d(ref):
                return jnp.concatenate([ref[pl.ds(base + j, sub_tokens, stride=8), :] for j in range(nchunk)],
                                       axis=1)

            x2 = rows_2d(x_ref)
            xb = x2.astype(BF16)
            hs = jax.nn.silu(_dot(xb, sg_ref[...])) * _dot(xb, su_ref[...])
            ffn = rows_2d(o_ref) + _dot(hs.astype(BF16), sd_ref[...])
            res = _layer_norm(alpha * x2 + ffn, lng_ref[...], lnb_ref[...])
            for j in range(nchunk):
                o_ref[pl.ds(base + j, sub_tokens, stride=8), :] = res[:, j * LANES:(j + 1) * LANES]
            return carry

        lax.fori_loop(0, chunk_tokens // sub_tokens, body, 0)


def _moe_routed(tile_ce, tile_flags, n_used, row_idx, gates, x_tm, wg, wu, wd, sg, su, sd, ln_g, ln_b, *,
                layer, alpha, chunk_tokens, sub_tokens):
    n_chunks, chunk_rows, _ = x_tm.shape
    n_tiles = row_idx.shape[0]
    _, ne, d, ff = wg.shape
    kernel = functools.partial(_moe_routed_kernel, alpha=alpha, chunk_tokens=chunk_tokens, sub_tokens=sub_tokens)

    def tile_map(i, ce, fl, used):
        return (jnp.minimum(i, used[0] - 1), 0, 0)

    def chunk_map(i, ce, fl, used):
        return (ce[i] // ne, 0, 0)

    def expert_map(i, ce, fl, used):
        return (layer, ce[i] % ne, 0, 0)

    def const2(i, ce, fl, used):
        return (0, 0)

    grid_spec = pltpu.PrefetchScalarGridSpec(
        num_scalar_prefetch=3,
        grid=(n_tiles,),
        in_specs=[
            pl.BlockSpec((1, 3, ROW_TILE), tile_map, memory_space=pltpu.SMEM),
            pl.BlockSpec((1, 1, ROW_TILE), tile_map),
            pl.BlockSpec((None, chunk_rows, LANES), chunk_map, pipeline_mode=pl.Buffered(1)),
            pl.BlockSpec((None, 1, d, ff), expert_map),
            pl.BlockSpec((None, 1, d, ff), expert_map),
            pl.BlockSpec((None, 1, ff, d), expert_map),
            pl.BlockSpec(sg.shape, const2, pipeline_mode=pl.Buffered(1)),
            pl.BlockSpec(su.shape, const2, pipeline_mode=pl.Buffered(1)),
            pl.BlockSpec(sd.shape, const2, pipeline_mode=pl.Buffered(1)),
            pl.BlockSpec(ln_g.shape, const2),
            pl.BlockSpec(ln_b.shape, const2),
        ],
        out_specs=pl.BlockSpec((None, chunk_rows, LANES), chunk_map, pipeline_mode=pl.Buffered(1)),
        scratch_shapes=[
            pltpu.VMEM((d, 2 * ff), BF16),
            pltpu.VMEM((ff, d), BF16),
            pltpu.VMEM((8 * TILE_PITCH, LANES), F32),
            pltpu.VMEM((8 * TILE_PITCH, LANES), F32),
            pltpu.VMEM((8 * TILE_PITCH, LANES), F32),
            pltpu.VMEM((8 * TILE_PITCH, LANES), F32),
        ],
    )
    return pl.pallas_call(
        kernel,
        grid_spec=grid_spec,
        out_shape=jax.ShapeDtypeStruct(x_tm.shape, F32),
        compiler_params=pltpu.CompilerParams(dimension_semantics=("arbitrary",), vmem_limit_bytes=VMEM_LIMIT),
        name="moe_routed",
    )(tile_ce, tile_flags, n_used, row_idx, gates, x_tm, wg, wu, wd, sg, su, sd, ln_g, ln_b)


MOE_CHUNKS = 4
MOE_SUB_MAX = 512


def _moe_tiling(n_prompt, n_sample):
    assert n_prompt % (MOE_CHUNKS * 8) == 0 and n_sample % (MOE_CHUNKS * 8) == 0
    chunk_tokens = (n_prompt + n_sample) // MOE_CHUNKS
    assert (chunk_tokens + 1) * TOP_K <= 1 << PAIR_BITS
    sub_tokens = max(s for s in range(8, MOE_SUB_MAX + 1, 8) if chunk_tokens % s == 0)
    return MOE_CHUNKS, chunk_tokens, sub_tokens


def _chunked(prompt, sample, n_chunks):
    return jnp.concatenate([prompt.reshape((n_chunks, -1) + prompt.shape[1:]),
                            sample.reshape((n_chunks, -1) + sample.shape[1:])], axis=1)


def _route_plan(eidx_p, eidx_s, gate_p, gate_s, *, chunk_tokens, n_chunks):
    n_seg = n_chunks * N_EXPERTS
    eidx = _chunked(eidx_p.T, eidx_s.T, n_chunks)
    gate = _chunked(gate_p.T, gate_s.T, n_chunks)
    seg = jnp.arange(n_chunks, dtype=jnp.int32)[:, None, None] * N_EXPERTS + eidx
    pair = jnp.arange(chunk_tokens * TOP_K, dtype=jnp.int32).reshape(1, chunk_tokens, TOP_K)
    keys_real = ((seg << PAIR_BITS) | pair).reshape(-1)
    counts = jnp.sum(eidx[..., None] == jnp.arange(N_EXPERTS, dtype=jnp.int32), axis=(1, 2),
                     dtype=jnp.int32).reshape(n_seg)
    n_pad = (-counts) % ROW_TILE
    slot = jnp.arange(ROW_TILE - 1, dtype=jnp.int32)
    int_max = jnp.iinfo(jnp.int32).max
    pad_pair = (1 << PAIR_BITS) - 1
    keys_pad = jnp.where(slot[None, :] < n_pad[:, None],
                         (jnp.arange(n_seg, dtype=jnp.int32)[:, None] << PAIR_BITS) | pad_pair, int_max)
    keys = jnp.concatenate([keys_real, keys_pad.reshape(-1)])
    vals = jnp.concatenate([gate.reshape(-1), jnp.zeros((keys_pad.size,), F32)])
    assert keys.size % ROW_TILE == 0
    n_tiles = keys.size // ROW_TILE
    keys, vals = lax.sort((keys, vals), num_keys=1)
    keys = keys.reshape(n_tiles, ROW_TILE)
    head = keys[:, 0]
    valid = head != int_max
    tile_ce = jnp.minimum(head >> PAIR_BITS, n_seg - 1)
    tile_c = tile_ce // N_EXPERTS
    prev_ce = jnp.concatenate([jnp.full((1,), -N_EXPERTS, jnp.int32), tile_ce[:-1]])
    next_c = jnp.concatenate([tile_c[1:], jnp.full((1,), -1, jnp.int32)])
    next_valid = jnp.concatenate([valid[1:], jnp.zeros((1,), jnp.bool_)])
    first = valid & (tile_c != prev_ce // N_EXPERTS)
    last = valid & (~next_valid | (next_c != tile_c))
    new_expert = valid & (tile_ce != prev_ce)
    flags = (first * FLAG_FIRST + last * FLAG_LAST + new_expert * FLAG_NEW_EXPERT + valid * FLAG_VALID)
    row_idx = jnp.minimum((keys & pad_pair) // TOP_K, chunk_tokens) * 8
    row_idx3 = jnp.stack([jnp.concatenate([row_idx[:1], row_idx[:-1]]), row_idx,
                          jnp.concatenate([row_idx[1:], row_idx[-1:]])], axis=1)
    n_used = jnp.sum(valid, dtype=jnp.int32).reshape(1)
    return tile_ce, flags.astype(jnp.int32), n_used, row_idx3, vals.reshape(n_tiles, 1, ROW_TILE)


def _t5_bucket(dist):
    n = jnp.maximum(dist, 0)
    max_exact = N_BUCKETS // 2
    large = max_exact + (jnp.log(jnp.maximum(n, 1).astype(F32) / max_exact)
                         / math.log(MAX_DISTANCE / max_exact) * (N_BUCKETS - max_exact)).astype(jnp.int32)
    large = jnp.minimum(large, N_BUCKETS - 1)
    return jnp.where(n < max_exact, n, large)


def _bias_lookup(rel_bias, bucket):
    onehot = (bucket[..., None] == jnp.arange(N_BUCKETS, dtype=bucket.dtype)).astype(F32)
    return jnp.einsum("...b,bh->h...", onehot, rel_bias.astype(F32), precision=lax.Precision.HIGHEST)


def _bias_tables(rel_bias, win):
    qi = jnp.arange(ATTN_BLOCK)[:, None]
    ki = jnp.arange(2 * ATTN_BLOCK)[None, :]
    dist = qi + ATTN_BLOCK - ki
    valid = (dist >= 0) & (dist < WINDOW)
    bias = _bias_lookup(rel_bias, _t5_bucket(dist))
    bias = jnp.where(valid[None], bias, NEG).reshape(N_KV_HEADS, GROUP, ATTN_BLOCK, 2 * ATTN_BLOCK)
    bias_ab = jnp.stack([jnp.concatenate([bias[:, t], bias[:, t + 2]], axis=1) for t in range(2)], axis=1)
    dist_s = (win - 1) - jnp.arange(win)
    bias_s = _bias_lookup(rel_bias, _t5_bucket(dist_s))
    return bias_ab, bias_s


def _sink_tables(sink):
    s = sink.astype(F32).reshape(N_KV_HEADS, GROUP)
    rows = [jnp.concatenate([jnp.broadcast_to(s[:, t, None], (N_KV_HEADS, ATTN_BLOCK)),
                             jnp.broadcast_to(s[:, t + 2, None], (N_KV_HEADS, ATTN_BLOCK))], axis=1)
            for t in range(2)]
    return jnp.stack(rows, axis=1)[..., None], sink.astype(F32)[:, None]


def kernel(x_prompt, x_sample, cache_k_win, cache_v_win, state_conv, rel_bias, w_in, attn_sink, conv_w,
           w_attn_out, w_conv_out, w_out, ln1_g, ln1_b, router_w, router_bias, exp_w_gate, exp_w_up,
           exp_w_down, shared_w_gate, shared_w_up, shared_w_down, ln2_g, ln2_b):
    depth = w_in.shape[0]
    alpha = (2 * depth) ** 0.25
    nb, seq, d = x_prompt.shape
    nd = x_sample.shape[0]
    win = cache_k_win.shape[2]
    assert x_sample.shape[1] == 1 and win == WINDOW and seq % 512 == 0

    n_prompt = nb * seq
    n_chunks, chunk_tokens, sub_tokens = _moe_tiling(n_prompt, nd)
    prompt_rows = n_prompt // n_chunks * SLAB_ROWS
    slab_shape = (n_chunks, (chunk_tokens + SPARE_TOKENS) * SLAB_ROWS, LANES)

    bias_ab, bias_s = _bias_tables(rel_bias, win)
    hmask = (jnp.arange(KV_DIM)[None, :] // HEAD_DIM == jnp.arange(N_HEADS)[:, None] // GROUP).astype(F32)

    yp = x_prompt
    ys = x_sample.reshape(nd, d)
    outs = [[] for _ in range(6)]
    for l in range(depth):
        w_in_b = w_in[l].astype(BF16)
        w_ao_b = w_attn_out[l].astype(BF16)
        w_co_b = w_conv_out[l].astype(BF16)
        w_o_b = w_out[l].astype(BF16)
        g1, b1 = ln1_g[l][None, :], ln1_b[l][None, :]
        g2, b2 = ln2_g[l][None, :], ln2_b[l][None, :]
        sink_ab, sink_col = _sink_tables(attn_sink[l])

        slab, kp, vp, cp = _mixer_prompt(yp, w_in_b, w_attn_out[l].T.astype(BF16), w_co_b, w_o_b, conv_w[l],
                                         jnp.swapaxes(bias_ab, -1, -2), jnp.swapaxes(sink_ab, -1, -2), g1, b1,
                                         alpha=alpha, tq=512, batch=nb, seq=seq, slab_shape=slab_shape)

        proj = _sample_proj(ys, w_in_b, tn=IN_DIM // 4)
        q4 = proj[:, :Q_DIM].reshape(nd, N_KV_HEADS, GROUP, HEAD_DIM).transpose(0, 2, 1, 3).reshape(nd, GROUP, KV_DIM)
        ksn, vsn, ag = _sample_attn(q4, proj[:, OFF_K:OFF_V], proj[:, OFF_V:OFF_B],
                                    cache_k_win.reshape(depth, nd, win, KV_DIM),
                                    cache_v_win.reshape(depth, nd, win, KV_DIM),
                                    bias_s, sink_col, hmask, layer=l, bt=8)
        att = ag.reshape(nd, GROUP, N_KV_HEADS, HEAD_DIM).transpose(0, 2, 1, 3).reshape(nd, Q_DIM)
        state_t = jnp.swapaxes(state_conv[l], 0, 1)
        ys, us = _sample_post(ys, att, proj, state_t, conv_w[l], w_ao_b, w_co_b, w_o_b, g1, b1, alpha=alpha)

        outs[0].append(kp.reshape(nb, WINDOW, N_KV_HEADS, HEAD_DIM))
        outs[1].append(vp.reshape(nb, WINDOW, N_KV_HEADS, HEAD_DIM))
        outs[2].append(cp)
        outs[3].append(ksn.reshape(nd, win, N_KV_HEADS, HEAD_DIM))
        outs[4].append(vsn.reshape(nd, win, N_KV_HEADS, HEAD_DIM))
        outs[5].append(jnp.concatenate([state_conv[l][:, 1:], us[:, None, :]], axis=1))

        rw_t = router_w[l].T.astype(BF16)
        rb_col = router_bias[l].astype(F32)[:, None]
        sg, su, sd = (shared_w_gate[l].astype(BF16), shared_w_up[l].astype(BF16), shared_w_down[l].astype(BF16))
        tail = jnp.concatenate([ys.reshape(n_chunks, -1, LANES),
                                jnp.zeros((n_chunks, SPARE_TOKENS * SLAB_ROWS, LANES), F32)], axis=1)
        slab = lax.dynamic_update_slice(slab, tail, (0, prompt_rows, 0))
        ep, wp = _router(slab, rw_t, rb_col, tm=512, tokens=n_prompt)
        es, ws = _router(ys, rw_t, rb_col, tm=nd)
        plan = _route_plan(ep, es, wp, ws, chunk_tokens=chunk_tokens, n_chunks=n_chunks)
        yp = _moe_routed(*plan, slab, exp_w_gate, exp_w_up, exp_w_down, sg, su, sd, g2, b2,
                         layer=l, alpha=alpha, chunk_tokens=chunk_tokens, sub_tokens=sub_tokens)
        ys = yp[:, prompt_rows:chunk_tokens * SLAB_ROWS].reshape(nd, d)

    y_prompt = yp[:, :prompt_rows].reshape(nb, seq, d)
    return (y_prompt, ys.reshape(nd, 1, d)) + tuple(jnp.stack(o) for o in outs)
```

```python
import functools
import math

import jax
import jax.numpy as jnp
from jax import lax
from jax.experimental import pallas as pl
from jax.experimental.pallas import tpu as pltpu

D_MODEL = 1024
N_HEADS = 16
N_KV_HEADS = 4
HEAD_DIM = 64
GROUP = N_HEADS // N_KV_HEADS
WINDOW = 128
ATTN_BLOCK = 128
N_BUCKETS = 32
MAX_DISTANCE = 128
CONV_DIM = 1024
CONV_K = 3
N_EXPERTS = 64
TOP_K = 8
N_GROUPS = 8
TOPK_GROUPS = 4
GROUP_SIZE = N_EXPERTS // N_GROUPS
EXPERT_FF = 256
ROUTED_SCALE = 2.5
LN_EPS = 1e-5
NEG = -1e30

Q_DIM = N_HEADS * HEAD_DIM
KV_DIM = N_KV_HEADS * HEAD_DIM
OFF_K = Q_DIM
OFF_V = OFF_K + KV_DIM
OFF_B = OFF_V + KV_DIM
OFF_C = OFF_B + CONV_DIM
OFF_H = OFF_C + CONV_DIM
OFF_GA = OFF_H + CONV_DIM
OFF_GB = OFF_GA + D_MODEL
IN_DIM = OFF_GB + D_MODEL

LANES = 128
CONV_PAD = 8
VMEM_LIMIT = 60 * 1024 * 1024

BF16 = jnp.bfloat16
F32 = jnp.float32


def _dot(a, b):
    return jnp.dot(a, b, preferred_element_type=F32)


def _dot_nt(a, b):
    return lax.dot_general(a, b, (((1,), (1,)), ((), ())), preferred_element_type=F32)


def _layer_norm(z, g, b):
    mu = jnp.mean(z, axis=-1, keepdims=True)
    d = z - mu
    var = jnp.mean(d * d, axis=-1, keepdims=True)
    return d * lax.rsqrt(var + LN_EPS) * g + b


def _sink_softmax(s, sink, axis=-1):
    m = jnp.maximum(jnp.max(s, axis=axis, keepdims=True), sink)
    e = jnp.exp(s - m)
    den = jnp.sum(e, axis=axis, keepdims=True) + jnp.exp(sink - m)
    return e * (1.0 / den)


SLAB_ROWS = D_MODEL // LANES


def _slab_load(ref, tokens, base=0):
    return jnp.concatenate([ref[pl.ds(base + j, tokens, stride=SLAB_ROWS), :] for j in range(SLAB_ROWS)], axis=1)


def _slab_store(ref, val, base=0):
    tokens = val.shape[0]
    for j in range(SLAB_ROWS):
        ref[pl.ds(base + j, tokens, stride=SLAB_ROWS), :] = val[:, j * LANES:(j + 1) * LANES]


def _merge_project(x, attn_o, y_conv, g_a, g_b, w_co_ref, w_o_ref, lng_ref, lnb_ref, alpha):
    merged = jax.nn.sigmoid(g_a) * attn_o + jax.nn.sigmoid(g_b) * _dot(y_conv.astype(BF16), w_co_ref[...])
    out = _dot(merged.astype(BF16), w_o_ref[...])
    return _layer_norm(alpha * x + out, lng_ref[...], lnb_ref[...])


def _mixer_prompt_kernel(x_ref, w_in_ref, w_ao_ref, w_co_ref, w_o_ref, convw_ref, bias_ref, sink_ref,
                         lng_ref, lnb_ref, base_ref,
                         x1_ref, kwin_ref, vwin_ref, conv_ref,
                         ka_ref, kb_ref, vat_ref, vbt_ref, s_ref, p_ref, att_ref, ubuf_ref, *, alpha, tq, slab_in):
    del base_ref
    i = pl.program_id(1)
    nblk = tq // ATTN_BLOCK
    half = LANES // 2
    scale = HEAD_DIM ** -0.5
    assert math.frexp(scale)[0] == 0.5

    @pl.when(i == 0)
    def _init():
        for ref in (ka_ref, kb_ref):
            ref[:, 0:ATTN_BLOCK, :] = jnp.zeros((N_KV_HEADS, ATTN_BLOCK, LANES), BF16)
        for ref in (vat_ref, vbt_ref):
            ref[:, :, 0:ATTN_BLOCK] = jnp.zeros((N_KV_HEADS, LANES, ATTN_BLOCK), BF16)
        ubuf_ref[0:CONV_PAD, :] = jnp.zeros((CONV_PAD, CONV_DIM), F32)

    x = _slab_load(x_ref, tq) if slab_in else x_ref[0]
    xb = x.astype(BF16)
    qkv = _dot(xb, w_in_ref[:, 0:OFF_B])

    lo = lax.broadcasted_iota(jnp.int32, (tq, LANES), 1) < half
    zeros_t = jnp.zeros((half, tq), BF16)
    for c in range(N_KV_HEADS // 2):
        chunk = qkv[:, OFF_K + c * LANES: OFF_K + (c + 1) * LANES]
        c_lo = jnp.where(lo, chunk, 0.0)
        c_hi = jnp.where(lo, 0.0, chunk)
        ka_ref[2 * c, ATTN_BLOCK:, :] = c_lo.astype(BF16)
        kb_ref[2 * c, ATTN_BLOCK:, :] = pltpu.roll(c_lo, half, 1).astype(BF16)
        kb_ref[2 * c + 1, ATTN_BLOCK:, :] = c_hi.astype(BF16)
        ka_ref[2 * c + 1, ATTN_BLOCK:, :] = pltpu.roll(c_hi, half, 1).astype(BF16)
        vt = qkv[:, OFF_V + c * LANES: OFF_V + (c + 1) * LANES].T.astype(BF16)
        vat_ref[2 * c, :, ATTN_BLOCK:] = jnp.concatenate([vt[0:half], zeros_t], axis=0)
        vbt_ref[2 * c, :, ATTN_BLOCK:] = jnp.concatenate([zeros_t, vt[0:half]], axis=0)
        vbt_ref[2 * c + 1, :, ATTN_BLOCK:] = jnp.concatenate([zeros_t, vt[half:]], axis=0)
        vat_ref[2 * c + 1, :, ATTN_BLOCK:] = jnp.concatenate([vt[half:], zeros_t], axis=0)

    key_row = lax.broadcasted_iota(jnp.int32, (2 * ATTN_BLOCK, 2 * ATTN_BLOCK), 0)
    for j in range(nblk):
        rows = slice(j * ATTN_BLOCK, (j + 1) * ATTN_BLOCK)
        keys = slice(j * ATTN_BLOCK, (j + 2) * ATTN_BLOCK)
        for h in range(N_KV_HEADS):
            q0 = h * GROUP * HEAD_DIM
            q2 = (jnp.concatenate([qkv[rows, q0:q0 + LANES], qkv[rows, q0 + LANES:q0 + 2 * LANES]], axis=0)
                  * scale).astype(BF16)
            for t, k_ref in enumerate((ka_ref, kb_ref)):
                s = _dot_nt(k_ref[h, keys, :], q2)
                if j == 0:
                    s = jnp.where(jnp.logical_and(i == 0, key_row < ATTN_BLOCK), NEG, s)
                s_ref[j, h, t] = s
    for h in range(N_KV_HEADS):
        for t in range(2):
            s = s_ref[:, h, t] + bias_ref[h, t][None]
            p_ref[:, h, t] = _sink_softmax(s, sink_ref[h, t][None], axis=1).astype(BF16)
    for j in range(nblk):
        rows = slice(j * ATTN_BLOCK, (j + 1) * ATTN_BLOCK)
        keys = slice(j * ATTN_BLOCK, (j + 2) * ATTN_BLOCK)
        for h in range(N_KV_HEADS):
            q0 = h * GROUP * HEAD_DIM
            o_t = _dot(vat_ref[h, :, keys], p_ref[j, h, 0]) + _dot(vbt_ref[h, :, keys], p_ref[j, h, 1])
            att_ref[q0:q0 + LANES, rows] = o_t[:, 0:ATTN_BLOCK].astype(BF16)
            att_ref[q0 + LANES:q0 + 2 * LANES, rows] = o_t[:, ATTN_BLOCK:].astype(BF16)
    for ref in (ka_ref, kb_ref):
        ref[:, 0:ATTN_BLOCK, :] = ref[:, tq:tq + ATTN_BLOCK, :]
    for ref in (vat_ref, vbt_ref):
        ref[:, :, 0:ATTN_BLOCK] = ref[:, :, tq:tq + ATTN_BLOCK]

    kwin_ref[0] = qkv[tq - WINDOW:tq, OFF_K:OFF_V]
    vwin_ref[0] = qkv[tq - WINDOW:tq, OFF_V:OFF_B]

    attn_o = _dot(w_ao_ref[...], att_ref[...]).T

    bch = _dot(xb, w_in_ref[:, OFF_B:OFF_GA])
    u = bch[:, CONV_DIM:2 * CONV_DIM] * bch[:, 2 * CONV_DIM:3 * CONV_DIM]
    ubuf_ref[CONV_PAD:CONV_PAD + tq, :] = u
    cw = convw_ref[...]
    y = (cw[0:1] * ubuf_ref[CONV_PAD - 2:CONV_PAD - 2 + tq, :]
         + cw[1:2] * ubuf_ref[CONV_PAD - 1:CONV_PAD - 1 + tq, :]
         + cw[2:3] * u)
    conv_ref[0] = ubuf_ref[CONV_PAD + tq - (CONV_K - 1):CONV_PAD + tq, :]
    ubuf_ref[0:CONV_PAD, :] = ubuf_ref[tq:tq + CONV_PAD, :]
    y_conv = bch[:, 0:CONV_DIM] * y

    gab = _dot(xb, w_in_ref[:, OFF_GA:IN_DIM])
    _slab_store(x1_ref, _merge_project(x, attn_o, y_conv, gab[:, 0:D_MODEL], gab[:, D_MODEL:], w_co_ref, w_o_ref,
                                       lng_ref, lnb_ref, alpha))


def _const_spec(shape):
    nd = len(shape)
    return pl.BlockSpec(shape, lambda *_: (0,) * nd, pipeline_mode=pl.Buffered(1))


def _mixer_prompt(x, w_in, w_ao, w_co, w_o, conv_w, bias_ab, sink_ab, ln_g, ln_b, *, alpha, tq, batch, seq,
                  slab_shape, base=None):
    b, s = batch, seq
    n_chunks, chunk_rows, _ = slab_shape
    slab_in = x.ndim == 3 and x.shape == slab_shape
    steps_per_chunk = (b * s // n_chunks) // tq
    assert (b * s) % (n_chunks * tq) == 0 and s % tq == 0

    def slab_map(bi, i):
        g = bi * (s // tq) + i
        return (g // steps_per_chunk, g % steps_per_chunk, 0)

    slab_spec = pl.BlockSpec((None, tq * SLAB_ROWS, LANES), slab_map)
    kernel = functools.partial(_mixer_prompt_kernel, alpha=alpha, tq=tq, slab_in=slab_in)
    return pl.pallas_call(
        kernel,
        grid=(b, s // tq),
        in_specs=[
            slab_spec if slab_in else pl.BlockSpec((1, tq, D_MODEL), lambda bi, i: (bi, i, 0)),
            _const_spec(w_in.shape), _const_spec(w_ao.shape), _const_spec(w_co.shape), _const_spec(w_o.shape),
            _const_spec(conv_w.shape), _const_spec(bias_ab.shape), _const_spec(sink_ab.shape),
            _const_spec(ln_g.shape), _const_spec(ln_b.shape),
            pl.BlockSpec(memory_space=pl.ANY),
        ],
        out_specs=[
            slab_spec,
            pl.BlockSpec((1, WINDOW, KV_DIM), lambda bi, i: (bi, 0, 0)),
            pl.BlockSpec((1, WINDOW, KV_DIM), lambda bi, i: (bi, 0, 0)),
            pl.BlockSpec((1, CONV_K - 1, CONV_DIM), lambda bi, i: (bi, 0, 0)),
        ],
        out_shape=[
            jax.ShapeDtypeStruct(slab_shape, F32),
            jax.ShapeDtypeStruct((b, WINDOW, KV_DIM), F32),
            jax.ShapeDtypeStruct((b, WINDOW, KV_DIM), F32),
            jax.ShapeDtypeStruct((b, CONV_K - 1, CONV_DIM), F32),
        ],
        scratch_shapes=[
            pltpu.VMEM((N_KV_HEADS, ATTN_BLOCK + tq, LANES), BF16),
            pltpu.VMEM((N_KV_HEADS, ATTN_BLOCK + tq, LANES), BF16),
            pltpu.VMEM((N_KV_HEADS, LANES, ATTN_BLOCK + tq), BF16),
            pltpu.VMEM((N_KV_HEADS, LANES, ATTN_BLOCK + tq), BF16),
            pltpu.VMEM((tq // ATTN_BLOCK, N_KV_HEADS, 2, 2 * ATTN_BLOCK, 2 * ATTN_BLOCK), F32),
            pltpu.VMEM((tq // ATTN_BLOCK, N_KV_HEADS, 2, 2 * ATTN_BLOCK, 2 * ATTN_BLOCK), BF16),
            pltpu.VMEM((Q_DIM, tq), BF16),
            pltpu.VMEM((tq + CONV_PAD, CONV_DIM), F32),
        ],
        compiler_params=pltpu.CompilerParams(
            dimension_semantics=("arbitrary", "arbitrary"), vmem_limit_bytes=VMEM_LIMIT),
        input_output_aliases={0 if slab_in else 10: 0},
        name="mixer_prompt",
    )(x, w_in, w_ao, w_co, w_o, conv_w, bias_ab, sink_ab, ln_g, ln_b,
      jnp.zeros((1, SLAB_ROWS, LANES), F32) if slab_in else base)


def _proj_kernel(x_ref, w_ref, o_ref):
    o_ref[...] = _dot(x_ref[...].astype(BF16), w_ref[...])


def _sample_proj(x, w_in, *, tn):
    m, d = x.shape
    n = w_in.shape[1]
    return pl.pallas_call(
        _proj_kernel,
        grid=(n // tn,),
        in_specs=[pl.BlockSpec((m, d), lambda j: (0, 0)), pl.BlockSpec((d, tn), lambda j: (0, j))],
        out_specs=pl.BlockSpec((m, tn), lambda j: (0, j)),
        out_shape=jax.ShapeDtypeStruct((m, n), F32),
        compiler_params=pltpu.CompilerParams(dimension_semantics=("arbitrary",), vmem_limit_bytes=VMEM_LIMIT),
        name="sample_proj",
    )(x, w_in)


def _sample_attn_kernel(q4_ref, knew_ref, vnew_ref, ck_ref, cv_ref, bias_ref, sink_ref, hmask_ref,
                        nk_ref, nv_ref, ag_ref):
    bt = ck_ref.shape[0]
    win = ck_ref.shape[1]
    scale = HEAD_DIM ** -0.5
    row = lax.broadcasted_iota(jnp.int32, (win, KV_DIM), 0)
    last = row == win - 1
    hmask = hmask_ref[...]
    for b in range(bt):
        kb = jnp.where(last, knew_ref[b:b + 1, :], pltpu.roll(ck_ref[b], win - 1, 0))
        vb = jnp.where(last, vnew_ref[b:b + 1, :], pltpu.roll(cv_ref[b], win - 1, 0))
        nk_ref[b] = kb
        nv_ref[b] = vb
        q4 = q4_ref[b]
        qm = (jnp.concatenate([q4] * N_KV_HEADS, axis=0) * hmask).astype(BF16)
        s = _dot_nt(qm, kb.astype(BF16)) * scale + bias_ref[...]
        p = _sink_softmax(s, sink_ref[...]).astype(BF16)
        o = _dot(p, vb.astype(BF16)) * hmask
        o4 = o[0:GROUP]
        for h in range(1, N_KV_HEADS):
            o4 = o4 + o[h * GROUP:(h + 1) * GROUP]
        ag_ref[b] = o4


def _sample_attn(q4, k_new, v_new, cache_k, cache_v, bias_s, sink_col, hmask, *, layer, bt):
    _, nb, win, kvd = cache_k.shape
    return pl.pallas_call(
        _sample_attn_kernel,
        grid=(nb // bt,),
        in_specs=[
            pl.BlockSpec((bt, GROUP, kvd), lambda i: (i, 0, 0)),
            pl.BlockSpec((bt, kvd), lambda i: (i, 0)),
            pl.BlockSpec((bt, kvd), lambda i: (i, 0)),
            pl.BlockSpec((None, bt, win, kvd), lambda i: (layer, i, 0, 0)),
            pl.BlockSpec((None, bt, win, kvd), lambda i: (layer, i, 0, 0)),
            pl.BlockSpec(bias_s.shape, lambda i: (0, 0)),
            pl.BlockSpec(sink_col.shape, lambda i: (0, 0)),
            pl.BlockSpec(hmask.shape, lambda i: (0, 0)),
        ],
        out_specs=[
            pl.BlockSpec((bt, win, kvd), lambda i: (i, 0, 0)),
            pl.BlockSpec((bt, win, kvd), lambda i: (i, 0, 0)),
            pl.BlockSpec((bt, GROUP, kvd), lambda i: (i, 0, 0)),
        ],
        out_shape=[
            jax.ShapeDtypeStruct((nb, win, kvd), F32),
            jax.ShapeDtypeStruct((nb, win, kvd), F32),
            jax.ShapeDtypeStruct((nb, GROUP, kvd), F32),
        ],
        compiler_params=pltpu.CompilerParams(dimension_semantics=("arbitrary",), vmem_limit_bytes=VMEM_LIMIT),
        name="sample_attn",
    )(q4, k_new, v_new, cache_k, cache_v, bias_s, sink_col, hmask)


def _sample_post_kernel(x_ref, att_ref, proj_ref, st_ref, convw_ref, w_ao_ref, w_co_ref, w_o_ref,
                        lng_ref, lnb_ref, x1_ref, u_ref, *, alpha):
    attn_o = _dot(att_ref[...].astype(BF16), w_ao_ref[...])
    u = proj_ref[:, OFF_C:OFF_H] * proj_ref[:, OFF_H:OFF_GA]
    cw = convw_ref[...]
    y = cw[0:1] * st_ref[0] + cw[1:2] * st_ref[1] + cw[2:3] * u
    u_ref[...] = u
    y_conv = proj_ref[:, OFF_B:OFF_C] * y
    x1_ref[...] = _merge_project(x_ref[...], attn_o, y_conv, proj_ref[:, OFF_GA:OFF_GB], proj_ref[:, OFF_GB:IN_DIM],
                                 w_co_ref, w_o_ref, lng_ref, lnb_ref, alpha)


def _sample_post(x, att, proj, state, conv_w, w_ao, w_co, w_o, ln_g, ln_b, *, alpha):
    m, d = x.shape
    kernel = functools.partial(_sample_post_kernel, alpha=alpha)
    return pl.pallas_call(
        kernel,
        out_shape=[jax.ShapeDtypeStruct((m, d), F32), jax.ShapeDtypeStruct((m, CONV_DIM), F32)],
        compiler_params=pltpu.CompilerParams(vmem_limit_bytes=VMEM_LIMIT),
        name="sample_post",
    )(x, att, proj, state, conv_w, w_ao, w_co, w_o, ln_g, ln_b)


def _first_max(cur, ids, axes, big):
    m = cur
    for ax in axes:
        m = jnp.max(m, axis=ax, keepdims=True)
    idx = jnp.where(cur == m, ids, big)
    for ax in axes:
        idx = jnp.min(idx, axis=ax, keepdims=True)
    return m, idx


def _router_kernel(x_ref, rwt_ref, rb_ref, eidx_ref, gate_ref, *, tm, slab_in):
    x = _slab_load(x_ref, tm) if slab_in else x_ref[...]
    logits_t = _dot_nt(rwt_ref[...], x.astype(BF16))
    scores = jax.nn.sigmoid(logits_t)
    sel = scores + rb_ref[...]
    shape3 = (N_GROUPS, GROUP_SIZE, tm)
    sel3 = sel.reshape(shape3)
    scores3 = scores.reshape(shape3)
    member = lax.broadcasted_iota(jnp.int32, shape3, 1)
    m1, i1 = _first_max(sel3, member, (1,), GROUP_SIZE)
    m2 = jnp.max(jnp.where(member == i1, -jnp.inf, sel3), axis=1, keepdims=True)
    gscore = m1 + m2
    gid = lax.broadcasted_iota(jnp.int32, gscore.shape, 0)
    gsel = jnp.zeros(gscore.shape, jnp.bool_)
    for _ in range(TOPK_GROUPS):
        _, gi = _first_max(gscore, gid, (0,), N_GROUPS)
        hit = gid == gi
        gsel = jnp.logical_or(gsel, hit)
        gscore = jnp.where(hit, -jnp.inf, gscore)
    eid = lax.broadcasted_iota(jnp.int32, shape3, 0) * GROUP_SIZE + member
    cur = jnp.where(gsel, sel3, -jnp.inf)
    ids, ws = [], []
    for _ in range(TOP_K):
        _, ei = _first_max(cur, eid, (1, 0), N_EXPERTS)
        hit = eid == ei
        sc = jnp.sum(jnp.sum(jnp.where(hit, scores3, 0.0), axis=1, keepdims=True), axis=0, keepdims=True)
        ids.append(ei[0])
        ws.append(sc[0])
        cur = jnp.where(hit, -jnp.inf, cur)
    w = jnp.concatenate(ws, axis=0)
    tot = jnp.sum(w, axis=0, keepdims=True)
    eidx_ref[...] = jnp.concatenate(ids, axis=0)
    gate_ref[...] = w / tot * ROUTED_SCALE


def _router(x, rw_t, rb_col, *, tm, tokens=None):
    slab_in = x.ndim == 3
    if slab_in:
        t = tokens
        steps_per_chunk = (t // x.shape[0]) // tm
        assert t % (x.shape[0] * tm) == 0
        x_spec = pl.BlockSpec((None, tm * SLAB_ROWS, LANES), lambda i: (i // steps_per_chunk, i % steps_per_chunk, 0))
    else:
        t = x.shape[0]
        x_spec = pl.BlockSpec((tm, D_MODEL), lambda i: (i, 0))
    return pl.pallas_call(
        functools.partial(_router_kernel, tm=tm, slab_in=slab_in),
        grid=(t // tm,),
        in_specs=[
            x_spec,
            pl.BlockSpec(rw_t.shape, lambda i: (0, 0)),
            pl.BlockSpec(rb_col.shape, lambda i: (0, 0)),
        ],
        out_specs=[pl.BlockSpec((TOP_K, tm), lambda i: (0, i)), pl.BlockSpec((TOP_K, tm), lambda i: (0, i))],
        out_shape=[jax.ShapeDtypeStruct((TOP_K, t), jnp.int32), jax.ShapeDtypeStruct((TOP_K, t), F32)],
        compiler_params=pltpu.CompilerParams(dimension_semantics=("arbitrary",), vmem_limit_bytes=VMEM_LIMIT),
        name="router",
    )(x, rw_t, rb_col)


ROW_TILE = 256
TILE_PITCH = ROW_TILE + 1
SPARE_TOKENS = 8
PAIR_BITS = 16
FLAG_FIRST, FLAG_LAST, FLAG_NEW_EXPERT, FLAG_VALID, FLAG_HAS_NEXT = 1, 2, 4, 8, 16
SCATTER_BATCH = 8


PREV, CUR, NEXT = 0, 1, 2


def _moe_routed_kernel(ce_ref, flags_ref, used_ref, nexte_ref, idx_ref, gate_ref, x_ref, wg_ref, wu_ref,
                       wd_ref, sg_ref, su_ref, sd_ref, lng_ref, lnb_ref, o_ref, wgu_ref, wdb_ref,
                       gat0_ref, gat1_ref, res0_ref, res1_ref, wsg_ref, wsu_ref, wsd_ref, wsem_ref, *,
                       layer, alpha, chunk_tokens, sub_tokens):
    del used_ref
    step = pl.program_id(0)
    flags = flags_ref[step]
    odd = (step & 1) == 1
    nchunk = D_MODEL // LANES
    ne, ff = wg_ref.shape[1], wg_ref.shape[3]

    def slab_row(which, r):
        return pl.multiple_of(idx_ref[0, which, r], 8)

    def gather_row(which, gat_ref, r):
        gat_ref[pl.ds(r, nchunk, stride=TILE_PITCH), :] = x_ref[pl.ds(slab_row(which, r), 8), :]

    def scatter_rows(which, res_ref, rows):
        dst = [slab_row(which, r) for r in rows]
        acc = [o_ref[pl.ds(d, 8), :] + res_ref[pl.ds(r, nchunk, stride=TILE_PITCH), :] for d, r in zip(dst, rows)]
        for d, a in zip(dst, acc):
            o_ref[pl.ds(d, 8), :] = a

    def expert_mlp(gat_ref, res_ref):
        lhs = jnp.concatenate([gat_ref[j * TILE_PITCH:j * TILE_PITCH + ROW_TILE, :] for j in range(nchunk)],
                              axis=1).astype(BF16)
        h = _dot(lhs, wgu_ref[...])
        gate_col = jnp.broadcast_to(gate_ref[0], (LANES, ROW_TILE)).T
        hid = jax.nn.silu(h[:, 0:ff]) * h[:, ff:2 * ff] * jnp.concatenate([gate_col] * (ff // LANES), axis=1)
        y = _dot(hid.astype(BF16), wdb_ref[...])
        for j in range(nchunk):
            res_ref[j * TILE_PITCH:j * TILE_PITCH + ROW_TILE, :] = y[:, j * LANES:(j + 1) * LANES]

    def by_parity(fn):
        @pl.when(jnp.logical_not(odd))
        def _even():
            fn(gat0_ref, gat1_ref, res0_ref, res1_ref)

        @pl.when(odd)
        def _odd():
            fn(gat1_ref, gat0_ref, res1_ref, res0_ref)

    @pl.when((flags & FLAG_FIRST) != 0)
    def _start_chunk():
        o_ref[...] = jnp.zeros(o_ref.shape, F32)

        def start(gat_cur, gat_other, res_cur, res_other):
            res_other[...] = jnp.zeros(res_other.shape, F32)

            def body(r, carry):
                gather_row(CUR, gat_cur, r)
                return carry

            lax.fori_loop(0, ROW_TILE, body, 0)

        by_parity(start)

    def weight_copies(e):
        return [pltpu.make_async_copy(src.at[layer, e], dst, wsem_ref.at[k])
                for k, (src, dst) in enumerate(((wg_ref, wsg_ref), (wu_ref, wsu_ref), (wd_ref, wsd_ref)))]

    @pl.when(step == 0)
    def _first_fetch():
        for cp in weight_copies(ce_ref[0] % ne):
            cp.start()

    @pl.when((flags & FLAG_NEW_EXPERT) != 0)
    def _next_expert():
        for cp in weight_copies(0):
            cp.wait()
        wgu_ref[:, 0:ff] = wsg_ref[...].astype(BF16)
        wgu_ref[:, ff:2 * ff] = wsu_ref[...].astype(BF16)
        wdb_ref[...] = wsd_ref[...].astype(BF16)

        @pl.when((flags & FLAG_HAS_NEXT) != 0)
        def _prefetch():
            for cp in weight_copies(nexte_ref[step]):
                cp.start()

    @pl.when((flags & FLAG_VALID) != 0)
    def _tile():
        def main(gat_cur, gat_other, res_cur, res_other):
            for r0 in range(0, ROW_TILE, SCATTER_BATCH):
                scatter_rows(PREV, res_other, range(r0, r0 + SCATTER_BATCH))
            expert_mlp(gat_cur, res_cur)
            for r in range(ROW_TILE):
                gather_row(NEXT, gat_other, r)

        by_parity(main)

    @pl.when((flags & FLAG_LAST) != 0)
    def _finish():
        def flush(gat_cur, gat_other, res_cur, res_other):
            def body(r, carry):
                scatter_rows(CUR, res_cur, [r])
                return carry

            lax.fori_loop(0, ROW_TILE, body, 0)

        by_parity(flush)

        def body(s, carry):
            base = pl.multiple_of(s * (sub_tokens * 8), 8)

            def rows_2d(ref):
                return jnp.concatenate([ref[pl.ds(base + j, sub_tokens, stride=8), :] for j in range(nchunk)],
                                       axis=1)

            x2 = rows_2d(x_ref)
            xb = x2.astype(BF16)
            hs = jax.nn.silu(_dot(xb, sg_ref[...])) * _dot(xb, su_ref[...])
            ffn = rows_2d(o_ref) + _dot(hs.astype(BF16), sd_ref[...])
            res = _layer_norm(alpha * x2 + ffn, lng_ref[...], lnb_ref[...])
            for j in range(nchunk):
                o_ref[pl.ds(base + j, sub_tokens, stride=8), :] = res[:, j * LANES:(j + 1) * LANES]
            return carry

        lax.fori_loop(0, chunk_tokens // sub_tokens, body, 0)


def _moe_routed(tile_ce, tile_flags, n_used, next_expert, row_idx, gates, x_tm, wg, wu, wd, sg, su, sd, ln_g,
                ln_b, *, layer, alpha, chunk_tokens, sub_tokens):
    n_chunks, chunk_rows, _ = x_tm.shape
    n_tiles = row_idx.shape[0]
    _, ne, d, ff = wg.shape
    kernel = functools.partial(_moe_routed_kernel, layer=layer, alpha=alpha, chunk_tokens=chunk_tokens,
                               sub_tokens=sub_tokens)

    def tile_map(i, ce, fl, used, nxt):
        return (jnp.minimum(i, used[0] - 1), 0, 0)

    def chunk_map(i, ce, fl, used, nxt):
        return (ce[i] // ne, 0, 0)

    def const2(i, ce, fl, used, nxt):
        return (0, 0)

    grid_spec = pltpu.PrefetchScalarGridSpec(
        num_scalar_prefetch=4,
        grid=(n_tiles,),
        in_specs=[
            pl.BlockSpec((1, 3, ROW_TILE), tile_map, memory_space=pltpu.SMEM),
            pl.BlockSpec((1, 1, ROW_TILE), tile_map),
            pl.BlockSpec((None, chunk_rows, LANES), chunk_map, pipeline_mode=pl.Buffered(1)),
            pl.BlockSpec(memory_space=pl.ANY),
            pl.BlockSpec(memory_space=pl.ANY),
            pl.BlockSpec(memory_space=pl.ANY),
            pl.BlockSpec(sg.shape, const2, pipeline_mode=pl.Buffered(1)),
            pl.BlockSpec(su.shape, const2, pipeline_mode=pl.Buffered(1)),
            pl.BlockSpec(sd.shape, const2, pipeline_mode=pl.Buffered(1)),
            pl.BlockSpec(ln_g.shape, const2),
            pl.BlockSpec(ln_b.shape, const2),
        ],
        out_specs=pl.BlockSpec((None, chunk_rows, LANES), chunk_map, pipeline_mode=pl.Buffered(1)),
        scratch_shapes=[
            pltpu.VMEM((d, 2 * ff), BF16),
            pltpu.VMEM((ff, d), BF16),
            pltpu.VMEM((8 * TILE_PITCH, LANES), F32),
            pltpu.VMEM((8 * TILE_PITCH, LANES), F32),
            pltpu.VMEM((8 * TILE_PITCH, LANES), F32),
            pltpu.VMEM((8 * TILE_PITCH, LANES), F32),
            pltpu.VMEM((d, ff), F32),
            pltpu.VMEM((d, ff), F32),
            pltpu.VMEM((ff, d), F32),
            pltpu.SemaphoreType.DMA((3,)),
        ],
    )
    return pl.pallas_call(
        kernel,
        grid_spec=grid_spec,
        out_shape=jax.ShapeDtypeStruct(x_tm.shape, F32),
        compiler_params=pltpu.CompilerParams(dimension_semantics=("arbitrary",), vmem_limit_bytes=VMEM_LIMIT),
        name="moe_routed",
    )(tile_ce, tile_flags, n_used, next_expert, row_idx, gates, x_tm, wg, wu, wd, sg, su, sd, ln_g, ln_b)


MOE_CHUNKS = 4
MOE_SUB_MAX = 512


def _moe_tiling(n_prompt, n_sample):
    assert n_prompt % (MOE_CHUNKS * 8) == 0 and n_sample % (MOE_CHUNKS * 8) == 0
    chunk_tokens = (n_prompt + n_sample) // MOE_CHUNKS
    assert (chunk_tokens + 1) * TOP_K <= 1 << PAIR_BITS
    sub_tokens = max(s for s in range(8, MOE_SUB_MAX + 1, 8) if chunk_tokens % s == 0)
    return MOE_CHUNKS, chunk_tokens, sub_tokens


def _chunked(prompt, sample, n_chunks):
    return jnp.concatenate([prompt.reshape((n_chunks, -1) + prompt.shape[1:]),
                            sample.reshape((n_chunks, -1) + sample.shape[1:])], axis=1)


def _route_plan(eidx_p, eidx_s, gate_p, gate_s, *, chunk_tokens, n_chunks):
    n_seg = n_chunks * N_EXPERTS
    eidx = _chunked(eidx_p.T, eidx_s.T, n_chunks)
    gate = _chunked(gate_p.T, gate_s.T, n_chunks)
    seg = jnp.arange(n_chunks, dtype=jnp.int32)[:, None, None] * N_EXPERTS + eidx
    pair = jnp.arange(chunk_tokens * TOP_K, dtype=jnp.int32).reshape(1, chunk_tokens, TOP_K)
    keys_real = ((seg << PAIR_BITS) | pair).reshape(-1)
    counts = jnp.sum(eidx[..., None] == jnp.arange(N_EXPERTS, dtype=jnp.int32), axis=(1, 2),
                     dtype=jnp.int32).reshape(n_seg)
    n_pad = (-counts) % ROW_TILE
    slot = jnp.arange(ROW_TILE - 1, dtype=jnp.int32)
    int_max = jnp.iinfo(jnp.int32).max
    pad_pair = (1 << PAIR_BITS) - 1
    keys_pad = jnp.where(slot[None, :] < n_pad[:, None],
                         (jnp.arange(n_seg, dtype=jnp.int32)[:, None] << PAIR_BITS) | pad_pair, int_max)
    keys = jnp.concatenate([keys_real, keys_pad.reshape(-1)])
    vals = jnp.concatenate([gate.reshape(-1), jnp.zeros((keys_pad.size,), F32)])
    assert keys.size % ROW_TILE == 0
    n_tiles = keys.size // ROW_TILE
    keys, vals = lax.sort((keys, vals), num_keys=1)
    keys = keys.reshape(n_tiles, ROW_TILE)
    head = keys[:, 0]
    valid = head != int_max
    tile_ce = jnp.minimum(head >> PAIR_BITS, n_seg - 1)
    tile_c = tile_ce // N_EXPERTS
    prev_ce = jnp.concatenate([jnp.full((1,), -N_EXPERTS, jnp.int32), tile_ce[:-1]])
    next_c = jnp.concatenate([tile_c[1:], jnp.full((1,), -1, jnp.int32)])
    next_valid = jnp.concatenate([valid[1:], jnp.zeros((1,), jnp.bool_)])
    first = valid & (tile_c != prev_ce // N_EXPERTS)
    last = valid & (~next_valid | (next_c != tile_c))
    new_expert = valid & (tile_ce != prev_ce)
    tile_id = jnp.arange(n_tiles, dtype=jnp.int32)
    change_at = jnp.where(new_expert, tile_id, n_tiles)
    next_change = jnp.concatenate([lax.cummin(change_at, reverse=True)[1:], jnp.full((1,), n_tiles, jnp.int32)])
    has_next = new_expert & (next_change < n_tiles)
    next_expert = tile_ce[jnp.minimum(next_change, n_tiles - 1)] % N_EXPERTS
    flags = (first * FLAG_FIRST + last * FLAG_LAST + new_expert * FLAG_NEW_EXPERT + valid * FLAG_VALID
             + has_next * FLAG_HAS_NEXT)
    row_idx = jnp.minimum((keys & pad_pair) // TOP_K, chunk_tokens) * 8
    row_idx3 = jnp.stack([jnp.concatenate([row_idx[:1], row_idx[:-1]]), row_idx,
                          jnp.concatenate([row_idx[1:], row_idx[-1:]])], axis=1)
    n_used = jnp.sum(valid, dtype=jnp.int32).reshape(1)
    return (tile_ce, flags.astype(jnp.int32), n_used, next_expert.astype(jnp.int32), row_idx3,
            vals.reshape(n_tiles, 1, ROW_TILE))


def _t5_bucket(dist):
    n = jnp.maximum(dist, 0)
    max_exact = N_BUCKETS // 2
    large = max_exact + (jnp.log(jnp.maximum(n, 1).astype(F32) / max_exact)
                         / math.log(MAX_DISTANCE / max_exact) * (N_BUCKETS - max_exact)).astype(jnp.int32)
    large = jnp.minimum(large, N_BUCKETS - 1)
    return jnp.where(n < max_exact, n, large)


def _bias_lookup(rel_bias, bucket):
    onehot = (bucket[..., None] == jnp.arange(N_BUCKETS, dtype=bucket.dtype)).astype(F32)
    return jnp.einsum("...b,bh->h...", onehot, rel_bias.astype(F32), precision=lax.Precision.HIGHEST)


def _bias_tables(rel_bias, win):
    qi = jnp.arange(ATTN_BLOCK)[:, None]
    ki = jnp.arange(2 * ATTN_BLOCK)[None, :]
    dist = qi + ATTN_BLOCK - ki
    valid = (dist >= 0) & (dist < WINDOW)
    bias = _bias_lookup(rel_bias, _t5_bucket(dist))
    bias = jnp.where(valid[None], bias, NEG).reshape(N_KV_HEADS, GROUP, ATTN_BLOCK, 2 * ATTN_BLOCK)
    bias_ab = jnp.stack([jnp.concatenate([bias[:, t], bias[:, t + 2]], axis=1) for t in range(2)], axis=1)
    dist_s = (win - 1) - jnp.arange(win)
    bias_s = _bias_lookup(rel_bias, _t5_bucket(dist_s))
    return bias_ab, bias_s


def _sink_tables(sink):
    s = sink.astype(F32).reshape(N_KV_HEADS, GROUP)
    rows = [jnp.concatenate([jnp.broadcast_to(s[:, t, None], (N_KV_HEADS, ATTN_BLOCK)),
                             jnp.broadcast_to(s[:, t + 2, None], (N_KV_HEADS, ATTN_BLOCK))], axis=1)
            for t in range(2)]
    return jnp.stack(rows, axis=1)[..., None], sink.astype(F32)[:, None]


def kernel(x_prompt, x_sample, cache_k_win, cache_v_win, state_conv, rel_bias, w_in, attn_sink, conv_w,
           w_attn_out, w_conv_out, w_out, ln1_g, ln1_b, router_w, router_bias, exp_w_gate, exp_w_up,
           exp_w_down, shared_w_gate, shared_w_up, shared_w_down, ln2_g, ln2_b):
    depth = w_in.shape[0]
    alpha = (2 * depth) ** 0.25
    nb, seq, d = x_prompt.shape
    nd = x_sample.shape[0]
    win = cache_k_win.shape[2]
    assert x_sample.shape[1] == 1 and win == WINDOW and seq % 512 == 0

    n_prompt = nb * seq
    n_chunks, chunk_tokens, sub_tokens = _moe_tiling(n_prompt, nd)
    prompt_rows = n_prompt // n_chunks * SLAB_ROWS
    slab_shape = (n_chunks, (chunk_tokens + SPARE_TOKENS) * SLAB_ROWS, LANES)

    bias_ab, bias_s = _bias_tables(rel_bias, win)
    hmask = (jnp.arange(KV_DIM)[None, :] // HEAD_DIM == jnp.arange(N_HEADS)[:, None] // GROUP).astype(F32)

    yp = x_prompt
    ys = x_sample.reshape(nd, d)
    outs = [[] for _ in range(6)]
    for l in range(depth):
        w_in_b = w_in[l].astype(BF16)
        w_ao_b = w_attn_out[l].astype(BF16)
        w_co_b = w_conv_out[l].astype(BF16)
        w_o_b = w_out[l].astype(BF16)
        g1, b1 = ln1_g[l][None, :], ln1_b[l][None, :]
        g2, b2 = ln2_g[l][None, :], ln2_b[l][None, :]
        sink_ab, sink_col = _sink_tables(attn_sink[l])

        slab, kp, vp, cp = _mixer_prompt(yp, w_in_b, w_attn_out[l].T.astype(BF16), w_co_b, w_o_b, conv_w[l],
                                         jnp.swapaxes(bias_ab, -1, -2), jnp.swapaxes(sink_ab, -1, -2), g1, b1,
                                         alpha=alpha, tq=512, batch=nb, seq=seq, slab_shape=slab_shape,
                                         base=jnp.zeros(slab_shape, F32) if l == 0 else None)

        proj = _sample_proj(ys, w_in_b, tn=IN_DIM // 4)
        q4 = proj[:, :Q_DIM].reshape(nd, N_KV_HEADS, GROUP, HEAD_DIM).transpose(0, 2, 1, 3).reshape(nd, GROUP, KV_DIM)
        ksn, vsn, ag = _sample_attn(q4, proj[:, OFF_K:OFF_V], proj[:, OFF_V:OFF_B],
                                    cache_k_win.reshape(depth, nd, win, KV_DIM),
                                    cache_v_win.reshape(depth, nd, win, KV_DIM),
                                    bias_s, sink_col, hmask, layer=l, bt=8)
        att = ag.reshape(nd, GROUP, N_KV_HEADS, HEAD_DIM).transpose(0, 2, 1, 3).reshape(nd, Q_DIM)
        state_t = jnp.swapaxes(state_conv[l], 0, 1)
        ys, us = _sample_post(ys, att, proj, state_t, conv_w[l], w_ao_b, w_co_b, w_o_b, g1, b1, alpha=alpha)

        outs[0].append(kp.reshape(nb, WINDOW, N_KV_HEADS, HEAD_DIM))
        outs[1].append(vp.reshape(nb, WINDOW, N_KV_HEADS, HEAD_DIM))
        outs[2].append(cp)
        outs[3].append(ksn.reshape(nd, win, N_KV_HEADS, HEAD_DIM))
        outs[4].append(vsn.reshape(nd, win, N_KV_HEADS, HEAD_DIM))
        outs[5].append(jnp.concatenate([state_conv[l][:, 1:], us[:, None, :]], axis=1))

        rw_t = router_w[l].T.astype(BF16)
        rb_col = router_bias[l].astype(F32)[:, None]
        sg, su, sd = (shared_w_gate[l].astype(BF16), shared_w_up[l].astype(BF16), shared_w_down[l].astype(BF16))
        tail = jnp.concatenate([ys.reshape(n_chunks, -1, LANES),
                                jnp.zeros((n_chunks, SPARE_TOKENS * SLAB_ROWS, LANES), F32)], axis=1)
        slab = lax.dynamic_update_slice(slab, tail, (0, prompt_rows, 0))
        ep, wp = _router(slab, rw_t, rb_col, tm=512, tokens=n_prompt)
        es, ws = _router(ys, rw_t, rb_col, tm=nd)
        plan = _route_plan(ep, es, wp, ws, chunk_tokens=chunk_tokens, n_chunks=n_chunks)
        yp = _moe_routed(*plan, slab, exp_w_gate, exp_w_up, exp_w_down, sg, su, sd, g2, b2,
                         layer=l, alpha=alpha, chunk_tokens=chunk_tokens, sub_tokens=sub_tokens)
        ys = yp[:, prompt_rows:chunk_tokens * SLAB_ROWS].reshape(nd, d)

    y_prompt = yp[:, :prompt_rows].reshape(nb, seq, d)
    return (y_prompt, ys.reshape(nd, 1, d)) + tuple(jnp.stack(o) for o in outs)
```

```python
import functools
import math

import jax
import jax.numpy as jnp
from jax import lax
from jax.experimental import pallas as pl
from jax.experimental.pallas import tpu as pltpu

D_MODEL = 1024
N_HEADS = 16
N_KV_HEADS = 4
HEAD_DIM = 64
GROUP = N_HEADS // N_KV_HEADS
WINDOW = 128
ATTN_BLOCK = 128
N_BUCKETS = 32
MAX_DISTANCE = 128
CONV_DIM = 1024
CONV_K = 3
N_EXPERTS = 64
TOP_K = 8
N_GROUPS = 8
TOPK_GROUPS = 4
GROUP_SIZE = N_EXPERTS // N_GROUPS
EXPERT_FF = 256
ROUTED_SCALE = 2.5
LN_EPS = 1e-5
NEG = -1e30

Q_DIM = N_HEADS * HEAD_DIM
KV_DIM = N_KV_HEADS * HEAD_DIM
OFF_K = Q_DIM
OFF_V = OFF_K + KV_DIM
OFF_B = OFF_V + KV_DIM
OFF_C = OFF_B + CONV_DIM
OFF_H = OFF_C + CONV_DIM
OFF_GA = OFF_H + CONV_DIM
OFF_GB = OFF_GA + D_MODEL
IN_DIM = OFF_GB + D_MODEL

LANES = 128
CONV_PAD = 8
VMEM_LIMIT = 60 * 1024 * 1024

BF16 = jnp.bfloat16
F32 = jnp.float32


def _dot(a, b):
    return jnp.dot(a, b, preferred_element_type=F32)


def _dot_nt(a, b):
    return lax.dot_general(a, b, (((1,), (1,)), ((), ())), preferred_element_type=F32)


def _layer_norm(z, g, b):
    mu = jnp.mean(z, axis=-1, keepdims=True)
    d = z - mu
    var = jnp.mean(d * d, axis=-1, keepdims=True)
    return d * lax.rsqrt(var + LN_EPS) * g + b


def _sink_softmax(s, sink, axis=-1):
    m = jnp.maximum(jnp.max(s, axis=axis, keepdims=True), sink)
    e = jnp.exp(s - m)
    den = jnp.sum(e, axis=axis, keepdims=True) + jnp.exp(sink - m)
    return e * (1.0 / den)


SLAB_ROWS = D_MODEL // LANES


def _slab_load(ref, tokens, base=0):
    return jnp.concatenate([ref[pl.ds(base + j, tokens, stride=SLAB_ROWS), :] for j in range(SLAB_ROWS)], axis=1)


def _slab_store(ref, val, base=0):
    tokens = val.shape[0]
    for j in range(SLAB_ROWS):
        ref[pl.ds(base + j, tokens, stride=SLAB_ROWS), :] = val[:, j * LANES:(j + 1) * LANES]


def _merge_project(x, attn_o, y_conv, g_a, g_b, w_co_ref, w_o_ref, lng_ref, lnb_ref, alpha):
    merged = jax.nn.sigmoid(g_a) * attn_o + jax.nn.sigmoid(g_b) * _dot(y_conv.astype(BF16), w_co_ref[...])
    out = _dot(merged.astype(BF16), w_o_ref[...])
    return _layer_norm(alpha * x + out, lng_ref[...], lnb_ref[...])


def _mixer_prompt_kernel(x_ref, w_in_ref, w_ao_ref, w_co_ref, w_o_ref, convw_ref, bias_ref, sink_ref,
                         lng_ref, lnb_ref, base_ref,
                         x1_ref, kwin_ref, vwin_ref, conv_ref,
                         ka_ref, kb_ref, vat_ref, vbt_ref, s_ref, p_ref, att_ref, ubuf_ref, *, alpha, tq, slab_in):
    del base_ref
    i = pl.program_id(1)
    nblk = tq // ATTN_BLOCK
    half = LANES // 2
    scale = HEAD_DIM ** -0.5
    assert math.frexp(scale)[0] == 0.5

    @pl.when(i == 0)
    def _init():
        for ref in (ka_ref, kb_ref):
            ref[:, 0:ATTN_BLOCK, :] = jnp.zeros((N_KV_HEADS, ATTN_BLOCK, LANES), BF16)
        for ref in (vat_ref, vbt_ref):
            ref[:, :, 0:ATTN_BLOCK] = jnp.zeros((N_KV_HEADS, LANES, ATTN_BLOCK), BF16)
        ubuf_ref[0:CONV_PAD, :] = jnp.zeros((CONV_PAD, CONV_DIM), F32)

    x = _slab_load(x_ref, tq) if slab_in else x_ref[0]
    xb = x.astype(BF16)
    qkv = _dot(xb, w_in_ref[:, 0:OFF_B])

    lo = lax.broadcasted_iota(jnp.int32, (tq, LANES), 1) < half
    zeros_t = jnp.zeros((half, tq), BF16)
    for c in range(N_KV_HEADS // 2):
        chunk = qkv[:, OFF_K + c * LANES: OFF_K + (c + 1) * LANES]
        c_lo = jnp.where(lo, chunk, 0.0)
        c_hi = jnp.where(lo, 0.0, chunk)
        ka_ref[2 * c, ATTN_BLOCK:, :] = c_lo.astype(BF16)
        kb_ref[2 * c, ATTN_BLOCK:, :] = pltpu.roll(c_lo, half, 1).astype(BF16)
        kb_ref[2 * c + 1, ATTN_BLOCK:, :] = c_hi.astype(BF16)
        ka_ref[2 * c + 1, ATTN_BLOCK:, :] = pltpu.roll(c_hi, half, 1).astype(BF16)
        vt = qkv[:, OFF_V + c * LANES: OFF_V + (c + 1) * LANES].T.astype(BF16)
        vat_ref[2 * c, :, ATTN_BLOCK:] = jnp.concatenate([vt[0:half], zeros_t], axis=0)
        vbt_ref[2 * c, :, ATTN_BLOCK:] = jnp.concatenate([zeros_t, vt[0:half]], axis=0)
        vbt_ref[2 * c + 1, :, ATTN_BLOCK:] = jnp.concatenate([zeros_t, vt[half:]], axis=0)
        vat_ref[2 * c + 1, :, ATTN_BLOCK:] = jnp.concatenate([vt[half:], zeros_t], axis=0)

    key_row = lax.broadcasted_iota(jnp.int32, (2 * ATTN_BLOCK, 2 * ATTN_BLOCK), 0)
    for j in range(nblk):
        rows = slice(j * ATTN_BLOCK, (j + 1) * ATTN_BLOCK)
        keys = slice(j * ATTN_BLOCK, (j + 2) * ATTN_BLOCK)
        for h in range(N_KV_HEADS):
            q0 = h * GROUP * HEAD_DIM
            q2 = (jnp.concatenate([qkv[rows, q0:q0 + LANES], qkv[rows, q0 + LANES:q0 + 2 * LANES]], axis=0)
                  * scale).astype(BF16)
            for t, k_ref in enumerate((ka_ref, kb_ref)):
                s = _dot_nt(k_ref[h, keys, :], q2)
                if j == 0:
                    s = jnp.where(jnp.logical_and(i == 0, key_row < ATTN_BLOCK), NEG, s)
                s_ref[j, h, t] = s
    for h in range(N_KV_HEADS):
        for t in range(2):
            s = s_ref[:, h, t] + bias_ref[h, t][None]
            p_ref[:, h, t] = _sink_softmax(s, sink_ref[h, t][None], axis=1).astype(BF16)
    for j in range(nblk):
        rows = slice(j * ATTN_BLOCK, (j + 1) * ATTN_BLOCK)
        keys = slice(j * ATTN_BLOCK, (j + 2) * ATTN_BLOCK)
        for h in range(N_KV_HEADS):
            q0 = h * GROUP * HEAD_DIM
            o_t = _dot(vat_ref[h, :, keys], p_ref[j, h, 0]) + _dot(vbt_ref[h, :, keys], p_ref[j, h, 1])
            att_ref[q0:q0 + LANES, rows] = o_t[:, 0:ATTN_BLOCK].astype(BF16)
            att_ref[q0 + LANES:q0 + 2 * LANES, rows] = o_t[:, ATTN_BLOCK:].astype(BF16)
    for ref in (ka_ref, kb_ref):
        ref[:, 0:ATTN_BLOCK, :] = ref[:, tq:tq + ATTN_BLOCK, :]
    for ref in (vat_ref, vbt_ref):
        ref[:, :, 0:ATTN_BLOCK] = ref[:, :, tq:tq + ATTN_BLOCK]

    kwin_ref[0] = qkv[tq - WINDOW:tq, OFF_K:OFF_V]
    vwin_ref[0] = qkv[tq - WINDOW:tq, OFF_V:OFF_B]

    attn_o = _dot(w_ao_ref[...], att_ref[...]).T

    bch = _dot(xb, w_in_ref[:, OFF_B:OFF_GA])
    u = bch[:, CONV_DIM:2 * CONV_DIM] * bch[:, 2 * CONV_DIM:3 * CONV_DIM]
    ubuf_ref[CONV_PAD:CONV_PAD + tq, :] = u
    cw = convw_ref[...]
    y = (cw[0:1] * ubuf_ref[CONV_PAD - 2:CONV_PAD - 2 + tq, :]
         + cw[1:2] * ubuf_ref[CONV_PAD - 1:CONV_PAD - 1 + tq, :]
         + cw[2:3] * u)
    conv_ref[0] = ubuf_ref[CONV_PAD + tq - (CONV_K - 1):CONV_PAD + tq, :]
    ubuf_ref[0:CONV_PAD, :] = ubuf_ref[tq:tq + CONV_PAD, :]
    y_conv = bch[:, 0:CONV_DIM] * y

    gab = _dot(xb, w_in_ref[:, OFF_GA:IN_DIM])
    _slab_store(x1_ref, _merge_project(x, attn_o, y_conv, gab[:, 0:D_MODEL], gab[:, D_MODEL:], w_co_ref, w_o_ref,
                                       lng_ref, lnb_ref, alpha))


def _const_spec(shape):
    nd = len(shape)
    return pl.BlockSpec(shape, lambda *_: (0,) * nd, pipeline_mode=pl.Buffered(1))


def _mixer_prompt(x, w_in, w_ao, w_co, w_o, conv_w, bias_ab, sink_ab, ln_g, ln_b, *, alpha, tq, batch, seq,
                  slab_shape, base=None):
    b, s = batch, seq
    n_chunks, chunk_rows, _ = slab_shape
    slab_in = x.ndim == 3 and x.shape == slab_shape
    steps_per_chunk = (b * s // n_chunks) // tq
    assert (b * s) % (n_chunks * tq) == 0 and s % tq == 0

    def slab_map(bi, i):
        g = bi * (s // tq) + i
        return (g // steps_per_chunk, g % steps_per_chunk, 0)

    slab_spec = pl.BlockSpec((None, tq * SLAB_ROWS, LANES), slab_map)
    kernel = functools.partial(_mixer_prompt_kernel, alpha=alpha, tq=tq, slab_in=slab_in)
    return pl.pallas_call(
        kernel,
        grid=(b, s // tq),
        in_specs=[
            slab_spec if slab_in else pl.BlockSpec((1, tq, D_MODEL), lambda bi, i: (bi, i, 0)),
            _const_spec(w_in.shape), _const_spec(w_ao.shape), _const_spec(w_co.shape), _const_spec(w_o.shape),
            _const_spec(conv_w.shape), _const_spec(bias_ab.shape), _const_spec(sink_ab.shape),
            _const_spec(ln_g.shape), _const_spec(ln_b.shape),
            pl.BlockSpec(memory_space=pl.ANY),
        ],
        out_specs=[
            slab_spec,
            pl.BlockSpec((1, WINDOW, KV_DIM), lambda bi, i: (bi, 0, 0)),
            pl.BlockSpec((1, WINDOW, KV_DIM), lambda bi, i: (bi, 0, 0)),
            pl.BlockSpec((1, CONV_K - 1, CONV_DIM), lambda bi, i: (bi, 0, 0)),
        ],
        out_shape=[
            jax.ShapeDtypeStruct(slab_shape, F32),
            jax.ShapeDtypeStruct((b, WINDOW, KV_DIM), F32),
            jax.ShapeDtypeStruct((b, WINDOW, KV_DIM), F32),
            jax.ShapeDtypeStruct((b, CONV_K - 1, CONV_DIM), F32),
        ],
        scratch_shapes=[
            pltpu.VMEM((N_KV_HEADS, ATTN_BLOCK + tq, LANES), BF16),
            pltpu.VMEM((N_KV_HEADS, ATTN_BLOCK + tq, LANES), BF16),
            pltpu.VMEM((N_KV_HEADS, LANES, ATTN_BLOCK + tq), BF16),
            pltpu.VMEM((N_KV_HEADS, LANES, ATTN_BLOCK + tq), BF16),
            pltpu.VMEM((tq // ATTN_BLOCK, N_KV_HEADS, 2, 2 * ATTN_BLOCK, 2 * ATTN_BLOCK), F32),
            pltpu.VMEM((tq // ATTN_BLOCK, N_KV_HEADS, 2, 2 * ATTN_BLOCK, 2 * ATTN_BLOCK), BF16),
            pltpu.VMEM((Q_DIM, tq), BF16),
            pltpu.VMEM((tq + CONV_PAD, CONV_DIM), F32),
        ],
        compiler_params=pltpu.CompilerParams(
            dimension_semantics=("arbitrary", "arbitrary"), vmem_limit_bytes=VMEM_LIMIT),
        input_output_aliases={0 if slab_in else 10: 0},
        name="mixer_prompt",
    )(x, w_in, w_ao, w_co, w_o, conv_w, bias_ab, sink_ab, ln_g, ln_b,
      jnp.zeros((1, SLAB_ROWS, LANES), F32) if slab_in else base)


def _proj_kernel(x_ref, w_ref, o_ref):
    o_ref[...] = _dot(x_ref[...].astype(BF16), w_ref[...])


def _sample_proj(x, w_in, *, tn):
    m, d = x.shape
    n = w_in.shape[1]
    return pl.pallas_call(
        _proj_kernel,
        grid=(n // tn,),
        in_specs=[pl.BlockSpec((m, d), lambda j: (0, 0)), pl.BlockSpec((d, tn), lambda j: (0, j))],
        out_specs=pl.BlockSpec((m, tn), lambda j: (0, j)),
        out_shape=jax.ShapeDtypeStruct((m, n), F32),
        compiler_params=pltpu.CompilerParams(dimension_semantics=("arbitrary",), vmem_limit_bytes=VMEM_LIMIT),
        name="sample_proj",
    )(x, w_in)


def _sample_attn_kernel(q4_ref, knew_ref, vnew_ref, ck_ref, cv_ref, bias_ref, sink_ref, hmask_ref,
                        nk_ref, nv_ref, ag_ref):
    bt = ck_ref.shape[0]
    win = ck_ref.shape[1]
    scale = HEAD_DIM ** -0.5
    row = lax.broadcasted_iota(jnp.int32, (win, KV_DIM), 0)
    last = row == win - 1
    hmask = hmask_ref[...]
    for b in range(bt):
        kb = jnp.where(last, knew_ref[b:b + 1, :], pltpu.roll(ck_ref[b], win - 1, 0))
        vb = jnp.where(last, vnew_ref[b:b + 1, :], pltpu.roll(cv_ref[b], win - 1, 0))
        nk_ref[b] = kb
        nv_ref[b] = vb
        q4 = q4_ref[b]
        qm = (jnp.concatenate([q4] * N_KV_HEADS, axis=0) * hmask).astype(BF16)
        s = _dot_nt(qm, kb.astype(BF16)) * scale + bias_ref[...]
        p = _sink_softmax(s, sink_ref[...]).astype(BF16)
        o = _dot(p, vb.astype(BF16)) * hmask
        o4 = o[0:GROUP]
        for h in range(1, N_KV_HEADS):
            o4 = o4 + o[h * GROUP:(h + 1) * GROUP]
        ag_ref[b] = o4


def _sample_attn(q4, k_new, v_new, cache_k, cache_v, bias_s, sink_col, hmask, *, layer, bt):
    _, nb, win, kvd = cache_k.shape
    return pl.pallas_call(
        _sample_attn_kernel,
        grid=(nb // bt,),
        in_specs=[
            pl.BlockSpec((bt, GROUP, kvd), lambda i: (i, 0, 0)),
            pl.BlockSpec((bt, kvd), lambda i: (i, 0)),
            pl.BlockSpec((bt, kvd), lambda i: (i, 0)),
            pl.BlockSpec((None, bt, win, kvd), lambda i: (layer, i, 0, 0)),
            pl.BlockSpec((None, bt, win, kvd), lambda i: (layer, i, 0, 0)),
            pl.BlockSpec(bias_s.shape, lambda i: (0, 0)),
            pl.BlockSpec(sink_col.shape, lambda i: (0, 0)),
            pl.BlockSpec(hmask.shape, lambda i: (0, 0)),
        ],
        out_specs=[
            pl.BlockSpec((bt, win, kvd), lambda i: (i, 0, 0)),
            pl.BlockSpec((bt, win, kvd), lambda i: (i, 0, 0)),
            pl.BlockSpec((bt, GROUP, kvd), lambda i: (i, 0, 0)),
        ],
        out_shape=[
            jax.ShapeDtypeStruct((nb, win, kvd), F32),
            jax.ShapeDtypeStruct((nb, win, kvd), F32),
            jax.ShapeDtypeStruct((nb, GROUP, kvd), F32),
        ],
        compiler_params=pltpu.CompilerParams(dimension_semantics=("arbitrary",), vmem_limit_bytes=VMEM_LIMIT),
        name="sample_attn",
    )(q4, k_new, v_new, cache_k, cache_v, bias_s, sink_col, hmask)


def _sample_post_kernel(x_ref, att_ref, proj_ref, st_ref, convw_ref, w_ao_ref, w_co_ref, w_o_ref,
                        lng_ref, lnb_ref, x1_ref, u_ref, *, alpha):
    attn_o = _dot(att_ref[...].astype(BF16), w_ao_ref[...])
    u = proj_ref[:, OFF_C:OFF_H] * proj_ref[:, OFF_H:OFF_GA]
    cw = convw_ref[...]
    y = cw[0:1] * st_ref[0] + cw[1:2] * st_ref[1] + cw[2:3] * u
    u_ref[...] = u
    y_conv = proj_ref[:, OFF_B:OFF_C] * y
    x1_ref[...] = _merge_project(x_ref[...], attn_o, y_conv, proj_ref[:, OFF_GA:OFF_GB], proj_ref[:, OFF_GB:IN_DIM],
                                 w_co_ref, w_o_ref, lng_ref, lnb_ref, alpha)


def _sample_post(x, att, proj, state, conv_w, w_ao, w_co, w_o, ln_g, ln_b, *, alpha):
    m, d = x.shape
    kernel = functools.partial(_sample_post_kernel, alpha=alpha)
    return pl.pallas_call(
        kernel,
        out_shape=[jax.ShapeDtypeStruct((m, d), F32), jax.ShapeDtypeStruct((m, CONV_DIM), F32)],
        compiler_params=pltpu.CompilerParams(vmem_limit_bytes=VMEM_LIMIT),
        name="sample_post",
    )(x, att, proj, state, conv_w, w_ao, w_co, w_o, ln_g, ln_b)


def _first_max(cur, ids, axes, big):
    m = cur
    for ax in axes:
        m = jnp.max(m, axis=ax, keepdims=True)
    idx = jnp.where(cur == m, ids, big)
    for ax in axes:
        idx = jnp.min(idx, axis=ax, keepdims=True)
    return m, idx


def _router_kernel(x_ref, rwt_ref, rb_ref, eidx_ref, gate_ref, *, tm, slab_in):
    x = _slab_load(x_ref, tm) if slab_in else x_ref[...]
    logits_t = _dot_nt(rwt_ref[...], x.astype(BF16))
    scores = jax.nn.sigmoid(logits_t)
    sel = scores + rb_ref[...]
    shape3 = (N_GROUPS, GROUP_SIZE, tm)
    sel3 = sel.reshape(shape3)
    scores3 = scores.reshape(shape3)
    member = lax.broadcasted_iota(jnp.int32, shape3, 1)
    m1, i1 = _first_max(sel3, member, (1,), GROUP_SIZE)
    m2 = jnp.max(jnp.where(member == i1, -jnp.inf, sel3), axis=1, keepdims=True)
    gscore = m1 + m2
    gid = lax.broadcasted_iota(jnp.int32, gscore.shape, 0)
    gsel = jnp.zeros(gscore.shape, jnp.bool_)
    for _ in range(TOPK_GROUPS):
        _, gi = _first_max(gscore, gid, (0,), N_GROUPS)
        hit = gid == gi
        gsel = jnp.logical_or(gsel, hit)
        gscore = jnp.where(hit, -jnp.inf, gscore)
    eid = lax.broadcasted_iota(jnp.int32, shape3, 0) * GROUP_SIZE + member
    cur = jnp.where(gsel, sel3, -jnp.inf)
    ids, ws = [], []
    for _ in range(TOP_K):
        _, ei = _first_max(cur, eid, (1, 0), N_EXPERTS)
        hit = eid == ei
        sc = jnp.sum(jnp.sum(jnp.where(hit, scores3, 0.0), axis=1, keepdims=True), axis=0, keepdims=True)
        ids.append(ei[0])
        ws.append(sc[0])
        cur = jnp.where(hit, -jnp.inf, cur)
    w = jnp.concatenate(ws, axis=0)
    tot = jnp.sum(w, axis=0, keepdims=True)
    eidx_ref[...] = jnp.concatenate(ids, axis=0)
    gate_ref[...] = w / tot * ROUTED_SCALE


def _router(x, rw_t, rb_col, *, tm, tokens=None):
    slab_in = x.ndim == 3
    if slab_in:
        t = tokens
        steps_per_chunk = (t // x.shape[0]) // tm
        assert t % (x.shape[0] * tm) == 0
        x_spec = pl.BlockSpec((None, tm * SLAB_ROWS, LANES), lambda i: (i // steps_per_chunk, i % steps_per_chunk, 0))
    else:
        t = x.shape[0]
        x_spec = pl.BlockSpec((tm, D_MODEL), lambda i: (i, 0))
    return pl.pallas_call(
        functools.partial(_router_kernel, tm=tm, slab_in=slab_in),
        grid=(t // tm,),
        in_specs=[
            x_spec,
            pl.BlockSpec(rw_t.shape, lambda i: (0, 0)),
            pl.BlockSpec(rb_col.shape, lambda i: (0, 0)),
        ],
        out_specs=[pl.BlockSpec((TOP_K, tm), lambda i: (0, i)), pl.BlockSpec((TOP_K, tm), lambda i: (0, i))],
        out_shape=[jax.ShapeDtypeStruct((TOP_K, t), jnp.int32), jax.ShapeDtypeStruct((TOP_K, t), F32)],
        compiler_params=pltpu.CompilerParams(dimension_semantics=("arbitrary",), vmem_limit_bytes=VMEM_LIMIT),
        name="router",
    )(x, rw_t, rb_col)


ROW_TILE = 288
TILE_PITCH = ROW_TILE + 1
GATE_LANES = -(-ROW_TILE // LANES) * LANES
SPARE_TOKENS = 8
PAIR_BITS = 16
FLAG_FIRST, FLAG_LAST, FLAG_NEW_EXPERT, FLAG_VALID, FLAG_HAS_NEXT = 1, 2, 4, 8, 16
SCATTER_BATCH = 8


PREV, CUR, NEXT = 0, 1, 2


def _moe_routed_kernel(ce_ref, flags_ref, used_ref, nexte_ref, idx_ref, gate_ref, x_ref, wg_ref, wu_ref,
                       wd_ref, sg_ref, su_ref, sd_ref, lng_ref, lnb_ref, o_ref, wgu_ref, wdb_ref,
                       gat0_ref, gat1_ref, res0_ref, res1_ref, wsg_ref, wsu_ref, wsd_ref, wsem_ref, *,
                       layer, alpha, chunk_tokens, sub_tokens):
    del used_ref
    step = pl.program_id(0)
    flags = flags_ref[step]
    odd = (step & 1) == 1
    nchunk = D_MODEL // LANES
    ne, ff = wg_ref.shape[1], wg_ref.shape[3]

    def slab_row(which, r):
        return pl.multiple_of(idx_ref[0, which, r], 8)

    def gather_row(which, gat_ref, r):
        gat_ref[pl.ds(r, nchunk, stride=TILE_PITCH), :] = x_ref[pl.ds(slab_row(which, r), 8), :]

    def scatter_rows(which, res_ref, rows):
        dst = [slab_row(which, r) for r in rows]
        acc = [o_ref[pl.ds(d, 8), :] + res_ref[pl.ds(r, nchunk, stride=TILE_PITCH), :] for d, r in zip(dst, rows)]
        for d, a in zip(dst, acc):
            o_ref[pl.ds(d, 8), :] = a

    def expert_mlp(gat_ref, res_ref):
        lhs = jnp.concatenate([gat_ref[j * TILE_PITCH:j * TILE_PITCH + ROW_TILE, :] for j in range(nchunk)],
                              axis=1).astype(BF16)
        h = _dot(lhs, wgu_ref[...])
        gate_col = jnp.broadcast_to(gate_ref[0], (LANES, GATE_LANES)).T[0:ROW_TILE]
        hid = jax.nn.silu(h[:, 0:ff]) * h[:, ff:2 * ff] * jnp.concatenate([gate_col] * (ff // LANES), axis=1)
        y = _dot(hid.astype(BF16), wdb_ref[...])
        for j in range(nchunk):
            res_ref[j * TILE_PITCH:j * TILE_PITCH + ROW_TILE, :] = y[:, j * LANES:(j + 1) * LANES]

    def by_parity(fn):
        @pl.when(jnp.logical_not(odd))
        def _even():
            fn(gat0_ref, gat1_ref, res0_ref, res1_ref)

        @pl.when(odd)
        def _odd():
            fn(gat1_ref, gat0_ref, res1_ref, res0_ref)

    @pl.when((flags & FLAG_FIRST) != 0)
    def _start_chunk():
        o_ref[...] = jnp.zeros(o_ref.shape, F32)

        def start(gat_cur, gat_other, res_cur, res_other):
            res_other[...] = jnp.zeros(res_other.shape, F32)

            def body(r, carry):
                gather_row(CUR, gat_cur, r)
                return carry

            lax.fori_loop(0, ROW_TILE, body, 0)

        by_parity(start)

    def weight_copies(e):
        return [pltpu.make_async_copy(src.at[layer, e], dst, wsem_ref.at[k])
                for k, (src, dst) in enumerate(((wg_ref, wsg_ref), (wu_ref, wsu_ref), (wd_ref, wsd_ref)))]

    @pl.when(step == 0)
    def _first_fetch():
        for cp in weight_copies(ce_ref[0] % ne):
            cp.start()

    @pl.when((flags & FLAG_NEW_EXPERT) != 0)
    def _next_expert():
        for cp in weight_copies(0):
            cp.wait()
        wgu_ref[:, 0:ff] = wsg_ref[...].astype(BF16)
        wgu_ref[:, ff:2 * ff] = wsu_ref[...].astype(BF16)
        wdb_ref[...] = wsd_ref[...].astype(BF16)

        @pl.when((flags & FLAG_HAS_NEXT) != 0)
        def _prefetch():
            for cp in weight_copies(nexte_ref[step]):
                cp.start()

    @pl.when((flags & FLAG_VALID) != 0)
    def _tile():
        def main(gat_cur, gat_other, res_cur, res_other):
            for r0 in range(0, ROW_TILE, SCATTER_BATCH):
                scatter_rows(PREV, res_other, range(r0, r0 + SCATTER_BATCH))
            expert_mlp(gat_cur, res_cur)
            for r in range(ROW_TILE):
                gather_row(NEXT, gat_other, r)

        by_parity(main)

    @pl.when((flags & FLAG_LAST) != 0)
    def _finish():
        def flush(gat_cur, gat_other, res_cur, res_other):
            def body(r, carry):
                scatter_rows(CUR, res_cur, [r])
                return carry

            lax.fori_loop(0, ROW_TILE, body, 0)

        by_parity(flush)

        def body(s, carry):
            base = pl.multiple_of(s * (sub_tokens * 8), 8)

            def rows_2d(ref):
                return jnp.concatenate([ref[pl.ds(base + j, sub_tokens, stride=8), :] for j in range(nchunk)],
                                       axis=1)

            x2 = rows_2d(x_ref)
            xb = x2.astype(BF16)
            hs = jax.nn.silu(_dot(xb, sg_ref[...])) * _dot(xb, su_ref[...])
            ffn = rows_2d(o_ref) + _dot(hs.astype(BF16), sd_ref[...])
            res = _layer_norm(alpha * x2 + ffn, lng_ref[...], lnb_ref[...])
            for j in range(nchunk):
                o_ref[pl.ds(base + j, sub_tokens, stride=8), :] = res[:, j * LANES:(j + 1) * LANES]
            return carry

        lax.fori_loop(0, chunk_tokens // sub_tokens, body, 0)


def _moe_routed(tile_ce, tile_flags, n_used, next_expert, row_idx, gates, x_tm, wg, wu, wd, sg, su, sd, ln_g,
                ln_b, *, layer, alpha, chunk_tokens, sub_tokens):
    n_chunks, chunk_rows, _ = x_tm.shape
    n_tiles = row_idx.shape[0]
    _, ne, d, ff = wg.shape
    kernel = functools.partial(_moe_routed_kernel, layer=layer, alpha=alpha, chunk_tokens=chunk_tokens,
                               sub_tokens=sub_tokens)

    def tile_map(i, ce, fl, used, nxt):
        return (jnp.minimum(i, used[0] - 1), 0, 0)

    def chunk_map(i, ce, fl, used, nxt):
        return (ce[i] // ne, 0, 0)

    def const2(i, ce, fl, used, nxt):
        return (0, 0)

    grid_spec = pltpu.PrefetchScalarGridSpec(
        num_scalar_prefetch=4,
        grid=(n_tiles,),
        in_specs=[
            pl.BlockSpec((1, 3, ROW_TILE), tile_map, memory_space=pltpu.SMEM),
            pl.BlockSpec((1, 1, GATE_LANES), tile_map),
            pl.BlockSpec((None, chunk_rows, LANES), chunk_map, pipeline_mode=pl.Buffered(1)),
            pl.BlockSpec(memory_space=pl.ANY),
            pl.BlockSpec(memory_space=pl.ANY),
            pl.BlockSpec(memory_space=pl.ANY),
            pl.BlockSpec(sg.shape, const2, pipeline_mode=pl.Buffered(1)),
            pl.BlockSpec(su.shape, const2, pipeline_mode=pl.Buffered(1)),
            pl.BlockSpec(sd.shape, const2, pipeline_mode=pl.Buffered(1)),
            pl.BlockSpec(ln_g.shape, const2),
            pl.BlockSpec(ln_b.shape, const2),
        ],
        out_specs=pl.BlockSpec((None, chunk_rows, LANES), chunk_map, pipeline_mode=pl.Buffered(1)),
        scratch_shapes=[
            pltpu.VMEM((d, 2 * ff), BF16),
            pltpu.VMEM((ff, d), BF16),
            pltpu.VMEM((8 * TILE_PITCH, LANES), F32),
            pltpu.VMEM((8 * TILE_PITCH, LANES), F32),
            pltpu.VMEM((8 * TILE_PITCH, LANES), F32),
            pltpu.VMEM((8 * TILE_PITCH, LANES), F32),
            pltpu.VMEM((d, ff), F32),
            pltpu.VMEM((d, ff), F32),
            pltpu.VMEM((ff, d), F32),
            pltpu.SemaphoreType.DMA((3,)),
        ],
    )
    return pl.pallas_call(
        kernel,
        grid_spec=grid_spec,
        out_shape=jax.ShapeDtypeStruct(x_tm.shape, F32),
        compiler_params=pltpu.CompilerParams(dimension_semantics=("arbitrary",), vmem_limit_bytes=VMEM_LIMIT),
        name="moe_routed",
    )(tile_ce, tile_flags, n_used, next_expert, row_idx, gates, x_tm, wg, wu, wd, sg, su, sd, ln_g, ln_b)


MOE_CHUNKS = 4
MOE_SUB_MAX = 512


def _moe_tiling(n_prompt, n_sample):
    assert n_prompt % (MOE_CHUNKS * 8) == 0 and n_sample % (MOE_CHUNKS * 8) == 0
    chunk_tokens = (n_prompt + n_sample) // MOE_CHUNKS
    assert (chunk_tokens + 1) * TOP_K <= 1 << PAIR_BITS
    sub_tokens = max(s for s in range(8, MOE_SUB_MAX + 1, 8) if chunk_tokens % s == 0)
    return MOE_CHUNKS, chunk_tokens, sub_tokens


def _chunked(prompt, sample, n_chunks):
    return jnp.concatenate([prompt.reshape((n_chunks, -1) + prompt.shape[1:]),
                            sample.reshape((n_chunks, -1) + sample.shape[1:])], axis=1)


def _route_plan(eidx_p, eidx_s, gate_p, gate_s, *, chunk_tokens, n_chunks):
    n_seg = n_chunks * N_EXPERTS
    eidx = _chunked(eidx_p.T, eidx_s.T, n_chunks)
    gate = _chunked(gate_p.T, gate_s.T, n_chunks)
    seg = jnp.arange(n_chunks, dtype=jnp.int32)[:, None, None] * N_EXPERTS + eidx
    pair = jnp.arange(chunk_tokens * TOP_K, dtype=jnp.int32).reshape(1, chunk_tokens, TOP_K)
    keys_real = ((seg << PAIR_BITS) | pair).reshape(-1)
    counts = jnp.sum(eidx[..., None] == jnp.arange(N_EXPERTS, dtype=jnp.int32), axis=(1, 2),
                     dtype=jnp.int32).reshape(n_seg)
    n_pad = (-counts) % ROW_TILE
    slot = jnp.arange(ROW_TILE - 1, dtype=jnp.int32)
    int_max = jnp.iinfo(jnp.int32).max
    pad_pair = (1 << PAIR_BITS) - 1
    keys_pad = jnp.where(slot[None, :] < n_pad[:, None],
                         (jnp.arange(n_seg, dtype=jnp.int32)[:, None] << PAIR_BITS) | pad_pair, int_max)
    n_fill = (-(keys_real.size + keys_pad.size)) % ROW_TILE
    keys = jnp.concatenate([keys_real, keys_pad.reshape(-1), jnp.full((n_fill,), int_max, jnp.int32)])
    vals = jnp.concatenate([gate.reshape(-1), jnp.zeros((keys_pad.size + n_fill,), F32)])
    n_tiles = keys.size // ROW_TILE
    keys, vals = lax.sort((keys, vals), num_keys=1)
    keys = keys.reshape(n_tiles, ROW_TILE)
    head = keys[:, 0]
    valid = head != int_max
    tile_ce = jnp.minimum(head >> PAIR_BITS, n_seg - 1)
    tile_c = tile_ce // N_EXPERTS
    prev_ce = jnp.concatenate([jnp.full((1,), -N_EXPERTS, jnp.int32), tile_ce[:-1]])
    next_c = jnp.concatenate([tile_c[1:], jnp.full((1,), -1, jnp.int32)])
    next_valid = jnp.concatenate([valid[1:], jnp.zeros((1,), jnp.bool_)])
    first = valid & (tile_c != prev_ce // N_EXPERTS)
    last = valid & (~next_valid | (next_c != tile_c))
    new_expert = valid & (tile_ce != prev_ce)
    tile_id = jnp.arange(n_tiles, dtype=jnp.int32)
    change_at = jnp.where(new_expert, tile_id, n_tiles)
    next_change = jnp.concatenate([lax.cummin(change_at, reverse=True)[1:], jnp.full((1,), n_tiles, jnp.int32)])
    has_next = new_expert & (next_change < n_tiles)
    next_expert = tile_ce[jnp.minimum(next_change, n_tiles - 1)] % N_EXPERTS
    flags = (first * FLAG_FIRST + last * FLAG_LAST + new_expert * FLAG_NEW_EXPERT + valid * FLAG_VALID
             + has_next * FLAG_HAS_NEXT)
    row_idx = jnp.minimum((keys & pad_pair) // TOP_K, chunk_tokens) * 8
    row_idx3 = jnp.stack([jnp.concatenate([row_idx[:1], row_idx[:-1]]), row_idx,
                          jnp.concatenate([row_idx[1:], row_idx[-1:]])], axis=1)
    n_used = jnp.sum(valid, dtype=jnp.int32).reshape(1)
    gates = jnp.pad(vals.reshape(n_tiles, 1, ROW_TILE), ((0, 0), (0, 0), (0, GATE_LANES - ROW_TILE)))
    return tile_ce, flags.astype(jnp.int32), n_used, next_expert.astype(jnp.int32), row_idx3, gates


def _t5_bucket(dist):
    n = jnp.maximum(dist, 0)
    max_exact = N_BUCKETS // 2
    large = max_exact + (jnp.log(jnp.maximum(n, 1).astype(F32) / max_exact)
                         / math.log(MAX_DISTANCE / max_exact) * (N_BUCKETS - max_exact)).astype(jnp.int32)
    large = jnp.minimum(large, N_BUCKETS - 1)
    return jnp.where(n < max_exact, n, large)


def _bias_lookup(rel_bias, bucket):
    onehot = (bucket[..., None] == jnp.arange(N_BUCKETS, dtype=bucket.dtype)).astype(F32)
    return jnp.einsum("...b,bh->h...", onehot, rel_bias.astype(F32), precision=lax.Precision.HIGHEST)


def _bias_tables(rel_bias, win):
    qi = jnp.arange(ATTN_BLOCK)[:, None]
    ki = jnp.arange(2 * ATTN_BLOCK)[None, :]
    dist = qi + ATTN_BLOCK - ki
    valid = (dist >= 0) & (dist < WINDOW)
    bias = _bias_lookup(rel_bias, _t5_bucket(dist))
    bias = jnp.where(valid[None], bias, NEG).reshape(N_KV_HEADS, GROUP, ATTN_BLOCK, 2 * ATTN_BLOCK)
    bias_ab = jnp.stack([jnp.concatenate([bias[:, t], bias[:, t + 2]], axis=1) for t in range(2)], axis=1)
    dist_s = (win - 1) - jnp.arange(win)
    bias_s = _bias_lookup(rel_bias, _t5_bucket(dist_s))
    return bias_ab, bias_s


def _sink_tables(sink):
    s = sink.astype(F32).reshape(N_KV_HEADS, GROUP)
    rows = [jnp.concatenate([jnp.broadcast_to(s[:, t, None], (N_KV_HEADS, ATTN_BLOCK)),
                             jnp.broadcast_to(s[:, t + 2, None], (N_KV_HEADS, ATTN_BLOCK))], axis=1)
            for t in range(2)]
    return jnp.stack(rows, axis=1)[..., None], sink.astype(F32)[:, None]


def kernel(x_prompt, x_sample, cache_k_win, cache_v_win, state_conv, rel_bias, w_in, attn_sink, conv_w,
           w_attn_out, w_conv_out, w_out, ln1_g, ln1_b, router_w, router_bias, exp_w_gate, exp_w_up,
           exp_w_down, shared_w_gate, shared_w_up, shared_w_down, ln2_g, ln2_b):
    depth = w_in.shape[0]
    alpha = (2 * depth) ** 0.25
    nb, seq, d = x_prompt.shape
    nd = x_sample.shape[0]
    win = cache_k_win.shape[2]
    assert x_sample.shape[1] == 1 and win == WINDOW and seq % 512 == 0

    n_prompt = nb * seq
    n_chunks, chunk_tokens, sub_tokens = _moe_tiling(n_prompt, nd)
    prompt_rows = n_prompt // n_chunks * SLAB_ROWS
    slab_shape = (n_chunks, (chunk_tokens + SPARE_TOKENS) * SLAB_ROWS, LANES)

    bias_ab, bias_s = _bias_tables(rel_bias, win)
    hmask = (jnp.arange(KV_DIM)[None, :] // HEAD_DIM == jnp.arange(N_HEADS)[:, None] // GROUP).astype(F32)

    yp = x_prompt
    ys = x_sample.reshape(nd, d)
    outs = [[] for _ in range(6)]
    for l in range(depth):
        w_in_b = w_in[l].astype(BF16)
        w_ao_b = w_attn_out[l].astype(BF16)
        w_co_b = w_conv_out[l].astype(BF16)
        w_o_b = w_out[l].astype(BF16)
        g1, b1 = ln1_g[l][None, :], ln1_b[l][None, :]
        g2, b2 = ln2_g[l][None, :], ln2_b[l][None, :]
        sink_ab, sink_col = _sink_tables(attn_sink[l])

        slab, kp, vp, cp = _mixer_prompt(yp, w_in_b, w_attn_out[l].T.astype(BF16), w_co_b, w_o_b, conv_w[l],
                                         jnp.swapaxes(bias_ab, -1, -2), jnp.swapaxes(sink_ab, -1, -2), g1, b1,
                                         alpha=alpha, tq=512, batch=nb, seq=seq, slab_shape=slab_shape,
                                         base=jnp.zeros(slab_shape, F32) if l == 0 else None)

        proj = _sample_proj(ys, w_in_b, tn=IN_DIM // 4)
        q4 = proj[:, :Q_DIM].reshape(nd, N_KV_HEADS, GROUP, HEAD_DIM).transpose(0, 2, 1, 3).reshape(nd, GROUP, KV_DIM)
        ksn, vsn, ag = _sample_attn(q4, proj[:, OFF_K:OFF_V], proj[:, OFF_V:OFF_B],
                                    cache_k_win.reshape(depth, nd, win, KV_DIM),
                                    cache_v_win.reshape(depth, nd, win, KV_DIM),
                                    bias_s, sink_col, hmask, layer=l, bt=8)
        att = ag.reshape(nd, GROUP, N_KV_HEADS, HEAD_DIM).transpose(0, 2, 1, 3).reshape(nd, Q_DIM)
        state_t = jnp.swapaxes(state_conv[l], 0, 1)
        ys, us = _sample_post(ys, att, proj, state_t, conv_w[l], w_ao_b, w_co_b, w_o_b, g1, b1, alpha=alpha)

        outs[0].append(kp.reshape(nb, WINDOW, N_KV_HEADS, HEAD_DIM))
        outs[1].append(vp.reshape(nb, WINDOW, N_KV_HEADS, HEAD_DIM))
        outs[2].append(cp)
        outs[3].append(ksn.reshape(nd, win, N_KV_HEADS, HEAD_DIM))
        outs[4].append(vsn.reshape(nd, win, N_KV_HEADS, HEAD_DIM))
        outs[5].append(jnp.concatenate([state_conv[l][:, 1:], us[:, None, :]], axis=1))

        rw_t = router_w[l].T.astype(BF16)
        rb_col = router_bias[l].astype(F32)[:, None]
        sg, su, sd = (shared_w_gate[l].astype(BF16), shared_w_up[l].astype(BF16), shared_w_down[l].astype(BF16))
        tail = jnp.concatenate([ys.reshape(n_chunks, -1, LANES),
                                jnp.zeros((n_chunks, SPARE_TOKENS * SLAB_ROWS, LANES), F32)], axis=1)
        slab = lax.dynamic_update_slice(slab, tail, (0, prompt_rows, 0))
        ep, wp = _router(slab, rw_t, rb_col, tm=512, tokens=n_prompt)
        es, ws = _router(ys, rw_t, rb_col, tm=nd)
        plan = _route_plan(ep, es, wp, ws, chunk_tokens=chunk_tokens, n_chunks=n_chunks)
        yp = _moe_routed(*plan, slab, exp_w_gate, exp_w_up, exp_w_down, sg, su, sd, g2, b2,
                         layer=l, alpha=alpha, chunk_tokens=chunk_tokens, sub_tokens=sub_tokens)
        ys = yp[:, prompt_rows:chunk_tokens * SLAB_ROWS].reshape(nd, d)

    y_prompt = yp[:, :prompt_rows].reshape(nb, seq, d)
    return (y_prompt, ys.reshape(nd, 1, d)) + tuple(jnp.stack(o) for o in outs)
```

```python
import functools
import math

import jax
import jax.numpy as jnp
from jax import lax
from jax.experimental import pallas as pl
from jax.experimental.pallas import tpu as pltpu

D_MODEL = 1024
N_HEADS = 16
N_KV_HEADS = 4
HEAD_DIM = 64
GROUP = N_HEADS // N_KV_HEADS
WINDOW = 128
ATTN_BLOCK = 128
N_BUCKETS = 32
MAX_DISTANCE = 128
CONV_DIM = 1024
CONV_K = 3
N_EXPERTS = 64
TOP_K = 8
N_GROUPS = 8
TOPK_GROUPS = 4
GROUP_SIZE = N_EXPERTS // N_GROUPS
EXPERT_FF = 256
ROUTED_SCALE = 2.5
LN_EPS = 1e-5
NEG = -1e30

Q_DIM = N_HEADS * HEAD_DIM
KV_DIM = N_KV_HEADS * HEAD_DIM
OFF_K = Q_DIM
OFF_V = OFF_K + KV_DIM
OFF_B = OFF_V + KV_DIM
OFF_C = OFF_B + CONV_DIM
OFF_H = OFF_C + CONV_DIM
OFF_GA = OFF_H + CONV_DIM
OFF_GB = OFF_GA + D_MODEL
IN_DIM = OFF_GB + D_MODEL

LANES = 128
CONV_PAD = 8
VMEM_LIMIT = 60 * 1024 * 1024

BF16 = jnp.bfloat16
F32 = jnp.float32


def _dot(a, b):
    return jnp.dot(a, b, preferred_element_type=F32)


def _dot_nt(a, b):
    return lax.dot_general(a, b, (((1,), (1,)), ((), ())), preferred_element_type=F32)


def _layer_norm(z, g, b):
    mu = jnp.mean(z, axis=-1, keepdims=True)
    d = z - mu
    var = jnp.mean(d * d, axis=-1, keepdims=True)
    return d * lax.rsqrt(var + LN_EPS) * g + b


def _sink_softmax(s, sink, axis=-1):
    m = jnp.maximum(jnp.max(s, axis=axis, keepdims=True), sink)
    e = jnp.exp(s - m)
    den = jnp.sum(e, axis=axis, keepdims=True) + jnp.exp(sink - m)
    return e * (1.0 / den)


SLAB_ROWS = D_MODEL // LANES


def _slab_load(ref, tokens, base=0):
    return jnp.concatenate([ref[pl.ds(base + j, tokens, stride=SLAB_ROWS), :] for j in range(SLAB_ROWS)], axis=1)


def _slab_store(ref, val, base=0):
    tokens = val.shape[0]
    for j in range(SLAB_ROWS):
        ref[pl.ds(base + j, tokens, stride=SLAB_ROWS), :] = val[:, j * LANES:(j + 1) * LANES]


def _merge_project(x, attn_o, y_conv, g_a, g_b, w_co_ref, w_o_ref, lng_ref, lnb_ref, alpha):
    merged = jax.nn.sigmoid(g_a) * attn_o + jax.nn.sigmoid(g_b) * _dot(y_conv.astype(BF16), w_co_ref[...])
    out = _dot(merged.astype(BF16), w_o_ref[...])
    return _layer_norm(alpha * x + out, lng_ref[...], lnb_ref[...])


def _mixer_prompt_kernel(x_ref, w_in_ref, w_ao_ref, w_co_ref, w_o_ref, convw_ref, bias_ref, sink_ref,
                         lng_ref, lnb_ref, base_ref,
                         x1_ref, kwin_ref, vwin_ref, conv_ref,
                         ka_ref, kb_ref, vat_ref, vbt_ref, s_ref, p_ref, att_ref, ubuf_ref, *, alpha, tq, slab_in):
    del base_ref
    i = pl.program_id(1)
    nblk = tq // ATTN_BLOCK
    half = LANES // 2
    scale = HEAD_DIM ** -0.5
    assert math.frexp(scale)[0] == 0.5

    @pl.when(i == 0)
    def _init():
        for ref in (ka_ref, kb_ref):
            ref[:, 0:ATTN_BLOCK, :] = jnp.zeros((N_KV_HEADS, ATTN_BLOCK, LANES), BF16)
        for ref in (vat_ref, vbt_ref):
            ref[:, :, 0:ATTN_BLOCK] = jnp.zeros((N_KV_HEADS, LANES, ATTN_BLOCK), BF16)
        ubuf_ref[0:CONV_PAD, :] = jnp.zeros((CONV_PAD, CONV_DIM), F32)

    x = _slab_load(x_ref, tq) if slab_in else x_ref[0]
    xb = x.astype(BF16)
    qkv = _dot(xb, w_in_ref[:, 0:OFF_B])

    lo = lax.broadcasted_iota(jnp.int32, (tq, LANES), 1) < half
    zeros_t = jnp.zeros((half, tq), BF16)
    for c in range(N_KV_HEADS // 2):
        chunk = qkv[:, OFF_K + c * LANES: OFF_K + (c + 1) * LANES]
        c_lo = jnp.where(lo, chunk, 0.0)
        c_hi = jnp.where(lo, 0.0, chunk)
        ka_ref[2 * c, ATTN_BLOCK:, :] = c_lo.astype(BF16)
        kb_ref[2 * c, ATTN_BLOCK:, :] = pltpu.roll(c_lo, half, 1).astype(BF16)
        kb_ref[2 * c + 1, ATTN_BLOCK:, :] = c_hi.astype(BF16)
        ka_ref[2 * c + 1, ATTN_BLOCK:, :] = pltpu.roll(c_hi, half, 1).astype(BF16)
        vt = qkv[:, OFF_V + c * LANES: OFF_V + (c + 1) * LANES].T.astype(BF16)
        vat_ref[2 * c, :, ATTN_BLOCK:] = jnp.concatenate([vt[0:half], zeros_t], axis=0)
        vbt_ref[2 * c, :, ATTN_BLOCK:] = jnp.concatenate([zeros_t, vt[0:half]], axis=0)
        vbt_ref[2 * c + 1, :, ATTN_BLOCK:] = jnp.concatenate([zeros_t, vt[half:]], axis=0)
        vat_ref[2 * c + 1, :, ATTN_BLOCK:] = jnp.concatenate([vt[half:], zeros_t], axis=0)

    key_row = lax.broadcasted_iota(jnp.int32, (2 * ATTN_BLOCK, 2 * ATTN_BLOCK), 0)
    for j in range(nblk):
        rows = slice(j * ATTN_BLOCK, (j + 1) * ATTN_BLOCK)
        keys = slice(j * ATTN_BLOCK, (j + 2) * ATTN_BLOCK)
        for h in range(N_KV_HEADS):
            q0 = h * GROUP * HEAD_DIM
            q2 = (jnp.concatenate([qkv[rows, q0:q0 + LANES], qkv[rows, q0 + LANES:q0 + 2 * LANES]], axis=0)
                  * scale).astype(BF16)
            for t, k_ref in enumerate((ka_ref, kb_ref)):
                s = _dot_nt(k_ref[h, keys, :], q2)
                if j == 0:
                    s = jnp.where(jnp.logical_and(i == 0, key_row < ATTN_BLOCK), NEG, s)
                s_ref[j, h, t] = s
    for h in range(N_KV_HEADS):
        for t in range(2):
            s = s_ref[:, h, t] + bias_ref[h, t][None]
            p_ref[:, h, t] = _sink_softmax(s, sink_ref[h, t][None], axis=1).astype(BF16)
    for j in range(nblk):
        rows = slice(j * ATTN_BLOCK, (j + 1) * ATTN_BLOCK)
        keys = slice(j * ATTN_BLOCK, (j + 2) * ATTN_BLOCK)
        for h in range(N_KV_HEADS):
            q0 = h * GROUP * HEAD_DIM
            o_t = _dot(vat_ref[h, :, keys], p_ref[j, h, 0]) + _dot(vbt_ref[h, :, keys], p_ref[j, h, 1])
            att_ref[q0:q0 + LANES, rows] = o_t[:, 0:ATTN_BLOCK].astype(BF16)
            att_ref[q0 + LANES:q0 + 2 * LANES, rows] = o_t[:, ATTN_BLOCK:].astype(BF16)
    for ref in (ka_ref, kb_ref):
        ref[:, 0:ATTN_BLOCK, :] = ref[:, tq:tq + ATTN_BLOCK, :]
    for ref in (vat_ref, vbt_ref):
        ref[:, :, 0:ATTN_BLOCK] = ref[:, :, tq:tq + ATTN_BLOCK]

    kwin_ref[0] = qkv[tq - WINDOW:tq, OFF_K:OFF_V]
    vwin_ref[0] = qkv[tq - WINDOW:tq, OFF_V:OFF_B]

    attn_o = _dot(w_ao_ref[...], att_ref[...]).T

    bch = _dot(xb, w_in_ref[:, OFF_B:OFF_GA])
    u = bch[:, CONV_DIM:2 * CONV_DIM] * bch[:, 2 * CONV_DIM:3 * CONV_DIM]
    ubuf_ref[CONV_PAD:CONV_PAD + tq, :] = u
    cw = convw_ref[...]
    y = (cw[0:1] * ubuf_ref[CONV_PAD - 2:CONV_PAD - 2 + tq, :]
         + cw[1:2] * ubuf_ref[CONV_PAD - 1:CONV_PAD - 1 + tq, :]
         + cw[2:3] * u)
    conv_ref[0] = ubuf_ref[CONV_PAD + tq - (CONV_K - 1):CONV_PAD + tq, :]
    ubuf_ref[0:CONV_PAD, :] = ubuf_ref[tq:tq + CONV_PAD, :]
    y_conv = bch[:, 0:CONV_DIM] * y

    gab = _dot(xb, w_in_ref[:, OFF_GA:IN_DIM])
    _slab_store(x1_ref, _merge_project(x, attn_o, y_conv, gab[:, 0:D_MODEL], gab[:, D_MODEL:], w_co_ref, w_o_ref,
                                       lng_ref, lnb_ref, alpha))


def _const_spec(shape):
    nd = len(shape)
    return pl.BlockSpec(shape, lambda *_: (0,) * nd, pipeline_mode=pl.Buffered(1))


def _mixer_prompt(x, w_in, w_ao, w_co, w_o, conv_w, bias_ab, sink_ab, ln_g, ln_b, *, alpha, tq, batch, seq,
                  slab_shape, base=None):
    b, s = batch, seq
    n_chunks, chunk_rows, _ = slab_shape
    slab_in = x.ndim == 3 and x.shape == slab_shape
    steps_per_chunk = (b * s // n_chunks) // tq
    assert (b * s) % (n_chunks * tq) == 0 and s % tq == 0

    def slab_map(bi, i):
        g = bi * (s // tq) + i
        return (g // steps_per_chunk, g % steps_per_chunk, 0)

    slab_spec = pl.BlockSpec((None, tq * SLAB_ROWS, LANES), slab_map)
    kernel = functools.partial(_mixer_prompt_kernel, alpha=alpha, tq=tq, slab_in=slab_in)
    return pl.pallas_call(
        kernel,
        grid=(b, s // tq),
        in_specs=[
            slab_spec if slab_in else pl.BlockSpec((1, tq, D_MODEL), lambda bi, i: (bi, i, 0)),
            _const_spec(w_in.shape), _const_spec(w_ao.shape), _const_spec(w_co.shape), _const_spec(w_o.shape),
            _const_spec(conv_w.shape), _const_spec(bias_ab.shape), _const_spec(sink_ab.shape),
            _const_spec(ln_g.shape), _const_spec(ln_b.shape),
            pl.BlockSpec(memory_space=pl.ANY),
        ],
        out_specs=[
            slab_spec,
            pl.BlockSpec((1, WINDOW, KV_DIM), lambda bi, i: (bi, 0, 0)),
            pl.BlockSpec((1, WINDOW, KV_DIM), lambda bi, i: (bi, 0, 0)),
            pl.BlockSpec((1, CONV_K - 1, CONV_DIM), lambda bi, i: (bi, 0, 0)),
        ],
        out_shape=[
            jax.ShapeDtypeStruct(slab_shape, F32),
            jax.ShapeDtypeStruct((b, WINDOW, KV_DIM), F32),
            jax.ShapeDtypeStruct((b, WINDOW, KV_DIM), F32),
            jax.ShapeDtypeStruct((b, CONV_K - 1, CONV_DIM), F32),
        ],
        scratch_shapes=[
            pltpu.VMEM((N_KV_HEADS, ATTN_BLOCK + tq, LANES), BF16),
            pltpu.VMEM((N_KV_HEADS, ATTN_BLOCK + tq, LANES), BF16),
            pltpu.VMEM((N_KV_HEADS, LANES, ATTN_BLOCK + tq), BF16),
            pltpu.VMEM((N_KV_HEADS, LANES, ATTN_BLOCK + tq), BF16),
            pltpu.VMEM((tq // ATTN_BLOCK, N_KV_HEADS, 2, 2 * ATTN_BLOCK, 2 * ATTN_BLOCK), F32),
            pltpu.VMEM((tq // ATTN_BLOCK, N_KV_HEADS, 2, 2 * ATTN_BLOCK, 2 * ATTN_BLOCK), BF16),
            pltpu.VMEM((Q_DIM, tq), BF16),
            pltpu.VMEM((tq + CONV_PAD, CONV_DIM), F32),
        ],
        compiler_params=pltpu.CompilerParams(
            dimension_semantics=("arbitrary", "arbitrary"), vmem_limit_bytes=VMEM_LIMIT),
        input_output_aliases={0 if slab_in else 10: 0},
        name="mixer_prompt",
    )(x, w_in, w_ao, w_co, w_o, conv_w, bias_ab, sink_ab, ln_g, ln_b,
      jnp.zeros((1, SLAB_ROWS, LANES), F32) if slab_in else base)


def _proj_kernel(x_ref, w_ref, o_ref):
    o_ref[...] = _dot(x_ref[...].astype(BF16), w_ref[...])


def _sample_proj(x, w_in, *, tn):
    m, d = x.shape
    n = w_in.shape[1]
    return pl.pallas_call(
        _proj_kernel,
        grid=(n // tn,),
        in_specs=[pl.BlockSpec((m, d), lambda j: (0, 0)), pl.BlockSpec((d, tn), lambda j: (0, j))],
        out_specs=pl.BlockSpec((m, tn), lambda j: (0, j)),
        out_shape=jax.ShapeDtypeStruct((m, n), F32),
        compiler_params=pltpu.CompilerParams(dimension_semantics=("arbitrary",), vmem_limit_bytes=VMEM_LIMIT),
        name="sample_proj",
    )(x, w_in)


def _sample_attn_kernel(q4_ref, knew_ref, vnew_ref, ck_ref, cv_ref, bias_ref, sink_ref, hmask_ref,
                        nk_ref, nv_ref, ag_ref):
    bt = ck_ref.shape[0]
    win = ck_ref.shape[1]
    scale = HEAD_DIM ** -0.5
    row = lax.broadcasted_iota(jnp.int32, (win, KV_DIM), 0)
    last = row == win - 1
    hmask = hmask_ref[...]
    for b in range(bt):
        kb = jnp.where(last, knew_ref[b:b + 1, :], pltpu.roll(ck_ref[b], win - 1, 0))
        vb = jnp.where(last, vnew_ref[b:b + 1, :], pltpu.roll(cv_ref[b], win - 1, 0))
        nk_ref[b] = kb
        nv_ref[b] = vb
        q4 = q4_ref[b]
        qm = (jnp.concatenate([q4] * N_KV_HEADS, axis=0) * hmask).astype(BF16)
        s = _dot_nt(qm, kb.astype(BF16)) * scale + bias_ref[...]
        p = _sink_softmax(s, sink_ref[...]).astype(BF16)
        o = _dot(p, vb.astype(BF16)) * hmask
        o4 = o[0:GROUP]
        for h in range(1, N_KV_HEADS):
            o4 = o4 + o[h * GROUP:(h + 1) * GROUP]
        ag_ref[b] = o4


def _sample_attn(q4, k_new, v_new, cache_k, cache_v, bias_s, sink_col, hmask, *, layer, bt):
    _, nb, win, kvd = cache_k.shape
    return pl.pallas_call(
        _sample_attn_kernel,
        grid=(nb // bt,),
        in_specs=[
            pl.BlockSpec((bt, GROUP, kvd), lambda i: (i, 0, 0)),
            pl.BlockSpec((bt, kvd), lambda i: (i, 0)),
            pl.BlockSpec((bt, kvd), lambda i: (i, 0)),
            pl.BlockSpec((None, bt, win, kvd), lambda i: (layer, i, 0, 0)),
            pl.BlockSpec((None, bt, win, kvd), lambda i: (layer, i, 0, 0)),
            pl.BlockSpec(bias_s.shape, lambda i: (0, 0)),
            pl.BlockSpec(sink_col.shape, lambda i: (0, 0)),
            pl.BlockSpec(hmask.shape, lambda i: (0, 0)),
        ],
        out_specs=[
            pl.BlockSpec((bt, win, kvd), lambda i: (i, 0, 0)),
            pl.BlockSpec((bt, win, kvd), lambda i: (i, 0, 0)),
            pl.BlockSpec((bt, GROUP, kvd), lambda i: (i, 0, 0)),
        ],
        out_shape=[
            jax.ShapeDtypeStruct((nb, win, kvd), F32),
            jax.ShapeDtypeStruct((nb, win, kvd), F32),
            jax.ShapeDtypeStruct((nb, GROUP, kvd), F32),
        ],
        compiler_params=pltpu.CompilerParams(dimension_semantics=("arbitrary",), vmem_limit_bytes=VMEM_LIMIT),
        name="sample_attn",
    )(q4, k_new, v_new, cache_k, cache_v, bias_s, sink_col, hmask)


def _sample_post_kernel(x_ref, att_ref, proj_ref, st_ref, convw_ref, w_ao_ref, w_co_ref, w_o_ref,
                        lng_ref, lnb_ref, x1_ref, u_ref, *, alpha):
    attn_o = _dot(att_ref[...].astype(BF16), w_ao_ref[...])
    u = proj_ref[:, OFF_C:OFF_H] * proj_ref[:, OFF_H:OFF_GA]
    cw = convw_ref[...]
    y = cw[0:1] * st_ref[0] + cw[1:2] * st_ref[1] + cw[2:3] * u
    u_ref[...] = u
    y_conv = proj_ref[:, OFF_B:OFF_C] * y
    x1_ref[...] = _merge_project(x_ref[...], attn_o, y_conv, proj_ref[:, OFF_GA:OFF_GB], proj_ref[:, OFF_GB:IN_DIM],
                                 w_co_ref, w_o_ref, lng_ref, lnb_ref, alpha)


def _sample_post(x, att, proj, state, conv_w, w_ao, w_co, w_o, ln_g, ln_b, *, alpha):
    m, d = x.shape
    kernel = functools.partial(_sample_post_kernel, alpha=alpha)
    return pl.pallas_call(
        kernel,
        out_shape=[jax.ShapeDtypeStruct((m, d), F32), jax.ShapeDtypeStruct((m, CONV_DIM), F32)],
        compiler_params=pltpu.CompilerParams(vmem_limit_bytes=VMEM_LIMIT),
        name="sample_post",
    )(x, att, proj, state, conv_w, w_ao, w_co, w_o, ln_g, ln_b)


def _first_max(cur, ids, axes, big):
    m = cur
    for ax in axes:
        m = jnp.max(m, axis=ax, keepdims=True)
    idx = jnp.where(cur == m, ids, big)
    for ax in axes:
        idx = jnp.min(idx, axis=ax, keepdims=True)
    return m, idx


def _router_kernel(x_ref, rwt_ref, rb_ref, eidx_ref, gate_ref, *, tm, slab_in):
    x = _slab_load(x_ref, tm) if slab_in else x_ref[...]
    logits_t = _dot_nt(rwt_ref[...], x.astype(BF16))
    scores = jax.nn.sigmoid(logits_t)
    sel = scores + rb_ref[...]
    shape3 = (N_GROUPS, GROUP_SIZE, tm)
    sel3 = sel.reshape(shape3)
    scores3 = scores.reshape(shape3)
    member = lax.broadcasted_iota(jnp.int32, shape3, 1)
    m1, i1 = _first_max(sel3, member, (1,), GROUP_SIZE)
    m2 = jnp.max(jnp.where(member == i1, -jnp.inf, sel3), axis=1, keepdims=True)
    gscore = m1 + m2
    gid = lax.broadcasted_iota(jnp.int32, gscore.shape, 0)
    gsel = jnp.zeros(gscore.shape, jnp.bool_)
    for _ in range(TOPK_GROUPS):
        _, gi = _first_max(gscore, gid, (0,), N_GROUPS)
        hit = gid == gi
        gsel = jnp.logical_or(gsel, hit)
        gscore = jnp.where(hit, -jnp.inf, gscore)
    eid = lax.broadcasted_iota(jnp.int32, shape3, 0) * GROUP_SIZE + member
    cur = jnp.where(gsel, sel3, -jnp.inf)
    ids, ws = [], []
    for _ in range(TOP_K):
        _, ei = _first_max(cur, eid, (1, 0), N_EXPERTS)
        hit = eid == ei
        sc = jnp.sum(jnp.sum(jnp.where(hit, scores3, 0.0), axis=1, keepdims=True), axis=0, keepdims=True)
        ids.append(ei[0])
        ws.append(sc[0])
        cur = jnp.where(hit, -jnp.inf, cur)
    w = jnp.concatenate(ws, axis=0)
    tot = jnp.sum(w, axis=0, keepdims=True)
    eidx_ref[...] = jnp.concatenate(ids, axis=0)
    gate_ref[...] = w / tot * ROUTED_SCALE


def _router(x, rw_t, rb_col, *, tm, tokens=None):
    slab_in = x.ndim == 3
    if slab_in:
        t = tokens
        steps_per_chunk = (t // x.shape[0]) // tm
        assert t % (x.shape[0] * tm) == 0
        x_spec = pl.BlockSpec((None, tm * SLAB_ROWS, LANES), lambda i: (i // steps_per_chunk, i % steps_per_chunk, 0))
    else:
        t = x.shape[0]
        x_spec = pl.BlockSpec((tm, D_MODEL), lambda i: (i, 0))
    return pl.pallas_call(
        functools.partial(_router_kernel, tm=tm, slab_in=slab_in),
        grid=(t // tm,),
        in_specs=[
            x_spec,
            pl.BlockSpec(rw_t.shape, lambda i: (0, 0)),
            pl.BlockSpec(rb_col.shape, lambda i: (0, 0)),
        ],
        out_specs=[pl.BlockSpec((TOP_K, tm), lambda i: (0, i)), pl.BlockSpec((TOP_K, tm), lambda i: (0, i))],
        out_shape=[jax.ShapeDtypeStruct((TOP_K, t), jnp.int32), jax.ShapeDtypeStruct((TOP_K, t), F32)],
        compiler_params=pltpu.CompilerParams(dimension_semantics=("arbitrary",), vmem_limit_bytes=VMEM_LIMIT),
        name="router",
    )(x, rw_t, rb_col)


ROW_TILE = 288
TILE_PITCH = ROW_TILE + 1
GATE_LANES = -(-ROW_TILE // LANES) * LANES
SPARE_TOKENS = 8
PAIR_BITS = 16
FLAG_FIRST, FLAG_LAST, FLAG_NEW_EXPERT, FLAG_VALID, FLAG_HAS_NEXT = 1, 2, 4, 8, 16
SCATTER_BATCH = 8


PREV, CUR, NEXT = 0, 1, 2


def _moe_routed_kernel(ce_ref, flags_ref, used_ref, nexte_ref, idx_ref, gate_ref, x_ref, wg_ref, wu_ref,
                       wd_ref, sg_ref, su_ref, sd_ref, lng_ref, lnb_ref, o_ref, wgu_ref, wdb_ref,
                       gat0_ref, gat1_ref, res0_ref, res1_ref, wsg_ref, wsu_ref, wsd_ref, wsem_ref, *,
                       layer, alpha, chunk_tokens, sub_tokens):
    del used_ref
    step = pl.program_id(0)
    flags = flags_ref[step]
    odd = (step & 1) == 1
    nchunk = D_MODEL // LANES
    ne, ff = wg_ref.shape[1], wg_ref.shape[3]

    def slab_row(which, r):
        return pl.multiple_of(idx_ref[0, which, r], 8)

    def gather_row(which, gat_ref, r):
        gat_ref[pl.ds(r, nchunk, stride=TILE_PITCH), :] = x_ref[pl.ds(slab_row(which, r), 8), :]

    def scatter_rows(which, res_ref, rows):
        dst = [slab_row(which, r) for r in rows]
        acc = [o_ref[pl.ds(d, 8), :] + res_ref[pl.ds(r, nchunk, stride=TILE_PITCH), :] for d, r in zip(dst, rows)]
        for d, a in zip(dst, acc):
            o_ref[pl.ds(d, 8), :] = a

    def expert_mlp(gat_ref, res_ref):
        lhs = jnp.concatenate([gat_ref[j * TILE_PITCH:j * TILE_PITCH + ROW_TILE, :] for j in range(nchunk)],
                              axis=1).astype(BF16)
        h = _dot(lhs, wgu_ref[...])
        gate_col = jnp.broadcast_to(gate_ref[0], (LANES, GATE_LANES)).T[0:ROW_TILE]
        hid = jax.nn.silu(h[:, 0:ff]) * h[:, ff:2 * ff] * jnp.concatenate([gate_col] * (ff // LANES), axis=1)
        y = _dot(hid.astype(BF16), wdb_ref[...])
        for j in range(nchunk):
            res_ref[j * TILE_PITCH:j * TILE_PITCH + ROW_TILE, :] = y[:, j * LANES:(j + 1) * LANES]

    def by_parity(fn):
        @pl.when(jnp.logical_not(odd))
        def _even():
            fn(gat0_ref, gat1_ref, res0_ref, res1_ref)

        @pl.when(odd)
        def _odd():
            fn(gat1_ref, gat0_ref, res1_ref, res0_ref)

    @pl.when((flags & FLAG_FIRST) != 0)
    def _start_chunk():
        o_ref[...] = jnp.zeros(o_ref.shape, F32)

        def start(gat_cur, gat_other, res_cur, res_other):
            res_other[...] = jnp.zeros(res_other.shape, F32)

            def body(r, carry):
                gather_row(CUR, gat_cur, r)
                return carry

            lax.fori_loop(0, ROW_TILE, body, 0)

        by_parity(start)

    def weight_copies(e):
        return [pltpu.make_async_copy(src.at[layer, e], dst, wsem_ref.at[k])
                for k, (src, dst) in enumerate(((wg_ref, wsg_ref), (wu_ref, wsu_ref), (wd_ref, wsd_ref)))]

    @pl.when(step == 0)
    def _first_fetch():
        for cp in weight_copies(ce_ref[0] % ne):
            cp.start()

    @pl.when((flags & FLAG_NEW_EXPERT) != 0)
    def _next_expert():
        for cp in weight_copies(0):
            cp.wait()
        wgu_ref[:, 0:ff] = wsg_ref[...].astype(BF16)
        wgu_ref[:, ff:2 * ff] = wsu_ref[...].astype(BF16)
        wdb_ref[...] = wsd_ref[...].astype(BF16)

        @pl.when((flags & FLAG_HAS_NEXT) != 0)
        def _prefetch():
            for cp in weight_copies(nexte_ref[step]):
                cp.start()

    @pl.when((flags & FLAG_VALID) != 0)
    def _tile():
        def main(gat_cur, gat_other, res_cur, res_other):
            for r0 in range(0, ROW_TILE, SCATTER_BATCH):
                scatter_rows(PREV, res_other, range(r0, r0 + SCATTER_BATCH))
            for r in range(ROW_TILE):
                gather_row(NEXT, gat_other, r)
            expert_mlp(gat_cur, res_cur)

        by_parity(main)

    @pl.when((flags & FLAG_LAST) != 0)
    def _finish():
        def flush(gat_cur, gat_other, res_cur, res_other):
            def body(r, carry):
                scatter_rows(CUR, res_cur, [r])
                return carry

            lax.fori_loop(0, ROW_TILE, body, 0)

        by_parity(flush)

        def body(s, carry):
            base = pl.multiple_of(s * (sub_tokens * 8), 8)

            def rows_2d(ref):
                return jnp.concatenate([ref[pl.ds(base + j, sub_tokens, stride=8), :] for j in range(nchunk)],
                                       axis=1)

            x2 = rows_2d(x_ref)
            xb = x2.astype(BF16)
            hs = jax.nn.silu(_dot(xb, sg_ref[...])) * _dot(xb, su_ref[...])
            ffn = rows_2d(o_ref) + _dot(hs.astype(BF16), sd_ref[...])
            res = _layer_norm(alpha * x2 + ffn, lng_ref[...], lnb_ref[...])
            for j in range(nchunk):
                o_ref[pl.ds(base + j, sub_tokens, stride=8), :] = res[:, j * LANES:(j + 1) * LANES]
            return carry

        lax.fori_loop(0, chunk_tokens // sub_tokens, body, 0)


def _moe_routed(tile_ce, tile_flags, n_used, next_expert, row_idx, gates, x_tm, wg, wu, wd, sg, su, sd, ln_g,
                ln_b, *, layer, alpha, chunk_tokens, sub_tokens):
    n_chunks, chunk_rows, _ = x_tm.shape
    n_tiles = row_idx.shape[0]
    _, ne, d, ff = wg.shape
    kernel = functools.partial(_moe_routed_kernel, layer=layer, alpha=alpha, chunk_tokens=chunk_tokens,
                               sub_tokens=sub_tokens)

    def tile_map(i, ce, fl, used, nxt):
        return (jnp.minimum(i, used[0] - 1), 0, 0)

    def chunk_map(i, ce, fl, used, nxt):
        return (ce[i] // ne, 0, 0)

    def const2(i, ce, fl, used, nxt):
        return (0, 0)

    grid_spec = pltpu.PrefetchScalarGridSpec(
        num_scalar_prefetch=4,
        grid=(n_tiles,),
        in_specs=[
            pl.BlockSpec((1, 3, ROW_TILE), tile_map, memory_space=pltpu.SMEM),
            pl.BlockSpec((1, 1, GATE_LANES), tile_map),
            pl.BlockSpec((None, chunk_rows, LANES), chunk_map, pipeline_mode=pl.Buffered(1)),
            pl.BlockSpec(memory_space=pl.ANY),
            pl.BlockSpec(memory_space=pl.ANY),
            pl.BlockSpec(memory_space=pl.ANY),
            pl.BlockSpec(sg.shape, const2, pipeline_mode=pl.Buffered(1)),
            pl.BlockSpec(su.shape, const2, pipeline_mode=pl.Buffered(1)),
            pl.BlockSpec(sd.shape, const2, pipeline_mode=pl.Buffered(1)),
            pl.BlockSpec(ln_g.shape, const2),
            pl.BlockSpec(ln_b.shape, const2),
        ],
        out_specs=pl.BlockSpec((None, chunk_rows, LANES), chunk_map, pipeline_mode=pl.Buffered(1)),
        scratch_shapes=[
            pltpu.VMEM((d, 2 * ff), BF16),
            pltpu.VMEM((ff, d), BF16),
            pltpu.VMEM((8 * TILE_PITCH, LANES), F32),
            pltpu.VMEM((8 * TILE_PITCH, LANES), F32),
            pltpu.VMEM((8 * TILE_PITCH, LANES), F32),
            pltpu.VMEM((8 * TILE_PITCH, LANES), F32),
            pltpu.VMEM((d, ff), F32),
            pltpu.VMEM((d, ff), F32),
            pltpu.VMEM((ff, d), F32),
            pltpu.SemaphoreType.DMA((3,)),
        ],
    )
    return pl.pallas_call(
        kernel,
        grid_spec=grid_spec,
        out_shape=jax.ShapeDtypeStruct(x_tm.shape, F32),
        compiler_params=pltpu.CompilerParams(dimension_semantics=("arbitrary",), vmem_limit_bytes=VMEM_LIMIT),
        name="moe_routed",
    )(tile_ce, tile_flags, n_used, next_expert, row_idx, gates, x_tm, wg, wu, wd, sg, su, sd, ln_g, ln_b)


MOE_CHUNKS = 4
MOE_SUB_MAX = 512


def _moe_tiling(n_prompt, n_sample):
    assert n_prompt % (MOE_CHUNKS * 8) == 0 and n_sample % (MOE_CHUNKS * 8) == 0
    chunk_tokens = (n_prompt + n_sample) // MOE_CHUNKS
    assert (chunk_tokens + 1) * TOP_K <= 1 << PAIR_BITS
    sub_tokens = max(s for s in range(8, MOE_SUB_MAX + 1, 8) if chunk_tokens % s == 0)
    return MOE_CHUNKS, chunk_tokens, sub_tokens


def _chunked(prompt, sample, n_chunks):
    return jnp.concatenate([prompt.reshape((n_chunks, -1) + prompt.shape[1:]),
                            sample.reshape((n_chunks, -1) + sample.shape[1:])], axis=1)


def _route_plan(eidx_p, eidx_s, gate_p, gate_s, *, chunk_tokens, n_chunks):
    n_seg = n_chunks * N_EXPERTS
    eidx = _chunked(eidx_p.T, eidx_s.T, n_chunks)
    gate = _chunked(gate_p.T, gate_s.T, n_chunks)
    seg = jnp.arange(n_chunks, dtype=jnp.int32)[:, None, None] * N_EXPERTS + eidx
    pair = jnp.arange(chunk_tokens * TOP_K, dtype=jnp.int32).reshape(1, chunk_tokens, TOP_K)
    keys_real = ((seg << PAIR_BITS) | pair).reshape(-1)
    counts = jnp.sum(eidx[..., None] == jnp.arange(N_EXPERTS, dtype=jnp.int32), axis=(1, 2),
                     dtype=jnp.int32).reshape(n_seg)
    n_pad = (-counts) % ROW_TILE
    slot = jnp.arange(ROW_TILE - 1, dtype=jnp.int32)
    int_max = jnp.iinfo(jnp.int32).max
    pad_pair = (1 << PAIR_BITS) - 1
    keys_pad = jnp.where(slot[None, :] < n_pad[:, None],
                         (jnp.arange(n_seg, dtype=jnp.int32)[:, None] << PAIR_BITS) | pad_pair, int_max)
    n_fill = (-(keys_real.size + keys_pad.size)) % ROW_TILE
    keys = jnp.concatenate([keys_real, keys_pad.reshape(-1), jnp.full((n_fill,), int_max, jnp.int32)])
    vals = jnp.concatenate([gate.reshape(-1), jnp.zeros((keys_pad.size + n_fill,), F32)])
    n_tiles = keys.size // ROW_TILE
    keys, vals = lax.sort((keys, vals), num_keys=1)
    keys = keys.reshape(n_tiles, ROW_TILE)
    head = keys[:, 0]
    valid = head != int_max
    tile_ce = jnp.minimum(head >> PAIR_BITS, n_seg - 1)
    tile_c = tile_ce // N_EXPERTS
    prev_ce = jnp.concatenate([jnp.full((1,), -N_EXPERTS, jnp.int32), tile_ce[:-1]])
    next_c = jnp.concatenate([tile_c[1:], jnp.full((1,), -1, jnp.int32)])
    next_valid = jnp.concatenate([valid[1:], jnp.zeros((1,), jnp.bool_)])
    first = valid & (tile_c != prev_ce // N_EXPERTS)
    last = valid & (~next_valid | (next_c != tile_c))
    new_expert = valid & (tile_ce != prev_ce)
    tile_id = jnp.arange(n_tiles, dtype=jnp.int32)
    change_at = jnp.where(new_expert, tile_id, n_tiles)
    next_change = jnp.concatenate([lax.cummin(change_at, reverse=True)[1:], jnp.full((1,), n_tiles, jnp.int32)])
    has_next = new_expert & (next_change < n_tiles)
    next_expert = tile_ce[jnp.minimum(next_change, n_tiles - 1)] % N_EXPERTS
    flags = (first * FLAG_FIRST + last * FLAG_LAST + new_expert * FLAG_NEW_EXPERT + valid * FLAG_VALID
             + has_next * FLAG_HAS_NEXT)
    row_idx = jnp.minimum((keys & pad_pair) // TOP_K, chunk_tokens) * 8
    row_idx3 = jnp.stack([jnp.concatenate([row_idx[:1], row_idx[:-1]]), row_idx,
                          jnp.concatenate([row_idx[1:], row_idx[-1:]])], axis=1)
    n_used = jnp.sum(valid, dtype=jnp.int32).reshape(1)
    gates = jnp.pad(vals.reshape(n_tiles, 1, ROW_TILE), ((0, 0), (0, 0), (0, GATE_LANES - ROW_TILE)))
    return tile_ce, flags.astype(jnp.int32), n_used, next_expert.astype(jnp.int32), row_idx3, gates


def _t5_bucket(dist):
    n = jnp.maximum(dist, 0)
    max_exact = N_BUCKETS // 2
    large = max_exact + (jnp.log(jnp.maximum(n, 1).astype(F32) / max_exact)
                         / math.log(MAX_DISTANCE / max_exact) * (N_BUCKETS - max_exact)).astype(jnp.int32)
    large = jnp.minimum(large, N_BUCKETS - 1)
    return jnp.where(n < max_exact, n, large)


def _bias_lookup(rel_bias, bucket):
    onehot = (bucket[..., None] == jnp.arange(N_BUCKETS, dtype=bucket.dtype)).astype(F32)
    return jnp.einsum("...b,bh->h...", onehot, rel_bias.astype(F32), precision=lax.Precision.HIGHEST)


def _bias_tables(rel_bias, win):
    qi = jnp.arange(ATTN_BLOCK)[:, None]
    ki = jnp.arange(2 * ATTN_BLOCK)[None, :]
    dist = qi + ATTN_BLOCK - ki
    valid = (dist >= 0) & (dist < WINDOW)
    bias = _bias_lookup(rel_bias, _t5_bucket(dist))
    bias = jnp.where(valid[None], bias, NEG).reshape(N_KV_HEADS, GROUP, ATTN_BLOCK, 2 * ATTN_BLOCK)
    bias_ab = jnp.stack([jnp.concatenate([bias[:, t], bias[:, t + 2]], axis=1) for t in range(2)], axis=1)
    dist_s = (win - 1) - jnp.arange(win)
    bias_s = _bias_lookup(rel_bias, _t5_bucket(dist_s))
    return bias_ab, bias_s


def _sink_tables(sink):
    s = sink.astype(F32).reshape(N_KV_HEADS, GROUP)
    rows = [jnp.concatenate([jnp.broadcast_to(s[:, t, None], (N_KV_HEADS, ATTN_BLOCK)),
                             jnp.broadcast_to(s[:, t + 2, None], (N_KV_HEADS, ATTN_BLOCK))], axis=1)
            for t in range(2)]
    return jnp.stack(rows, axis=1)[..., None], sink.astype(F32)[:, None]


def kernel(x_prompt, x_sample, cache_k_win, cache_v_win, state_conv, rel_bias, w_in, attn_sink, conv_w,
           w_attn_out, w_conv_out, w_out, ln1_g, ln1_b, router_w, router_bias, exp_w_gate, exp_w_up,
           exp_w_down, shared_w_gate, shared_w_up, shared_w_down, ln2_g, ln2_b):
    depth = w_in.shape[0]
    alpha = (2 * depth) ** 0.25
    nb, seq, d = x_prompt.shape
    nd = x_sample.shape[0]
    win = cache_k_win.shape[2]
    assert x_sample.shape[1] == 1 and win == WINDOW and seq % 512 == 0

    n_prompt = nb * seq
    n_chunks, chunk_tokens, sub_tokens = _moe_tiling(n_prompt, nd)
    prompt_rows = n_prompt // n_chunks * SLAB_ROWS
    slab_shape = (n_chunks, (chunk_tokens + SPARE_TOKENS) * SLAB_ROWS, LANES)

    bias_ab, bias_s = _bias_tables(rel_bias, win)
    hmask = (jnp.arange(KV_DIM)[None, :] // HEAD_DIM == jnp.arange(N_HEADS)[:, None] // GROUP).astype(F32)

    yp = x_prompt
    ys = x_sample.reshape(nd, d)
    outs = [[] for _ in range(6)]
    for l in range(depth):
        w_in_b = w_in[l].astype(BF16)
        w_ao_b = w_attn_out[l].astype(BF16)
        w_co_b = w_conv_out[l].astype(BF16)
        w_o_b = w_out[l].astype(BF16)
        g1, b1 = ln1_g[l][None, :], ln1_b[l][None, :]
        g2, b2 = ln2_g[l][None, :], ln2_b[l][None, :]
        sink_ab, sink_col = _sink_tables(attn_sink[l])

        slab, kp, vp, cp = _mixer_prompt(yp, w_in_b, w_attn_out[l].T.astype(BF16), w_co_b, w_o_b, conv_w[l],
                                         jnp.swapaxes(bias_ab, -1, -2), jnp.swapaxes(sink_ab, -1, -2), g1, b1,
                                         alpha=alpha, tq=512, batch=nb, seq=seq, slab_shape=slab_shape,
                                         base=jnp.zeros(slab_shape, F32) if l == 0 else None)

        proj = _sample_proj(ys, w_in_b, tn=IN_DIM // 4)
        q4 = proj[:, :Q_DIM].reshape(nd, N_KV_HEADS, GROUP, HEAD_DIM).transpose(0, 2, 1, 3).reshape(nd, GROUP, KV_DIM)
        ksn, vsn, ag = _sample_attn(q4, proj[:, OFF_K:OFF_V], proj[:, OFF_V:OFF_B],
                                    cache_k_win.reshape(depth, nd, win, KV_DIM),
                                    cache_v_win.reshape(depth, nd, win, KV_DIM),
                                    bias_s, sink_col, hmask, layer=l, bt=16)
        att = ag.reshape(nd, GROUP, N_KV_HEADS, HEAD_DIM).transpose(0, 2, 1, 3).reshape(nd, Q_DIM)
        state_t = jnp.swapaxes(state_conv[l], 0, 1)
        ys, us = _sample_post(ys, att, proj, state_t, conv_w[l], w_ao_b, w_co_b, w_o_b, g1, b1, alpha=alpha)

        outs[0].append(kp.reshape(nb, WINDOW, N_KV_HEADS, HEAD_DIM))
        outs[1].append(vp.reshape(nb, WINDOW, N_KV_HEADS, HEAD_DIM))
        outs[2].append(cp)
        outs[3].append(ksn.reshape(nd, win, N_KV_HEADS, HEAD_DIM))
        outs[4].append(vsn.reshape(nd, win, N_KV_HEADS, HEAD_DIM))
        outs[5].append(jnp.concatenate([state_conv[l][:, 1:], us[:, None, :]], axis=1))

        rw_t = router_w[l].T.astype(BF16)
        rb_col = router_bias[l].astype(F32)[:, None]
        sg, su, sd = (shared_w_gate[l].astype(BF16), shared_w_up[l].astype(BF16), shared_w_down[l].astype(BF16))
        tail = jnp.concatenate([ys.reshape(n_chunks, -1, LANES),
                                jnp.zeros((n_chunks, SPARE_TOKENS * SLAB_ROWS, LANES), F32)], axis=1)
        slab = lax.dynamic_update_slice(slab, tail, (0, prompt_rows, 0))
        ep, wp = _router(slab, rw_t, rb_col, tm=512, tokens=n_prompt)
        es, ws = _router(ys, rw_t, rb_col, tm=nd)
        plan = _route_plan(ep, es, wp, ws, chunk_tokens=chunk_tokens, n_chunks=n_chunks)
        yp = _moe_routed(*plan, slab, exp_w_gate, exp_w_up, exp_w_down, sg, su, sd, g2, b2,
                         layer=l, alpha=alpha, chunk_tokens=chunk_tokens, sub_tokens=sub_tokens)
        ys = yp[:, prompt_rows:chunk_tokens * SLAB_ROWS].reshape(nd, d)

    y_prompt = yp[:, :prompt_rows].reshape(nb, seq, d)
    return (y_prompt, ys.reshape(nd, 1, d)) + tuple(jnp.stack(o) for o in outs)
```

```python
import functools
import math

import jax
import jax.numpy as jnp
from jax import lax
from jax.experimental import pallas as pl
from jax.experimental.pallas import tpu as pltpu

D_MODEL = 1024
N_HEADS = 16
N_KV_HEADS = 4
HEAD_DIM = 64
GROUP = N_HEADS // N_KV_HEADS
WINDOW = 128
ATTN_BLOCK = 128
N_BUCKETS = 32
MAX_DISTANCE = 128
CONV_DIM = 1024
CONV_K = 3
N_EXPERTS = 64
TOP_K = 8
N_GROUPS = 8
TOPK_GROUPS = 4
GROUP_SIZE = N_EXPERTS // N_GROUPS
EXPERT_FF = 256
ROUTED_SCALE = 2.5
LN_EPS = 1e-5
NEG = -1e30

Q_DIM = N_HEADS * HEAD_DIM
KV_DIM = N_KV_HEADS * HEAD_DIM
OFF_K = Q_DIM
OFF_V = OFF_K + KV_DIM
OFF_B = OFF_V + KV_DIM
OFF_C = OFF_B + CONV_DIM
OFF_H = OFF_C + CONV_DIM
OFF_GA = OFF_H + CONV_DIM
OFF_GB = OFF_GA + D_MODEL
IN_DIM = OFF_GB + D_MODEL

LANES = 128
CONV_PAD = 8
VMEM_LIMIT = 60 * 1024 * 1024

BF16 = jnp.bfloat16
F32 = jnp.float32


def _dot(a, b):
    return jnp.dot(a, b, preferred_element_type=F32)


def _dot_nt(a, b):
    return lax.dot_general(a, b, (((1,), (1,)), ((), ())), preferred_element_type=F32)


def _layer_norm(z, g, b):
    mu = jnp.mean(z, axis=-1, keepdims=True)
    d = z - mu
    var = jnp.mean(d * d, axis=-1, keepdims=True)
    return d * lax.rsqrt(var + LN_EPS) * g + b


def _sink_softmax(s, sink, axis=-1):
    m = jnp.maximum(jnp.max(s, axis=axis, keepdims=True), sink)
    e = jnp.exp(s - m)
    den = jnp.sum(e, axis=axis, keepdims=True) + jnp.exp(sink - m)
    return e * (1.0 / den)


SLAB_ROWS = D_MODEL // LANES


def _slab_load(ref, tokens, base=0):
    return jnp.concatenate([ref[pl.ds(base + j, tokens, stride=SLAB_ROWS), :] for j in range(SLAB_ROWS)], axis=1)


def _slab_store(ref, val, base=0):
    tokens = val.shape[0]
    for j in range(SLAB_ROWS):
        ref[pl.ds(base + j, tokens, stride=SLAB_ROWS), :] = val[:, j * LANES:(j + 1) * LANES]


def _merge_project(x, attn_o, y_conv, g_a, g_b, w_co_ref, w_o_ref, lng_ref, lnb_ref, alpha):
    merged = jax.nn.sigmoid(g_a) * attn_o + jax.nn.sigmoid(g_b) * _dot(y_conv.astype(BF16), w_co_ref[...])
    out = _dot(merged.astype(BF16), w_o_ref[...])
    return _layer_norm(alpha * x + out, lng_ref[...], lnb_ref[...])


def _mixer_prompt_kernel(x_ref, w_in_ref, w_ao_ref, w_co_ref, w_o_ref, convw_ref, bias_ref, sink_ref,
                         lng_ref, lnb_ref, base_ref,
                         x1_ref, kwin_ref, vwin_ref, conv_ref,
                         ka_ref, kb_ref, vat_ref, vbt_ref, s_ref, p_ref, att_ref, ubuf_ref, *, alpha, tq, slab_in):
    del base_ref
    i = pl.program_id(1)
    nblk = tq // ATTN_BLOCK
    half = LANES // 2
    scale = HEAD_DIM ** -0.5
    assert math.frexp(scale)[0] == 0.5

    @pl.when(i == 0)
    def _init():
        for ref in (ka_ref, kb_ref):
            ref[:, 0:ATTN_BLOCK, :] = jnp.zeros((N_KV_HEADS, ATTN_BLOCK, LANES), BF16)
        for ref in (vat_ref, vbt_ref):
            ref[:, :, 0:ATTN_BLOCK] = jnp.zeros((N_KV_HEADS, LANES, ATTN_BLOCK), BF16)
        ubuf_ref[0:CONV_PAD, :] = jnp.zeros((CONV_PAD, CONV_DIM), F32)

    x = _slab_load(x_ref, tq) if slab_in else x_ref[0]
    xb = x.astype(BF16)
    qkv = _dot(xb, w_in_ref[:, 0:OFF_B])

    lo = lax.broadcasted_iota(jnp.int32, (tq, LANES), 1) < half
    zeros_t = jnp.zeros((half, tq), BF16)
    for c in range(N_KV_HEADS // 2):
        chunk = qkv[:, OFF_K + c * LANES: OFF_K + (c + 1) * LANES]
        c_lo = jnp.where(lo, chunk, 0.0)
        c_hi = jnp.where(lo, 0.0, chunk)
        ka_ref[2 * c, ATTN_BLOCK:, :] = c_lo.astype(BF16)
        kb_ref[2 * c, ATTN_BLOCK:, :] = pltpu.roll(c_lo, half, 1).astype(BF16)
        kb_ref[2 * c + 1, ATTN_BLOCK:, :] = c_hi.astype(BF16)
        ka_ref[2 * c + 1, ATTN_BLOCK:, :] = pltpu.roll(c_hi, half, 1).astype(BF16)
        vt = qkv[:, OFF_V + c * LANES: OFF_V + (c + 1) * LANES].T.astype(BF16)
        vat_ref[2 * c, :, ATTN_BLOCK:] = jnp.concatenate([vt[0:half], zeros_t], axis=0)
        vbt_ref[2 * c, :, ATTN_BLOCK:] = jnp.concatenate([zeros_t, vt[0:half]], axis=0)
        vbt_ref[2 * c + 1, :, ATTN_BLOCK:] = jnp.concatenate([zeros_t, vt[half:]], axis=0)
        vat_ref[2 * c + 1, :, ATTN_BLOCK:] = jnp.concatenate([vt[half:], zeros_t], axis=0)

    key_row = lax.broadcasted_iota(jnp.int32, (2 * ATTN_BLOCK, 2 * ATTN_BLOCK), 0)
    for j in range(nblk):
        rows = slice(j * ATTN_BLOCK, (j + 1) * ATTN_BLOCK)
        keys = slice(j * ATTN_BLOCK, (j + 2) * ATTN_BLOCK)
        for h in range(N_KV_HEADS):
            q0 = h * GROUP * HEAD_DIM
            q2 = (jnp.concatenate([qkv[rows, q0:q0 + LANES], qkv[rows, q0 + LANES:q0 + 2 * LANES]], axis=0)
                  * scale).astype(BF16)
            for t, k_ref in enumerate((ka_ref, kb_ref)):
                s = _dot_nt(k_ref[h, keys, :], q2)
                if j == 0:
                    s = jnp.where(jnp.logical_and(i == 0, key_row < ATTN_BLOCK), NEG, s)
                s_ref[j, h, t] = s
    for h in range(N_KV_HEADS):
        for t in range(2):
            s = s_ref[:, h, t] + bias_ref[h, t][None]
            p_ref[:, h, t] = _sink_softmax(s, sink_ref[h, t][None], axis=1).astype(BF16)
    for j in range(nblk):
        rows = slice(j * ATTN_BLOCK, (j + 1) * ATTN_BLOCK)
        keys = slice(j * ATTN_BLOCK, (j + 2) * ATTN_BLOCK)
        for h in range(N_KV_HEADS):
            q0 = h * GROUP * HEAD_DIM
            o_t = _dot(vat_ref[h, :, keys], p_ref[j, h, 0]) + _dot(vbt_ref[h, :, keys], p_ref[j, h, 1])
            att_ref[q0:q0 + LANES, rows] = o_t[:, 0:ATTN_BLOCK].astype(BF16)
            att_ref[q0 + LANES:q0 + 2 * LANES, rows] = o_t[:, ATTN_BLOCK:].astype(BF16)
    for ref in (ka_ref, kb_ref):
        ref[:, 0:ATTN_BLOCK, :] = ref[:, tq:tq + ATTN_BLOCK, :]
    for ref in (vat_ref, vbt_ref):
        ref[:, :, 0:ATTN_BLOCK] = ref[:, :, tq:tq + ATTN_BLOCK]

    kwin_ref[0] = qkv[tq - WINDOW:tq, OFF_K:OFF_V]
    vwin_ref[0] = qkv[tq - WINDOW:tq, OFF_V:OFF_B]

    attn_o = _dot(w_ao_ref[...], att_ref[...]).T

    bch = _dot(xb, w_in_ref[:, OFF_B:OFF_GA])
    u = bch[:, CONV_DIM:2 * CONV_DIM] * bch[:, 2 * CONV_DIM:3 * CONV_DIM]
    ubuf_ref[CONV_PAD:CONV_PAD + tq, :] = u
    cw = convw_ref[...]
    y = (cw[0:1] * ubuf_ref[CONV_PAD - 2:CONV_PAD - 2 + tq, :]
         + cw[1:2] * ubuf_ref[CONV_PAD - 1:CONV_PAD - 1 + tq, :]
         + cw[2:3] * u)
    conv_ref[0] = ubuf_ref[CONV_PAD + tq - (CONV_K - 1):CONV_PAD + tq, :]
    ubuf_ref[0:CONV_PAD, :] = ubuf_ref[tq:tq + CONV_PAD, :]
    y_conv = bch[:, 0:CONV_DIM] * y

    gab = _dot(xb, w_in_ref[:, OFF_GA:IN_DIM])
    _slab_store(x1_ref, _merge_project(x, attn_o, y_conv, gab[:, 0:D_MODEL], gab[:, D_MODEL:], w_co_ref, w_o_ref,
                                       lng_ref, lnb_ref, alpha))


def _const_spec(shape):
    nd = len(shape)
    return pl.BlockSpec(shape, lambda *_: (0,) * nd, pipeline_mode=pl.Buffered(1))


def _mixer_prompt(x, w_in, w_ao, w_co, w_o, conv_w, bias_ab, sink_ab, ln_g, ln_b, *, alpha, tq, batch, seq,
                  slab_shape, base=None):
    b, s = batch, seq
    n_chunks, chunk_rows, _ = slab_shape
    slab_in = x.ndim == 3 and x.shape == slab_shape
    steps_per_chunk = (b * s // n_chunks) // tq
    assert (b * s) % (n_chunks * tq) == 0 and s % tq == 0

    def slab_map(bi, i):
        g = bi * (s // tq) + i
        return (g // steps_per_chunk, g % steps_per_chunk, 0)

    slab_spec = pl.BlockSpec((None, tq * SLAB_ROWS, LANES), slab_map)
    kernel = functools.partial(_mixer_prompt_kernel, alpha=alpha, tq=tq, slab_in=slab_in)
    return pl.pallas_call(
        kernel,
        grid=(b, s // tq),
        in_specs=[
            slab_spec if slab_in else pl.BlockSpec((1, tq, D_MODEL), lambda bi, i: (bi, i, 0)),
            _const_spec(w_in.shape), _const_spec(w_ao.shape), _const_spec(w_co.shape), _const_spec(w_o.shape),
            _const_spec(conv_w.shape), _const_spec(bias_ab.shape), _const_spec(sink_ab.shape),
            _const_spec(ln_g.shape), _const_spec(ln_b.shape),
            pl.BlockSpec(memory_space=pl.ANY),
        ],
        out_specs=[
            slab_spec,
            pl.BlockSpec((1, WINDOW, KV_DIM), lambda bi, i: (bi, 0, 0)),
            pl.BlockSpec((1, WINDOW, KV_DIM), lambda bi, i: (bi, 0, 0)),
            pl.BlockSpec((1, CONV_K - 1, CONV_DIM), lambda bi, i: (bi, 0, 0)),
        ],
        out_shape=[
            jax.ShapeDtypeStruct(slab_shape, F32),
            jax.ShapeDtypeStruct((b, WINDOW, KV_DIM), F32),
            jax.ShapeDtypeStruct((b, WINDOW, KV_DIM), F32),
            jax.ShapeDtypeStruct((b, CONV_K - 1, CONV_DIM), F32),
        ],
        scratch_shapes=[
            pltpu.VMEM((N_KV_HEADS, ATTN_BLOCK + tq, LANES), BF16),
            pltpu.VMEM((N_KV_HEADS, ATTN_BLOCK + tq, LANES), BF16),
            pltpu.VMEM((N_KV_HEADS, LANES, ATTN_BLOCK + tq), BF16),
            pltpu.VMEM((N_KV_HEADS, LANES, ATTN_BLOCK + tq), BF16),
            pltpu.VMEM((tq // ATTN_BLOCK, N_KV_HEADS, 2, 2 * ATTN_BLOCK, 2 * ATTN_BLOCK), F32),
            pltpu.VMEM((tq // ATTN_BLOCK, N_KV_HEADS, 2, 2 * ATTN_BLOCK, 2 * ATTN_BLOCK), BF16),
            pltpu.VMEM((Q_DIM, tq), BF16),
            pltpu.VMEM((tq + CONV_PAD, CONV_DIM), F32),
        ],
        compiler_params=pltpu.CompilerParams(
            dimension_semantics=("arbitrary", "arbitrary"), vmem_limit_bytes=VMEM_LIMIT),
        input_output_aliases={0 if slab_in else 10: 0},
        name="mixer_prompt",
    )(x, w_in, w_ao, w_co, w_o, conv_w, bias_ab, sink_ab, ln_g, ln_b,
      jnp.zeros((1, SLAB_ROWS, LANES), F32) if slab_in else base)


def _proj_kernel(x_ref, w_ref, o_ref):
    o_ref[...] = _dot(x_ref[...].astype(BF16), w_ref[...])


def _sample_proj(x, w_in, *, tn):
    m, d = x.shape
    n = w_in.shape[1]
    return pl.pallas_call(
        _proj_kernel,
        grid=(n // tn,),
        in_specs=[pl.BlockSpec((m, d), lambda j: (0, 0)), pl.BlockSpec((d, tn), lambda j: (0, j))],
        out_specs=pl.BlockSpec((m, tn), lambda j: (0, j)),
        out_shape=jax.ShapeDtypeStruct((m, n), F32),
        compiler_params=pltpu.CompilerParams(dimension_semantics=("arbitrary",), vmem_limit_bytes=VMEM_LIMIT),
        name="sample_proj",
    )(x, w_in)


def _sample_attn_kernel(q4_ref, knew_ref, vnew_ref, ck_ref, cv_ref, bias_ref, sink_ref, hmask_ref,
                        nk_ref, nv_ref, ag_ref):
    bt = ck_ref.shape[0]
    win = ck_ref.shape[1]
    scale = HEAD_DIM ** -0.5
    row = lax.broadcasted_iota(jnp.int32, (win, KV_DIM), 0)
    last = row == win - 1
    hmask = hmask_ref[...]
    for b in range(bt):
        kb = jnp.where(last, knew_ref[b:b + 1, :], pltpu.roll(ck_ref[b], win - 1, 0))
        vb = jnp.where(last, vnew_ref[b:b + 1, :], pltpu.roll(cv_ref[b], win - 1, 0))
        nk_ref[b] = kb
        nv_ref[b] = vb
        q4 = q4_ref[b]
        qm = (jnp.concatenate([q4] * N_KV_HEADS, axis=0) * hmask).astype(BF16)
        s = _dot_nt(qm, kb.astype(BF16)) * scale + bias_ref[...]
        p = _sink_softmax(s, sink_ref[...]).astype(BF16)
        o = _dot(p, vb.astype(BF16)) * hmask
        o4 = o[0:GROUP]
        for h in range(1, N_KV_HEADS):
            o4 = o4 + o[h * GROUP:(h + 1) * GROUP]
        ag_ref[b] = o4


def _sample_attn(q4, k_new, v_new, cache_k, cache_v, bias_s, sink_col, hmask, *, layer, bt):
    _, nb, win, kvd = cache_k.shape
    return pl.pallas_call(
        _sample_attn_kernel,
        grid=(nb // bt,),
        in_specs=[
            pl.BlockSpec((bt, GROUP, kvd), lambda i: (i, 0, 0)),
            pl.BlockSpec((bt, kvd), lambda i: (i, 0)),
            pl.BlockSpec((bt, kvd), lambda i: (i, 0)),
            pl.BlockSpec((None, bt, win, kvd), lambda i: (layer, i, 0, 0)),
            pl.BlockSpec((None, bt, win, kvd), lambda i: (layer, i, 0, 0)),
            pl.BlockSpec(bias_s.shape, lambda i: (0, 0)),
            pl.BlockSpec(sink_col.shape, lambda i: (0, 0)),
            pl.BlockSpec(hmask.shape, lambda i: (0, 0)),
        ],
        out_specs=[
            pl.BlockSpec((bt, win, kvd), lambda i: (i, 0, 0)),
            pl.BlockSpec((bt, win, kvd), lambda i: (i, 0, 0)),
            pl.BlockSpec((bt, GROUP, kvd), lambda i: (i, 0, 0)),
        ],
        out_shape=[
            jax.ShapeDtypeStruct((nb, win, kvd), F32),
            jax.ShapeDtypeStruct((nb, win, kvd), F32),
            jax.ShapeDtypeStruct((nb, GROUP, kvd), F32),
        ],
        compiler_params=pltpu.CompilerParams(dimension_semantics=("arbitrary",), vmem_limit_bytes=VMEM_LIMIT),
        name="sample_attn",
    )(q4, k_new, v_new, cache_k, cache_v, bias_s, sink_col, hmask)


def _sample_post_kernel(x_ref, att_ref, proj_ref, st_ref, convw_ref, w_ao_ref, w_co_ref, w_o_ref,
                        lng_ref, lnb_ref, x1_ref, u_ref, *, alpha):
    attn_o = _dot(att_ref[...].astype(BF16), w_ao_ref[...])
    u = proj_ref[:, OFF_C:OFF_H] * proj_ref[:, OFF_H:OFF_GA]
    cw = convw_ref[...]
    y = cw[0:1] * st_ref[0] + cw[1:2] * st_ref[1] + cw[2:3] * u
    u_ref[...] = u
    y_conv = proj_ref[:, OFF_B:OFF_C] * y
    x1_ref[...] = _merge_project(x_ref[...], attn_o, y_conv, proj_ref[:, OFF_GA:OFF_GB], proj_ref[:, OFF_GB:IN_DIM],
                                 w_co_ref, w_o_ref, lng_ref, lnb_ref, alpha)


def _sample_post(x, att, proj, state, conv_w, w_ao, w_co, w_o, ln_g, ln_b, *, alpha):
    m, d = x.shape
    kernel = functools.partial(_sample_post_kernel, alpha=alpha)
    return pl.pallas_call(
        kernel,
        out_shape=[jax.ShapeDtypeStruct((m, d), F32), jax.ShapeDtypeStruct((m, CONV_DIM), F32)],
        compiler_params=pltpu.CompilerParams(vmem_limit_bytes=VMEM_LIMIT),
        name="sample_post",
    )(x, att, proj, state, conv_w, w_ao, w_co, w_o, ln_g, ln_b)


def _first_max(cur, ids, axes, big):
    m = cur
    for ax in axes:
        m = jnp.max(m, axis=ax, keepdims=True)
    idx = jnp.where(cur == m, ids, big)
    for ax in axes:
        idx = jnp.min(idx, axis=ax, keepdims=True)
    return m, idx


def _router_kernel(x_ref, rwt_ref, rb_ref, chosen_ref, gate_ref, *, tm, slab_in):
    x = _slab_load(x_ref, tm) if slab_in else x_ref[...]
    logits_t = _dot_nt(rwt_ref[...], x.astype(BF16))
    scores = jax.nn.sigmoid(logits_t)
    sel = scores + rb_ref[...]
    shape3 = (N_GROUPS, GROUP_SIZE, tm)
    sel3 = sel.reshape(shape3)
    scores3 = scores.reshape(shape3)
    member = lax.broadcasted_iota(jnp.int32, shape3, 1)
    m1, i1 = _first_max(sel3, member, (1,), GROUP_SIZE)
    m2 = jnp.max(jnp.where(member == i1, -jnp.inf, sel3), axis=1, keepdims=True)
    gscore = m1 + m2
    gid = lax.broadcasted_iota(jnp.int32, gscore.shape, 0)
    gsel = jnp.zeros(gscore.shape, jnp.bool_)
    for _ in range(TOPK_GROUPS):
        _, gi = _first_max(gscore, gid, (0,), N_GROUPS)
        hit = gid == gi
        gsel = jnp.logical_or(gsel, hit)
        gscore = jnp.where(hit, -jnp.inf, gscore)
    eid = lax.broadcasted_iota(jnp.int32, shape3, 0) * GROUP_SIZE + member
    cur = jnp.where(gsel, sel3, -jnp.inf)
    chosen = jnp.zeros(shape3, F32)
    for _ in range(TOP_K):
        _, ei = _first_max(cur, eid, (1, 0), N_EXPERTS)
        hit = eid == ei
        chosen = jnp.where(hit, 1.0, chosen)
        cur = jnp.where(hit, -jnp.inf, cur)
    w = scores3 * chosen
    tot = jnp.sum(jnp.sum(w, axis=1, keepdims=True), axis=0, keepdims=True)
    chosen_ref[...] = chosen.reshape(N_EXPERTS, tm)
    gate_ref[...] = (w / tot * ROUTED_SCALE).reshape(N_EXPERTS, tm)


def _router(x, rw_t, rb_col, *, tm, tokens=None):
    slab_in = x.ndim == 3
    if slab_in:
        t = tokens
        steps_per_chunk = (t // x.shape[0]) // tm
        assert t % (x.shape[0] * tm) == 0
        x_spec = pl.BlockSpec((None, tm * SLAB_ROWS, LANES), lambda i: (i // steps_per_chunk, i % steps_per_chunk, 0))
    else:
        t = x.shape[0]
        x_spec = pl.BlockSpec((tm, D_MODEL), lambda i: (i, 0))
    return pl.pallas_call(
        functools.partial(_router_kernel, tm=tm, slab_in=slab_in),
        grid=(t // tm,),
        in_specs=[
            x_spec,
            pl.BlockSpec(rw_t.shape, lambda i: (0, 0)),
            pl.BlockSpec(rb_col.shape, lambda i: (0, 0)),
        ],
        out_specs=[pl.BlockSpec((N_EXPERTS, tm), lambda i: (0, i)), pl.BlockSpec((N_EXPERTS, tm), lambda i: (0, i))],
        out_shape=[jax.ShapeDtypeStruct((N_EXPERTS, t), F32), jax.ShapeDtypeStruct((N_EXPERTS, t), F32)],
        compiler_params=pltpu.CompilerParams(dimension_semantics=("arbitrary",), vmem_limit_bytes=VMEM_LIMIT),
        name="router",
    )(x, rw_t, rb_col)


ROW_TILE = 288
TILE_PITCH = ROW_TILE + 1
GATE_LANES = -(-ROW_TILE // LANES) * LANES
SPARE_TOKENS = 8
FLAG_FIRST, FLAG_LAST, FLAG_NEW_EXPERT, FLAG_VALID, FLAG_HAS_NEXT = 1, 2, 4, 8, 16
SCATTER_BATCH = 8


PREV, CUR, NEXT = 0, 1, 2


def _moe_routed_kernel(ce_ref, flags_ref, used_ref, nexte_ref, idx_ref, gate_ref, x_ref, wg_ref, wu_ref,
                       wd_ref, sg_ref, su_ref, sd_ref, lng_ref, lnb_ref, o_ref, wgu_ref, wdb_ref,
                       gat0_ref, gat1_ref, res0_ref, res1_ref, wsg_ref, wsu_ref, wsd_ref, wsem_ref, *,
                       layer, alpha, chunk_tokens, sub_tokens):
    del used_ref
    step = pl.program_id(0)
    flags = flags_ref[step]
    odd = (step & 1) == 1
    nchunk = D_MODEL // LANES
    ne, ff = wg_ref.shape[1], wg_ref.shape[3]

    def slab_row(which, r):
        return pl.multiple_of(idx_ref[0, which, r], 8)

    def gather_row(which, gat_ref, r):
        gat_ref[pl.ds(r, nchunk, stride=TILE_PITCH), :] = x_ref[pl.ds(slab_row(which, r), 8), :]

    def scatter_rows(which, res_ref, rows):
        dst = [slab_row(which, r) for r in rows]
        acc = [o_ref[pl.ds(d, 8), :] + res_ref[pl.ds(r, nchunk, stride=TILE_PITCH), :] for d, r in zip(dst, rows)]
        for d, a in zip(dst, acc):
            o_ref[pl.ds(d, 8), :] = a

    def expert_mlp(gat_ref, res_ref):
        lhs = jnp.concatenate([gat_ref[j * TILE_PITCH:j * TILE_PITCH + ROW_TILE, :] for j in range(nchunk)],
                              axis=1).astype(BF16)
        h = _dot(lhs, wgu_ref[...])
        gate_col = jnp.broadcast_to(gate_ref[0], (LANES, GATE_LANES)).T[0:ROW_TILE]
        hid = jax.nn.silu(h[:, 0:ff]) * h[:, ff:2 * ff] * jnp.concatenate([gate_col] * (ff // LANES), axis=1)
        y = _dot(hid.astype(BF16), wdb_ref[...])
        for j in range(nchunk):
            res_ref[j * TILE_PITCH:j * TILE_PITCH + ROW_TILE, :] = y[:, j * LANES:(j + 1) * LANES]

    def by_parity(fn):
        @pl.when(jnp.logical_not(odd))
        def _even():
            fn(gat0_ref, gat1_ref, res0_ref, res1_ref)

        @pl.when(odd)
        def _odd():
            fn(gat1_ref, gat0_ref, res1_ref, res0_ref)

    @pl.when((flags & FLAG_FIRST) != 0)
    def _start_chunk():
        o_ref[...] = jnp.zeros(o_ref.shape, F32)

        def start(gat_cur, gat_other, res_cur, res_other):
            res_other[...] = jnp.zeros(res_other.shape, F32)

            def body(r, carry):
                gather_row(CUR, gat_cur, r)
                return carry

            lax.fori_loop(0, ROW_TILE, body, 0)

        by_parity(start)

    def weight_copies(e):
        return [pltpu.make_async_copy(src.at[layer, e], dst, wsem_ref.at[k])
                for k, (src, dst) in enumerate(((wg_ref, wsg_ref), (wu_ref, wsu_ref), (wd_ref, wsd_ref)))]

    @pl.when(step == 0)
    def _first_fetch():
        for cp in weight_copies(ce_ref[0] % ne):
            cp.start()

    @pl.when((flags & FLAG_NEW_EXPERT) != 0)
    def _next_expert():
        for cp in weight_copies(0):
            cp.wait()
        wgu_ref[:, 0:ff] = wsg_ref[...].astype(BF16)
        wgu_ref[:, ff:2 * ff] = wsu_ref[...].astype(BF16)
        wdb_ref[...] = wsd_ref[...].astype(BF16)

        @pl.when((flags & FLAG_HAS_NEXT) != 0)
        def _prefetch():
            for cp in weight_copies(nexte_ref[step]):
                cp.start()

    @pl.when((flags & FLAG_VALID) != 0)
    def _tile():
        def main(gat_cur, gat_other, res_cur, res_other):
            for r0 in range(0, ROW_TILE, SCATTER_BATCH):
                scatter_rows(PREV, res_other, range(r0, r0 + SCATTER_BATCH))
            for r in range(ROW_TILE):
                gather_row(NEXT, gat_other, r)
            expert_mlp(gat_cur, res_cur)

        by_parity(main)

    @pl.when((flags & FLAG_LAST) != 0)
    def _finish():
        def flush(gat_cur, gat_other, res_cur, res_other):
            def body(r, carry):
                scatter_rows(CUR, res_cur, [r])
                return carry

            lax.fori_loop(0, ROW_TILE, body, 0)

        by_parity(flush)

        def body(s, carry):
            base = pl.multiple_of(s * (sub_tokens * 8), 8)

            def rows_2d(ref):
                return jnp.concatenate([ref[pl.ds(base + j, sub_tokens, stride=8), :] for j in range(nchunk)],
                                       axis=1)

            x2 = rows_2d(x_ref)
            xb = x2.astype(BF16)
            hs = jax.nn.silu(_dot(xb, sg_ref[...])) * _dot(xb, su_ref[...])
            ffn = rows_2d(o_ref) + _dot(hs.astype(BF16), sd_ref[...])
            res = _layer_norm(alpha * x2 + ffn, lng_ref[...], lnb_ref[...])
            for j in range(nchunk):
                o_ref[pl.ds(base + j, sub_tokens, stride=8), :] = res[:, j * LANES:(j + 1) * LANES]
            return carry

        lax.fori_loop(0, chunk_tokens // sub_tokens, body, 0)


def _moe_routed(tile_ce, tile_flags, n_used, next_expert, row_idx, gates, x_tm, wg, wu, wd, sg, su, sd, ln_g,
                ln_b, *, layer, alpha, chunk_tokens, sub_tokens):
    n_chunks, chunk_rows, _ = x_tm.shape
    n_tiles = row_idx.shape[0]
    _, ne, d, ff = wg.shape
    kernel = functools.partial(_moe_routed_kernel, layer=layer, alpha=alpha, chunk_tokens=chunk_tokens,
                               sub_tokens=sub_tokens)

    def tile_map(i, ce, fl, used, nxt):
        return (jnp.minimum(i, used[0] - 1), 0, 0)

    def chunk_map(i, ce, fl, used, nxt):
        return (ce[i] // ne, 0, 0)

    def const2(i, ce, fl, used, nxt):
        return (0, 0)

    grid_spec = pltpu.PrefetchScalarGridSpec(
        num_scalar_prefetch=4,
        grid=(n_tiles,),
        in_specs=[
            pl.BlockSpec((1, 3, ROW_TILE), tile_map, memory_space=pltpu.SMEM),
            pl.BlockSpec((1, 1, GATE_LANES), tile_map),
            pl.BlockSpec((None, chunk_rows, LANES), chunk_map, pipeline_mode=pl.Buffered(1)),
            pl.BlockSpec(memory_space=pl.ANY),
            pl.BlockSpec(memory_space=pl.ANY),
            pl.BlockSpec(memory_space=pl.ANY),
            pl.BlockSpec(sg.shape, const2, pipeline_mode=pl.Buffered(1)),
            pl.BlockSpec(su.shape, const2, pipeline_mode=pl.Buffered(1)),
            pl.BlockSpec(sd.shape, const2, pipeline_mode=pl.Buffered(1)),
            pl.BlockSpec(ln_g.shape, const2),
            pl.BlockSpec(ln_b.shape, const2),
        ],
        out_specs=pl.BlockSpec((None, chunk_rows, LANES), chunk_map, pipeline_mode=pl.Buffered(1)),
        scratch_shapes=[
            pltpu.VMEM((d, 2 * ff), BF16),
            pltpu.VMEM((ff, d), BF16),
            pltpu.VMEM((8 * TILE_PITCH, LANES), F32),
            pltpu.VMEM((8 * TILE_PITCH, LANES), F32),
            pltpu.VMEM((8 * TILE_PITCH, LANES), F32),
            pltpu.VMEM((8 * TILE_PITCH, LANES), F32),
            pltpu.VMEM((d, ff), F32),
            pltpu.VMEM((d, ff), F32),
            pltpu.VMEM((ff, d), F32),
            pltpu.SemaphoreType.DMA((3,)),
        ],
    )
    return pl.pallas_call(
        kernel,
        grid_spec=grid_spec,
        out_shape=jax.ShapeDtypeStruct(x_tm.shape, F32),
        compiler_params=pltpu.CompilerParams(dimension_semantics=("arbitrary",), vmem_limit_bytes=VMEM_LIMIT),
        name="moe_routed",
    )(tile_ce, tile_flags, n_used, next_expert, row_idx, gates, x_tm, wg, wu, wd, sg, su, sd, ln_g, ln_b)


MOE_CHUNKS = 4
MOE_SUB_MAX = 512


def _moe_tiling(n_prompt, n_sample):
    assert n_prompt % (MOE_CHUNKS * 8) == 0 and n_sample % (MOE_CHUNKS * 8) == 0
    chunk_tokens = (n_prompt + n_sample) // MOE_CHUNKS
    sub_tokens = max(s for s in range(8, MOE_SUB_MAX + 1, 8) if chunk_tokens % s == 0)
    return MOE_CHUNKS, chunk_tokens, sub_tokens


PLAN_GROUP = 32


def _chunked_t(prompt, sample, n_chunks):
    ne = prompt.shape[0]
    return jnp.concatenate([prompt.reshape(ne, n_chunks, -1).transpose(1, 0, 2),
                            sample.reshape(ne, n_chunks, -1).transpose(1, 0, 2)], axis=2)


def _nth_set_bit(word, n):
    pos = jnp.zeros(word.shape, jnp.int32)
    for width in (16, 8, 4, 2, 1):
        low = word & jnp.uint32((1 << width) - 1)
        below = lax.population_count(low).astype(jnp.int32)
        upper = n >= below
        n = jnp.where(upper, n - below, n)
        word = jnp.where(upper, word >> jnp.uint32(width), low)
        pos = pos + jnp.where(upper, width, 0)
    return pos


def _route_plan(chosen_p, chosen_s, gate_p, gate_s, *, chunk_tokens, n_chunks):
    n_seg = n_chunks * N_EXPERTS
    n_group = chunk_tokens // PLAN_GROUP
    assert chunk_tokens % PLAN_GROUP == 0
    chosen = _chunked_t(chosen_p, chosen_s, n_chunks) > 0
    gate = _chunked_t(gate_p, gate_s, n_chunks)
    shifts = jnp.arange(PLAN_GROUP, dtype=jnp.uint32)
    words = jnp.sum(chosen.reshape(n_seg, n_group, PLAN_GROUP).astype(jnp.uint32) << shifts, axis=-1,
                    dtype=jnp.uint32)
    per_group = lax.population_count(words).astype(jnp.int32)
    group_start = jnp.cumsum(per_group, axis=-1) - per_group
    counts = jnp.sum(per_group, axis=-1)
    padded = -(-counts // ROW_TILE) * ROW_TILE
    seg_end = jnp.cumsum(padded)
    seg_start = seg_end - padded
    n_tiles = -(-(n_chunks * chunk_tokens * TOP_K + n_seg * (ROW_TILE - 1)) // ROW_TILE)
    row0 = jnp.arange(n_tiles, dtype=jnp.int32) * ROW_TILE
    seg = jnp.sum(seg_end[None, :] <= row0[:, None], axis=1, dtype=jnp.int32)
    valid = seg < n_seg
    tile_ce = jnp.minimum(seg, n_seg - 1)
    j = (row0 - seg_start[tile_ce])[:, None] + jnp.arange(ROW_TILE, dtype=jnp.int32)[None, :]
    real = valid[:, None] & (j < counts[tile_ce][:, None])
    starts = group_start[tile_ce][:, None, :]
    group = jnp.sum(starts <= j[:, :, None], axis=-1, dtype=jnp.int32) - 1
    in_group = jnp.arange(n_group, dtype=jnp.int32)[None, None, :] == group[:, :, None]
    start = jnp.sum(jnp.where(in_group, starts, 0), axis=-1, dtype=jnp.int32)
    word = jnp.sum(jnp.where(in_group, words[tile_ce][:, None, :], jnp.uint32(0)), axis=-1, dtype=jnp.uint32)
    token = group * PLAN_GROUP + _nth_set_bit(word, j - start)
    token = jnp.where(real, token, chunk_tokens)
    vals = jnp.where(real, gate.reshape(-1)[tile_ce[:, None] * chunk_tokens + jnp.minimum(token, chunk_tokens - 1)],
                     0.0)
    tile_c = tile_ce // N_EXPERTS
    prev_ce = jnp.concatenate([jnp.full((1,), -N_EXPERTS, jnp.int32), tile_ce[:-1]])
    next_c = jnp.concatenate([tile_c[1:], jnp.full((1,), -1, jnp.int32)])
    next_valid = jnp.concatenate([valid[1:], jnp.zeros((1,), jnp.bool_)])
    first = valid & (tile_c != prev_ce // N_EXPERTS)
    last = valid & (~next_valid | (next_c != tile_c))
    new_expert = valid & (tile_ce != prev_ce)
    tile_id = jnp.arange(n_tiles, dtype=jnp.int32)
    change_at = jnp.where(new_expert, tile_id, n_tiles)
    next_change = jnp.concatenate([lax.cummin(change_at, reverse=True)[1:], jnp.full((1,), n_tiles, jnp.int32)])
    has_next = new_expert & (next_change < n_tiles)
    next_expert = tile_ce[jnp.minimum(next_change, n_tiles - 1)] % N_EXPERTS
    flags = (first * FLAG_FIRST + last * FLAG_LAST + new_expert * FLAG_NEW_EXPERT + valid * FLAG_VALID
             + has_next * FLAG_HAS_NEXT)
    row_idx = token * SLAB_ROWS
    row_idx3 = jnp.stack([jnp.concatenate([row_idx[:1], row_idx[:-1]]), row_idx,
                          jnp.concatenate([row_idx[1:], row_idx[-1:]])], axis=1)
    n_used = jnp.sum(valid, dtype=jnp.int32).reshape(1)
    gates = jnp.pad(vals.reshape(n_tiles, 1, ROW_TILE), ((0, 0), (0, 0), (0, GATE_LANES - ROW_TILE)))
    return tile_ce, flags.astype(jnp.int32), n_used, next_expert.astype(jnp.int32), row_idx3, gates


def _t5_bucket(dist):
    n = jnp.maximum(dist, 0)
    max_exact = N_BUCKETS // 2
    large = max_exact + (jnp.log(jnp.maximum(n, 1).astype(F32) / max_exact)
                         / math.log(MAX_DISTANCE / max_exact) * (N_BUCKETS - max_exact)).astype(jnp.int32)
    large = jnp.minimum(large, N_BUCKETS - 1)
    return jnp.where(n < max_exact, n, large)


def _bias_lookup(rel_bias, bucket):
    onehot = (bucket[..., None] == jnp.arange(N_BUCKETS, dtype=bucket.dtype)).astype(F32)
    return jnp.einsum("...b,bh->h...", onehot, rel_bias.astype(F32), precision=lax.Precision.HIGHEST)


def _bias_tables(rel_bias, win):
    qi = jnp.arange(ATTN_BLOCK)[:, None]
    ki = jnp.arange(2 * ATTN_BLOCK)[None, :]
    dist = qi + ATTN_BLOCK - ki
    valid = (dist >= 0) & (dist < WINDOW)
    bias = _bias_lookup(rel_bias, _t5_bucket(dist))
    bias = jnp.where(valid[None], bias, NEG).reshape(N_KV_HEADS, GROUP, ATTN_BLOCK, 2 * ATTN_BLOCK)
    bias_ab = jnp.stack([jnp.concatenate([bias[:, t], bias[:, t + 2]], axis=1) for t in range(2)], axis=1)
    dist_s = (win - 1) - jnp.arange(win)
    bias_s = _bias_lookup(rel_bias, _t5_bucket(dist_s))
    return bias_ab, bias_s


def _sink_tables(sink):
    s = sink.astype(F32).reshape(N_KV_HEADS, GROUP)
    rows = [jnp.concatenate([jnp.broadcast_to(s[:, t, None], (N_KV_HEADS, ATTN_BLOCK)),
                             jnp.broadcast_to(s[:, t + 2, None], (N_KV_HEADS, ATTN_BLOCK))], axis=1)
            for t in range(2)]
    return jnp.stack(rows, axis=1)[..., None], sink.astype(F32)[:, None]


def kernel(x_prompt, x_sample, cache_k_win, cache_v_win, state_conv, rel_bias, w_in, attn_sink, conv_w,
           w_attn_out, w_conv_out, w_out, ln1_g, ln1_b, router_w, router_bias, exp_w_gate, exp_w_up,
           exp_w_down, shared_w_gate, shared_w_up, shared_w_down, ln2_g, ln2_b):
    depth = w_in.shape[0]
    alpha = (2 * depth) ** 0.25
    nb, seq, d = x_prompt.shape
    nd = x_sample.shape[0]
    win = cache_k_win.shape[2]
    assert x_sample.shape[1] == 1 and win == WINDOW and seq % 512 == 0

    n_prompt = nb * seq
    n_chunks, chunk_tokens, sub_tokens = _moe_tiling(n_prompt, nd)
    prompt_rows = n_prompt // n_chunks * SLAB_ROWS
    slab_shape = (n_chunks, (chunk_tokens + SPARE_TOKENS) * SLAB_ROWS, LANES)

    bias_ab, bias_s = _bias_tables(rel_bias, win)
    hmask = (jnp.arange(KV_DIM)[None, :] // HEAD_DIM == jnp.arange(N_HEADS)[:, None] // GROUP).astype(F32)

    yp = x_prompt
    ys = x_sample.reshape(nd, d)
    outs = [[] for _ in range(6)]
    for l in range(depth):
        w_in_b = w_in[l].astype(BF16)
        w_ao_b = w_attn_out[l].astype(BF16)
        w_co_b = w_conv_out[l].astype(BF16)
        w_o_b = w_out[l].astype(BF16)
        g1, b1 = ln1_g[l][None, :], ln1_b[l][None, :]
        g2, b2 = ln2_g[l][None, :], ln2_b[l][None, :]
        sink_ab, sink_col = _sink_tables(attn_sink[l])

        slab, kp, vp, cp = _mixer_prompt(yp, w_in_b, w_attn_out[l].T.astype(BF16), w_co_b, w_o_b, conv_w[l],
                                         jnp.swapaxes(bias_ab, -1, -2), jnp.swapaxes(sink_ab, -1, -2), g1, b1,
                                         alpha=alpha, tq=512, batch=nb, seq=seq, slab_shape=slab_shape,
                                         base=jnp.zeros(slab_shape, F32) if l == 0 else None)

        proj = _sample_proj(ys, w_in_b, tn=IN_DIM // 4)
        q4 = proj[:, :Q_DIM].reshape(nd, N_KV_HEADS, GROUP, HEAD_DIM).transpose(0, 2, 1, 3).reshape(nd, GROUP, KV_DIM)
        ksn, vsn, ag = _sample_attn(q4, proj[:, OFF_K:OFF_V], proj[:, OFF_V:OFF_B],
                                    cache_k_win.reshape(depth, nd, win, KV_DIM),
                                    cache_v_win.reshape(depth, nd, win, KV_DIM),
                                    bias_s, sink_col, hmask, layer=l, bt=16)
        att = ag.reshape(nd, GROUP, N_KV_HEADS, HEAD_DIM).transpose(0, 2, 1, 3).reshape(nd, Q_DIM)
        state_t = jnp.swapaxes(state_conv[l], 0, 1)
        ys, us = _sample_post(ys, att, proj, state_t, conv_w[l], w_ao_b, w_co_b, w_o_b, g1, b1, alpha=alpha)

        outs[0].append(kp.reshape(nb, WINDOW, N_KV_HEADS, HEAD_DIM))
        outs[1].append(vp.reshape(nb, WINDOW, N_KV_HEADS, HEAD_DIM))
        outs[2].append(cp)
        outs[3].append(ksn.reshape(nd, win, N_KV_HEADS, HEAD_DIM))
        outs[4].append(vsn.reshape(nd, win, N_KV_HEADS, HEAD_DIM))
        outs[5].append(jnp.concatenate([state_conv[l][:, 1:], us[:, None, :]], axis=1))

        rw_t = router_w[l].T.astype(BF16)
        rb_col = router_bias[l].astype(F32)[:, None]
        sg, su, sd = (shared_w_gate[l].astype(BF16), shared_w_up[l].astype(BF16), shared_w_down[l].astype(BF16))
        tail = jnp.concatenate([ys.reshape(n_chunks, -1, LANES),
                                jnp.zeros((n_chunks, SPARE_TOKENS * SLAB_ROWS, LANES), F32)], axis=1)
        slab = lax.dynamic_update_slice(slab, tail, (0, prompt_rows, 0))
        ep, wp = _router(slab, rw_t, rb_col, tm=512, tokens=n_prompt)
        es, ws = _router(ys, rw_t, rb_col, tm=nd)
        plan = _route_plan(ep, es, wp, ws, chunk_tokens=chunk_tokens, n_chunks=n_chunks)
        yp = _moe_routed(*plan, slab, exp_w_gate, exp_w_up, exp_w_down, sg, su, sd, g2, b2,
                         layer=l, alpha=alpha, chunk_tokens=chunk_tokens, sub_tokens=sub_tokens)
        ys = yp[:, prompt_rows:chunk_tokens * SLAB_ROWS].reshape(nd, d)

    y_prompt = yp[:, :prompt_rows].reshape(nb, seq, d)
    return (y_prompt, ys.reshape(nd, 1, d)) + tuple(jnp.stack(o) for o in outs)
```

```python
import functools
import math

import jax
import jax.numpy as jnp
from jax import lax
from jax.experimental import pallas as pl
from jax.experimental.pallas import tpu as pltpu

D_MODEL = 1024
N_HEADS = 16
N_KV_HEADS = 4
HEAD_DIM = 64
GROUP = N_HEADS // N_KV_HEADS
WINDOW = 128
ATTN_BLOCK = 128
N_BUCKETS = 32
MAX_DISTANCE = 128
CONV_DIM = 1024
CONV_K = 3
N_EXPERTS = 64
TOP_K = 8
N_GROUPS = 8
TOPK_GROUPS = 4
GROUP_SIZE = N_EXPERTS // N_GROUPS
EXPERT_FF = 256
ROUTED_SCALE = 2.5
LN_EPS = 1e-5
NEG = -1e30

Q_DIM = N_HEADS * HEAD_DIM
KV_DIM = N_KV_HEADS * HEAD_DIM
OFF_K = Q_DIM
OFF_V = OFF_K + KV_DIM
OFF_B = OFF_V + KV_DIM
OFF_C = OFF_B + CONV_DIM
OFF_H = OFF_C + CONV_DIM
OFF_GA = OFF_H + CONV_DIM
OFF_GB = OFF_GA + D_MODEL
IN_DIM = OFF_GB + D_MODEL

LANES = 128
CONV_PAD = 8
VMEM_LIMIT = 60 * 1024 * 1024

BF16 = jnp.bfloat16
F32 = jnp.float32


def _dot(a, b):
    return jnp.dot(a, b, preferred_element_type=F32)


def _dot_nt(a, b):
    return lax.dot_general(a, b, (((1,), (1,)), ((), ())), preferred_element_type=F32)


def _layer_norm(z, g, b):
    mu = jnp.mean(z, axis=-1, keepdims=True)
    d = z - mu
    var = jnp.mean(d * d, axis=-1, keepdims=True)
    return d * lax.rsqrt(var + LN_EPS) * g + b


def _sink_softmax(s, sink, axis=-1):
    m = jnp.maximum(jnp.max(s, axis=axis, keepdims=True), sink)
    e = jnp.exp(s - m)
    den = jnp.sum(e, axis=axis, keepdims=True) + jnp.exp(sink - m)
    return e * (1.0 / den)


SLAB_ROWS = D_MODEL // LANES


def _slab_load(ref, tokens, base=0):
    return jnp.concatenate([ref[pl.ds(base + j, tokens, stride=SLAB_ROWS), :] for j in range(SLAB_ROWS)], axis=1)


def _slab_store(ref, val, base=0):
    tokens = val.shape[0]
    for j in range(SLAB_ROWS):
        ref[pl.ds(base + j, tokens, stride=SLAB_ROWS), :] = val[:, j * LANES:(j + 1) * LANES]


def _merge_project(x, attn_o, y_conv, g_a, g_b, w_co_ref, w_o_ref, lng_ref, lnb_ref, alpha):
    merged = jax.nn.sigmoid(g_a) * attn_o + jax.nn.sigmoid(g_b) * _dot(y_conv.astype(BF16), w_co_ref[...])
    out = _dot(merged.astype(BF16), w_o_ref[...])
    return _layer_norm(alpha * x + out, lng_ref[...], lnb_ref[...])


def _mixer_prompt_kernel(x_ref, w_in_ref, w_ao_ref, w_co_ref, w_o_ref, convw_ref, bias_ref, sink_ref,
                         lng_ref, lnb_ref, base_ref,
                         x1_ref, kwin_ref, vwin_ref, conv_ref,
                         ka_ref, kb_ref, vat_ref, vbt_ref, s_ref, p_ref, att_ref, ubuf_ref, *, alpha, tq, slab_in):
    del base_ref
    i = pl.program_id(1)
    nblk = tq // ATTN_BLOCK
    half = LANES // 2
    scale = HEAD_DIM ** -0.5
    assert math.frexp(scale)[0] == 0.5

    @pl.when(i == 0)
    def _init():
        for ref in (ka_ref, kb_ref):
            ref[:, 0:ATTN_BLOCK, :] = jnp.zeros((N_KV_HEADS, ATTN_BLOCK, LANES), BF16)
        for ref in (vat_ref, vbt_ref):
            ref[:, :, 0:ATTN_BLOCK] = jnp.zeros((N_KV_HEADS, LANES, ATTN_BLOCK), BF16)
        ubuf_ref[0:CONV_PAD, :] = jnp.zeros((CONV_PAD, CONV_DIM), F32)

    x = _slab_load(x_ref, tq) if slab_in else x_ref[0]
    xb = x.astype(BF16)
    qkv = _dot(xb, w_in_ref[:, 0:OFF_B])

    lo = lax.broadcasted_iota(jnp.int32, (tq, LANES), 1) < half
    zeros_t = jnp.zeros((half, tq), BF16)
    for c in range(N_KV_HEADS // 2):
        chunk = qkv[:, OFF_K + c * LANES: OFF_K + (c + 1) * LANES]
        c_lo = jnp.where(lo, chunk, 0.0)
        c_hi = jnp.where(lo, 0.0, chunk)
        ka_ref[2 * c, ATTN_BLOCK:, :] = c_lo.astype(BF16)
        kb_ref[2 * c, ATTN_BLOCK:, :] = pltpu.roll(c_lo, half, 1).astype(BF16)
        kb_ref[2 * c + 1, ATTN_BLOCK:, :] = c_hi.astype(BF16)
        ka_ref[2 * c + 1, ATTN_BLOCK:, :] = pltpu.roll(c_hi, half, 1).astype(BF16)
        vt = qkv[:, OFF_V + c * LANES: OFF_V + (c + 1) * LANES].T.astype(BF16)
        vat_ref[2 * c, :, ATTN_BLOCK:] = jnp.concatenate([vt[0:half], zeros_t], axis=0)
        vbt_ref[2 * c, :, ATTN_BLOCK:] = jnp.concatenate([zeros_t, vt[0:half]], axis=0)
        vbt_ref[2 * c + 1, :, ATTN_BLOCK:] = jnp.concatenate([zeros_t, vt[half:]], axis=0)
        vat_ref[2 * c + 1, :, ATTN_BLOCK:] = jnp.concatenate([vt[half:], zeros_t], axis=0)

    key_row = lax.broadcasted_iota(jnp.int32, (2 * ATTN_BLOCK, 2 * ATTN_BLOCK), 0)
    for j in range(nblk):
        rows = slice(j * ATTN_BLOCK, (j + 1) * ATTN_BLOCK)
        keys = slice(j * ATTN_BLOCK, (j + 2) * ATTN_BLOCK)
        for h in range(N_KV_HEADS):
            q0 = h * GROUP * HEAD_DIM
            q2 = (jnp.concatenate([qkv[rows, q0:q0 + LANES], qkv[rows, q0 + LANES:q0 + 2 * LANES]], axis=0)
                  * scale).astype(BF16)
            for t, k_ref in enumerate((ka_ref, kb_ref)):
                s = _dot_nt(k_ref[h, keys, :], q2)
                if j == 0:
                    s = jnp.where(jnp.logical_and(i == 0, key_row < ATTN_BLOCK), NEG, s)
                s_ref[j, h, t] = s
    for h in range(N_KV_HEADS):
        for t in range(2):
            s = s_ref[:, h, t] + bias_ref[h, t][None]
            p_ref[:, h, t] = _sink_softmax(s, sink_ref[h, t][None], axis=1).astype(BF16)
    for j in range(nblk):
        rows = slice(j * ATTN_BLOCK, (j + 1) * ATTN_BLOCK)
        keys = slice(j * ATTN_BLOCK, (j + 2) * ATTN_BLOCK)
        for h in range(N_KV_HEADS):
            q0 = h * GROUP * HEAD_DIM
            o_t = _dot(vat_ref[h, :, keys], p_ref[j, h, 0]) + _dot(vbt_ref[h, :, keys], p_ref[j, h, 1])
            att_ref[q0:q0 + LANES, rows] = o_t[:, 0:ATTN_BLOCK].astype(BF16)
            att_ref[q0 + LANES:q0 + 2 * LANES, rows] = o_t[:, ATTN_BLOCK:].astype(BF16)
    for ref in (ka_ref, kb_ref):
        ref[:, 0:ATTN_BLOCK, :] = ref[:, tq:tq + ATTN_BLOCK, :]
    for ref in (vat_ref, vbt_ref):
        ref[:, :, 0:ATTN_BLOCK] = ref[:, :, tq:tq + ATTN_BLOCK]

    kwin_ref[0] = qkv[tq - WINDOW:tq, OFF_K:OFF_V]
    vwin_ref[0] = qkv[tq - WINDOW:tq, OFF_V:OFF_B]

    attn_o = _dot(w_ao_ref[...], att_ref[...]).T

    bch = _dot(xb, w_in_ref[:, OFF_B:OFF_GA])
    u = bch[:, CONV_DIM:2 * CONV_DIM] * bch[:, 2 * CONV_DIM:3 * CONV_DIM]
    ubuf_ref[CONV_PAD:CONV_PAD + tq, :] = u
    cw = convw_ref[...]
    y = (cw[0:1] * ubuf_ref[CONV_PAD - 2:CONV_PAD - 2 + tq, :]
         + cw[1:2] * ubuf_ref[CONV_PAD - 1:CONV_PAD - 1 + tq, :]
         + cw[2:3] * u)
    conv_ref[0] = ubuf_ref[CONV_PAD + tq - (CONV_K - 1):CONV_PAD + tq, :]
    ubuf_ref[0:CONV_PAD, :] = ubuf_ref[tq:tq + CONV_PAD, :]
    y_conv = bch[:, 0:CONV_DIM] * y

    gab = _dot(xb, w_in_ref[:, OFF_GA:IN_DIM])
    _slab_store(x1_ref, _merge_project(x, attn_o, y_conv, gab[:, 0:D_MODEL], gab[:, D_MODEL:], w_co_ref, w_o_ref,
                                       lng_ref, lnb_ref, alpha))


def _const_spec(shape):
    nd = len(shape)
    return pl.BlockSpec(shape, lambda *_: (0,) * nd, pipeline_mode=pl.Buffered(1))


def _mixer_prompt(x, w_in, w_ao, w_co, w_o, conv_w, bias_ab, sink_ab, ln_g, ln_b, *, alpha, tq, batch, seq,
                  slab_shape, base=None):
    b, s = batch, seq
    n_chunks, chunk_rows, _ = slab_shape
    slab_in = x.ndim == 3 and x.shape == slab_shape
    steps_per_chunk = (b * s // n_chunks) // tq
    assert (b * s) % (n_chunks * tq) == 0 and s % tq == 0

    def slab_map(bi, i):
        g = bi * (s // tq) + i
        return (g // steps_per_chunk, g % steps_per_chunk, 0)

    slab_spec = pl.BlockSpec((None, tq * SLAB_ROWS, LANES), slab_map)
    kernel = functools.partial(_mixer_prompt_kernel, alpha=alpha, tq=tq, slab_in=slab_in)
    return pl.pallas_call(
        kernel,
        grid=(b, s // tq),
        in_specs=[
            slab_spec if slab_in else pl.BlockSpec((1, tq, D_MODEL), lambda bi, i: (bi, i, 0)),
            _const_spec(w_in.shape), _const_spec(w_ao.shape), _const_spec(w_co.shape), _const_spec(w_o.shape),
            _const_spec(conv_w.shape), _const_spec(bias_ab.shape), _const_spec(sink_ab.shape),
            _const_spec(ln_g.shape), _const_spec(ln_b.shape),
            pl.BlockSpec(memory_space=pl.ANY),
        ],
        out_specs=[
            slab_spec,
            pl.BlockSpec((1, WINDOW, KV_DIM), lambda bi, i: (bi, 0, 0)),
            pl.BlockSpec((1, WINDOW, KV_DIM), lambda bi, i: (bi, 0, 0)),
            pl.BlockSpec((1, CONV_K - 1, CONV_DIM), lambda bi, i: (bi, 0, 0)),
        ],
        out_shape=[
            jax.ShapeDtypeStruct(slab_shape, F32),
            jax.ShapeDtypeStruct((b, WINDOW, KV_DIM), F32),
            jax.ShapeDtypeStruct((b, WINDOW, KV_DIM), F32),
            jax.ShapeDtypeStruct((b, CONV_K - 1, CONV_DIM), F32),
        ],
        scratch_shapes=[
            pltpu.VMEM((N_KV_HEADS, ATTN_BLOCK + tq, LANES), BF16),
            pltpu.VMEM((N_KV_HEADS, ATTN_BLOCK + tq, LANES), BF16),
            pltpu.VMEM((N_KV_HEADS, LANES, ATTN_BLOCK + tq), BF16),
            pltpu.VMEM((N_KV_HEADS, LANES, ATTN_BLOCK + tq), BF16),
            pltpu.VMEM((tq // ATTN_BLOCK, N_KV_HEADS, 2, 2 * ATTN_BLOCK, 2 * ATTN_BLOCK), F32),
            pltpu.VMEM((tq // ATTN_BLOCK, N_KV_HEADS, 2, 2 * ATTN_BLOCK, 2 * ATTN_BLOCK), BF16),
            pltpu.VMEM((Q_DIM, tq), BF16),
            pltpu.VMEM((tq + CONV_PAD, CONV_DIM), F32),
        ],
        compiler_params=pltpu.CompilerParams(
            dimension_semantics=("arbitrary", "arbitrary"), vmem_limit_bytes=VMEM_LIMIT),
        input_output_aliases={0 if slab_in else 10: 0},
        name="mixer_prompt",
    )(x, w_in, w_ao, w_co, w_o, conv_w, bias_ab, sink_ab, ln_g, ln_b,
      jnp.zeros((1, SLAB_ROWS, LANES), F32) if slab_in else base)


def _proj_kernel(x_ref, w_ref, o_ref):
    o_ref[...] = _dot(x_ref[...].astype(BF16), w_ref[...])


def _sample_proj(x, w_in, *, tn):
    m, d = x.shape
    n = w_in.shape[1]
    return pl.pallas_call(
        _proj_kernel,
        grid=(n // tn,),
        in_specs=[pl.BlockSpec((m, d), lambda j: (0, 0)), pl.BlockSpec((d, tn), lambda j: (0, j))],
        out_specs=pl.BlockSpec((m, tn), lambda j: (0, j)),
        out_shape=jax.ShapeDtypeStruct((m, n), F32),
        compiler_params=pltpu.CompilerParams(dimension_semantics=("arbitrary",), vmem_limit_bytes=VMEM_LIMIT),
        name="sample_proj",
    )(x, w_in)


def _sample_attn_kernel(q4_ref, knew_ref, vnew_ref, ck_ref, cv_ref, bias_ref, sink_ref, hmask_ref,
                        nk_ref, nv_ref, ag_ref):
    bt = ck_ref.shape[0]
    win = ck_ref.shape[1]
    scale = HEAD_DIM ** -0.5
    row = lax.broadcasted_iota(jnp.int32, (win, KV_DIM), 0)
    last = row == win - 1
    hmask = hmask_ref[...]
    for b in range(bt):
        kb = jnp.where(last, knew_ref[b:b + 1, :], pltpu.roll(ck_ref[b], win - 1, 0))
        vb = jnp.where(last, vnew_ref[b:b + 1, :], pltpu.roll(cv_ref[b], win - 1, 0))
        nk_ref[b] = kb
        nv_ref[b] = vb
        q4 = q4_ref[b]
        qm = (jnp.concatenate([q4] * N_KV_HEADS, axis=0) * hmask).astype(BF16)
        s = _dot_nt(qm, kb.astype(BF16)) * scale + bias_ref[...]
        p = _sink_softmax(s, sink_ref[...]).astype(BF16)
        o = _dot(p, vb.astype(BF16)) * hmask
        o4 = o[0:GROUP]
        for h in range(1, N_KV_HEADS):
            o4 = o4 + o[h * GROUP:(h + 1) * GROUP]
        ag_ref[b] = o4


def _sample_attn(q4, k_new, v_new, cache_k, cache_v, bias_s, sink_col, hmask, *, layer, bt):
    _, nb, win, kvd = cache_k.shape
    return pl.pallas_call(
        _sample_attn_kernel,
        grid=(nb // bt,),
        in_specs=[
            pl.BlockSpec((bt, GROUP, kvd), lambda i: (i, 0, 0)),
            pl.BlockSpec((bt, kvd), lambda i: (i, 0)),
            pl.BlockSpec((bt, kvd), lambda i: (i, 0)),
            pl.BlockSpec((None, bt, win, kvd), lambda i: (layer, i, 0, 0)),
            pl.BlockSpec((None, bt, win, kvd), lambda i: (layer, i, 0, 0)),
            pl.BlockSpec(bias_s.shape, lambda i: (0, 0)),
            pl.BlockSpec(sink_col.shape, lambda i: (0, 0)),
            pl.BlockSpec(hmask.shape, lambda i: (0, 0)),
        ],
        out_specs=[
            pl.BlockSpec((bt, win, kvd), lambda i: (i, 0, 0)),
            pl.BlockSpec((bt, win, kvd), lambda i: (i, 0, 0)),
            pl.BlockSpec((bt, GROUP, kvd), lambda i: (i, 0, 0)),
        ],
        out_shape=[
            jax.ShapeDtypeStruct((nb, win, kvd), F32),
            jax.ShapeDtypeStruct((nb, win, kvd), F32),
            jax.ShapeDtypeStruct((nb, GROUP, kvd), F32),
        ],
        compiler_params=pltpu.CompilerParams(dimension_semantics=("arbitrary",), vmem_limit_bytes=VMEM_LIMIT),
        name="sample_attn",
    )(q4, k_new, v_new, cache_k, cache_v, bias_s, sink_col, hmask)


def _sample_post_kernel(x_ref, att_ref, proj_ref, st_ref, convw_ref, w_ao_ref, w_co_ref, w_o_ref,
                        lng_ref, lnb_ref, x1_ref, u_ref, *, alpha):
    attn_o = _dot(att_ref[...].astype(BF16), w_ao_ref[...])
    u = proj_ref[:, OFF_C:OFF_H] * proj_ref[:, OFF_H:OFF_GA]
    cw = convw_ref[...]
    y = cw[0:1] * st_ref[0] + cw[1:2] * st_ref[1] + cw[2:3] * u
    u_ref[...] = u
    y_conv = proj_ref[:, OFF_B:OFF_C] * y
    x1_ref[...] = _merge_project(x_ref[...], attn_o, y_conv, proj_ref[:, OFF_GA:OFF_GB], proj_ref[:, OFF_GB:IN_DIM],
                                 w_co_ref, w_o_ref, lng_ref, lnb_ref, alpha)


def _sample_post(x, att, proj, state, conv_w, w_ao, w_co, w_o, ln_g, ln_b, *, alpha):
    m, d = x.shape
    kernel = functools.partial(_sample_post_kernel, alpha=alpha)
    return pl.pallas_call(
        kernel,
        out_shape=[jax.ShapeDtypeStruct((m, d), F32), jax.ShapeDtypeStruct((m, CONV_DIM), F32)],
        compiler_params=pltpu.CompilerParams(vmem_limit_bytes=VMEM_LIMIT),
        name="sample_post",
    )(x, att, proj, state, conv_w, w_ao, w_co, w_o, ln_g, ln_b)


def _first_max(cur, ids, axes, big):
    m = cur
    for ax in axes:
        m = jnp.max(m, axis=ax, keepdims=True)
    idx = jnp.where(cur == m, ids, big)
    for ax in axes:
        idx = jnp.min(idx, axis=ax, keepdims=True)
    return m, idx


def _router_kernel(x_ref, rwt_ref, rb_ref, chosen_ref, gate_ref, *, tm, slab_in):
    x = _slab_load(x_ref, tm) if slab_in else x_ref[...]
    logits_t = _dot_nt(rwt_ref[...], x.astype(BF16))
    scores = jax.nn.sigmoid(logits_t)
    sel = scores + rb_ref[...]
    shape3 = (N_GROUPS, GROUP_SIZE, tm)
    sel3 = sel.reshape(shape3)
    scores3 = scores.reshape(shape3)
    member = lax.broadcasted_iota(jnp.int32, shape3, 1)
    m1, i1 = _first_max(sel3, member, (1,), GROUP_SIZE)
    m2 = jnp.max(jnp.where(member == i1, -jnp.inf, sel3), axis=1, keepdims=True)
    gscore = m1 + m2
    gid = lax.broadcasted_iota(jnp.int32, gscore.shape, 0)
    gsel = jnp.zeros(gscore.shape, jnp.bool_)
    for _ in range(TOPK_GROUPS):
        _, gi = _first_max(gscore, gid, (0,), N_GROUPS)
        hit = gid == gi
        gsel = jnp.logical_or(gsel, hit)
        gscore = jnp.where(hit, -jnp.inf, gscore)
    eid = lax.broadcasted_iota(jnp.int32, shape3, 0) * GROUP_SIZE + member
    cur = jnp.where(gsel, sel3, -jnp.inf)
    chosen = jnp.zeros(shape3, F32)
    for _ in range(TOP_K):
        _, ei = _first_max(cur, eid, (1, 0), N_EXPERTS)
        hit = eid == ei
        chosen = jnp.where(hit, 1.0, chosen)
        cur = jnp.where(hit, -jnp.inf, cur)
    w = scores3 * chosen
    tot = jnp.sum(jnp.sum(w, axis=1, keepdims=True), axis=0, keepdims=True)
    chosen_ref[...] = chosen.reshape(N_EXPERTS, tm)
    gate_ref[...] = (w / tot * ROUTED_SCALE).reshape(N_EXPERTS, tm)


def _router(x, rw_t, rb_col, *, tm, tokens=None):
    slab_in = x.ndim == 3
    if slab_in:
        t = tokens
        steps_per_chunk = (t // x.shape[0]) // tm
        assert t % (x.shape[0] * tm) == 0
        x_spec = pl.BlockSpec((None, tm * SLAB_ROWS, LANES), lambda i: (i // steps_per_chunk, i % steps_per_chunk, 0))
    else:
        t = x.shape[0]
        x_spec = pl.BlockSpec((tm, D_MODEL), lambda i: (i, 0))
    return pl.pallas_call(
        functools.partial(_router_kernel, tm=tm, slab_in=slab_in),
        grid=(t // tm,),
        in_specs=[
            x_spec,
            pl.BlockSpec(rw_t.shape, lambda i: (0, 0)),
            pl.BlockSpec(rb_col.shape, lambda i: (0, 0)),
        ],
        out_specs=[pl.BlockSpec((N_EXPERTS, tm), lambda i: (0, i)), pl.BlockSpec((N_EXPERTS, tm), lambda i: (0, i))],
        out_shape=[jax.ShapeDtypeStruct((N_EXPERTS, t), F32), jax.ShapeDtypeStruct((N_EXPERTS, t), F32)],
        compiler_params=pltpu.CompilerParams(dimension_semantics=("arbitrary",), vmem_limit_bytes=VMEM_LIMIT),
        name="router",
    )(x, rw_t, rb_col)


ROW_TILE = 288
TILE_PITCH = ROW_TILE + 1
GATE_LANES = -(-ROW_TILE // LANES) * LANES
SPARE_TOKENS = 8
FLAG_FIRST, FLAG_LAST, FLAG_NEW_EXPERT, FLAG_VALID, FLAG_HAS_NEXT = 1, 2, 4, 8, 16
SCATTER_BATCH = 8


PREV, CUR, NEXT = 0, 1, 2


def _moe_routed_kernel(ce_ref, flags_ref, used_ref, nexte_ref, idx_ref, where_ref, gseg_ref, x_ref, wg_ref, wu_ref,
                       wd_ref, sg_ref, su_ref, sd_ref, lng_ref, lnb_ref, o_ref, wgu_ref, wdb_ref,
                       gat0_ref, gat1_ref, res0_ref, res1_ref, wsg_ref, wsu_ref, wsd_ref, wsem_ref, *,
                       layer, alpha, chunk_tokens, sub_tokens):
    del used_ref
    step = pl.program_id(0)
    flags = flags_ref[step]
    odd = (step & 1) == 1
    nchunk = D_MODEL // LANES
    ne, ff = wg_ref.shape[1], wg_ref.shape[3]

    def slab_row(which, r):
        return pl.multiple_of(idx_ref[0, which, r], 8)

    def gather_row(which, gat_ref, r):
        gat_ref[pl.ds(r, nchunk, stride=TILE_PITCH), :] = x_ref[pl.ds(slab_row(which, r), 8), :]

    def scatter_rows(which, res_ref, rows):
        dst = [slab_row(which, r) for r in rows]
        acc = [o_ref[pl.ds(d, 8), :] + res_ref[pl.ds(r, nchunk, stride=TILE_PITCH), :] for d, r in zip(dst, rows)]
        for d, a in zip(dst, acc):
            o_ref[pl.ds(d, 8), :] = a

    def expert_mlp(gat_ref, res_ref):
        lhs = jnp.concatenate([gat_ref[j * TILE_PITCH:j * TILE_PITCH + ROW_TILE, :] for j in range(nchunk)],
                              axis=1).astype(BF16)
        h = _dot(lhs, wgu_ref[...])
        def row_to_cols(row):
            return jnp.broadcast_to(row, (LANES, GATE_LANES)).T[0:ROW_TILE]

        group_col, pos_col = row_to_cols(where_ref[0, 0:1, :]), row_to_cols(where_ref[0, 1:2, :])
        gl = gseg_ref.shape[2]
        pick_group = (lax.broadcasted_iota(jnp.int32, (ROW_TILE, gl), 1).astype(F32)
                      == jnp.concatenate([group_col] * (gl // LANES), axis=1)).astype(F32)
        group_gates = lax.dot_general(pick_group, gseg_ref[0], (((1,), (1,)), ((), ())),
                                      precision=lax.Precision.HIGHEST, preferred_element_type=F32)
        pick_pos = (lax.broadcasted_iota(jnp.int32, (ROW_TILE, PLAN_GROUP), 1).astype(F32)
                    == pos_col[:, 0:PLAN_GROUP]).astype(F32)
        gate_col = jnp.dot(group_gates * pick_pos, jnp.ones((PLAN_GROUP, LANES), F32),
                           precision=lax.Precision.HIGHEST, preferred_element_type=F32)
        hid = jax.nn.silu(h[:, 0:ff]) * h[:, ff:2 * ff] * jnp.concatenate([gate_col] * (ff // LANES), axis=1)
        y = _dot(hid.astype(BF16), wdb_ref[...])
        for j in range(nchunk):
            res_ref[j * TILE_PITCH:j * TILE_PITCH + ROW_TILE, :] = y[:, j * LANES:(j + 1) * LANES]

    def by_parity(fn):
        @pl.when(jnp.logical_not(odd))
        def _even():
            fn(gat0_ref, gat1_ref, res0_ref, res1_ref)

        @pl.when(odd)
        def _odd():
            fn(gat1_ref, gat0_ref, res1_ref, res0_ref)

    @pl.when((flags & FLAG_FIRST) != 0)
    def _start_chunk():
        o_ref[...] = jnp.zeros(o_ref.shape, F32)

        def start(gat_cur, gat_other, res_cur, res_other):
            res_other[...] = jnp.zeros(res_other.shape, F32)

            def body(r, carry):
                gather_row(CUR, gat_cur, r)
                return carry

            lax.fori_loop(0, ROW_TILE, body, 0)

        by_parity(start)

    def weight_copies(e):
        return [pltpu.make_async_copy(src.at[layer, e], dst, wsem_ref.at[k])
                for k, (src, dst) in enumerate(((wg_ref, wsg_ref), (wu_ref, wsu_ref), (wd_ref, wsd_ref)))]

    @pl.when(step == 0)
    def _first_fetch():
        for cp in weight_copies(ce_ref[0] % ne):
            cp.start()

    @pl.when((flags & FLAG_NEW_EXPERT) != 0)
    def _next_expert():
        for cp in weight_copies(0):
            cp.wait()
        wgu_ref[:, 0:ff] = wsg_ref[...].astype(BF16)
        wgu_ref[:, ff:2 * ff] = wsu_ref[...].astype(BF16)
        wdb_ref[...] = wsd_ref[...].astype(BF16)

        @pl.when((flags & FLAG_HAS_NEXT) != 0)
        def _prefetch():
            for cp in weight_copies(nexte_ref[step]):
                cp.start()

    @pl.when((flags & FLAG_VALID) != 0)
    def _tile():
        def main(gat_cur, gat_other, res_cur, res_other):
            for r0 in range(0, ROW_TILE, SCATTER_BATCH):
                scatter_rows(PREV, res_other, range(r0, r0 + SCATTER_BATCH))
            for r in range(ROW_TILE):
                gather_row(NEXT, gat_other, r)
            expert_mlp(gat_cur, res_cur)

        by_parity(main)

    @pl.when((flags & FLAG_LAST) != 0)
    def _finish():
        def flush(gat_cur, gat_other, res_cur, res_other):
            def body(r, carry):
                scatter_rows(CUR, res_cur, [r])
                return carry

            lax.fori_loop(0, ROW_TILE, body, 0)

        by_parity(flush)

        def body(s, carry):
            base = pl.multiple_of(s * (sub_tokens * 8), 8)

            def rows_2d(ref):
                return jnp.concatenate([ref[pl.ds(base + j, sub_tokens, stride=8), :] for j in range(nchunk)],
                                       axis=1)

            x2 = rows_2d(x_ref)
            xb = x2.astype(BF16)
            hs = jax.nn.silu(_dot(xb, sg_ref[...])) * _dot(xb, su_ref[...])
            ffn = rows_2d(o_ref) + _dot(hs.astype(BF16), sd_ref[...])
            res = _layer_norm(alpha * x2 + ffn, lng_ref[...], lnb_ref[...])
            for j in range(nchunk):
                o_ref[pl.ds(base + j, sub_tokens, stride=8), :] = res[:, j * LANES:(j + 1) * LANES]
            return carry

        lax.fori_loop(0, chunk_tokens // sub_tokens, body, 0)


def _moe_routed(tile_ce, tile_flags, n_used, next_expert, row_idx, row_where, seg_gates, x_tm, wg, wu, wd, sg, su, sd, ln_g,
                ln_b, *, layer, alpha, chunk_tokens, sub_tokens):
    n_chunks, chunk_rows, _ = x_tm.shape
    n_tiles = row_idx.shape[0]
    _, ne, d, ff = wg.shape
    kernel = functools.partial(_moe_routed_kernel, layer=layer, alpha=alpha, chunk_tokens=chunk_tokens,
                               sub_tokens=sub_tokens)

    def tile_map(i, ce, fl, used, nxt):
        return (jnp.minimum(i, used[0] - 1), 0, 0)

    def chunk_map(i, ce, fl, used, nxt):
        return (ce[i] // ne, 0, 0)

    def const2(i, ce, fl, used, nxt):
        return (0, 0)

    grid_spec = pltpu.PrefetchScalarGridSpec(
        num_scalar_prefetch=4,
        grid=(n_tiles,),
        in_specs=[
            pl.BlockSpec((1, 3, ROW_TILE), tile_map, memory_space=pltpu.SMEM),
            pl.BlockSpec((1, 2, GATE_LANES), tile_map),
            pl.BlockSpec((1,) + seg_gates.shape[1:], lambda i, ce, fl, used, nxt: (ce[i], 0, 0)),
            pl.BlockSpec((None, chunk_rows, LANES), chunk_map, pipeline_mode=pl.Buffered(1)),
            pl.BlockSpec(memory_space=pl.ANY),
            pl.BlockSpec(memory_space=pl.ANY),
            pl.BlockSpec(memory_space=pl.ANY),
            pl.BlockSpec(sg.shape, const2, pipeline_mode=pl.Buffered(1)),
            pl.BlockSpec(su.shape, const2, pipeline_mode=pl.Buffered(1)),
            pl.BlockSpec(sd.shape, const2, pipeline_mode=pl.Buffered(1)),
            pl.BlockSpec(ln_g.shape, const2),
            pl.BlockSpec(ln_b.shape, const2),
        ],
        out_specs=pl.BlockSpec((None, chunk_rows, LANES), chunk_map, pipeline_mode=pl.Buffered(1)),
        scratch_shapes=[
            pltpu.VMEM((d, 2 * ff), BF16),
            pltpu.VMEM((ff, d), BF16),
            pltpu.VMEM((8 * TILE_PITCH, LANES), F32),
            pltpu.VMEM((8 * TILE_PITCH, LANES), F32),
            pltpu.VMEM((8 * TILE_PITCH, LANES), F32),
            pltpu.VMEM((8 * TILE_PITCH, LANES), F32),
            pltpu.VMEM((d, ff), F32),
            pltpu.VMEM((d, ff), F32),
            pltpu.VMEM((ff, d), F32),
            pltpu.SemaphoreType.DMA((3,)),
        ],
    )
    return pl.pallas_call(
        kernel,
        grid_spec=grid_spec,
        out_shape=jax.ShapeDtypeStruct(x_tm.shape, F32),
        compiler_params=pltpu.CompilerParams(dimension_semantics=("arbitrary",), vmem_limit_bytes=VMEM_LIMIT),
        name="moe_routed",
    )(tile_ce, tile_flags, n_used, next_expert, row_idx, row_where, seg_gates, x_tm, wg, wu, wd, sg, su, sd, ln_g, ln_b)


MOE_CHUNKS = 4
MOE_SUB_MAX = 512


def _moe_tiling(n_prompt, n_sample):
    assert n_prompt % (MOE_CHUNKS * 8) == 0 and n_sample % (MOE_CHUNKS * 8) == 0
    chunk_tokens = (n_prompt + n_sample) // MOE_CHUNKS
    sub_tokens = max(s for s in range(8, MOE_SUB_MAX + 1, 8) if chunk_tokens % s == 0)
    return MOE_CHUNKS, chunk_tokens, sub_tokens


PLAN_GROUP = 32


def _chunked_t(prompt, sample, n_chunks):
    ne = prompt.shape[0]
    return jnp.concatenate([prompt.reshape(ne, n_chunks, -1).transpose(1, 0, 2),
                            sample.reshape(ne, n_chunks, -1).transpose(1, 0, 2)], axis=2)


def _nth_set_bit(word, n):
    pos = jnp.zeros(word.shape, jnp.int32)
    for width in (16, 8, 4, 2, 1):
        low = word & jnp.uint32((1 << width) - 1)
        below = lax.population_count(low).astype(jnp.int32)
        upper = n >= below
        n = jnp.where(upper, n - below, n)
        word = jnp.where(upper, word >> jnp.uint32(width), low)
        pos = pos + jnp.where(upper, width, 0)
    return pos


def _route_plan(chosen_p, chosen_s, gate_p, gate_s, *, chunk_tokens, n_chunks):
    n_seg = n_chunks * N_EXPERTS
    n_group = chunk_tokens // PLAN_GROUP
    assert chunk_tokens % PLAN_GROUP == 0
    chosen = _chunked_t(chosen_p, chosen_s, n_chunks) > 0
    gate = _chunked_t(gate_p, gate_s, n_chunks)
    shifts = jnp.arange(PLAN_GROUP, dtype=jnp.uint32)
    words = jnp.sum(chosen.reshape(n_seg, n_group, PLAN_GROUP).astype(jnp.uint32) << shifts, axis=-1,
                    dtype=jnp.uint32)
    per_group = lax.population_count(words).astype(jnp.int32)
    group_start = jnp.cumsum(per_group, axis=-1) - per_group
    counts = jnp.sum(per_group, axis=-1)
    padded = -(-counts // ROW_TILE) * ROW_TILE
    seg_end = jnp.cumsum(padded)
    seg_start = seg_end - padded
    n_tiles = -(-(n_chunks * chunk_tokens * TOP_K + n_seg * (ROW_TILE - 1)) // ROW_TILE)
    row0 = jnp.arange(n_tiles, dtype=jnp.int32) * ROW_TILE
    seg = jnp.sum(seg_end[None, :] <= row0[:, None], axis=1, dtype=jnp.int32)
    valid = seg < n_seg
    tile_ce = jnp.minimum(seg, n_seg - 1)
    j = (row0 - seg_start[tile_ce])[:, None] + jnp.arange(ROW_TILE, dtype=jnp.int32)[None, :]
    real = valid[:, None] & (j < counts[tile_ce][:, None])
    starts = group_start[tile_ce][:, None, :]
    group = jnp.sum(starts <= j[:, :, None], axis=-1, dtype=jnp.int32) - 1
    in_group = jnp.arange(n_group, dtype=jnp.int32)[None, None, :] == group[:, :, None]
    start = jnp.sum(jnp.where(in_group, starts, 0), axis=-1, dtype=jnp.int32)
    word = jnp.sum(jnp.where(in_group, words[tile_ce][:, None, :], jnp.uint32(0)), axis=-1, dtype=jnp.uint32)
    pos = _nth_set_bit(word, j - start)
    token = jnp.where(real, group * PLAN_GROUP + pos, chunk_tokens)
    row_where = jnp.stack([jnp.where(real, group, -1), jnp.where(real, pos, 0)], axis=1).astype(F32)
    row_where = jnp.pad(row_where, ((0, 0), (0, 0), (0, GATE_LANES - ROW_TILE)), constant_values=-1.0)
    group_lanes = -(-n_group // LANES) * LANES
    seg_gates = jnp.pad(jnp.swapaxes(gate.reshape(n_seg, n_group, PLAN_GROUP), 1, 2),
                        ((0, 0), (0, 0), (0, group_lanes - n_group)))
    tile_c = tile_ce // N_EXPERTS
    prev_ce = jnp.concatenate([jnp.full((1,), -N_EXPERTS, jnp.int32), tile_ce[:-1]])
    next_c = jnp.concatenate([tile_c[1:], jnp.full((1,), -1, jnp.int32)])
    next_valid = jnp.concatenate([valid[1:], jnp.zeros((1,), jnp.bool_)])
    first = valid & (tile_c != prev_ce // N_EXPERTS)
    last = valid & (~next_valid | (next_c != tile_c))
    new_expert = valid & (tile_ce != prev_ce)
    tile_id = jnp.arange(n_tiles, dtype=jnp.int32)
    change_at = jnp.where(new_expert, tile_id, n_tiles)
    next_change = jnp.concatenate([lax.cummin(change_at, reverse=True)[1:], jnp.full((1,), n_tiles, jnp.int32)])
    has_next = new_expert & (next_change < n_tiles)
    next_expert = tile_ce[jnp.minimum(next_change, n_tiles - 1)] % N_EXPERTS
    flags = (first * FLAG_FIRST + last * FLAG_LAST + new_expert * FLAG_NEW_EXPERT + valid * FLAG_VALID
             + has_next * FLAG_HAS_NEXT)
    row_idx = token * SLAB_ROWS
    row_idx3 = jnp.stack([jnp.concatenate([row_idx[:1], row_idx[:-1]]), row_idx,
                          jnp.concatenate([row_idx[1:], row_idx[-1:]])], axis=1)
    n_used = jnp.sum(valid, dtype=jnp.int32).reshape(1)
    return tile_ce, flags.astype(jnp.int32), n_used, next_expert.astype(jnp.int32), row_idx3, row_where, seg_gates


def _t5_bucket(dist):
    n = jnp.maximum(dist, 0)
    max_exact = N_BUCKETS // 2
    large = max_exact + (jnp.log(jnp.maximum(n, 1).astype(F32) / max_exact)
                         / math.log(MAX_DISTANCE / max_exact) * (N_BUCKETS - max_exact)).astype(jnp.int32)
    large = jnp.minimum(large, N_BUCKETS - 1)
    return jnp.where(n < max_exact, n, large)


def _bias_lookup(rel_bias, bucket):
    onehot = (bucket[..., None] == jnp.arange(N_BUCKETS, dtype=bucket.dtype)).astype(F32)
    return jnp.einsum("...b,bh->h...", onehot, rel_bias.astype(F32), precision=lax.Precision.HIGHEST)


def _bias_tables(rel_bias, win):
    qi = jnp.arange(ATTN_BLOCK)[:, None]
    ki = jnp.arange(2 * ATTN_BLOCK)[None, :]
    dist = qi + ATTN_BLOCK - ki
    valid = (dist >= 0) & (dist < WINDOW)
    bias = _bias_lookup(rel_bias, _t5_bucket(dist))
    bias = jnp.where(valid[None], bias, NEG).reshape(N_KV_HEADS, GROUP, ATTN_BLOCK, 2 * ATTN_BLOCK)
    bias_ab = jnp.stack([jnp.concatenate([bias[:, t], bias[:, t + 2]], axis=1) for t in range(2)], axis=1)
    dist_s = (win - 1) - jnp.arange(win)
    bias_s = _bias_lookup(rel_bias, _t5_bucket(dist_s))
    return bias_ab, bias_s


def _sink_tables(sink):
    s = sink.astype(F32).reshape(N_KV_HEADS, GROUP)
    rows = [jnp.concatenate([jnp.broadcast_to(s[:, t, None], (N_KV_HEADS, ATTN_BLOCK)),
                             jnp.broadcast_to(s[:, t + 2, None], (N_KV_HEADS, ATTN_BLOCK))], axis=1)
            for t in range(2)]
    return jnp.stack(rows, axis=1)[..., None], sink.astype(F32)[:, None]


def kernel(x_prompt, x_sample, cache_k_win, cache_v_win, state_conv, rel_bias, w_in, attn_sink, conv_w,
           w_attn_out, w_conv_out, w_out, ln1_g, ln1_b, router_w, router_bias, exp_w_gate, exp_w_up,
           exp_w_down, shared_w_gate, shared_w_up, shared_w_down, ln2_g, ln2_b):
    depth = w_in.shape[0]
    alpha = (2 * depth) ** 0.25
    nb, seq, d = x_prompt.shape
    nd = x_sample.shape[0]
    win = cache_k_win.shape[2]
    assert x_sample.shape[1] == 1 and win == WINDOW and seq % 512 == 0

    n_prompt = nb * seq
    n_chunks, chunk_tokens, sub_tokens = _moe_tiling(n_prompt, nd)
    prompt_rows = n_prompt // n_chunks * SLAB_ROWS
    slab_shape = (n_chunks, (chunk_tokens + SPARE_TOKENS) * SLAB_ROWS, LANES)

    bias_ab, bias_s = _bias_tables(rel_bias, win)
    hmask = (jnp.arange(KV_DIM)[None, :] // HEAD_DIM == jnp.arange(N_HEADS)[:, None] // GROUP).astype(F32)

    yp = x_prompt
    ys = x_sample.reshape(nd, d)
    outs = [[] for _ in range(6)]
    for l in range(depth):
        w_in_b = w_in[l].astype(BF16)
        w_ao_b = w_attn_out[l].astype(BF16)
        w_co_b = w_conv_out[l].astype(BF16)
        w_o_b = w_out[l].astype(BF16)
        g1, b1 = ln1_g[l][None, :], ln1_b[l][None, :]
        g2, b2 = ln2_g[l][None, :], ln2_b[l][None, :]
        sink_ab, sink_col = _sink_tables(attn_sink[l])

        slab, kp, vp, cp = _mixer_prompt(yp, w_in_b, w_attn_out[l].T.astype(BF16), w_co_b, w_o_b, conv_w[l],
                                         jnp.swapaxes(bias_ab, -1, -2), jnp.swapaxes(sink_ab, -1, -2), g1, b1,
                                         alpha=alpha, tq=512, batch=nb, seq=seq, slab_shape=slab_shape,
                                         base=jnp.zeros(slab_shape, F32) if l == 0 else None)

        proj = _sample_proj(ys, w_in_b, tn=IN_DIM // 4)
        q4 = proj[:, :Q_DIM].reshape(nd, N_KV_HEADS, GROUP, HEAD_DIM).transpose(0, 2, 1, 3).reshape(nd, GROUP, KV_DIM)
        ksn, vsn, ag = _sample_attn(q4, proj[:, OFF_K:OFF_V], proj[:, OFF_V:OFF_B],
                                    cache_k_win.reshape(depth, nd, win, KV_DIM),
                                    cache_v_win.reshape(depth, nd, win, KV_DIM),
                                    bias_s, sink_col, hmask, layer=l, bt=16)
        att = ag.reshape(nd, GROUP, N_KV_HEADS, HEAD_DIM).transpose(0, 2, 1, 3).reshape(nd, Q_DIM)
        state_t = jnp.swapaxes(state_conv[l], 0, 1)
        ys, us = _sample_post(ys, att, proj, state_t, conv_w[l], w_ao_b, w_co_b, w_o_b, g1, b1, alpha=alpha)

        outs[0].append(kp.reshape(nb, WINDOW, N_KV_HEADS, HEAD_DIM))
        outs[1].append(vp.reshape(nb, WINDOW, N_KV_HEADS, HEAD_DIM))
        outs[2].append(cp)
        outs[3].append(ksn.reshape(nd, win, N_KV_HEADS, HEAD_DIM))
        outs[4].append(vsn.reshape(nd, win, N_KV_HEADS, HEAD_DIM))
        outs[5].append(jnp.concatenate([state_conv[l][:, 1:], us[:, None, :]], axis=1))

        rw_t = router_w[l].T.astype(BF16)
        rb_col = router_bias[l].astype(F32)[:, None]
        sg, su, sd = (shared_w_gate[l].astype(BF16), shared_w_up[l].astype(BF16), shared_w_down[l].astype(BF16))
        tail = jnp.concatenate([ys.reshape(n_chunks, -1, LANES),
                                jnp.zeros((n_chunks, SPARE_TOKENS * SLAB_ROWS, LANES), F32)], axis=1)
        slab = lax.dynamic_update_slice(slab, tail, (0, prompt_rows, 0))
        ep, wp = _router(slab, rw_t, rb_col, tm=512, tokens=n_prompt)
        es, ws = _router(ys, rw_t, rb_col, tm=nd)
        plan = _route_plan(ep, es, wp, ws, chunk_tokens=chunk_tokens, n_chunks=n_chunks)
        yp = _moe_routed(*plan, slab, exp_w_gate, exp_w_up, exp_w_down, sg, su, sd, g2, b2,
                         layer=l, alpha=alpha, chunk_tokens=chunk_tokens, sub_tokens=sub_tokens)
        ys = yp[:, prompt_rows:chunk_tokens * SLAB_ROWS].reshape(nd, d)

    y_prompt = yp[:, :prompt_rows].reshape(nb, seq, d)
    return (y_prompt, ys.reshape(nd, 1, d)) + tuple(jnp.stack(o) for o in outs)
```

```python
import functools
import math

import jax
import jax.numpy as jnp
from jax import lax
from jax.experimental import pallas as pl
from jax.experimental.pallas import tpu as pltpu

D_MODEL = 1024
N_HEADS = 16
N_KV_HEADS = 4
HEAD_DIM = 64
GROUP = N_HEADS // N_KV_HEADS
WINDOW = 128
ATTN_BLOCK = 128
N_BUCKETS = 32
MAX_DISTANCE = 128
CONV_DIM = 1024
CONV_K = 3
N_EXPERTS = 64
TOP_K = 8
N_GROUPS = 8
TOPK_GROUPS = 4
GROUP_SIZE = N_EXPERTS // N_GROUPS
EXPERT_FF = 256
ROUTED_SCALE = 2.5
LN_EPS = 1e-5
NEG = -1e30

Q_DIM = N_HEADS * HEAD_DIM
KV_DIM = N_KV_HEADS * HEAD_DIM
OFF_K = Q_DIM
OFF_V = OFF_K + KV_DIM
OFF_B = OFF_V + KV_DIM
OFF_C = OFF_B + CONV_DIM
OFF_H = OFF_C + CONV_DIM
OFF_GA = OFF_H + CONV_DIM
OFF_GB = OFF_GA + D_MODEL
IN_DIM = OFF_GB + D_MODEL

LANES = 128
CONV_PAD = 8
VMEM_LIMIT = 60 * 1024 * 1024

BF16 = jnp.bfloat16
F32 = jnp.float32


def _dot(a, b):
    return jnp.dot(a, b, preferred_element_type=F32)


def _dot_nt(a, b):
    return lax.dot_general(a, b, (((1,), (1,)), ((), ())), preferred_element_type=F32)


def _layer_norm(z, g, b):
    mu = jnp.mean(z, axis=-1, keepdims=True)
    d = z - mu
    var = jnp.mean(d * d, axis=-1, keepdims=True)
    return d * lax.rsqrt(var + LN_EPS) * g + b


def _sink_softmax(s, sink, axis=-1):
    m = jnp.maximum(jnp.max(s, axis=axis, keepdims=True), sink)
    e = jnp.exp(s - m)
    den = jnp.sum(e, axis=axis, keepdims=True) + jnp.exp(sink - m)
    return e * (1.0 / den)


SLAB_ROWS = D_MODEL // LANES


def _slab_load(ref, tokens, base=0):
    return jnp.concatenate([ref[pl.ds(base + j, tokens, stride=SLAB_ROWS), :] for j in range(SLAB_ROWS)], axis=1)


def _slab_store(ref, val, base=0):
    tokens = val.shape[0]
    for j in range(SLAB_ROWS):
        ref[pl.ds(base + j, tokens, stride=SLAB_ROWS), :] = val[:, j * LANES:(j + 1) * LANES]


def _merge_project(x, attn_o, y_conv, g_a, g_b, w_co_ref, w_o_ref, lng_ref, lnb_ref, alpha):
    merged = jax.nn.sigmoid(g_a) * attn_o + jax.nn.sigmoid(g_b) * _dot(y_conv.astype(BF16), w_co_ref[...])
    out = _dot(merged.astype(BF16), w_o_ref[...])
    return _layer_norm(alpha * x + out, lng_ref[...], lnb_ref[...])


def _mixer_prompt_kernel(x_ref, w_in_ref, w_ao_ref, w_co_ref, w_o_ref, convw_ref, bias_ref, sink_ref,
                         lng_ref, lnb_ref, base_ref,
                         x1_ref, kwin_ref, vwin_ref, conv_ref,
                         ka_ref, kb_ref, vat_ref, vbt_ref, s_ref, p_ref, att_ref, ubuf_ref, *, alpha, tq, slab_in):
    del base_ref
    i = pl.program_id(1)
    nblk = tq // ATTN_BLOCK
    half = LANES // 2
    scale = HEAD_DIM ** -0.5
    assert math.frexp(scale)[0] == 0.5

    @pl.when(i == 0)
    def _init():
        for ref in (ka_ref, kb_ref):
            ref[:, 0:ATTN_BLOCK, :] = jnp.zeros((N_KV_HEADS, ATTN_BLOCK, LANES), BF16)
        for ref in (vat_ref, vbt_ref):
            ref[:, :, 0:ATTN_BLOCK] = jnp.zeros((N_KV_HEADS, LANES, ATTN_BLOCK), BF16)
        ubuf_ref[0:CONV_PAD, :] = jnp.zeros((CONV_PAD, CONV_DIM), F32)

    x = _slab_load(x_ref, tq) if slab_in else x_ref[0]
    xb = x.astype(BF16)
    qkv = _dot(xb, w_in_ref[:, 0:OFF_B])

    lo = lax.broadcasted_iota(jnp.int32, (tq, LANES), 1) < half
    zeros_t = jnp.zeros((half, tq), BF16)
    for c in range(N_KV_HEADS // 2):
        chunk = qkv[:, OFF_K + c * LANES: OFF_K + (c + 1) * LANES]
        c_lo = jnp.where(lo, chunk, 0.0)
        c_hi = jnp.where(lo, 0.0, chunk)
        ka_ref[2 * c, ATTN_BLOCK:, :] = c_lo.astype(BF16)
        kb_ref[2 * c, ATTN_BLOCK:, :] = pltpu.roll(c_lo, half, 1).astype(BF16)
        kb_ref[2 * c + 1, ATTN_BLOCK:, :] = c_hi.astype(BF16)
        ka_ref[2 * c + 1, ATTN_BLOCK:, :] = pltpu.roll(c_hi, half, 1).astype(BF16)
        vt = qkv[:, OFF_V + c * LANES: OFF_V + (c + 1) * LANES].T.astype(BF16)
        vat_ref[2 * c, :, ATTN_BLOCK:] = jnp.concatenate([vt[0:half], zeros_t], axis=0)
        vbt_ref[2 * c, :, ATTN_BLOCK:] = jnp.concatenate([zeros_t, vt[0:half]], axis=0)
        vbt_ref[2 * c + 1, :, ATTN_BLOCK:] = jnp.concatenate([zeros_t, vt[half:]], axis=0)
        vat_ref[2 * c + 1, :, ATTN_BLOCK:] = jnp.concatenate([vt[half:], zeros_t], axis=0)

    key_row = lax.broadcasted_iota(jnp.int32, (2 * ATTN_BLOCK, 2 * ATTN_BLOCK), 0)
    for j in range(nblk):
        rows = slice(j * ATTN_BLOCK, (j + 1) * ATTN_BLOCK)
        keys = slice(j * ATTN_BLOCK, (j + 2) * ATTN_BLOCK)
        for h in range(N_KV_HEADS):
            q0 = h * GROUP * HEAD_DIM
            q2 = (jnp.concatenate([qkv[rows, q0:q0 + LANES], qkv[rows, q0 + LANES:q0 + 2 * LANES]], axis=0)
                  * scale).astype(BF16)
            for t, k_ref in enumerate((ka_ref, kb_ref)):
                s = _dot_nt(k_ref[h, keys, :], q2)
                if j == 0:
                    s = jnp.where(jnp.logical_and(i == 0, key_row < ATTN_BLOCK), NEG, s)
                s_ref[j, h, t] = s
    for h in range(N_KV_HEADS):
        for t in range(2):
            s = s_ref[:, h, t] + bias_ref[h, t][None]
            p_ref[:, h, t] = _sink_softmax(s, sink_ref[h, t][None], axis=1).astype(BF16)
    for j in range(nblk):
        rows = slice(j * ATTN_BLOCK, (j + 1) * ATTN_BLOCK)
        keys = slice(j * ATTN_BLOCK, (j + 2) * ATTN_BLOCK)
        for h in range(N_KV_HEADS):
            q0 = h * GROUP * HEAD_DIM
            o_t = _dot(vat_ref[h, :, keys], p_ref[j, h, 0]) + _dot(vbt_ref[h, :, keys], p_ref[j, h, 1])
            att_ref[q0:q0 + LANES, rows] = o_t[:, 0:ATTN_BLOCK].astype(BF16)
            att_ref[q0 + LANES:q0 + 2 * LANES, rows] = o_t[:, ATTN_BLOCK:].astype(BF16)
    for ref in (ka_ref, kb_ref):
        ref[:, 0:ATTN_BLOCK, :] = ref[:, tq:tq + ATTN_BLOCK, :]
    for ref in (vat_ref, vbt_ref):
        ref[:, :, 0:ATTN_BLOCK] = ref[:, :, tq:tq + ATTN_BLOCK]

    kwin_ref[0] = qkv[tq - WINDOW:tq, OFF_K:OFF_V]
    vwin_ref[0] = qkv[tq - WINDOW:tq, OFF_V:OFF_B]

    attn_o = _dot(w_ao_ref[...], att_ref[...]).T

    bch = _dot(xb, w_in_ref[:, OFF_B:OFF_GA])
    u = bch[:, CONV_DIM:2 * CONV_DIM] * bch[:, 2 * CONV_DIM:3 * CONV_DIM]
    ubuf_ref[CONV_PAD:CONV_PAD + tq, :] = u
    cw = convw_ref[...]
    y = (cw[0:1] * ubuf_ref[CONV_PAD - 2:CONV_PAD - 2 + tq, :]
         + cw[1:2] * ubuf_ref[CONV_PAD - 1:CONV_PAD - 1 + tq, :]
         + cw[2:3] * u)
    conv_ref[0] = ubuf_ref[CONV_PAD + tq - (CONV_K - 1):CONV_PAD + tq, :]
    ubuf_ref[0:CONV_PAD, :] = ubuf_ref[tq:tq + CONV_PAD, :]
    y_conv = bch[:, 0:CONV_DIM] * y

    gab = _dot(xb, w_in_ref[:, OFF_GA:IN_DIM])
    _slab_store(x1_ref, _merge_project(x, attn_o, y_conv, gab[:, 0:D_MODEL], gab[:, D_MODEL:], w_co_ref, w_o_ref,
                                       lng_ref, lnb_ref, alpha))


def _const_spec(shape):
    nd = len(shape)
    return pl.BlockSpec(shape, lambda *_: (0,) * nd, pipeline_mode=pl.Buffered(1))


def _mixer_prompt(x, w_in, w_ao, w_co, w_o, conv_w, bias_ab, sink_ab, ln_g, ln_b, *, alpha, tq, batch, seq,
                  slab_shape, base=None):
    b, s = batch, seq
    n_chunks, chunk_rows, _ = slab_shape
    slab_in = x.ndim == 3 and x.shape == slab_shape
    steps_per_chunk = (b * s // n_chunks) // tq
    assert (b * s) % (n_chunks * tq) == 0 and s % tq == 0

    def slab_map(bi, i):
        g = bi * (s // tq) + i
        return (g // steps_per_chunk, g % steps_per_chunk, 0)

    slab_spec = pl.BlockSpec((None, tq * SLAB_ROWS, LANES), slab_map)
    kernel = functools.partial(_mixer_prompt_kernel, alpha=alpha, tq=tq, slab_in=slab_in)
    return pl.pallas_call(
        kernel,
        grid=(b, s // tq),
        in_specs=[
            slab_spec if slab_in else pl.BlockSpec((1, tq, D_MODEL), lambda bi, i: (bi, i, 0)),
            _const_spec(w_in.shape), _const_spec(w_ao.shape), _const_spec(w_co.shape), _const_spec(w_o.shape),
            _const_spec(conv_w.shape), _const_spec(bias_ab.shape), _const_spec(sink_ab.shape),
            _const_spec(ln_g.shape), _const_spec(ln_b.shape),
            pl.BlockSpec(memory_space=pl.ANY),
        ],
        out_specs=[
            slab_spec,
            pl.BlockSpec((1, WINDOW, KV_DIM), lambda bi, i: (bi, 0, 0)),
            pl.BlockSpec((1, WINDOW, KV_DIM), lambda bi, i: (bi, 0, 0)),
            pl.BlockSpec((1, CONV_K - 1, CONV_DIM), lambda bi, i: (bi, 0, 0)),
        ],
        out_shape=[
            jax.ShapeDtypeStruct(slab_shape, F32),
            jax.ShapeDtypeStruct((b, WINDOW, KV_DIM), F32),
            jax.ShapeDtypeStruct((b, WINDOW, KV_DIM), F32),
            jax.ShapeDtypeStruct((b, CONV_K - 1, CONV_DIM), F32),
        ],
        scratch_shapes=[
            pltpu.VMEM((N_KV_HEADS, ATTN_BLOCK + tq, LANES), BF16),
            pltpu.VMEM((N_KV_HEADS, ATTN_BLOCK + tq, LANES), BF16),
            pltpu.VMEM((N_KV_HEADS, LANES, ATTN_BLOCK + tq), BF16),
            pltpu.VMEM((N_KV_HEADS, LANES, ATTN_BLOCK + tq), BF16),
            pltpu.VMEM((tq // ATTN_BLOCK, N_KV_HEADS, 2, 2 * ATTN_BLOCK, 2 * ATTN_BLOCK), F32),
            pltpu.VMEM((tq // ATTN_BLOCK, N_KV_HEADS, 2, 2 * ATTN_BLOCK, 2 * ATTN_BLOCK), BF16),
            pltpu.VMEM((Q_DIM, tq), BF16),
            pltpu.VMEM((tq + CONV_PAD, CONV_DIM), F32),
        ],
        compiler_params=pltpu.CompilerParams(
            dimension_semantics=("arbitrary", "arbitrary"), vmem_limit_bytes=VMEM_LIMIT),
        input_output_aliases={0 if slab_in else 10: 0},
        name="mixer_prompt",
    )(x, w_in, w_ao, w_co, w_o, conv_w, bias_ab, sink_ab, ln_g, ln_b,
      jnp.zeros((1, SLAB_ROWS, LANES), F32) if slab_in else base)


def _proj_kernel(x_ref, w_ref, o_ref):
    o_ref[...] = _dot(x_ref[...].astype(BF16), w_ref[...])


def _sample_proj(x, w_in, *, tn):
    m, d = x.shape
    n = w_in.shape[1]
    return pl.pallas_call(
        _proj_kernel,
        grid=(n // tn,),
        in_specs=[pl.BlockSpec((m, d), lambda j: (0, 0)), pl.BlockSpec((d, tn), lambda j: (0, j))],
        out_specs=pl.BlockSpec((m, tn), lambda j: (0, j)),
        out_shape=jax.ShapeDtypeStruct((m, n), F32),
        compiler_params=pltpu.CompilerParams(dimension_semantics=("arbitrary",), vmem_limit_bytes=VMEM_LIMIT),
        name="sample_proj",
    )(x, w_in)


def _sample_attn_kernel(q4_ref, knew_ref, vnew_ref, ck_ref, cv_ref, bias_ref, sink_ref, hmask_ref,
                        nk_ref, nv_ref, ag_ref):
    bt = ck_ref.shape[0]
    win = ck_ref.shape[1]
    scale = HEAD_DIM ** -0.5
    row = lax.broadcasted_iota(jnp.int32, (win, KV_DIM), 0)
    last = row == win - 1
    hmask = hmask_ref[...]
    for b in range(bt):
        kb = jnp.where(last, knew_ref[b:b + 1, :], pltpu.roll(ck_ref[b], win - 1, 0))
        vb = jnp.where(last, vnew_ref[b:b + 1, :], pltpu.roll(cv_ref[b], win - 1, 0))
        nk_ref[b] = kb
        nv_ref[b] = vb
        q4 = q4_ref[b]
        qm = (jnp.concatenate([q4] * N_KV_HEADS, axis=0) * hmask).astype(BF16)
        s = _dot_nt(qm, kb.astype(BF16)) * scale + bias_ref[...]
        p = _sink_softmax(s, sink_ref[...]).astype(BF16)
        o = _dot(p, vb.astype(BF16)) * hmask
        o4 = o[0:GROUP]
        for h in range(1, N_KV_HEADS):
            o4 = o4 + o[h * GROUP:(h + 1) * GROUP]
        ag_ref[b] = o4


def _sample_attn(q4, k_new, v_new, cache_k, cache_v, bias_s, sink_col, hmask, *, layer, bt):
    _, nb, win, kvd = cache_k.shape
    return pl.pallas_call(
        _sample_attn_kernel,
        grid=(nb // bt,),
        in_specs=[
            pl.BlockSpec((bt, GROUP, kvd), lambda i: (i, 0, 0)),
            pl.BlockSpec((bt, kvd), lambda i: (i, 0)),
            pl.BlockSpec((bt, kvd), lambda i: (i, 0)),
            pl.BlockSpec((None, bt, win, kvd), lambda i: (layer, i, 0, 0)),
            pl.BlockSpec((None, bt, win, kvd), lambda i: (layer, i, 0, 0)),
            pl.BlockSpec(bias_s.shape, lambda i: (0, 0)),
            pl.BlockSpec(sink_col.shape, lambda i: (0, 0)),
            pl.BlockSpec(hmask.shape, lambda i: (0, 0)),
        ],
        out_specs=[
            pl.BlockSpec((bt, win, kvd), lambda i: (i, 0, 0)),
            pl.BlockSpec((bt, win, kvd), lambda i: (i, 0, 0)),
            pl.BlockSpec((bt, GROUP, kvd), lambda i: (i, 0, 0)),
        ],
        out_shape=[
            jax.ShapeDtypeStruct((nb, win, kvd), F32),
            jax.ShapeDtypeStruct((nb, win, kvd), F32),
            jax.ShapeDtypeStruct((nb, GROUP, kvd), F32),
        ],
        compiler_params=pltpu.CompilerParams(dimension_semantics=("arbitrary",), vmem_limit_bytes=VMEM_LIMIT),
        name="sample_attn",
    )(q4, k_new, v_new, cache_k, cache_v, bias_s, sink_col, hmask)


def _sample_post_kernel(x_ref, att_ref, proj_ref, st_ref, convw_ref, w_ao_ref, w_co_ref, w_o_ref,
                        lng_ref, lnb_ref, x1_ref, u_ref, *, alpha):
    attn_o = _dot(att_ref[...].astype(BF16), w_ao_ref[...])
    u = proj_ref[:, OFF_C:OFF_H] * proj_ref[:, OFF_H:OFF_GA]
    cw = convw_ref[...]
    y = cw[0:1] * st_ref[0] + cw[1:2] * st_ref[1] + cw[2:3] * u
    u_ref[...] = u
    y_conv = proj_ref[:, OFF_B:OFF_C] * y
    x1_ref[...] = _merge_project(x_ref[...], attn_o, y_conv, proj_ref[:, OFF_GA:OFF_GB], proj_ref[:, OFF_GB:IN_DIM],
                                 w_co_ref, w_o_ref, lng_ref, lnb_ref, alpha)


def _sample_post(x, att, proj, state, conv_w, w_ao, w_co, w_o, ln_g, ln_b, *, alpha):
    m, d = x.shape
    kernel = functools.partial(_sample_post_kernel, alpha=alpha)
    return pl.pallas_call(
        kernel,
        out_shape=[jax.ShapeDtypeStruct((m, d), F32), jax.ShapeDtypeStruct((m, CONV_DIM), F32)],
        compiler_params=pltpu.CompilerParams(vmem_limit_bytes=VMEM_LIMIT),
        name="sample_post",
    )(x, att, proj, state, conv_w, w_ao, w_co, w_o, ln_g, ln_b)


def _first_max(cur, ids, axes, big):
    m = cur
    for ax in axes:
        m = jnp.max(m, axis=ax, keepdims=True)
    idx = jnp.where(cur == m, ids, big)
    for ax in axes:
        idx = jnp.min(idx, axis=ax, keepdims=True)
    return m, idx


def _router_kernel(x_ref, rwt_ref, rb_ref, chosen_ref, gate_ref, *, tm, slab_in):
    x = _slab_load(x_ref, tm) if slab_in else x_ref[...]
    logits_t = _dot_nt(rwt_ref[...], x.astype(BF16))
    scores = jax.nn.sigmoid(logits_t)
    sel = scores + rb_ref[...]
    shape3 = (N_GROUPS, GROUP_SIZE, tm)
    sel3 = sel.reshape(shape3)
    scores3 = scores.reshape(shape3)
    member = lax.broadcasted_iota(jnp.int32, shape3, 1)
    m1, i1 = _first_max(sel3, member, (1,), GROUP_SIZE)
    m2 = jnp.max(jnp.where(member == i1, -jnp.inf, sel3), axis=1, keepdims=True)
    gscore = m1 + m2
    gid = lax.broadcasted_iota(jnp.int32, gscore.shape, 0)
    gsel = jnp.zeros(gscore.shape, jnp.bool_)
    for _ in range(TOPK_GROUPS):
        _, gi = _first_max(gscore, gid, (0,), N_GROUPS)
        hit = gid == gi
        gsel = jnp.logical_or(gsel, hit)
        gscore = jnp.where(hit, -jnp.inf, gscore)
    eid = lax.broadcasted_iota(jnp.int32, shape3, 0) * GROUP_SIZE + member
    cur = jnp.where(gsel, sel3, -jnp.inf)
    chosen = jnp.zeros(shape3, F32)
    for _ in range(TOP_K):
        _, ei = _first_max(cur, eid, (1, 0), N_EXPERTS)
        hit = eid == ei
        chosen = jnp.where(hit, 1.0, chosen)
        cur = jnp.where(hit, -jnp.inf, cur)
    w = scores3 * chosen
    tot = jnp.sum(jnp.sum(w, axis=1, keepdims=True), axis=0, keepdims=True)
    chosen_ref[...] = chosen.reshape(N_EXPERTS, tm)
    gate_ref[...] = (w / tot * ROUTED_SCALE).reshape(N_EXPERTS, tm)


def _router(x, rw_t, rb_col, *, tm, tokens=None):
    slab_in = x.ndim == 3
    if slab_in:
        t = tokens
        steps_per_chunk = (t // x.shape[0]) // tm
        assert t % (x.shape[0] * tm) == 0
        x_spec = pl.BlockSpec((None, tm * SLAB_ROWS, LANES), lambda i: (i // steps_per_chunk, i % steps_per_chunk, 0))
    else:
        t = x.shape[0]
        x_spec = pl.BlockSpec((tm, D_MODEL), lambda i: (i, 0))
    return pl.pallas_call(
        functools.partial(_router_kernel, tm=tm, slab_in=slab_in),
        grid=(t // tm,),
        in_specs=[
            x_spec,
            pl.BlockSpec(rw_t.shape, lambda i: (0, 0)),
            pl.BlockSpec(rb_col.shape, lambda i: (0, 0)),
        ],
        out_specs=[pl.BlockSpec((N_EXPERTS, tm), lambda i: (0, i)), pl.BlockSpec((N_EXPERTS, tm), lambda i: (0, i))],
        out_shape=[jax.ShapeDtypeStruct((N_EXPERTS, t), F32), jax.ShapeDtypeStruct((N_EXPERTS, t), F32)],
        compiler_params=pltpu.CompilerParams(dimension_semantics=("arbitrary",), vmem_limit_bytes=VMEM_LIMIT),
        name="router",
    )(x, rw_t, rb_col)


ROW_TILE = 288
TILE_PITCH = ROW_TILE + 1
GATE_LANES = -(-ROW_TILE // LANES) * LANES
SPARE_TOKENS = 8
FLAG_FIRST, FLAG_LAST, FLAG_NEW_EXPERT, FLAG_VALID, FLAG_HAS_NEXT = 1, 2, 4, 8, 16
SCATTER_BATCH = 8


PREV, CUR, NEXT = 0, 1, 2


def _moe_routed_kernel(ce_ref, flags_ref, used_ref, nexte_ref, idx_ref, where_ref, gseg_ref, x_ref, wg_ref, wu_ref,
                       wd_ref, sg_ref, su_ref, sd_ref, lng_ref, lnb_ref, o_ref, wgu_ref, wdb_ref,
                       gat0_ref, gat1_ref, res0_ref, res1_ref, wsg_ref, wsu_ref, wsd_ref, wsem_ref, *,
                       layer, alpha, chunk_tokens, sub_tokens):
    del used_ref
    step = pl.program_id(0)
    flags = flags_ref[step]
    odd = (step & 1) == 1
    nchunk = D_MODEL // LANES
    ne, ff = wg_ref.shape[1], wg_ref.shape[3]

    def slab_row(which, r):
        return pl.multiple_of(idx_ref[0, which, r], 8)

    def gather_row(which, gat_ref, r):
        gat_ref[pl.ds(r, nchunk, stride=TILE_PITCH), :] = x_ref[pl.ds(slab_row(which, r), 8), :]

    def scatter_rows(which, res_ref, rows):
        dst = [slab_row(which, r) for r in rows]
        acc = [o_ref[pl.ds(d, 8), :] + res_ref[pl.ds(r, nchunk, stride=TILE_PITCH), :] for d, r in zip(dst, rows)]
        for d, a in zip(dst, acc):
            o_ref[pl.ds(d, 8), :] = a

    def expert_mlp(gat_ref, res_ref):
        lhs = jnp.concatenate([gat_ref[j * TILE_PITCH:j * TILE_PITCH + ROW_TILE, :] for j in range(nchunk)],
                              axis=1).astype(BF16)
        h = _dot(lhs, wgu_ref[...])
        def row_to_cols(row):
            return jnp.broadcast_to(row, (LANES, GATE_LANES)).T[0:ROW_TILE]

        group_col, pos_col = row_to_cols(where_ref[0, 0:1, :]), row_to_cols(where_ref[0, 1:2, :])
        gl = gseg_ref.shape[3]
        pick_group = (lax.broadcasted_iota(jnp.int32, (ROW_TILE, gl), 1).astype(F32)
                      == jnp.concatenate([group_col] * (gl // LANES), axis=1)).astype(BF16)
        group_gates = sum(_dot_nt(pick_group, gseg_ref[0, part]) for part in range(gseg_ref.shape[1]))
        pick_pos = lax.broadcasted_iota(jnp.int32, (ROW_TILE, PLAN_GROUP), 1).astype(F32) == pos_col[:, 0:PLAN_GROUP]
        gate = jnp.sum(jnp.where(pick_pos, group_gates, 0.0), axis=1, keepdims=True)
        hid = jax.nn.silu(h[:, 0:ff]) * h[:, ff:2 * ff] * gate
        y = _dot(hid.astype(BF16), wdb_ref[...])
        for j in range(nchunk):
            res_ref[j * TILE_PITCH:j * TILE_PITCH + ROW_TILE, :] = y[:, j * LANES:(j + 1) * LANES]

    def by_parity(fn):
        @pl.when(jnp.logical_not(odd))
        def _even():
            fn(gat0_ref, gat1_ref, res0_ref, res1_ref)

        @pl.when(odd)
        def _odd():
            fn(gat1_ref, gat0_ref, res1_ref, res0_ref)

    @pl.when((flags & FLAG_FIRST) != 0)
    def _start_chunk():
        o_ref[...] = jnp.zeros(o_ref.shape, F32)

        def start(gat_cur, gat_other, res_cur, res_other):
            res_other[...] = jnp.zeros(res_other.shape, F32)

            def body(r, carry):
                gather_row(CUR, gat_cur, r)
                return carry

            lax.fori_loop(0, ROW_TILE, body, 0)

        by_parity(start)

    def weight_copies(e):
        return [pltpu.make_async_copy(src.at[layer, e], dst, wsem_ref.at[k])
                for k, (src, dst) in enumerate(((wg_ref, wsg_ref), (wu_ref, wsu_ref), (wd_ref, wsd_ref)))]

    @pl.when(step == 0)
    def _first_fetch():
        for cp in weight_copies(ce_ref[0] % ne):
            cp.start()

    @pl.when((flags & FLAG_NEW_EXPERT) != 0)
    def _next_expert():
        for cp in weight_copies(0):
            cp.wait()
        wgu_ref[:, 0:ff] = wsg_ref[...].astype(BF16)
        wgu_ref[:, ff:2 * ff] = wsu_ref[...].astype(BF16)
        wdb_ref[...] = wsd_ref[...].astype(BF16)

        @pl.when((flags & FLAG_HAS_NEXT) != 0)
        def _prefetch():
            for cp in weight_copies(nexte_ref[step]):
                cp.start()

    @pl.when((flags & FLAG_VALID) != 0)
    def _tile():
        def main(gat_cur, gat_other, res_cur, res_other):
            for r0 in range(0, ROW_TILE, SCATTER_BATCH):
                scatter_rows(PREV, res_other, range(r0, r0 + SCATTER_BATCH))
            for r in range(ROW_TILE):
                gather_row(NEXT, gat_other, r)
            expert_mlp(gat_cur, res_cur)

        by_parity(main)

    @pl.when((flags & FLAG_LAST) != 0)
    def _finish():
        def flush(gat_cur, gat_other, res_cur, res_other):
            def body(r, carry):
                scatter_rows(CUR, res_cur, [r])
                return carry

            lax.fori_loop(0, ROW_TILE, body, 0)

        by_parity(flush)

        def body(s, carry):
            base = pl.multiple_of(s * (sub_tokens * 8), 8)

            def rows_2d(ref):
                return jnp.concatenate([ref[pl.ds(base + j, sub_tokens, stride=8), :] for j in range(nchunk)],
                                       axis=1)

            x2 = rows_2d(x_ref)
            xb = x2.astype(BF16)
            hs = jax.nn.silu(_dot(xb, sg_ref[...])) * _dot(xb, su_ref[...])
            ffn = rows_2d(o_ref) + _dot(hs.astype(BF16), sd_ref[...])
            res = _layer_norm(alpha * x2 + ffn, lng_ref[...], lnb_ref[...])
            for j in range(nchunk):
                o_ref[pl.ds(base + j, sub_tokens, stride=8), :] = res[:, j * LANES:(j + 1) * LANES]
            return carry

        lax.fori_loop(0, chunk_tokens // sub_tokens, body, 0)


def _moe_routed(tile_ce, tile_flags, n_used, next_expert, row_idx, row_where, seg_gates, x_tm, wg, wu, wd, sg, su, sd, ln_g,
                ln_b, *, layer, alpha, chunk_tokens, sub_tokens):
    n_chunks, chunk_rows, _ = x_tm.shape
    n_tiles = row_idx.shape[0]
    _, ne, d, ff = wg.shape
    kernel = functools.partial(_moe_routed_kernel, layer=layer, alpha=alpha, chunk_tokens=chunk_tokens,
                               sub_tokens=sub_tokens)

    def tile_map(i, ce, fl, used, nxt):
        return (jnp.minimum(i, used[0] - 1), 0, 0)

    def chunk_map(i, ce, fl, used, nxt):
        return (ce[i] // ne, 0, 0)

    def const2(i, ce, fl, used, nxt):
        return (0, 0)

    grid_spec = pltpu.PrefetchScalarGridSpec(
        num_scalar_prefetch=4,
        grid=(n_tiles,),
        in_specs=[
            pl.BlockSpec((1, 3, ROW_TILE), tile_map, memory_space=pltpu.SMEM),
            pl.BlockSpec((1, 2, GATE_LANES), tile_map),
            pl.BlockSpec((1,) + seg_gates.shape[1:], lambda i, ce, fl, used, nxt: (ce[i], 0, 0, 0)),
            pl.BlockSpec((None, chunk_rows, LANES), chunk_map, pipeline_mode=pl.Buffered(1)),
            pl.BlockSpec(memory_space=pl.ANY),
            pl.BlockSpec(memory_space=pl.ANY),
            pl.BlockSpec(memory_space=pl.ANY),
            pl.BlockSpec(sg.shape, const2, pipeline_mode=pl.Buffered(1)),
            pl.BlockSpec(su.shape, const2, pipeline_mode=pl.Buffered(1)),
            pl.BlockSpec(sd.shape, const2, pipeline_mode=pl.Buffered(1)),
            pl.BlockSpec(ln_g.shape, const2),
            pl.BlockSpec(ln_b.shape, const2),
        ],
        out_specs=pl.BlockSpec((None, chunk_rows, LANES), chunk_map, pipeline_mode=pl.Buffered(1)),
        scratch_shapes=[
            pltpu.VMEM((d, 2 * ff), BF16),
            pltpu.VMEM((ff, d), BF16),
            pltpu.VMEM((8 * TILE_PITCH, LANES), F32),
            pltpu.VMEM((8 * TILE_PITCH, LANES), F32),
            pltpu.VMEM((8 * TILE_PITCH, LANES), F32),
            pltpu.VMEM((8 * TILE_PITCH, LANES), F32),
            pltpu.VMEM((d, ff), F32),
            pltpu.VMEM((d, ff), F32),
            pltpu.VMEM((ff, d), F32),
            pltpu.SemaphoreType.DMA((3,)),
        ],
    )
    return pl.pallas_call(
        kernel,
        grid_spec=grid_spec,
        out_shape=jax.ShapeDtypeStruct(x_tm.shape, F32),
        compiler_params=pltpu.CompilerParams(dimension_semantics=("arbitrary",), vmem_limit_bytes=VMEM_LIMIT),
        name="moe_routed",
    )(tile_ce, tile_flags, n_used, next_expert, row_idx, row_where, seg_gates, x_tm, wg, wu, wd, sg, su, sd, ln_g, ln_b)


MOE_CHUNKS = 4
MOE_SUB_MAX = 512


def _moe_tiling(n_prompt, n_sample):
    assert n_prompt % (MOE_CHUNKS * 8) == 0 and n_sample % (MOE_CHUNKS * 8) == 0
    chunk_tokens = (n_prompt + n_sample) // MOE_CHUNKS
    sub_tokens = max(s for s in range(8, MOE_SUB_MAX + 1, 8) if chunk_tokens % s == 0)
    return MOE_CHUNKS, chunk_tokens, sub_tokens


PLAN_GROUP = 32


def _chunked_t(prompt, sample, n_chunks):
    ne = prompt.shape[0]
    return jnp.concatenate([prompt.reshape(ne, n_chunks, -1).transpose(1, 0, 2),
                            sample.reshape(ne, n_chunks, -1).transpose(1, 0, 2)], axis=2)


def _nth_set_bit(word, n):
    pos = jnp.zeros(word.shape, jnp.int32)
    for width in (16, 8, 4, 2, 1):
        low = word & jnp.uint32((1 << width) - 1)
        below = lax.population_count(low).astype(jnp.int32)
        upper = n >= below
        n = jnp.where(upper, n - below, n)
        word = jnp.where(upper, word >> jnp.uint32(width), low)
        pos = pos + jnp.where(upper, width, 0)
    return pos


def _route_plan(chosen_p, chosen_s, gate_p, gate_s, *, chunk_tokens, n_chunks):
    n_seg = n_chunks * N_EXPERTS
    n_group = chunk_tokens // PLAN_GROUP
    assert chunk_tokens % PLAN_GROUP == 0
    chosen = _chunked_t(chosen_p, chosen_s, n_chunks) > 0
    gate = _chunked_t(gate_p, gate_s, n_chunks)
    shifts = jnp.arange(PLAN_GROUP, dtype=jnp.uint32)
    words = jnp.sum(chosen.reshape(n_seg, n_group, PLAN_GROUP).astype(jnp.uint32) << shifts, axis=-1,
                    dtype=jnp.uint32)
    per_group = lax.population_count(words).astype(jnp.int32)
    group_start = jnp.cumsum(per_group, axis=-1) - per_group
    counts = jnp.sum(per_group, axis=-1)
    padded = -(-counts // ROW_TILE) * ROW_TILE
    seg_end = jnp.cumsum(padded)
    seg_start = seg_end - padded
    n_tiles = -(-(n_chunks * chunk_tokens * TOP_K + n_seg * (ROW_TILE - 1)) // ROW_TILE)
    row0 = jnp.arange(n_tiles, dtype=jnp.int32) * ROW_TILE
    seg = jnp.sum(seg_end[None, :] <= row0[:, None], axis=1, dtype=jnp.int32)
    valid = seg < n_seg
    tile_ce = jnp.minimum(seg, n_seg - 1)
    j = (row0 - seg_start[tile_ce])[:, None] + jnp.arange(ROW_TILE, dtype=jnp.int32)[None, :]
    real = valid[:, None] & (j < counts[tile_ce][:, None])
    starts = group_start[tile_ce][:, None, :]
    group = jnp.sum(starts <= j[:, :, None], axis=-1, dtype=jnp.int32) - 1
    in_group = jnp.arange(n_group, dtype=jnp.int32)[None, None, :] == group[:, :, None]
    start = jnp.sum(jnp.where(in_group, starts, 0), axis=-1, dtype=jnp.int32)
    word = jnp.sum(jnp.where(in_group, words[tile_ce][:, None, :], jnp.uint32(0)), axis=-1, dtype=jnp.uint32)
    pos = _nth_set_bit(word, j - start)
    token = jnp.where(real, group * PLAN_GROUP + pos, chunk_tokens)
    row_where = jnp.stack([jnp.where(real, group, -1), jnp.where(real, pos, 0)], axis=1).astype(F32)
    row_where = jnp.pad(row_where, ((0, 0), (0, 0), (0, GATE_LANES - ROW_TILE)), constant_values=-1.0)
    group_lanes = -(-n_group // LANES) * LANES
    seg_gates = jnp.pad(jnp.swapaxes(gate.reshape(n_seg, n_group, PLAN_GROUP), 1, 2),
                        ((0, 0), (0, 0), (0, group_lanes - n_group)))
    parts, rest = [], seg_gates
    for _ in range(3):
        top = lax.bitcast_convert_type(lax.bitcast_convert_type(rest, jnp.uint32) & jnp.uint32(0xFFFF0000), F32)
        parts.append(top.astype(BF16))
        rest = rest - top
    seg_gates = jnp.stack(parts, axis=1)
    tile_c = tile_ce // N_EXPERTS
    prev_ce = jnp.concatenate([jnp.full((1,), -N_EXPERTS, jnp.int32), tile_ce[:-1]])
    next_c = jnp.concatenate([tile_c[1:], jnp.full((1,), -1, jnp.int32)])
    next_valid = jnp.concatenate([valid[1:], jnp.zeros((1,), jnp.bool_)])
    first = valid & (tile_c != prev_ce // N_EXPERTS)
    last = valid & (~next_valid | (next_c != tile_c))
    new_expert = valid & (tile_ce != prev_ce)
    tile_id = jnp.arange(n_tiles, dtype=jnp.int32)
    change_at = jnp.where(new_expert, tile_id, n_tiles)
    next_change = jnp.concatenate([lax.cummin(change_at, reverse=True)[1:], jnp.full((1,), n_tiles, jnp.int32)])
    has_next = new_expert & (next_change < n_tiles)
    next_expert = tile_ce[jnp.minimum(next_change, n_tiles - 1)] % N_EXPERTS
    flags = (first * FLAG_FIRST + last * FLAG_LAST + new_expert * FLAG_NEW_EXPERT + valid * FLAG_VALID
             + has_next * FLAG_HAS_NEXT)
    row_idx = token * SLAB_ROWS
    row_idx3 = jnp.stack([jnp.concatenate([row_idx[:1], row_idx[:-1]]), row_idx,
                          jnp.concatenate([row_idx[1:], row_idx[-1:]])], axis=1)
    n_used = jnp.sum(valid, dtype=jnp.int32).reshape(1)
    return tile_ce, flags.astype(jnp.int32), n_used, next_expert.astype(jnp.int32), row_idx3, row_where, seg_gates


def _t5_bucket(dist):
    n = jnp.maximum(dist, 0)
    max_exact = N_BUCKETS // 2
    large = max_exact + (jnp.log(jnp.maximum(n, 1).astype(F32) / max_exact)
                         / math.log(MAX_DISTANCE / max_exact) * (N_BUCKETS - max_exact)).astype(jnp.int32)
    large = jnp.minimum(large, N_BUCKETS - 1)
    return jnp.where(n < max_exact, n, large)


def _bias_lookup(rel_bias, bucket):
    onehot = (bucket[..., None] == jnp.arange(N_BUCKETS, dtype=bucket.dtype)).astype(F32)
    return jnp.einsum("...b,bh->h...", onehot, rel_bias.astype(F32), precision=lax.Precision.HIGHEST)


def _bias_tables(rel_bias, win):
    qi = jnp.arange(ATTN_BLOCK)[:, None]
    ki = jnp.arange(2 * ATTN_BLOCK)[None, :]
    dist = qi + ATTN_BLOCK - ki
    valid = (dist >= 0) & (dist < WINDOW)
    bias = _bias_lookup(rel_bias, _t5_bucket(dist))
    bias = jnp.where(valid[None], bias, NEG).reshape(N_KV_HEADS, GROUP, ATTN_BLOCK, 2 * ATTN_BLOCK)
    bias_ab = jnp.stack([jnp.concatenate([bias[:, t], bias[:, t + 2]], axis=1) for t in range(2)], axis=1)
    dist_s = (win - 1) - jnp.arange(win)
    bias_s = _bias_lookup(rel_bias, _t5_bucket(dist_s))
    return bias_ab, bias_s


def _sink_tables(sink):
    s = sink.astype(F32).reshape(N_KV_HEADS, GROUP)
    rows = [jnp.concatenate([jnp.broadcast_to(s[:, t, None], (N_KV_HEADS, ATTN_BLOCK)),
                             jnp.broadcast_to(s[:, t + 2, None], (N_KV_HEADS, ATTN_BLOCK))], axis=1)
            for t in range(2)]
    return jnp.stack(rows, axis=1)[..., None], sink.astype(F32)[:, None]


def kernel(x_prompt, x_sample, cache_k_win, cache_v_win, state_conv, rel_bias, w_in, attn_sink, conv_w,
           w_attn_out, w_conv_out, w_out, ln1_g, ln1_b, router_w, router_bias, exp_w_gate, exp_w_up,
           exp_w_down, shared_w_gate, shared_w_up, shared_w_down, ln2_g, ln2_b):
    depth = w_in.shape[0]
    alpha = (2 * depth) ** 0.25
    nb, seq, d = x_prompt.shape
    nd = x_sample.shape[0]
    win = cache_k_win.shape[2]
    assert x_sample.shape[1] == 1 and win == WINDOW and seq % 512 == 0

    n_prompt = nb * seq
    n_chunks, chunk_tokens, sub_tokens = _moe_tiling(n_prompt, nd)
    prompt_rows = n_prompt // n_chunks * SLAB_ROWS
    slab_shape = (n_chunks, (chunk_tokens + SPARE_TOKENS) * SLAB_ROWS, LANES)

    bias_ab, bias_s = _bias_tables(rel_bias, win)
    hmask = (jnp.arange(KV_DIM)[None, :] // HEAD_DIM == jnp.arange(N_HEADS)[:, None] // GROUP).astype(F32)

    yp = x_prompt
    ys = x_sample.reshape(nd, d)
    outs = [[] for _ in range(6)]
    for l in range(depth):
        w_in_b = w_in[l].astype(BF16)
        w_ao_b = w_attn_out[l].astype(BF16)
        w_co_b = w_conv_out[l].astype(BF16)
        w_o_b = w_out[l].astype(BF16)
        g1, b1 = ln1_g[l][None, :], ln1_b[l][None, :]
        g2, b2 = ln2_g[l][None, :], ln2_b[l][None, :]
        sink_ab, sink_col = _sink_tables(attn_sink[l])

        slab, kp, vp, cp = _mixer_prompt(yp, w_in_b, w_attn_out[l].T.astype(BF16), w_co_b, w_o_b, conv_w[l],
                                         jnp.swapaxes(bias_ab, -1, -2), jnp.swapaxes(sink_ab, -1, -2), g1, b1,
                                         alpha=alpha, tq=512, batch=nb, seq=seq, slab_shape=slab_shape,
                                         base=jnp.zeros(slab_shape, F32) if l == 0 else None)

        proj = _sample_proj(ys, w_in_b, tn=IN_DIM // 4)
        q4 = proj[:, :Q_DIM].reshape(nd, N_KV_HEADS, GROUP, HEAD_DIM).transpose(0, 2, 1, 3).reshape(nd, GROUP, KV_DIM)
        ksn, vsn, ag = _sample_attn(q4, proj[:, OFF_K:OFF_V], proj[:, OFF_V:OFF_B],
                                    cache_k_win.reshape(depth, nd, win, KV_DIM),
                                    cache_v_win.reshape(depth, nd, win, KV_DIM),
                                    bias_s, sink_col, hmask, layer=l, bt=16)
        att = ag.reshape(nd, GROUP, N_KV_HEADS, HEAD_DIM).transpose(0, 2, 1, 3).reshape(nd, Q_DIM)
        state_t = jnp.swapaxes(state_conv[l], 0, 1)
        ys, us = _sample_post(ys, att, proj, state_t, conv_w[l], w_ao_b, w_co_b, w_o_b, g1, b1, alpha=alpha)

        outs[0].append(kp.reshape(nb, WINDOW, N_KV_HEADS, HEAD_DIM))
        outs[1].append(vp.reshape(nb, WINDOW, N_KV_HEADS, HEAD_DIM))
        outs[2].append(cp)
        outs[3].append(ksn.reshape(nd, win, N_KV_HEADS, HEAD_DIM))
        outs[4].append(vsn.reshape(nd, win, N_KV_HEADS, HEAD_DIM))
        outs[5].append(jnp.concatenate([state_conv[l][:, 1:], us[:, None, :]], axis=1))

        rw_t = router_w[l].T.astype(BF16)
        rb_col = router_bias[l].astype(F32)[:, None]
        sg, su, sd = (shared_w_gate[l].astype(BF16), shared_w_up[l].astype(BF16), shared_w_down[l].astype(BF16))
        tail = jnp.concatenate([ys.reshape(n_chunks, -1, LANES),
                                jnp.zeros((n_chunks, SPARE_TOKENS * SLAB_ROWS, LANES), F32)], axis=1)
        slab = lax.dynamic_update_slice(slab, tail, (0, prompt_rows, 0))
        ep, wp = _router(slab, rw_t, rb_col, tm=512, tokens=n_prompt)
        es, ws = _router(ys, rw_t, rb_col, tm=nd)
        plan = _route_plan(ep, es, wp, ws, chunk_tokens=chunk_tokens, n_chunks=n_chunks)
        yp = _moe_routed(*plan, slab, exp_w_gate, exp_w_up, exp_w_down, sg, su, sd, g2, b2,
                         layer=l, alpha=alpha, chunk_tokens=chunk_tokens, sub_tokens=sub_tokens)
        ys = yp[:, prompt_rows:chunk_tokens * SLAB_ROWS].reshape(nd, d)

    y_prompt = yp[:, :prompt_rows].reshape(nb, seq, d)
    return (y_prompt, ys.reshape(nd, 1, d)) + tuple(jnp.stack(o) for o in outs)
```

```python
import functools
import math

import jax
import jax.numpy as jnp
from jax import lax
from jax.experimental import pallas as pl
from jax.experimental.pallas import tpu as pltpu

D_MODEL = 1024
N_HEADS = 16
N_KV_HEADS = 4
HEAD_DIM = 64
GROUP = N_HEADS // N_KV_HEADS
WINDOW = 128
ATTN_BLOCK = 128
N_BUCKETS = 32
MAX_DISTANCE = 128
CONV_DIM = 1024
CONV_K = 3
N_EXPERTS = 64
TOP_K = 8
N_GROUPS = 8
TOPK_GROUPS = 4
GROUP_SIZE = N_EXPERTS // N_GROUPS
EXPERT_FF = 256
ROUTED_SCALE = 2.5
LN_EPS = 1e-5
NEG = -1e30

Q_DIM = N_HEADS * HEAD_DIM
KV_DIM = N_KV_HEADS * HEAD_DIM
OFF_K = Q_DIM
OFF_V = OFF_K + KV_DIM
OFF_B = OFF_V + KV_DIM
OFF_C = OFF_B + CONV_DIM
OFF_H = OFF_C + CONV_DIM
OFF_GA = OFF_H + CONV_DIM
OFF_GB = OFF_GA + D_MODEL
IN_DIM = OFF_GB + D_MODEL

LANES = 128
CONV_PAD = 8
VMEM_LIMIT = 60 * 1024 * 1024

BF16 = jnp.bfloat16
F32 = jnp.float32


def _dot(a, b):
    return jnp.dot(a, b, preferred_element_type=F32)


def _dot_nt(a, b):
    return lax.dot_general(a, b, (((1,), (1,)), ((), ())), preferred_element_type=F32)


def _layer_norm(z, g, b):
    mu = jnp.mean(z, axis=-1, keepdims=True)
    d = z - mu
    var = jnp.mean(d * d, axis=-1, keepdims=True)
    return d * lax.rsqrt(var + LN_EPS) * g + b


def _sink_softmax(s, sink, axis=-1):
    m = jnp.maximum(jnp.max(s, axis=axis, keepdims=True), sink)
    e = jnp.exp(s - m)
    den = jnp.sum(e, axis=axis, keepdims=True) + jnp.exp(sink - m)
    return e * (1.0 / den)


SLAB_ROWS = D_MODEL // LANES


def _slab_load(ref, tokens, base=0):
    return jnp.concatenate([ref[pl.ds(base + j, tokens, stride=SLAB_ROWS), :] for j in range(SLAB_ROWS)], axis=1)


def _slab_store(ref, val, base=0):
    tokens = val.shape[0]
    for j in range(SLAB_ROWS):
        ref[pl.ds(base + j, tokens, stride=SLAB_ROWS), :] = val[:, j * LANES:(j + 1) * LANES]


def _merge_project(x, attn_o, y_conv, g_a, g_b, w_co_ref, w_o_ref, lng_ref, lnb_ref, alpha):
    merged = jax.nn.sigmoid(g_a) * attn_o + jax.nn.sigmoid(g_b) * _dot(y_conv.astype(BF16), w_co_ref[...])
    out = _dot(merged.astype(BF16), w_o_ref[...])
    return _layer_norm(alpha * x + out, lng_ref[...], lnb_ref[...])


def _mixer_prompt_kernel(x_ref, w_in_ref, w_ao_ref, w_co_ref, w_o_ref, convw_ref, bias_ref, sink_ref,
                         lng_ref, lnb_ref, base_ref,
                         x1_ref, kwin_ref, vwin_ref, conv_ref,
                         ka_ref, kb_ref, vat_ref, vbt_ref, s_ref, p_ref, att_ref, ubuf_ref, *, alpha, tq, slab_in):
    del base_ref
    i = pl.program_id(1)
    nblk = tq // ATTN_BLOCK
    half = LANES // 2
    scale = HEAD_DIM ** -0.5
    assert math.frexp(scale)[0] == 0.5

    @pl.when(i == 0)
    def _init():
        for ref in (ka_ref, kb_ref):
            ref[:, 0:ATTN_BLOCK, :] = jnp.zeros((N_KV_HEADS, ATTN_BLOCK, LANES), BF16)
        for ref in (vat_ref, vbt_ref):
            ref[:, :, 0:ATTN_BLOCK] = jnp.zeros((N_KV_HEADS, LANES, ATTN_BLOCK), BF16)
        ubuf_ref[0:CONV_PAD, :] = jnp.zeros((CONV_PAD, CONV_DIM), F32)

    x = _slab_load(x_ref, tq) if slab_in else x_ref[0]
    xb = x.astype(BF16)
    qkv = _dot(xb, w_in_ref[:, 0:OFF_B])

    lo = lax.broadcasted_iota(jnp.int32, (tq, LANES), 1) < half
    zeros_t = jnp.zeros((half, tq), BF16)
    for c in range(N_KV_HEADS // 2):
        chunk = qkv[:, OFF_K + c * LANES: OFF_K + (c + 1) * LANES]
        c_lo = jnp.where(lo, chunk, 0.0)
        c_hi = jnp.where(lo, 0.0, chunk)
        ka_ref[2 * c, ATTN_BLOCK:, :] = c_lo.astype(BF16)
        kb_ref[2 * c, ATTN_BLOCK:, :] = pltpu.roll(c_lo, half, 1).astype(BF16)
        kb_ref[2 * c + 1, ATTN_BLOCK:, :] = c_hi.astype(BF16)
        ka_ref[2 * c + 1, ATTN_BLOCK:, :] = pltpu.roll(c_hi, half, 1).astype(BF16)
        vt = qkv[:, OFF_V + c * LANES: OFF_V + (c + 1) * LANES].T.astype(BF16)
        vat_ref[2 * c, :, ATTN_BLOCK:] = jnp.concatenate([vt[0:half], zeros_t], axis=0)
        vbt_ref[2 * c, :, ATTN_BLOCK:] = jnp.concatenate([zeros_t, vt[0:half]], axis=0)
        vbt_ref[2 * c + 1, :, ATTN_BLOCK:] = jnp.concatenate([zeros_t, vt[half:]], axis=0)
        vat_ref[2 * c + 1, :, ATTN_BLOCK:] = jnp.concatenate([vt[half:], zeros_t], axis=0)

    key_row = lax.broadcasted_iota(jnp.int32, (2 * ATTN_BLOCK, 2 * ATTN_BLOCK), 0)
    for j in range(nblk):
        rows = slice(j * ATTN_BLOCK, (j + 1) * ATTN_BLOCK)
        keys = slice(j * ATTN_BLOCK, (j + 2) * ATTN_BLOCK)
        for h in range(N_KV_HEADS):
            q0 = h * GROUP * HEAD_DIM
            q2 = (jnp.concatenate([qkv[rows, q0:q0 + LANES], qkv[rows, q0 + LANES:q0 + 2 * LANES]], axis=0)
                  * scale).astype(BF16)
            for t, k_ref in enumerate((ka_ref, kb_ref)):
                s = _dot_nt(k_ref[h, keys, :], q2)
                if j == 0:
                    s = jnp.where(jnp.logical_and(i == 0, key_row < ATTN_BLOCK), NEG, s)
                s_ref[j, h, t] = s
    for h in range(N_KV_HEADS):
        for t in range(2):
            s = s_ref[:, h, t] + bias_ref[h, t][None]
            p_ref[:, h, t] = _sink_softmax(s, sink_ref[h, t][None], axis=1).astype(BF16)
    for j in range(nblk):
        rows = slice(j * ATTN_BLOCK, (j + 1) * ATTN_BLOCK)
        keys = slice(j * ATTN_BLOCK, (j + 2) * ATTN_BLOCK)
        for h in range(N_KV_HEADS):
            q0 = h * GROUP * HEAD_DIM
            o_t = _dot(vat_ref[h, :, keys], p_ref[j, h, 0]) + _dot(vbt_ref[h, :, keys], p_ref[j, h, 1])
            att_ref[q0:q0 + LANES, rows] = o_t[:, 0:ATTN_BLOCK].astype(BF16)
            att_ref[q0 + LANES:q0 + 2 * LANES, rows] = o_t[:, ATTN_BLOCK:].astype(BF16)
    for ref in (ka_ref, kb_ref):
        ref[:, 0:ATTN_BLOCK, :] = ref[:, tq:tq + ATTN_BLOCK, :]
    for ref in (vat_ref, vbt_ref):
        ref[:, :, 0:ATTN_BLOCK] = ref[:, :, tq:tq + ATTN_BLOCK]

    kwin_ref[0] = qkv[tq - WINDOW:tq, OFF_K:OFF_V]
    vwin_ref[0] = qkv[tq - WINDOW:tq, OFF_V:OFF_B]

    attn_o = _dot(w_ao_ref[...], att_ref[...]).T

    bch = _dot(xb, w_in_ref[:, OFF_B:OFF_GA])
    u = bch[:, CONV_DIM:2 * CONV_DIM] * bch[:, 2 * CONV_DIM:3 * CONV_DIM]
    ubuf_ref[CONV_PAD:CONV_PAD + tq, :] = u
    cw = convw_ref[...]
    y = (cw[0:1] * ubuf_ref[CONV_PAD - 2:CONV_PAD - 2 + tq, :]
         + cw[1:2] * ubuf_ref[CONV_PAD - 1:CONV_PAD - 1 + tq, :]
         + cw[2:3] * u)
    conv_ref[0] = ubuf_ref[CONV_PAD + tq - (CONV_K - 1):CONV_PAD + tq, :]
    ubuf_ref[0:CONV_PAD, :] = ubuf_ref[tq:tq + CONV_PAD, :]
    y_conv = bch[:, 0:CONV_DIM] * y

    gab = _dot(xb, w_in_ref[:, OFF_GA:IN_DIM])
    _slab_store(x1_ref, _merge_project(x, attn_o, y_conv, gab[:, 0:D_MODEL], gab[:, D_MODEL:], w_co_ref, w_o_ref,
                                       lng_ref, lnb_ref, alpha))


def _const_spec(shape):
    nd = len(shape)
    return pl.BlockSpec(shape, lambda *_: (0,) * nd, pipeline_mode=pl.Buffered(1))


def _mixer_prompt(x, w_in, w_ao, w_co, w_o, conv_w, bias_ab, sink_ab, ln_g, ln_b, *, alpha, tq, batch, seq,
                  slab_shape, base=None):
    b, s = batch, seq
    n_chunks, chunk_rows, _ = slab_shape
    slab_in = x.ndim == 3 and x.shape == slab_shape
    steps_per_chunk = (b * s // n_chunks) // tq
    assert (b * s) % (n_chunks * tq) == 0 and s % tq == 0

    def slab_map(bi, i):
        g = bi * (s // tq) + i
        return (g // steps_per_chunk, g % steps_per_chunk, 0)

    slab_spec = pl.BlockSpec((None, tq * SLAB_ROWS, LANES), slab_map)
    kernel = functools.partial(_mixer_prompt_kernel, alpha=alpha, tq=tq, slab_in=slab_in)
    return pl.pallas_call(
        kernel,
        grid=(b, s // tq),
        in_specs=[
            slab_spec if slab_in else pl.BlockSpec((1, tq, D_MODEL), lambda bi, i: (bi, i, 0)),
            _const_spec(w_in.shape), _const_spec(w_ao.shape), _const_spec(w_co.shape), _const_spec(w_o.shape),
            _const_spec(conv_w.shape), _const_spec(bias_ab.shape), _const_spec(sink_ab.shape),
            _const_spec(ln_g.shape), _const_spec(ln_b.shape),
            pl.BlockSpec(memory_space=pl.ANY),
        ],
        out_specs=[
            slab_spec,
            pl.BlockSpec((1, WINDOW, KV_DIM), lambda bi, i: (bi, 0, 0)),
            pl.BlockSpec((1, WINDOW, KV_DIM), lambda bi, i: (bi, 0, 0)),
            pl.BlockSpec((1, CONV_K - 1, CONV_DIM), lambda bi, i: (bi, 0, 0)),
        ],
        out_shape=[
            jax.ShapeDtypeStruct(slab_shape, F32),
            jax.ShapeDtypeStruct((b, WINDOW, KV_DIM), F32),
            jax.ShapeDtypeStruct((b, WINDOW, KV_DIM), F32),
            jax.ShapeDtypeStruct((b, CONV_K - 1, CONV_DIM), F32),
        ],
        scratch_shapes=[
            pltpu.VMEM((N_KV_HEADS, ATTN_BLOCK + tq, LANES), BF16),
            pltpu.VMEM((N_KV_HEADS, ATTN_BLOCK + tq, LANES), BF16),
            pltpu.VMEM((N_KV_HEADS, LANES, ATTN_BLOCK + tq), BF16),
            pltpu.VMEM((N_KV_HEADS, LANES, ATTN_BLOCK + tq), BF16),
            pltpu.VMEM((tq // ATTN_BLOCK, N_KV_HEADS, 2, 2 * ATTN_BLOCK, 2 * ATTN_BLOCK), F32),
            pltpu.VMEM((tq // ATTN_BLOCK, N_KV_HEADS, 2, 2 * ATTN_BLOCK, 2 * ATTN_BLOCK), BF16),
            pltpu.VMEM((Q_DIM, tq), BF16),
            pltpu.VMEM((tq + CONV_PAD, CONV_DIM), F32),
        ],
        compiler_params=pltpu.CompilerParams(
            dimension_semantics=("arbitrary", "arbitrary"), vmem_limit_bytes=VMEM_LIMIT),
        input_output_aliases={0 if slab_in else 10: 0},
        name="mixer_prompt",
    )(x, w_in, w_ao, w_co, w_o, conv_w, bias_ab, sink_ab, ln_g, ln_b,
      jnp.zeros((1, SLAB_ROWS, LANES), F32) if slab_in else base)


def _proj_kernel(x_ref, w_ref, o_ref):
    o_ref[...] = _dot(x_ref[...].astype(BF16), w_ref[...])


def _sample_proj(x, w_in, *, tn):
    m, d = x.shape
    n = w_in.shape[1]
    return pl.pallas_call(
        _proj_kernel,
        grid=(n // tn,),
        in_specs=[pl.BlockSpec((m, d), lambda j: (0, 0)), pl.BlockSpec((d, tn), lambda j: (0, j))],
        out_specs=pl.BlockSpec((m, tn), lambda j: (0, j)),
        out_shape=jax.ShapeDtypeStruct((m, n), F32),
        compiler_params=pltpu.CompilerParams(dimension_semantics=("arbitrary",), vmem_limit_bytes=VMEM_LIMIT),
        name="sample_proj",
    )(x, w_in)


def _sample_attn_kernel(q4_ref, knew_ref, vnew_ref, ck_ref, cv_ref, bias_ref, sink_ref, hmask_ref,
                        nk_ref, nv_ref, ag_ref):
    bt = ck_ref.shape[0]
    win = ck_ref.shape[1]
    scale = HEAD_DIM ** -0.5
    row = lax.broadcasted_iota(jnp.int32, (win, KV_DIM), 0)
    last = row == win - 1
    hmask = hmask_ref[...]
    for b in range(bt):
        kb = jnp.where(last, knew_ref[b:b + 1, :], pltpu.roll(ck_ref[b], win - 1, 0))
        vb = jnp.where(last, vnew_ref[b:b + 1, :], pltpu.roll(cv_ref[b], win - 1, 0))
        nk_ref[b] = kb
        nv_ref[b] = vb
        q4 = q4_ref[b]
        qm = (jnp.concatenate([q4] * N_KV_HEADS, axis=0) * hmask).astype(BF16)
        s = _dot_nt(qm, kb.astype(BF16)) * scale + bias_ref[...]
        p = _sink_softmax(s, sink_ref[...]).astype(BF16)
        o = _dot(p, vb.astype(BF16)) * hmask
        o4 = o[0:GROUP]
        for h in range(1, N_KV_HEADS):
            o4 = o4 + o[h * GROUP:(h + 1) * GROUP]
        ag_ref[b] = o4


def _sample_attn(q4, k_new, v_new, cache_k, cache_v, bias_s, sink_col, hmask, *, layer, bt):
    _, nb, win, kvd = cache_k.shape
    return pl.pallas_call(
        _sample_attn_kernel,
        grid=(nb // bt,),
        in_specs=[
            pl.BlockSpec((bt, GROUP, kvd), lambda i: (i, 0, 0)),
            pl.BlockSpec((bt, kvd), lambda i: (i, 0)),
            pl.BlockSpec((bt, kvd), lambda i: (i, 0)),
            pl.BlockSpec((None, bt, win, kvd), lambda i: (layer, i, 0, 0)),
            pl.BlockSpec((None, bt, win, kvd), lambda i: (layer, i, 0, 0)),
            pl.BlockSpec(bias_s.shape, lambda i: (0, 0)),
            pl.BlockSpec(sink_col.shape, lambda i: (0, 0)),
            pl.BlockSpec(hmask.shape, lambda i: (0, 0)),
        ],
        out_specs=[
            pl.BlockSpec((bt, win, kvd), lambda i: (i, 0, 0)),
            pl.BlockSpec((bt, win, kvd), lambda i: (i, 0, 0)),
            pl.BlockSpec((bt, GROUP, kvd), lambda i: (i, 0, 0)),
        ],
        out_shape=[
            jax.ShapeDtypeStruct((nb, win, kvd), F32),
            jax.ShapeDtypeStruct((nb, win, kvd), F32),
            jax.ShapeDtypeStruct((nb, GROUP, kvd), F32),
        ],
        compiler_params=pltpu.CompilerParams(dimension_semantics=("arbitrary",), vmem_limit_bytes=VMEM_LIMIT),
        name="sample_attn",
    )(q4, k_new, v_new, cache_k, cache_v, bias_s, sink_col, hmask)


def _sample_post_kernel(x_ref, att_ref, proj_ref, st_ref, convw_ref, w_ao_ref, w_co_ref, w_o_ref,
                        lng_ref, lnb_ref, x1_ref, u_ref, *, alpha):
    attn_o = _dot(att_ref[...].astype(BF16), w_ao_ref[...])
    u = proj_ref[:, OFF_C:OFF_H] * proj_ref[:, OFF_H:OFF_GA]
    cw = convw_ref[...]
    y = cw[0:1] * st_ref[0] + cw[1:2] * st_ref[1] + cw[2:3] * u
    u_ref[...] = u
    y_conv = proj_ref[:, OFF_B:OFF_C] * y
    x1_ref[...] = _merge_project(x_ref[...], attn_o, y_conv, proj_ref[:, OFF_GA:OFF_GB], proj_ref[:, OFF_GB:IN_DIM],
                                 w_co_ref, w_o_ref, lng_ref, lnb_ref, alpha)


def _sample_post(x, att, proj, state, conv_w, w_ao, w_co, w_o, ln_g, ln_b, *, alpha):
    m, d = x.shape
    kernel = functools.partial(_sample_post_kernel, alpha=alpha)
    return pl.pallas_call(
        kernel,
        out_shape=[jax.ShapeDtypeStruct((m, d), F32), jax.ShapeDtypeStruct((m, CONV_DIM), F32)],
        compiler_params=pltpu.CompilerParams(vmem_limit_bytes=VMEM_LIMIT),
        name="sample_post",
    )(x, att, proj, state, conv_w, w_ao, w_co, w_o, ln_g, ln_b)


def _first_max(cur, ids, axes, big):
    m = cur
    for ax in axes:
        m = jnp.max(m, axis=ax, keepdims=True)
    idx = jnp.where(cur == m, ids, big)
    for ax in axes:
        idx = jnp.min(idx, axis=ax, keepdims=True)
    return m, idx


def _router_kernel(x_ref, rwt_ref, rb_ref, chosen_ref, gate_ref, *, tm, slab_in):
    x = _slab_load(x_ref, tm) if slab_in else x_ref[...]
    logits_t = _dot_nt(rwt_ref[...], x.astype(BF16))
    scores = jax.nn.sigmoid(logits_t)
    sel = scores + rb_ref[...]
    shape3 = (N_GROUPS, GROUP_SIZE, tm)
    sel3 = sel.reshape(shape3)
    scores3 = scores.reshape(shape3)
    member = lax.broadcasted_iota(jnp.int32, shape3, 1)
    m1, i1 = _first_max(sel3, member, (1,), GROUP_SIZE)
    m2 = jnp.max(jnp.where(member == i1, -jnp.inf, sel3), axis=1, keepdims=True)
    gscore = m1 + m2
    gid = lax.broadcasted_iota(jnp.int32, gscore.shape, 0)
    gsel = jnp.zeros(gscore.shape, jnp.bool_)
    for _ in range(TOPK_GROUPS):
        _, gi = _first_max(gscore, gid, (0,), N_GROUPS)
        hit = gid == gi
        gsel = jnp.logical_or(gsel, hit)
        gscore = jnp.where(hit, -jnp.inf, gscore)
    eid = lax.broadcasted_iota(jnp.int32, shape3, 0) * GROUP_SIZE + member
    cur = jnp.where(gsel, sel3, -jnp.inf)
    chosen = jnp.zeros(shape3, F32)
    for _ in range(TOP_K):
        _, ei = _first_max(cur, eid, (1, 0), N_EXPERTS)
        hit = eid == ei
        chosen = jnp.where(hit, 1.0, chosen)
        cur = jnp.where(hit, -jnp.inf, cur)
    w = scores3 * chosen
    tot = jnp.sum(jnp.sum(w, axis=1, keepdims=True), axis=0, keepdims=True)
    chosen_ref[...] = chosen.reshape(N_EXPERTS, tm)
    gate_ref[...] = (w / tot * ROUTED_SCALE).reshape(N_EXPERTS, tm)


def _router(x, rw_t, rb_col, *, tm, tokens=None):
    slab_in = x.ndim == 3
    if slab_in:
        t = tokens
        steps_per_chunk = (t // x.shape[0]) // tm
        assert t % (x.shape[0] * tm) == 0
        x_spec = pl.BlockSpec((None, tm * SLAB_ROWS, LANES), lambda i: (i // steps_per_chunk, i % steps_per_chunk, 0))
    else:
        t = x.shape[0]
        x_spec = pl.BlockSpec((tm, D_MODEL), lambda i: (i, 0))
    return pl.pallas_call(
        functools.partial(_router_kernel, tm=tm, slab_in=slab_in),
        grid=(t // tm,),
        in_specs=[
            x_spec,
            pl.BlockSpec(rw_t.shape, lambda i: (0, 0)),
            pl.BlockSpec(rb_col.shape, lambda i: (0, 0)),
        ],
        out_specs=[pl.BlockSpec((N_EXPERTS, tm), lambda i: (0, i)), pl.BlockSpec((N_EXPERTS, tm), lambda i: (0, i))],
        out_shape=[jax.ShapeDtypeStruct((N_EXPERTS, t), F32), jax.ShapeDtypeStruct((N_EXPERTS, t), F32)],
        compiler_params=pltpu.CompilerParams(dimension_semantics=("arbitrary",), vmem_limit_bytes=VMEM_LIMIT),
        name="router",
    )(x, rw_t, rb_col)


ROW_TILE = 288
TILE_PITCH = ROW_TILE + 1
GATE_LANES = -(-ROW_TILE // LANES) * LANES
SPARE_TOKENS = 8
FLAG_FIRST, FLAG_LAST, FLAG_NEW_EXPERT, FLAG_VALID, FLAG_HAS_NEXT = 1, 2, 4, 8, 16
SCATTER_BATCH = 8


PREV, CUR, NEXT = 0, 1, 2


def _moe_routed_kernel(ce_ref, flags_ref, used_ref, nexte_ref, idx_ref, gate_ref, x_ref, wg_ref, wu_ref,
                       wd_ref, sg_ref, su_ref, sd_ref, lng_ref, lnb_ref, o_ref, wgu_ref, wdb_ref,
                       gat0_ref, gat1_ref, res0_ref, res1_ref, wsg_ref, wsu_ref, wsd_ref, wsem_ref, *,
                       layer, alpha, chunk_tokens, sub_tokens):
    del used_ref
    step = pl.program_id(0)
    flags = flags_ref[step]
    odd = (step & 1) == 1
    nchunk = D_MODEL // LANES
    ne, ff = wg_ref.shape[1], wg_ref.shape[3]

    def slab_row(which, r):
        return pl.multiple_of(idx_ref[0, which, r], 8)

    def gather_row(which, gat_ref, r):
        gat_ref[pl.ds(r, nchunk, stride=TILE_PITCH), :] = x_ref[pl.ds(slab_row(which, r), 8), :]

    def scatter_rows(which, res_ref, rows):
        dst = [slab_row(which, r) for r in rows]
        acc = [o_ref[pl.ds(d, 8), :] + res_ref[pl.ds(r, nchunk, stride=TILE_PITCH), :] for d, r in zip(dst, rows)]
        for d, a in zip(dst, acc):
            o_ref[pl.ds(d, 8), :] = a

    def expert_mlp(gat_ref, res_ref):
        lhs = jnp.concatenate([gat_ref[j * TILE_PITCH:j * TILE_PITCH + ROW_TILE, :] for j in range(nchunk)],
                              axis=1).astype(BF16)
        h = _dot(lhs, wgu_ref[...])
        gate_col = jnp.broadcast_to(gate_ref[0], (LANES, GATE_LANES)).T[0:ROW_TILE]
        hid = jax.nn.silu(h[:, 0:ff]) * h[:, ff:2 * ff] * jnp.concatenate([gate_col] * (ff // LANES), axis=1)
        y = _dot(hid.astype(BF16), wdb_ref[...])
        for j in range(nchunk):
            res_ref[j * TILE_PITCH:j * TILE_PITCH + ROW_TILE, :] = y[:, j * LANES:(j + 1) * LANES]

    def by_parity(fn):
        @pl.when(jnp.logical_not(odd))
        def _even():
            fn(gat0_ref, gat1_ref, res0_ref, res1_ref)

        @pl.when(odd)
        def _odd():
            fn(gat1_ref, gat0_ref, res1_ref, res0_ref)

    @pl.when((flags & FLAG_FIRST) != 0)
    def _start_chunk():
        o_ref[...] = jnp.zeros(o_ref.shape, F32)

        def start(gat_cur, gat_other, res_cur, res_other):
            res_other[...] = jnp.zeros(res_other.shape, F32)

            def body(r, carry):
                gather_row(CUR, gat_cur, r)
                return carry

            lax.fori_loop(0, ROW_TILE, body, 0)

        by_parity(start)

    def weight_copies(e):
        return [pltpu.make_async_copy(src.at[layer, e], dst, wsem_ref.at[k])
                for k, (src, dst) in enumerate(((wg_ref, wsg_ref), (wu_ref, wsu_ref), (wd_ref, wsd_ref)))]

    @pl.when(step == 0)
    def _first_fetch():
        for cp in weight_copies(ce_ref[0] % ne):
            cp.start()

    @pl.when((flags & FLAG_NEW_EXPERT) != 0)
    def _next_expert():
        for cp in weight_copies(0):
            cp.wait()
        wgu_ref[:, 0:ff] = wsg_ref[...].astype(BF16)
        wgu_ref[:, ff:2 * ff] = wsu_ref[...].astype(BF16)
        wdb_ref[...] = wsd_ref[...].astype(BF16)

        @pl.when((flags & FLAG_HAS_NEXT) != 0)
        def _prefetch():
            for cp in weight_copies(nexte_ref[step]):
                cp.start()

    @pl.when((flags & FLAG_VALID) != 0)
    def _tile():
        def main(gat_cur, gat_other, res_cur, res_other):
            for r0 in range(0, ROW_TILE, SCATTER_BATCH):
                scatter_rows(PREV, res_other, range(r0, r0 + SCATTER_BATCH))
            for r in range(ROW_TILE):
                gather_row(NEXT, gat_other, r)
            expert_mlp(gat_cur, res_cur)

        by_parity(main)

    @pl.when((flags & FLAG_LAST) != 0)
    def _finish():
        def flush(gat_cur, gat_other, res_cur, res_other):
            def body(r, carry):
                scatter_rows(CUR, res_cur, [r])
                return carry

            lax.fori_loop(0, ROW_TILE, body, 0)

        by_parity(flush)

        def body(s, carry):
            base = pl.multiple_of(s * (sub_tokens * 8), 8)

            def rows_2d(ref):
                return jnp.concatenate([ref[pl.ds(base + j, sub_tokens, stride=8), :] for j in range(nchunk)],
                                       axis=1)

            x2 = rows_2d(x_ref)
            xb = x2.astype(BF16)
            hs = jax.nn.silu(_dot(xb, sg_ref[...])) * _dot(xb, su_ref[...])
            ffn = rows_2d(o_ref) + _dot(hs.astype(BF16), sd_ref[...])
            res = _layer_norm(alpha * x2 + ffn, lng_ref[...], lnb_ref[...])
            for j in range(nchunk):
                o_ref[pl.ds(base + j, sub_tokens, stride=8), :] = res[:, j * LANES:(j + 1) * LANES]
            return carry

        lax.fori_loop(0, chunk_tokens // sub_tokens, body, 0)


def _moe_routed(tile_ce, tile_flags, n_used, next_expert, row_idx, gates, x_tm, wg, wu, wd, sg, su, sd, ln_g,
                ln_b, *, layer, alpha, chunk_tokens, sub_tokens):
    n_chunks, chunk_rows, _ = x_tm.shape
    n_tiles = row_idx.shape[0]
    _, ne, d, ff = wg.shape
    kernel = functools.partial(_moe_routed_kernel, layer=layer, alpha=alpha, chunk_tokens=chunk_tokens,
                               sub_tokens=sub_tokens)

    def tile_map(i, ce, fl, used, nxt):
        return (jnp.minimum(i, used[0] - 1), 0, 0)

    def chunk_map(i, ce, fl, used, nxt):
        return (ce[i] // ne, 0, 0)

    def const2(i, ce, fl, used, nxt):
        return (0, 0)

    grid_spec = pltpu.PrefetchScalarGridSpec(
        num_scalar_prefetch=4,
        grid=(n_tiles,),
        in_specs=[
            pl.BlockSpec((1, 3, ROW_TILE), tile_map, memory_space=pltpu.SMEM),
            pl.BlockSpec((1, 1, GATE_LANES), tile_map),
            pl.BlockSpec((None, chunk_rows, LANES), chunk_map, pipeline_mode=pl.Buffered(1)),
            pl.BlockSpec(memory_space=pl.ANY),
            pl.BlockSpec(memory_space=pl.ANY),
            pl.BlockSpec(memory_space=pl.ANY),
            pl.BlockSpec(sg.shape, const2, pipeline_mode=pl.Buffered(1)),
            pl.BlockSpec(su.shape, const2, pipeline_mode=pl.Buffered(1)),
            pl.BlockSpec(sd.shape, const2, pipeline_mode=pl.Buffered(1)),
            pl.BlockSpec(ln_g.shape, const2),
            pl.BlockSpec(ln_b.shape, const2),
        ],
        out_specs=pl.BlockSpec((None, chunk_rows, LANES), chunk_map, pipeline_mode=pl.Buffered(1)),
        scratch_shapes=[
            pltpu.VMEM((d, 2 * ff), BF16),
            pltpu.VMEM((ff, d), BF16),
            pltpu.VMEM((8 * TILE_PITCH, LANES), F32),
            pltpu.VMEM((8 * TILE_PITCH, LANES), F32),
            pltpu.VMEM((8 * TILE_PITCH, LANES), F32),
            pltpu.VMEM((8 * TILE_PITCH, LANES), F32),
            pltpu.VMEM((d, ff), F32),
            pltpu.VMEM((d, ff), F32),
            pltpu.VMEM((ff, d), F32),
            pltpu.SemaphoreType.DMA((3,)),
        ],
    )
    return pl.pallas_call(
        kernel,
        grid_spec=grid_spec,
        out_shape=jax.ShapeDtypeStruct(x_tm.shape, F32),
        compiler_params=pltpu.CompilerParams(dimension_semantics=("arbitrary",), vmem_limit_bytes=VMEM_LIMIT),
        name="moe_routed",
    )(tile_ce, tile_flags, n_used, next_expert, row_idx, gates, x_tm, wg, wu, wd, sg, su, sd, ln_g, ln_b)


MOE_CHUNKS = 4
MOE_SUB_MAX = 512


def _moe_tiling(n_prompt, n_sample):
    assert n_prompt % (MOE_CHUNKS * 8) == 0 and n_sample % (MOE_CHUNKS * 8) == 0
    chunk_tokens = (n_prompt + n_sample) // MOE_CHUNKS
    sub_tokens = max(s for s in range(8, MOE_SUB_MAX + 1, 8) if chunk_tokens % s == 0)
    return MOE_CHUNKS, chunk_tokens, sub_tokens


PLAN_GROUP = 32


def _chunked_t(prompt, sample, n_chunks):
    ne = prompt.shape[0]
    return jnp.concatenate([prompt.reshape(ne, n_chunks, -1).transpose(1, 0, 2),
                            sample.reshape(ne, n_chunks, -1).transpose(1, 0, 2)], axis=2)


def _nth_set_bit(word, n):
    pos = jnp.zeros(word.shape, jnp.int32)
    for width in (16, 8, 4, 2, 1):
        low = word & jnp.uint32((1 << width) - 1)
        below = lax.population_count(low).astype(jnp.int32)
        upper = n >= below
        n = jnp.where(upper, n - below, n)
        word = jnp.where(upper, word >> jnp.uint32(width), low)
        pos = pos + jnp.where(upper, width, 0)
    return pos


def _route_plan(chosen_p, chosen_s, gate_p, gate_s, *, chunk_tokens, n_chunks):
    n_seg = n_chunks * N_EXPERTS
    n_group = chunk_tokens // PLAN_GROUP
    assert chunk_tokens % PLAN_GROUP == 0
    chosen = _chunked_t(chosen_p, chosen_s, n_chunks) > 0
    gate = _chunked_t(gate_p, gate_s, n_chunks)
    shifts = jnp.arange(PLAN_GROUP, dtype=jnp.uint32)
    words = jnp.sum(chosen.reshape(n_seg, n_group, PLAN_GROUP).astype(jnp.uint32) << shifts, axis=-1,
                    dtype=jnp.uint32)
    per_group = lax.population_count(words).astype(jnp.int32)
    group_start = jnp.cumsum(per_group, axis=-1) - per_group
    counts = jnp.sum(per_group, axis=-1)
    padded = -(-counts // ROW_TILE) * ROW_TILE
    seg_end = jnp.cumsum(padded)
    seg_start = seg_end - padded
    n_tiles = -(-(n_chunks * chunk_tokens * TOP_K + n_seg * (ROW_TILE - 1)) // ROW_TILE)
    row0 = jnp.arange(n_tiles, dtype=jnp.int32) * ROW_TILE
    seg = jnp.sum(seg_end[None, :] <= row0[:, None], axis=1, dtype=jnp.int32)
    valid = seg < n_seg
    tile_ce = jnp.minimum(seg, n_seg - 1)
    j = (row0 - seg_start[tile_ce])[:, None] + jnp.arange(ROW_TILE, dtype=jnp.int32)[None, :]
    real = valid[:, None] & (j < counts[tile_ce][:, None])
    starts = group_start[tile_ce][:, None, :]
    group = jnp.sum(starts <= j[:, :, None], axis=-1, dtype=jnp.int32) - 1
    in_group = jnp.arange(n_group, dtype=jnp.int32)[None, None, :] == group[:, :, None]
    start = jnp.sum(jnp.where(in_group, starts, 0), axis=-1, dtype=jnp.int32)
    word = jnp.sum(jnp.where(in_group, words[tile_ce][:, None, :], jnp.uint32(0)), axis=-1, dtype=jnp.uint32)
    pos = _nth_set_bit(word, j - start)
    token = jnp.where(real, group * PLAN_GROUP + pos, chunk_tokens)
    seg_gates = gate.reshape(n_seg, n_group, PLAN_GROUP)
    pick_group = (jnp.arange(n_group, dtype=jnp.int32)[None, None, :]
                  == jnp.where(real, group, -1)[:, :, None]).astype(BF16)
    group_gates, rest = 0.0, seg_gates
    for _ in range(3):
        top = lax.bitcast_convert_type(lax.bitcast_convert_type(rest, jnp.uint32) & jnp.uint32(0xFFFF0000), F32)
        group_gates = group_gates + jnp.einsum("trg,tgp->trp", pick_group, top.astype(BF16)[tile_ce],
                                               preferred_element_type=F32)
        rest = rest - top
    vals = jnp.sum(jnp.where(jnp.arange(PLAN_GROUP, dtype=jnp.int32)[None, None, :] == pos[:, :, None],
                             group_gates, 0.0), axis=-1)
    tile_c = tile_ce // N_EXPERTS
    prev_ce = jnp.concatenate([jnp.full((1,), -N_EXPERTS, jnp.int32), tile_ce[:-1]])
    next_c = jnp.concatenate([tile_c[1:], jnp.full((1,), -1, jnp.int32)])
    next_valid = jnp.concatenate([valid[1:], jnp.zeros((1,), jnp.bool_)])
    first = valid & (tile_c != prev_ce // N_EXPERTS)
    last = valid & (~next_valid | (next_c != tile_c))
    new_expert = valid & (tile_ce != prev_ce)
    tile_id = jnp.arange(n_tiles, dtype=jnp.int32)
    change_at = jnp.where(new_expert, tile_id, n_tiles)
    next_change = jnp.concatenate([lax.cummin(change_at, reverse=True)[1:], jnp.full((1,), n_tiles, jnp.int32)])
    has_next = new_expert & (next_change < n_tiles)
    next_expert = tile_ce[jnp.minimum(next_change, n_tiles - 1)] % N_EXPERTS
    flags = (first * FLAG_FIRST + last * FLAG_LAST + new_expert * FLAG_NEW_EXPERT + valid * FLAG_VALID
             + has_next * FLAG_HAS_NEXT)
    row_idx = token * SLAB_ROWS
    row_idx3 = jnp.stack([jnp.concatenate([row_idx[:1], row_idx[:-1]]), row_idx,
                          jnp.concatenate([row_idx[1:], row_idx[-1:]])], axis=1)
    n_used = jnp.sum(valid, dtype=jnp.int32).reshape(1)
    gates = jnp.pad(vals.reshape(n_tiles, 1, ROW_TILE), ((0, 0), (0, 0), (0, GATE_LANES - ROW_TILE)))
    return tile_ce, flags.astype(jnp.int32), n_used, next_expert.astype(jnp.int32), row_idx3, gates


def _t5_bucket(dist):
    n = jnp.maximum(dist, 0)
    max_exact = N_BUCKETS // 2
    large = max_exact + (jnp.log(jnp.maximum(n, 1).astype(F32) / max_exact)
                         / math.log(MAX_DISTANCE / max_exact) * (N_BUCKETS - max_exact)).astype(jnp.int32)
    large = jnp.minimum(large, N_BUCKETS - 1)
    return jnp.where(n < max_exact, n, large)


def _bias_lookup(rel_bias, bucket):
    onehot = (bucket[..., None] == jnp.arange(N_BUCKETS, dtype=bucket.dtype)).astype(F32)
    return jnp.einsum("...b,bh->h...", onehot, rel_bias.astype(F32), precision=lax.Precision.HIGHEST)


def _bias_tables(rel_bias, win):
    qi = jnp.arange(ATTN_BLOCK)[:, None]
    ki = jnp.arange(2 * ATTN_BLOCK)[None, :]
    dist = qi + ATTN_BLOCK - ki
    valid = (dist >= 0) & (dist < WINDOW)
    bias = _bias_lookup(rel_bias, _t5_bucket(dist))
    bias = jnp.where(valid[None], bias, NEG).reshape(N_KV_HEADS, GROUP, ATTN_BLOCK, 2 * ATTN_BLOCK)
    bias_ab = jnp.stack([jnp.concatenate([bias[:, t], bias[:, t + 2]], axis=1) for t in range(2)], axis=1)
    dist_s = (win - 1) - jnp.arange(win)
    bias_s = _bias_lookup(rel_bias, _t5_bucket(dist_s))
    return bias_ab, bias_s


def _sink_tables(sink):
    s = sink.astype(F32).reshape(N_KV_HEADS, GROUP)
    rows = [jnp.concatenate([jnp.broadcast_to(s[:, t, None], (N_KV_HEADS, ATTN_BLOCK)),
                             jnp.broadcast_to(s[:, t + 2, None], (N_KV_HEADS, ATTN_BLOCK))], axis=1)
            for t in range(2)]
    return jnp.stack(rows, axis=1)[..., None], sink.astype(F32)[:, None]


def kernel(x_prompt, x_sample, cache_k_win, cache_v_win, state_conv, rel_bias, w_in, attn_sink, conv_w,
           w_attn_out, w_conv_out, w_out, ln1_g, ln1_b, router_w, router_bias, exp_w_gate, exp_w_up,
           exp_w_down, shared_w_gate, shared_w_up, shared_w_down, ln2_g, ln2_b):
    depth = w_in.shape[0]
    alpha = (2 * depth) ** 0.25
    nb, seq, d = x_prompt.shape
    nd = x_sample.shape[0]
    win = cache_k_win.shape[2]
    assert x_sample.shape[1] == 1 and win == WINDOW and seq % 512 == 0

    n_prompt = nb * seq
    n_chunks, chunk_tokens, sub_tokens = _moe_tiling(n_prompt, nd)
    prompt_rows = n_prompt // n_chunks * SLAB_ROWS
    slab_shape = (n_chunks, (chunk_tokens + SPARE_TOKENS) * SLAB_ROWS, LANES)

    bias_ab, bias_s = _bias_tables(rel_bias, win)
    hmask = (jnp.arange(KV_DIM)[None, :] // HEAD_DIM == jnp.arange(N_HEADS)[:, None] // GROUP).astype(F32)

    yp = x_prompt
    ys = x_sample.reshape(nd, d)
    outs = [[] for _ in range(6)]
    for l in range(depth):
        w_in_b = w_in[l].astype(BF16)
        w_ao_b = w_attn_out[l].astype(BF16)
        w_co_b = w_conv_out[l].astype(BF16)
        w_o_b = w_out[l].astype(BF16)
        g1, b1 = ln1_g[l][None, :], ln1_b[l][None, :]
        g2, b2 = ln2_g[l][None, :], ln2_b[l][None, :]
        sink_ab, sink_col = _sink_tables(attn_sink[l])

        slab, kp, vp, cp = _mixer_prompt(yp, w_in_b, w_attn_out[l].T.astype(BF16), w_co_b, w_o_b, conv_w[l],
                                         jnp.swapaxes(bias_ab, -1, -2), jnp.swapaxes(sink_ab, -1, -2), g1, b1,
                                         alpha=alpha, tq=512, batch=nb, seq=seq, slab_shape=slab_shape,
                                         base=jnp.zeros(slab_shape, F32) if l == 0 else None)

        proj = _sample_proj(ys, w_in_b, tn=IN_DIM // 4)
        q4 = proj[:, :Q_DIM].reshape(nd, N_KV_HEADS, GROUP, HEAD_DIM).transpose(0, 2, 1, 3).reshape(nd, GROUP, KV_DIM)
        ksn, vsn, ag = _sample_attn(q4, proj[:, OFF_K:OFF_V], proj[:, OFF_V:OFF_B],
                                    cache_k_win.reshape(depth, nd, win, KV_DIM),
                                    cache_v_win.reshape(depth, nd, win, KV_DIM),
                                    bias_s, sink_col, hmask, layer=l, bt=16)
        att = ag.reshape(nd, GROUP, N_KV_HEADS, HEAD_DIM).transpose(0, 2, 1, 3).reshape(nd, Q_DIM)
        state_t = jnp.swapaxes(state_conv[l], 0, 1)
        ys, us = _sample_post(ys, att, proj, state_t, conv_w[l], w_ao_b, w_co_b, w_o_b, g1, b1, alpha=alpha)

        outs[0].append(kp.reshape(nb, WINDOW, N_KV_HEADS, HEAD_DIM))
        outs[1].append(vp.reshape(nb, WINDOW, N_KV_HEADS, HEAD_DIM))
        outs[2].append(cp)
        outs[3].append(ksn.reshape(nd, win, N_KV_HEADS, HEAD_DIM))
        outs[4].append(vsn.reshape(nd, win, N_KV_HEADS, HEAD_DIM))
        outs[5].append(jnp.concatenate([state_conv[l][:, 1:], us[:, None, :]], axis=1))

        rw_t = router_w[l].T.astype(BF16)
        rb_col = router_bias[l].astype(F32)[:, None]
        sg, su, sd = (shared_w_gate[l].astype(BF16), shared_w_up[l].astype(BF16), shared_w_down[l].astype(BF16))
        tail = jnp.concatenate([ys.reshape(n_chunks, -1, LANES),
                                jnp.zeros((n_chunks, SPARE_TOKENS * SLAB_ROWS, LANES), F32)], axis=1)
        slab = lax.dynamic_update_slice(slab, tail, (0, prompt_rows, 0))
        ep, wp = _router(slab, rw_t, rb_col, tm=512, tokens=n_prompt)
        es, ws = _router(ys, rw_t, rb_col, tm=nd)
        plan = _route_plan(ep, es, wp, ws, chunk_tokens=chunk_tokens, n_chunks=n_chunks)
        yp = _moe_routed(*plan, slab, exp_w_gate, exp_w_up, exp_w_down, sg, su, sd, g2, b2,
                         layer=l, alpha=alpha, chunk_tokens=chunk_tokens, sub_tokens=sub_tokens)
        ys = yp[:, prompt_rows:chunk_tokens * SLAB_ROWS].reshape(nd, d)

    y_prompt = yp[:, :prompt_rows].reshape(nb, seq, d)
    return (y_prompt, ys.reshape(nd, 1, d)) + tuple(jnp.stack(o) for o in outs)
```

```python
import functools
import math

import jax
import jax.numpy as jnp
from jax import lax
from jax.experimental import pallas as pl
from jax.experimental.pallas import tpu as pltpu

D_MODEL = 1024
N_HEADS = 16
N_KV_HEADS = 4
HEAD_DIM = 64
GROUP = N_HEADS // N_KV_HEADS
WINDOW = 128
ATTN_BLOCK = 128
N_BUCKETS = 32
MAX_DISTANCE = 128
CONV_DIM = 1024
CONV_K = 3
N_EXPERTS = 64
TOP_K = 8
N_GROUPS = 8
TOPK_GROUPS = 4
GROUP_SIZE = N_EXPERTS // N_GROUPS
EXPERT_FF = 256
ROUTED_SCALE = 2.5
LN_EPS = 1e-5
NEG = -1e30

Q_DIM = N_HEADS * HEAD_DIM
KV_DIM = N_KV_HEADS * HEAD_DIM
OFF_K = Q_DIM
OFF_V = OFF_K + KV_DIM
OFF_B = OFF_V + KV_DIM
OFF_C = OFF_B + CONV_DIM
OFF_H = OFF_C + CONV_DIM
OFF_GA = OFF_H + CONV_DIM
OFF_GB = OFF_GA + D_MODEL
IN_DIM = OFF_GB + D_MODEL

LANES = 128
CONV_PAD = 8
VMEM_LIMIT = 60 * 1024 * 1024

BF16 = jnp.bfloat16
F32 = jnp.float32


def _dot(a, b):
    return jnp.dot(a, b, preferred_element_type=F32)


def _dot_nt(a, b):
    return lax.dot_general(a, b, (((1,), (1,)), ((), ())), preferred_element_type=F32)


def _layer_norm(z, g, b):
    mu = jnp.mean(z, axis=-1, keepdims=True)
    d = z - mu
    var = jnp.mean(d * d, axis=-1, keepdims=True)
    return d * lax.rsqrt(var + LN_EPS) * g + b


def _sink_softmax(s, sink, axis=-1):
    m = jnp.maximum(jnp.max(s, axis=axis, keepdims=True), sink)
    e = jnp.exp(s - m)
    den = jnp.sum(e, axis=axis, keepdims=True) + jnp.exp(sink - m)
    return e * (1.0 / den)


SLAB_ROWS = D_MODEL // LANES


def _slab_load(ref, tokens, base=0):
    return jnp.concatenate([ref[pl.ds(base + j, tokens, stride=SLAB_ROWS), :] for j in range(SLAB_ROWS)], axis=1)


def _slab_store(ref, val, base=0):
    tokens = val.shape[0]
    for j in range(SLAB_ROWS):
        ref[pl.ds(base + j, tokens, stride=SLAB_ROWS), :] = val[:, j * LANES:(j + 1) * LANES]


def _merge_project(x, attn_o, y_conv, g_a, g_b, w_co_ref, w_o_ref, lng_ref, lnb_ref, alpha):
    merged = jax.nn.sigmoid(g_a) * attn_o + jax.nn.sigmoid(g_b) * _dot(y_conv.astype(BF16), w_co_ref[...])
    out = _dot(merged.astype(BF16), w_o_ref[...])
    return _layer_norm(alpha * x + out, lng_ref[...], lnb_ref[...])


def _mixer_prompt_kernel(x_ref, w_in_ref, w_ao_ref, w_co_ref, w_o_ref, convw_ref, bias_ref, sink_ref,
                         lng_ref, lnb_ref, base_ref,
                         x1_ref, kwin_ref, vwin_ref, conv_ref,
                         ka_ref, kb_ref, vat_ref, vbt_ref, s_ref, p_ref, att_ref, ubuf_ref, *, alpha, tq, slab_in):
    del base_ref
    i = pl.program_id(1)
    nblk = tq // ATTN_BLOCK
    half = LANES // 2
    scale = HEAD_DIM ** -0.5
    assert math.frexp(scale)[0] == 0.5

    @pl.when(i == 0)
    def _init():
        for ref in (ka_ref, kb_ref):
            ref[:, 0:ATTN_BLOCK, :] = jnp.zeros((N_KV_HEADS, ATTN_BLOCK, LANES), BF16)
        for ref in (vat_ref, vbt_ref):
            ref[:, :, 0:ATTN_BLOCK] = jnp.zeros((N_KV_HEADS, LANES, ATTN_BLOCK), BF16)
        ubuf_ref[0:CONV_PAD, :] = jnp.zeros((CONV_PAD, CONV_DIM), F32)

    x = _slab_load(x_ref, tq) if slab_in else x_ref[0]
    xb = x.astype(BF16)
    qkv = _dot(xb, w_in_ref[:, 0:OFF_B])

    lo = lax.broadcasted_iota(jnp.int32, (tq, LANES), 1) < half
    zeros_t = jnp.zeros((half, tq), BF16)
    for c in range(N_KV_HEADS // 2):
        chunk = qkv[:, OFF_K + c * LANES: OFF_K + (c + 1) * LANES]
        c_lo = jnp.where(lo, chunk, 0.0)
        c_hi = jnp.where(lo, 0.0, chunk)
        ka_ref[2 * c, ATTN_BLOCK:, :] = c_lo.astype(BF16)
        kb_ref[2 * c, ATTN_BLOCK:, :] = pltpu.roll(c_lo, half, 1).astype(BF16)
        kb_ref[2 * c + 1, ATTN_BLOCK:, :] = c_hi.astype(BF16)
        ka_ref[2 * c + 1, ATTN_BLOCK:, :] = pltpu.roll(c_hi, half, 1).astype(BF16)
        vt = qkv[:, OFF_V + c * LANES: OFF_V + (c + 1) * LANES].T.astype(BF16)
        vat_ref[2 * c, :, ATTN_BLOCK:] = jnp.concatenate([vt[0:half], zeros_t], axis=0)
        vbt_ref[2 * c, :, ATTN_BLOCK:] = jnp.concatenate([zeros_t, vt[0:half]], axis=0)
        vbt_ref[2 * c + 1, :, ATTN_BLOCK:] = jnp.concatenate([zeros_t, vt[half:]], axis=0)
        vat_ref[2 * c + 1, :, ATTN_BLOCK:] = jnp.concatenate([vt[half:], zeros_t], axis=0)

    key_row = lax.broadcasted_iota(jnp.int32, (2 * ATTN_BLOCK, 2 * ATTN_BLOCK), 0)
    for j in range(nblk):
        rows = slice(j * ATTN_BLOCK, (j + 1) * ATTN_BLOCK)
        keys = slice(j * ATTN_BLOCK, (j + 2) * ATTN_BLOCK)
        for h in range(N_KV_HEADS):
            q0 = h * GROUP * HEAD_DIM
            q2 = (jnp.concatenate([qkv[rows, q0:q0 + LANES], qkv[rows, q0 + LANES:q0 + 2 * LANES]], axis=0)
                  * scale).astype(BF16)
            for t, k_ref in enumerate((ka_ref, kb_ref)):
                s = _dot_nt(k_ref[h, keys, :], q2)
                if j == 0:
                    s = jnp.where(jnp.logical_and(i == 0, key_row < ATTN_BLOCK), NEG, s)
                s_ref[j, h, t] = s
    for h in range(N_KV_HEADS):
        for t in range(2):
            s = s_ref[:, h, t] + bias_ref[h, t][None]
            p_ref[:, h, t] = _sink_softmax(s, sink_ref[h, t][None], axis=1).astype(BF16)
    for j in range(nblk):
        rows = slice(j * ATTN_BLOCK, (j + 1) * ATTN_BLOCK)
        keys = slice(j * ATTN_BLOCK, (j + 2) * ATTN_BLOCK)
        for h in range(N_KV_HEADS):
            q0 = h * GROUP * HEAD_DIM
            o_t = _dot(vat_ref[h, :, keys], p_ref[j, h, 0]) + _dot(vbt_ref[h, :, keys], p_ref[j, h, 1])
            att_ref[q0:q0 + LANES, rows] = o_t[:, 0:ATTN_BLOCK].astype(BF16)
            att_ref[q0 + LANES:q0 + 2 * LANES, rows] = o_t[:, ATTN_BLOCK:].astype(BF16)
    for ref in (ka_ref, kb_ref):
        ref[:, 0:ATTN_BLOCK, :] = ref[:, tq:tq + ATTN_BLOCK, :]
    for ref in (vat_ref, vbt_ref):
        ref[:, :, 0:ATTN_BLOCK] = ref[:, :, tq:tq + ATTN_BLOCK]

    kwin_ref[0] = qkv[tq - WINDOW:tq, OFF_K:OFF_V]
    vwin_ref[0] = qkv[tq - WINDOW:tq, OFF_V:OFF_B]

    attn_o = _dot(w_ao_ref[...], att_ref[...]).T

    bch = _dot(xb, w_in_ref[:, OFF_B:OFF_GA])
    u = bch[:, CONV_DIM:2 * CONV_DIM] * bch[:, 2 * CONV_DIM:3 * CONV_DIM]
    ubuf_ref[CONV_PAD:CONV_PAD + tq, :] = u
    cw = convw_ref[...]
    y = (cw[0:1] * ubuf_ref[CONV_PAD - 2:CONV_PAD - 2 + tq, :]
         + cw[1:2] * ubuf_ref[CONV_PAD - 1:CONV_PAD - 1 + tq, :]
         + cw[2:3] * u)
    conv_ref[0] = ubuf_ref[CONV_PAD + tq - (CONV_K - 1):CONV_PAD + tq, :]
    ubuf_ref[0:CONV_PAD, :] = ubuf_ref[tq:tq + CONV_PAD, :]
    y_conv = bch[:, 0:CONV_DIM] * y

    gab = _dot(xb, w_in_ref[:, OFF_GA:IN_DIM])
    _slab_store(x1_ref, _merge_project(x, attn_o, y_conv, gab[:, 0:D_MODEL], gab[:, D_MODEL:], w_co_ref, w_o_ref,
                                       lng_ref, lnb_ref, alpha))


def _const_spec(shape):
    nd = len(shape)
    return pl.BlockSpec(shape, lambda *_: (0,) * nd, pipeline_mode=pl.Buffered(1))


def _mixer_prompt(x, w_in, w_ao, w_co, w_o, conv_w, bias_ab, sink_ab, ln_g, ln_b, *, alpha, tq, batch, seq,
                  slab_shape, base=None):
    b, s = batch, seq
    n_chunks, chunk_rows, _ = slab_shape
    slab_in = x.ndim == 3 and x.shape == slab_shape
    steps_per_chunk = (b * s // n_chunks) // tq
    assert (b * s) % (n_chunks * tq) == 0 and s % tq == 0

    def slab_map(bi, i):
        g = bi * (s // tq) + i
        return (g // steps_per_chunk, g % steps_per_chunk, 0)

    slab_spec = pl.BlockSpec((None, tq * SLAB_ROWS, LANES), slab_map)
    kernel = functools.partial(_mixer_prompt_kernel, alpha=alpha, tq=tq, slab_in=slab_in)
    return pl.pallas_call(
        kernel,
        grid=(b, s // tq),
        in_specs=[
            slab_spec if slab_in else pl.BlockSpec((1, tq, D_MODEL), lambda bi, i: (bi, i, 0)),
            _const_spec(w_in.shape), _const_spec(w_ao.shape), _const_spec(w_co.shape), _const_spec(w_o.shape),
            _const_spec(conv_w.shape), _const_spec(bias_ab.shape), _const_spec(sink_ab.shape),
            _const_spec(ln_g.shape), _const_spec(ln_b.shape),
            pl.BlockSpec(memory_space=pl.ANY),
        ],
        out_specs=[
            slab_spec,
            pl.BlockSpec((1, WINDOW, KV_DIM), lambda bi, i: (bi, 0, 0)),
            pl.BlockSpec((1, WINDOW, KV_DIM), lambda bi, i: (bi, 0, 0)),
            pl.BlockSpec((1, CONV_K - 1, CONV_DIM), lambda bi, i: (bi, 0, 0)),
        ],
        out_shape=[
            jax.ShapeDtypeStruct(slab_shape, F32),
            jax.ShapeDtypeStruct((b, WINDOW, KV_DIM), F32),
            jax.ShapeDtypeStruct((b, WINDOW, KV_DIM), F32),
            jax.ShapeDtypeStruct((b, CONV_K - 1, CONV_DIM), F32),
        ],
        scratch_shapes=[
            pltpu.VMEM((N_KV_HEADS, ATTN_BLOCK + tq, LANES), BF16),
            pltpu.VMEM((N_KV_HEADS, ATTN_BLOCK + tq, LANES), BF16),
            pltpu.VMEM((N_KV_HEADS, LANES, ATTN_BLOCK + tq), BF16),
            pltpu.VMEM((N_KV_HEADS, LANES, ATTN_BLOCK + tq), BF16),
            pltpu.VMEM((tq // ATTN_BLOCK, N_KV_HEADS, 2, 2 * ATTN_BLOCK, 2 * ATTN_BLOCK), F32),
            pltpu.VMEM((tq // ATTN_BLOCK, N_KV_HEADS, 2, 2 * ATTN_BLOCK, 2 * ATTN_BLOCK), BF16),
            pltpu.VMEM((Q_DIM, tq), BF16),
            pltpu.VMEM((tq + CONV_PAD, CONV_DIM), F32),
        ],
        compiler_params=pltpu.CompilerParams(
            dimension_semantics=("arbitrary", "arbitrary"), vmem_limit_bytes=VMEM_LIMIT),
        input_output_aliases={0 if slab_in else 10: 0},
        name="mixer_prompt",
    )(x, w_in, w_ao, w_co, w_o, conv_w, bias_ab, sink_ab, ln_g, ln_b,
      jnp.zeros((1, SLAB_ROWS, LANES), F32) if slab_in else base)


def _proj_kernel(x_ref, w_ref, o_ref):
    o_ref[...] = _dot(x_ref[...].astype(BF16), w_ref[...])


def _sample_proj(x, w_in, *, tn):
    m, d = x.shape
    n = w_in.shape[1]
    return pl.pallas_call(
        _proj_kernel,
        grid=(n // tn,),
        in_specs=[pl.BlockSpec((m, d), lambda j: (0, 0)), pl.BlockSpec((d, tn), lambda j: (0, j))],
        out_specs=pl.BlockSpec((m, tn), lambda j: (0, j)),
        out_shape=jax.ShapeDtypeStruct((m, n), F32),
        compiler_params=pltpu.CompilerParams(dimension_semantics=("arbitrary",), vmem_limit_bytes=VMEM_LIMIT),
        name="sample_proj",
    )(x, w_in)


def _sample_attn_kernel(q4_ref, knew_ref, vnew_ref, ck_ref, cv_ref, bias_ref, sink_ref, hmask_ref,
                        nk_ref, nv_ref, ag_ref):
    bt = ck_ref.shape[0]
    win = ck_ref.shape[1]
    scale = HEAD_DIM ** -0.5
    row = lax.broadcasted_iota(jnp.int32, (win, KV_DIM), 0)
    last = row == win - 1
    hmask = hmask_ref[...]
    for b in range(bt):
        kb = jnp.where(last, knew_ref[b:b + 1, :], pltpu.roll(ck_ref[b], win - 1, 0))
        vb = jnp.where(last, vnew_ref[b:b + 1, :], pltpu.roll(cv_ref[b], win - 1, 0))
        nk_ref[b] = kb
        nv_ref[b] = vb
        q4 = q4_ref[b]
        qm = (jnp.concatenate([q4] * N_KV_HEADS, axis=0) * hmask).astype(BF16)
        s = _dot_nt(qm, kb.astype(BF16)) * scale + bias_ref[...]
        p = _sink_softmax(s, sink_ref[...]).astype(BF16)
        o = _dot(p, vb.astype(BF16)) * hmask
        o4 = o[0:GROUP]
        for h in range(1, N_KV_HEADS):
            o4 = o4 + o[h * GROUP:(h + 1) * GROUP]
        ag_ref[b] = o4


def _sample_attn(q4, k_new, v_new, cache_k, cache_v, bias_s, sink_col, hmask, *, layer, bt):
    _, nb, win, kvd = cache_k.shape
    return pl.pallas_call(
        _sample_attn_kernel,
        grid=(nb // bt,),
        in_specs=[
            pl.BlockSpec((bt, GROUP, kvd), lambda i: (i, 0, 0)),
            pl.BlockSpec((bt, kvd), lambda i: (i, 0)),
            pl.BlockSpec((bt, kvd), lambda i: (i, 0)),
            pl.BlockSpec((None, bt, win, kvd), lambda i: (layer, i, 0, 0)),
            pl.BlockSpec((None, bt, win, kvd), lambda i: (layer, i, 0, 0)),
            pl.BlockSpec(bias_s.shape, lambda i: (0, 0)),
            pl.BlockSpec(sink_col.shape, lambda i: (0, 0)),
            pl.BlockSpec(hmask.shape, lambda i: (0, 0)),
        ],
        out_specs=[
            pl.BlockSpec((bt, win, kvd), lambda i: (i, 0, 0)),
            pl.BlockSpec((bt, win, kvd), lambda i: (i, 0, 0)),
            pl.BlockSpec((bt, GROUP, kvd), lambda i: (i, 0, 0)),
        ],
        out_shape=[
            jax.ShapeDtypeStruct((nb, win, kvd), F32),
            jax.ShapeDtypeStruct((nb, win, kvd), F32),
            jax.ShapeDtypeStruct((nb, GROUP, kvd), F32),
        ],
        compiler_params=pltpu.CompilerParams(dimension_semantics=("arbitrary",), vmem_limit_bytes=VMEM_LIMIT),
        name="sample_attn",
    )(q4, k_new, v_new, cache_k, cache_v, bias_s, sink_col, hmask)


def _sample_post_kernel(x_ref, att_ref, proj_ref, st_ref, convw_ref, w_ao_ref, w_co_ref, w_o_ref,
                        lng_ref, lnb_ref, x1_ref, u_ref, *, alpha):
    attn_o = _dot(att_ref[...].astype(BF16), w_ao_ref[...])
    u = proj_ref[:, OFF_C:OFF_H] * proj_ref[:, OFF_H:OFF_GA]
    cw = convw_ref[...]
    y = cw[0:1] * st_ref[0] + cw[1:2] * st_ref[1] + cw[2:3] * u
    u_ref[...] = u
    y_conv = proj_ref[:, OFF_B:OFF_C] * y
    x1_ref[...] = _merge_project(x_ref[...], attn_o, y_conv, proj_ref[:, OFF_GA:OFF_GB], proj_ref[:, OFF_GB:IN_DIM],
                                 w_co_ref, w_o_ref, lng_ref, lnb_ref, alpha)


def _sample_post(x, att, proj, state, conv_w, w_ao, w_co, w_o, ln_g, ln_b, *, alpha):
    m, d = x.shape
    kernel = functools.partial(_sample_post_kernel, alpha=alpha)
    return pl.pallas_call(
        kernel,
        out_shape=[jax.ShapeDtypeStruct((m, d), F32), jax.ShapeDtypeStruct((m, CONV_DIM), F32)],
        compiler_params=pltpu.CompilerParams(vmem_limit_bytes=VMEM_LIMIT),
        name="sample_post",
    )(x, att, proj, state, conv_w, w_ao, w_co, w_o, ln_g, ln_b)


def _first_max(cur, ids, axes, big):
    m = cur
    for ax in axes:
        m = jnp.max(m, axis=ax, keepdims=True)
    idx = jnp.where(cur == m, ids, big)
    for ax in axes:
        idx = jnp.min(idx, axis=ax, keepdims=True)
    return m, idx


def _router_kernel(x_ref, rwt_ref, rb_ref, eidx_ref, gate_ref, *, tm, slab_in):
    x = _slab_load(x_ref, tm) if slab_in else x_ref[...]
    logits_t = _dot_nt(rwt_ref[...], x.astype(BF16))
    scores = jax.nn.sigmoid(logits_t)
    sel = scores + rb_ref[...]
    shape3 = (N_GROUPS, GROUP_SIZE, tm)
    sel3 = sel.reshape(shape3)
    scores3 = scores.reshape(shape3)
    member = lax.broadcasted_iota(jnp.int32, shape3, 1)
    m1, i1 = _first_max(sel3, member, (1,), GROUP_SIZE)
    m2 = jnp.max(jnp.where(member == i1, -jnp.inf, sel3), axis=1, keepdims=True)
    gscore = m1 + m2
    gid = lax.broadcasted_iota(jnp.int32, gscore.shape, 0)
    gsel = jnp.zeros(gscore.shape, jnp.bool_)
    for _ in range(TOPK_GROUPS):
        _, gi = _first_max(gscore, gid, (0,), N_GROUPS)
        hit = gid == gi
        gsel = jnp.logical_or(gsel, hit)
        gscore = jnp.where(hit, -jnp.inf, gscore)
    eid = lax.broadcasted_iota(jnp.int32, shape3, 0) * GROUP_SIZE + member
    cur = jnp.where(gsel, sel3, -jnp.inf)
    ids, ws = [], []
    for _ in range(TOP_K):
        _, ei = _first_max(cur, eid, (1, 0), N_EXPERTS)
        hit = eid == ei
        sc = jnp.sum(jnp.sum(jnp.where(hit, scores3, 0.0), axis=1, keepdims=True), axis=0, keepdims=True)
        ids.append(ei[0])
        ws.append(sc[0])
        cur = jnp.where(hit, -jnp.inf, cur)
    w = jnp.concatenate(ws, axis=0)
    tot = jnp.sum(w, axis=0, keepdims=True)
    eidx_ref[...] = jnp.concatenate(ids, axis=0)
    gate_ref[...] = w / tot * ROUTED_SCALE


def _router(x, rw_t, rb_col, *, tm, tokens=None):
    slab_in = x.ndim == 3
    if slab_in:
        t = tokens
        steps_per_chunk = (t // x.shape[0]) // tm
        assert t % (x.shape[0] * tm) == 0
        x_spec = pl.BlockSpec((None, tm * SLAB_ROWS, LANES), lambda i: (i // steps_per_chunk, i % steps_per_chunk, 0))
    else:
        t = x.shape[0]
        x_spec = pl.BlockSpec((tm, D_MODEL), lambda i: (i, 0))
    return pl.pallas_call(
        functools.partial(_router_kernel, tm=tm, slab_in=slab_in),
        grid=(t // tm,),
        in_specs=[
            x_spec,
            pl.BlockSpec(rw_t.shape, lambda i: (0, 0)),
            pl.BlockSpec(rb_col.shape, lambda i: (0, 0)),
        ],
        out_specs=[pl.BlockSpec((TOP_K, tm), lambda i: (0, i)), pl.BlockSpec((TOP_K, tm), lambda i: (0, i))],
        out_shape=[jax.ShapeDtypeStruct((TOP_K, t), jnp.int32), jax.ShapeDtypeStruct((TOP_K, t), F32)],
        compiler_params=pltpu.CompilerParams(dimension_semantics=("arbitrary",), vmem_limit_bytes=VMEM_LIMIT),
        name="router",
    )(x, rw_t, rb_col)


ROW_TILE = 288
TILE_PITCH = ROW_TILE + 1
GATE_LANES = -(-ROW_TILE // LANES) * LANES
SPARE_TOKENS = 8
PAIR_BITS = 16
FLAG_FIRST, FLAG_LAST, FLAG_NEW_EXPERT, FLAG_VALID, FLAG_HAS_NEXT = 1, 2, 4, 8, 16
SCATTER_BATCH = 16


PREV, CUR, NEXT = 0, 1, 2


def _moe_routed_kernel(ce_ref, flags_ref, used_ref, nexte_ref, idx_ref, gate_ref, x_ref, wg_ref, wu_ref,
                       wd_ref, sg_ref, su_ref, sd_ref, lng_ref, lnb_ref, o_ref, wgu_ref, wdb_ref,
                       gat0_ref, gat1_ref, res0_ref, res1_ref, wsg_ref, wsu_ref, wsd_ref, wsem_ref, *,
                       layer, alpha, chunk_tokens, sub_tokens):
    del used_ref
    step = pl.program_id(0)
    flags = flags_ref[step]
    odd = (step & 1) == 1
    nchunk = D_MODEL // LANES
    ne, ff = wg_ref.shape[1], wg_ref.shape[3]

    def slab_row(which, r):
        return pl.multiple_of(idx_ref[0, which, r], 8)

    def gather_row(which, gat_ref, r):
        gat_ref[pl.ds(r, nchunk, stride=TILE_PITCH), :] = x_ref[pl.ds(slab_row(which, r), 8), :]

    def scatter_rows(which, res_ref, rows):
        dst = [slab_row(which, r) for r in rows]
        acc = [o_ref[pl.ds(d, 8), :] + res_ref[pl.ds(r, nchunk, stride=TILE_PITCH), :] for d, r in zip(dst, rows)]
        for d, a in zip(dst, acc):
            o_ref[pl.ds(d, 8), :] = a

    def expert_mlp(gat_ref, res_ref):
        lhs = jnp.concatenate([gat_ref[j * TILE_PITCH:j * TILE_PITCH + ROW_TILE, :] for j in range(nchunk)],
                              axis=1).astype(BF16)
        h = _dot(lhs, wgu_ref[...])
        gate_col = jnp.broadcast_to(gate_ref[0], (LANES, GATE_LANES)).T[0:ROW_TILE]
        hid = jax.nn.silu(h[:, 0:ff]) * h[:, ff:2 * ff] * jnp.concatenate([gate_col] * (ff // LANES), axis=1)
        y = _dot(hid.astype(BF16), wdb_ref[...])
        for j in range(nchunk):
            res_ref[j * TILE_PITCH:j * TILE_PITCH + ROW_TILE, :] = y[:, j * LANES:(j + 1) * LANES]

    def by_parity(fn):
        @pl.when(jnp.logical_not(odd))
        def _even():
            fn(gat0_ref, gat1_ref, res0_ref, res1_ref)

        @pl.when(odd)
        def _odd():
            fn(gat1_ref, gat0_ref, res1_ref, res0_ref)

    @pl.when((flags & FLAG_FIRST) != 0)
    def _start_chunk():
        o_ref[...] = jnp.zeros(o_ref.shape, F32)

        def start(gat_cur, gat_other, res_cur, res_other):
            res_other[...] = jnp.zeros(res_other.shape, F32)

            def body(r, carry):
                gather_row(CUR, gat_cur, r)
                return carry

            lax.fori_loop(0, ROW_TILE, body, 0)

        by_parity(start)

    def weight_copies(e):
        return [pltpu.make_async_copy(src.at[layer, e], dst, wsem_ref.at[k])
                for k, (src, dst) in enumerate(((wg_ref, wsg_ref), (wu_ref, wsu_ref), (wd_ref, wsd_ref)))]

    @pl.when(step == 0)
    def _first_fetch():
        for cp in weight_copies(ce_ref[0] % ne):
            cp.start()

    @pl.when((flags & FLAG_NEW_EXPERT) != 0)
    def _next_expert():
        for cp in weight_copies(0):
            cp.wait()
        wgu_ref[:, 0:ff] = wsg_ref[...].astype(BF16)
        wgu_ref[:, ff:2 * ff] = wsu_ref[...].astype(BF16)
        wdb_ref[...] = wsd_ref[...].astype(BF16)

        @pl.when((flags & FLAG_HAS_NEXT) != 0)
        def _prefetch():
            for cp in weight_copies(nexte_ref[step]):
                cp.start()

    @pl.when((flags & FLAG_VALID) != 0)
    def _tile():
        def main(gat_cur, gat_other, res_cur, res_other):
            for r0 in range(0, ROW_TILE, SCATTER_BATCH):
                scatter_rows(PREV, res_other, range(r0, r0 + SCATTER_BATCH))
            for r in range(ROW_TILE):
                gather_row(NEXT, gat_other, r)
            expert_mlp(gat_cur, res_cur)

        by_parity(main)

    @pl.when((flags & FLAG_LAST) != 0)
    def _finish():
        def flush(gat_cur, gat_other, res_cur, res_other):
            def body(r, carry):
                scatter_rows(CUR, res_cur, [r])
                return carry

            lax.fori_loop(0, ROW_TILE, body, 0)

        by_parity(flush)

        def body(s, carry):
            base = pl.multiple_of(s * (sub_tokens * 8), 8)

            def rows_2d(ref):
                return jnp.concatenate([ref[pl.ds(base + j, sub_tokens, stride=8), :] for j in range(nchunk)],
                                       axis=1)

            x2 = rows_2d(x_ref)
            xb = x2.astype(BF16)
            hs = jax.nn.silu(_dot(xb, sg_ref[...])) * _dot(xb, su_ref[...])
            ffn = rows_2d(o_ref) + _dot(hs.astype(BF16), sd_ref[...])
            res = _layer_norm(alpha * x2 + ffn, lng_ref[...], lnb_ref[...])
            for j in range(nchunk):
                o_ref[pl.ds(base + j, sub_tokens, stride=8), :] = res[:, j * LANES:(j + 1) * LANES]
            return carry

        lax.fori_loop(0, chunk_tokens // sub_tokens, body, 0)


def _moe_routed(tile_ce, tile_flags, n_used, next_expert, row_idx, gates, x_tm, wg, wu, wd, sg, su, sd, ln_g,
                ln_b, *, layer, alpha, chunk_tokens, sub_tokens):
    n_chunks, chunk_rows, _ = x_tm.shape
    n_tiles = row_idx.shape[0]
    _, ne, d, ff = wg.shape
    kernel = functools.partial(_moe_routed_kernel, layer=layer, alpha=alpha, chunk_tokens=chunk_tokens,
                               sub_tokens=sub_tokens)

    def tile_map(i, ce, fl, used, nxt):
        return (jnp.minimum(i, used[0] - 1), 0, 0)

    def chunk_map(i, ce, fl, used, nxt):
        return (ce[i] // ne, 0, 0)

    def const2(i, ce, fl, used, nxt):
        return (0, 0)

    grid_spec = pltpu.PrefetchScalarGridSpec(
        num_scalar_prefetch=4,
        grid=(n_tiles,),
        in_specs=[
            pl.BlockSpec((1, 3, ROW_TILE), tile_map, memory_space=pltpu.SMEM),
            pl.BlockSpec((1, 1, GATE_LANES), tile_map),
            pl.BlockSpec((None, chunk_rows, LANES), chunk_map, pipeline_mode=pl.Buffered(1)),
            pl.BlockSpec(memory_space=pl.ANY),
            pl.BlockSpec(memory_space=pl.ANY),
            pl.BlockSpec(memory_space=pl.ANY),
            pl.BlockSpec(sg.shape, const2, pipeline_mode=pl.Buffered(1)),
            pl.BlockSpec(su.shape, const2, pipeline_mode=pl.Buffered(1)),
            pl.BlockSpec(sd.shape, const2, pipeline_mode=pl.Buffered(1)),
            pl.BlockSpec(ln_g.shape, const2),
            pl.BlockSpec(ln_b.shape, const2),
        ],
        out_specs=pl.BlockSpec((None, chunk_rows, LANES), chunk_map, pipeline_mode=pl.Buffered(1)),
        scratch_shapes=[
            pltpu.VMEM((d, 2 * ff), BF16),
            pltpu.VMEM((ff, d), BF16),
            pltpu.VMEM((8 * TILE_PITCH, LANES), F32),
            pltpu.VMEM((8 * TILE_PITCH, LANES), F32),
            pltpu.VMEM((8 * TILE_PITCH, LANES), F32),
            pltpu.VMEM((8 * TILE_PITCH, LANES), F32),
            pltpu.VMEM((d, ff), F32),
            pltpu.VMEM((d, ff), F32),
            pltpu.VMEM((ff, d), F32),
            pltpu.SemaphoreType.DMA((3,)),
        ],
    )
    return pl.pallas_call(
        kernel,
        grid_spec=grid_spec,
        out_shape=jax.ShapeDtypeStruct(x_tm.shape, F32),
        compiler_params=pltpu.CompilerParams(dimension_semantics=("arbitrary",), vmem_limit_bytes=VMEM_LIMIT),
        name="moe_routed",
    )(tile_ce, tile_flags, n_used, next_expert, row_idx, gates, x_tm, wg, wu, wd, sg, su, sd, ln_g, ln_b)


MOE_CHUNKS = 4
MOE_SUB_MAX = 512


def _moe_tiling(n_prompt, n_sample):
    assert n_prompt % (MOE_CHUNKS * 8) == 0 and n_sample % (MOE_CHUNKS * 8) == 0
    chunk_tokens = (n_prompt + n_sample) // MOE_CHUNKS
    assert (chunk_tokens + 1) * TOP_K <= 1 << PAIR_BITS
    sub_tokens = max(s for s in range(8, MOE_SUB_MAX + 1, 8) if chunk_tokens % s == 0)
    return MOE_CHUNKS, chunk_tokens, sub_tokens


def _chunked(prompt, sample, n_chunks):
    return jnp.concatenate([prompt.reshape((n_chunks, -1) + prompt.shape[1:]),
                            sample.reshape((n_chunks, -1) + sample.shape[1:])], axis=1)


def _route_plan(eidx_p, eidx_s, gate_p, gate_s, *, chunk_tokens, n_chunks):
    n_seg = n_chunks * N_EXPERTS
    eidx = _chunked(eidx_p.T, eidx_s.T, n_chunks)
    gate = _chunked(gate_p.T, gate_s.T, n_chunks)
    seg = jnp.arange(n_chunks, dtype=jnp.int32)[:, None, None] * N_EXPERTS + eidx
    pair = jnp.arange(chunk_tokens * TOP_K, dtype=jnp.int32).reshape(1, chunk_tokens, TOP_K)
    keys_real = ((seg << PAIR_BITS) | pair).reshape(-1)
    counts = jnp.sum(eidx[..., None] == jnp.arange(N_EXPERTS, dtype=jnp.int32), axis=(1, 2),
                     dtype=jnp.int32).reshape(n_seg)
    n_pad = (-counts) % ROW_TILE
    slot = jnp.arange(ROW_TILE - 1, dtype=jnp.int32)
    int_max = jnp.iinfo(jnp.int32).max
    pad_pair = (1 << PAIR_BITS) - 1
    keys_pad = jnp.where(slot[None, :] < n_pad[:, None],
                         (jnp.arange(n_seg, dtype=jnp.int32)[:, None] << PAIR_BITS) | pad_pair, int_max)
    n_fill = (-(keys_real.size + keys_pad.size)) % ROW_TILE
    keys = jnp.concatenate([keys_real, keys_pad.reshape(-1), jnp.full((n_fill,), int_max, jnp.int32)])
    vals = jnp.concatenate([gate.reshape(-1), jnp.zeros((keys_pad.size + n_fill,), F32)])
    n_tiles = keys.size // ROW_TILE
    keys, vals = lax.sort((keys, vals), num_keys=1)
    keys = keys.reshape(n_tiles, ROW_TILE)
    head = keys[:, 0]
    valid = head != int_max
    tile_ce = jnp.minimum(head >> PAIR_BITS, n_seg - 1)
    tile_c = tile_ce // N_EXPERTS
    prev_ce = jnp.concatenate([jnp.full((1,), -N_EXPERTS, jnp.int32), tile_ce[:-1]])
    next_c = jnp.concatenate([tile_c[1:], jnp.full((1,), -1, jnp.int32)])
    next_valid = jnp.concatenate([valid[1:], jnp.zeros((1,), jnp.bool_)])
    first = valid & (tile_c != prev_ce // N_EXPERTS)
    last = valid & (~next_valid | (next_c != tile_c))
    new_expert = valid & (tile_ce != prev_ce)
    tile_id = jnp.arange(n_tiles, dtype=jnp.int32)
    change_at = jnp.where(new_expert, tile_id, n_tiles)
    next_change = jnp.concatenate([lax.cummin(change_at, reverse=True)[1:], jnp.full((1,), n_tiles, jnp.int32)])
    has_next = new_expert & (next_change < n_tiles)
    next_expert = tile_ce[jnp.minimum(next_change, n_tiles - 1)] % N_EXPERTS
    flags = (first * FLAG_FIRST + last * FLAG_LAST + new_expert * FLAG_NEW_EXPERT + valid * FLAG_VALID
             + has_next * FLAG_HAS_NEXT)
    row_idx = jnp.minimum((keys & pad_pair) // TOP_K, chunk_tokens) * 8
    row_idx3 = jnp.stack([jnp.concatenate([row_idx[:1], row_idx[:-1]]), row_idx,
                          jnp.concatenate([row_idx[1:], row_idx[-1:]])], axis=1)
    n_used = jnp.sum(valid, dtype=jnp.int32).reshape(1)
    gates = jnp.pad(vals.reshape(n_tiles, 1, ROW_TILE), ((0, 0), (0, 0), (0, GATE_LANES - ROW_TILE)))
    return tile_ce, flags.astype(jnp.int32), n_used, next_expert.astype(jnp.int32), row_idx3, gates


def _t5_bucket(dist):
    n = jnp.maximum(dist, 0)
    max_exact = N_BUCKETS // 2
    large = max_exact + (jnp.log(jnp.maximum(n, 1).astype(F32) / max_exact)
                         / math.log(MAX_DISTANCE / max_exact) * (N_BUCKETS - max_exact)).astype(jnp.int32)
    large = jnp.minimum(large, N_BUCKETS - 1)
    return jnp.where(n < max_exact, n, large)


def _bias_lookup(rel_bias, bucket):
    onehot = (bucket[..., None] == jnp.arange(N_BUCKETS, dtype=bucket.dtype)).astype(F32)
    return jnp.einsum("...b,bh->h...", onehot, rel_bias.astype(F32), precision=lax.Precision.HIGHEST)


def _bias_tables(rel_bias, win):
    qi = jnp.arange(ATTN_BLOCK)[:, None]
    ki = jnp.arange(2 * ATTN_BLOCK)[None, :]
    dist = qi + ATTN_BLOCK - ki
    valid = (dist >= 0) & (dist < WINDOW)
    bias = _bias_lookup(rel_bias, _t5_bucket(dist))
    bias = jnp.where(valid[None], bias, NEG).reshape(N_KV_HEADS, GROUP, ATTN_BLOCK, 2 * ATTN_BLOCK)
    bias_ab = jnp.stack([jnp.concatenate([bias[:, t], bias[:, t + 2]], axis=1) for t in range(2)], axis=1)
    dist_s = (win - 1) - jnp.arange(win)
    bias_s = _bias_lookup(rel_bias, _t5_bucket(dist_s))
    return bias_ab, bias_s


def _sink_tables(sink):
    s = sink.astype(F32).reshape(N_KV_HEADS, GROUP)
    rows = [jnp.concatenate([jnp.broadcast_to(s[:, t, None], (N_KV_HEADS, ATTN_BLOCK)),
                             jnp.broadcast_to(s[:, t + 2, None], (N_KV_HEADS, ATTN_BLOCK))], axis=1)
            for t in range(2)]
    return jnp.stack(rows, axis=1)[..., None], sink.astype(F32)[:, None]


def kernel(x_prompt, x_sample, cache_k_win, cache_v_win, state_conv, rel_bias, w_in, attn_sink, conv_w,
           w_attn_out, w_conv_out, w_out, ln1_g, ln1_b, router_w, router_bias, exp_w_gate, exp_w_up,
           exp_w_down, shared_w_gate, shared_w_up, shared_w_down, ln2_g, ln2_b):
    depth = w_in.shape[0]
    alpha = (2 * depth) ** 0.25
    nb, seq, d = x_prompt.shape
    nd = x_sample.shape[0]
    win = cache_k_win.shape[2]
    assert x_sample.shape[1] == 1 and win == WINDOW and seq % 512 == 0

    n_prompt = nb * seq
    n_chunks, chunk_tokens, sub_tokens = _moe_tiling(n_prompt, nd)
    prompt_rows = n_prompt // n_chunks * SLAB_ROWS
    slab_shape = (n_chunks, (chunk_tokens + SPARE_TOKENS) * SLAB_ROWS, LANES)

    bias_ab, bias_s = _bias_tables(rel_bias, win)
    hmask = (jnp.arange(KV_DIM)[None, :] // HEAD_DIM == jnp.arange(N_HEADS)[:, None] // GROUP).astype(F32)

    yp = x_prompt
    ys = x_sample.reshape(nd, d)
    outs = [[] for _ in range(6)]
    for l in range(depth):
        w_in_b = w_in[l].astype(BF16)
        w_ao_b = w_attn_out[l].astype(BF16)
        w_co_b = w_conv_out[l].astype(BF16)
        w_o_b = w_out[l].astype(BF16)
        g1, b1 = ln1_g[l][None, :], ln1_b[l][None, :]
        g2, b2 = ln2_g[l][None, :], ln2_b[l][None, :]
        sink_ab, sink_col = _sink_tables(attn_sink[l])

        slab, kp, vp, cp = _mixer_prompt(yp, w_in_b, w_attn_out[l].T.astype(BF16), w_co_b, w_o_b, conv_w[l],
                                         jnp.swapaxes(bias_ab, -1, -2), jnp.swapaxes(sink_ab, -1, -2), g1, b1,
                                         alpha=alpha, tq=512, batch=nb, seq=seq, slab_shape=slab_shape,
                                         base=jnp.zeros(slab_shape, F32) if l == 0 else None)

        proj = _sample_proj(ys, w_in_b, tn=IN_DIM // 4)
        q4 = proj[:, :Q_DIM].reshape(nd, N_KV_HEADS, GROUP, HEAD_DIM).transpose(0, 2, 1, 3).reshape(nd, GROUP, KV_DIM)
        ksn, vsn, ag = _sample_attn(q4, proj[:, OFF_K:OFF_V], proj[:, OFF_V:OFF_B],
                                    cache_k_win.reshape(depth, nd, win, KV_DIM),
                                    cache_v_win.reshape(depth, nd, win, KV_DIM),
                                    bias_s, sink_col, hmask, layer=l, bt=16)
        att = ag.reshape(nd, GROUP, N_KV_HEADS, HEAD_DIM).transpose(0, 2, 1, 3).reshape(nd, Q_DIM)
        state_t = jnp.swapaxes(state_conv[l], 0, 1)
        ys, us = _sample_post(ys, att, proj, state_t, conv_w[l], w_ao_b, w_co_b, w_o_b, g1, b1, alpha=alpha)

        outs[0].append(kp.reshape(nb, WINDOW, N_KV_HEADS, HEAD_DIM))
        outs[1].append(vp.reshape(nb, WINDOW, N_KV_HEADS, HEAD_DIM))
        outs[2].append(cp)
        outs[3].append(ksn.reshape(nd, win, N_KV_HEADS, HEAD_DIM))
        outs[4].append(vsn.reshape(nd, win, N_KV_HEADS, HEAD_DIM))
        outs[5].append(jnp.concatenate([state_conv[l][:, 1:], us[:, None, :]], axis=1))

        rw_t = router_w[l].T.astype(BF16)
        rb_col = router_bias[l].astype(F32)[:, None]
        sg, su, sd = (shared_w_gate[l].astype(BF16), shared_w_up[l].astype(BF16), shared_w_down[l].astype(BF16))
        tail = jnp.concatenate([ys.reshape(n_chunks, -1, LANES),
                                jnp.zeros((n_chunks, SPARE_TOKENS * SLAB_ROWS, LANES), F32)], axis=1)
        slab = lax.dynamic_update_slice(slab, tail, (0, prompt_rows, 0))
        ep, wp = _router(slab, rw_t, rb_col, tm=512, tokens=n_prompt)
        es, ws = _router(ys, rw_t, rb_col, tm=nd)
        plan = _route_plan(ep, es, wp, ws, chunk_tokens=chunk_tokens, n_chunks=n_chunks)
        yp = _moe_routed(*plan, slab, exp_w_gate, exp_w_up, exp_w_down, sg, su, sd, g2, b2,
                         layer=l, alpha=alpha, chunk_tokens=chunk_tokens, sub_tokens=sub_tokens)
        ys = yp[:, prompt_rows:chunk_tokens * SLAB_ROWS].reshape(nd, d)

    y_prompt = yp[:, :prompt_rows].reshape(nb, seq, d)
    return (y_prompt, ys.reshape(nd, 1, d)) + tuple(jnp.stack(o) for o in outs)
```

```python
import functools
import math

import jax
import jax.numpy as jnp
from jax import lax
from jax.experimental import pallas as pl
from jax.experimental.pallas import tpu as pltpu

D_MODEL = 1024
N_HEADS = 16
N_KV_HEADS = 4
HEAD_DIM = 64
GROUP = N_HEADS // N_KV_HEADS
WINDOW = 128
ATTN_BLOCK = 128
N_BUCKETS = 32
MAX_DISTANCE = 128
CONV_DIM = 1024
CONV_K = 3
N_EXPERTS = 64
TOP_K = 8
N_GROUPS = 8
TOPK_GROUPS = 4
GROUP_SIZE = N_EXPERTS // N_GROUPS
EXPERT_FF = 256
ROUTED_SCALE = 2.5
LN_EPS = 1e-5
NEG = -1e30

Q_DIM = N_HEADS * HEAD_DIM
KV_DIM = N_KV_HEADS * HEAD_DIM
OFF_K = Q_DIM
OFF_V = OFF_K + KV_DIM
OFF_B = OFF_V + KV_DIM
OFF_C = OFF_B + CONV_DIM
OFF_H = OFF_C + CONV_DIM
OFF_GA = OFF_H + CONV_DIM
OFF_GB = OFF_GA + D_MODEL
IN_DIM = OFF_GB + D_MODEL

LANES = 128
CONV_PAD = 8
VMEM_LIMIT = 60 * 1024 * 1024

BF16 = jnp.bfloat16
F32 = jnp.float32


def _dot(a, b):
    return jnp.dot(a, b, preferred_element_type=F32)


def _dot_nt(a, b):
    return lax.dot_general(a, b, (((1,), (1,)), ((), ())), preferred_element_type=F32)


def _layer_norm(z, g, b):
    mu = jnp.mean(z, axis=-1, keepdims=True)
    d = z - mu
    var = jnp.mean(d * d, axis=-1, keepdims=True)
    return d * lax.rsqrt(var + LN_EPS) * g + b


def _sink_softmax(s, sink, axis=-1):
    m = jnp.maximum(jnp.max(s, axis=axis, keepdims=True), sink)
    e = jnp.exp(s - m)
    den = jnp.sum(e, axis=axis, keepdims=True) + jnp.exp(sink - m)
    return e * (1.0 / den)


SLAB_ROWS = D_MODEL // LANES


def _slab_load(ref, tokens, base=0):
    return jnp.concatenate([ref[pl.ds(base + j, tokens, stride=SLAB_ROWS), :] for j in range(SLAB_ROWS)], axis=1)


def _slab_store(ref, val, base=0):
    tokens = val.shape[0]
    for j in range(SLAB_ROWS):
        ref[pl.ds(base + j, tokens, stride=SLAB_ROWS), :] = val[:, j * LANES:(j + 1) * LANES]


def _merge_project(x, attn_o, y_conv, g_a, g_b, w_co_ref, w_o_ref, lng_ref, lnb_ref, alpha):
    merged = jax.nn.sigmoid(g_a) * attn_o + jax.nn.sigmoid(g_b) * _dot(y_conv.astype(BF16), w_co_ref[...])
    out = _dot(merged.astype(BF16), w_o_ref[...])
    return _layer_norm(alpha * x + out, lng_ref[...], lnb_ref[...])


def _mixer_prompt_kernel(x_ref, w_in_ref, w_ao_ref, w_co_ref, w_o_ref, convw_ref, bias_ref, sink_ref,
                         lng_ref, lnb_ref, base_ref,
                         x1_ref, kwin_ref, vwin_ref, conv_ref,
                         ka_ref, kb_ref, vat_ref, vbt_ref, s_ref, p_ref, att_ref, ubuf_ref, *, alpha, tq, slab_in):
    del base_ref
    i = pl.program_id(1)
    nblk = tq // ATTN_BLOCK
    half = LANES // 2
    scale = HEAD_DIM ** -0.5
    assert math.frexp(scale)[0] == 0.5

    @pl.when(i == 0)
    def _init():
        for ref in (ka_ref, kb_ref):
            ref[:, 0:ATTN_BLOCK, :] = jnp.zeros((N_KV_HEADS, ATTN_BLOCK, LANES), BF16)
        for ref in (vat_ref, vbt_ref):
            ref[:, :, 0:ATTN_BLOCK] = jnp.zeros((N_KV_HEADS, LANES, ATTN_BLOCK), BF16)
        ubuf_ref[0:CONV_PAD, :] = jnp.zeros((CONV_PAD, CONV_DIM), F32)

    x = _slab_load(x_ref, tq) if slab_in else x_ref[0]
    xb = x.astype(BF16)
    qkv = _dot(xb, w_in_ref[:, 0:OFF_B])

    lo = lax.broadcasted_iota(jnp.int32, (tq, LANES), 1) < half
    zeros_t = jnp.zeros((half, tq), BF16)
    for c in range(N_KV_HEADS // 2):
        chunk = qkv[:, OFF_K + c * LANES: OFF_K + (c + 1) * LANES]
        c_lo = jnp.where(lo, chunk, 0.0)
        c_hi = jnp.where(lo, 0.0, chunk)
        ka_ref[2 * c, ATTN_BLOCK:, :] = c_lo.astype(BF16)
        kb_ref[2 * c, ATTN_BLOCK:, :] = pltpu.roll(c_lo, half, 1).astype(BF16)
        kb_ref[2 * c + 1, ATTN_BLOCK:, :] = c_hi.astype(BF16)
        ka_ref[2 * c + 1, ATTN_BLOCK:, :] = pltpu.roll(c_hi, half, 1).astype(BF16)
        vt = qkv[:, OFF_V + c * LANES: OFF_V + (c + 1) * LANES].T.astype(BF16)
        vat_ref[2 * c, :, ATTN_BLOCK:] = jnp.concatenate([vt[0:half], zeros_t], axis=0)
        vbt_ref[2 * c, :, ATTN_BLOCK:] = jnp.concatenate([zeros_t, vt[0:half]], axis=0)
        vbt_ref[2 * c + 1, :, ATTN_BLOCK:] = jnp.concatenate([zeros_t, vt[half:]], axis=0)
        vat_ref[2 * c + 1, :, ATTN_BLOCK:] = jnp.concatenate([vt[half:], zeros_t], axis=0)

    key_row = lax.broadcasted_iota(jnp.int32, (2 * ATTN_BLOCK, 2 * ATTN_BLOCK), 0)
    for j in range(nblk):
        rows = slice(j * ATTN_BLOCK, (j + 1) * ATTN_BLOCK)
        keys = slice(j * ATTN_BLOCK, (j + 2) * ATTN_BLOCK)
        for h in range(N_KV_HEADS):
            q0 = h * GROUP * HEAD_DIM
            q2 = (jnp.concatenate([qkv[rows, q0:q0 + LANES], qkv[rows, q0 + LANES:q0 + 2 * LANES]], axis=0)
                  * scale).astype(BF16)
            for t, k_ref in enumerate((ka_ref, kb_ref)):
                s = _dot_nt(k_ref[h, keys, :], q2)
                if j == 0:
                    s = jnp.where(jnp.logical_and(i == 0, key_row < ATTN_BLOCK), NEG, s)
                s_ref[j, h, t] = s
    for h in range(N_KV_HEADS):
        for t in range(2):
            s = s_ref[:, h, t] + bias_ref[h, t][None]
            p_ref[:, h, t] = _sink_softmax(s, sink_ref[h, t][None], axis=1).astype(BF16)
    for j in range(nblk):
        rows = slice(j * ATTN_BLOCK, (j + 1) * ATTN_BLOCK)
        keys = slice(j * ATTN_BLOCK, (j + 2) * ATTN_BLOCK)
        for h in range(N_KV_HEADS):
            q0 = h * GROUP * HEAD_DIM
            o_t = _dot(vat_ref[h, :, keys], p_ref[j, h, 0]) + _dot(vbt_ref[h, :, keys], p_ref[j, h, 1])
            att_ref[q0:q0 + LANES, rows] = o_t[:, 0:ATTN_BLOCK].astype(BF16)
            att_ref[q0 + LANES:q0 + 2 * LANES, rows] = o_t[:, ATTN_BLOCK:].astype(BF16)
    for ref in (ka_ref, kb_ref):
        ref[:, 0:ATTN_BLOCK, :] = ref[:, tq:tq + ATTN_BLOCK, :]
    for ref in (vat_ref, vbt_ref):
        ref[:, :, 0:ATTN_BLOCK] = ref[:, :, tq:tq + ATTN_BLOCK]

    kwin_ref[0] = qkv[tq - WINDOW:tq, OFF_K:OFF_V]
    vwin_ref[0] = qkv[tq - WINDOW:tq, OFF_V:OFF_B]

    attn_o = _dot(w_ao_ref[...], att_ref[...]).T

    bch = _dot(xb, w_in_ref[:, OFF_B:OFF_GA])
    u = bch[:, CONV_DIM:2 * CONV_DIM] * bch[:, 2 * CONV_DIM:3 * CONV_DIM]
    ubuf_ref[CONV_PAD:CONV_PAD + tq, :] = u
    cw = convw_ref[...]
    y = (cw[0:1] * ubuf_ref[CONV_PAD - 2:CONV_PAD - 2 + tq, :]
         + cw[1:2] * ubuf_ref[CONV_PAD - 1:CONV_PAD - 1 + tq, :]
         + cw[2:3] * u)
    conv_ref[0] = ubuf_ref[CONV_PAD + tq - (CONV_K - 1):CONV_PAD + tq, :]
    ubuf_ref[0:CONV_PAD, :] = ubuf_ref[tq:tq + CONV_PAD, :]
    y_conv = bch[:, 0:CONV_DIM] * y

    gab = _dot(xb, w_in_ref[:, OFF_GA:IN_DIM])
    _slab_store(x1_ref, _merge_project(x, attn_o, y_conv, gab[:, 0:D_MODEL], gab[:, D_MODEL:], w_co_ref, w_o_ref,
                                       lng_ref, lnb_ref, alpha))


def _const_spec(shape):
    nd = len(shape)
    return pl.BlockSpec(shape, lambda *_: (0,) * nd, pipeline_mode=pl.Buffered(1))


def _mixer_prompt(x, w_in, w_ao, w_co, w_o, conv_w, bias_ab, sink_ab, ln_g, ln_b, *, alpha, tq, batch, seq,
                  slab_shape, base=None):
    b, s = batch, seq
    n_chunks, chunk_rows, _ = slab_shape
    slab_in = x.ndim == 3 and x.shape == slab_shape
    steps_per_chunk = (b * s // n_chunks) // tq
    assert (b * s) % (n_chunks * tq) == 0 and s % tq == 0

    def slab_map(bi, i):
        g = bi * (s // tq) + i
        return (g // steps_per_chunk, g % steps_per_chunk, 0)

    slab_spec = pl.BlockSpec((None, tq * SLAB_ROWS, LANES), slab_map)
    kernel = functools.partial(_mixer_prompt_kernel, alpha=alpha, tq=tq, slab_in=slab_in)
    return pl.pallas_call(
        kernel,
        grid=(b, s // tq),
        in_specs=[
            slab_spec if slab_in else pl.BlockSpec((1, tq, D_MODEL), lambda bi, i: (bi, i, 0)),
            _const_spec(w_in.shape), _const_spec(w_ao.shape), _const_spec(w_co.shape), _const_spec(w_o.shape),
            _const_spec(conv_w.shape), _const_spec(bias_ab.shape), _const_spec(sink_ab.shape),
            _const_spec(ln_g.shape), _const_spec(ln_b.shape),
            pl.BlockSpec(memory_space=pl.ANY),
        ],
        out_specs=[
            slab_spec,
            pl.BlockSpec((1, WINDOW, KV_DIM), lambda bi, i: (bi, 0, 0)),
            pl.BlockSpec((1, WINDOW, KV_DIM), lambda bi, i: (bi, 0, 0)),
            pl.BlockSpec((1, CONV_K - 1, CONV_DIM), lambda bi, i: (bi, 0, 0)),
        ],
        out_shape=[
            jax.ShapeDtypeStruct(slab_shape, F32),
            jax.ShapeDtypeStruct((b, WINDOW, KV_DIM), F32),
            jax.ShapeDtypeStruct((b, WINDOW, KV_DIM), F32),
            jax.ShapeDtypeStruct((b, CONV_K - 1, CONV_DIM), F32),
        ],
        scratch_shapes=[
            pltpu.VMEM((N_KV_HEADS, ATTN_BLOCK + tq, LANES), BF16),
            pltpu.VMEM((N_KV_HEADS, ATTN_BLOCK + tq, LANES), BF16),
            pltpu.VMEM((N_KV_HEADS, LANES, ATTN_BLOCK + tq), BF16),
            pltpu.VMEM((N_KV_HEADS, LANES, ATTN_BLOCK + tq), BF16),
            pltpu.VMEM((tq // ATTN_BLOCK, N_KV_HEADS, 2, 2 * ATTN_BLOCK, 2 * ATTN_BLOCK), F32),
            pltpu.VMEM((tq // ATTN_BLOCK, N_KV_HEADS, 2, 2 * ATTN_BLOCK, 2 * ATTN_BLOCK), BF16),
            pltpu.VMEM((Q_DIM, tq), BF16),
            pltpu.VMEM((tq + CONV_PAD, CONV_DIM), F32),
        ],
        compiler_params=pltpu.CompilerParams(
            dimension_semantics=("arbitrary", "arbitrary"), vmem_limit_bytes=VMEM_LIMIT),
        input_output_aliases={0 if slab_in else 10: 0},
        name="mixer_prompt",
    )(x, w_in, w_ao, w_co, w_o, conv_w, bias_ab, sink_ab, ln_g, ln_b,
      jnp.zeros((1, SLAB_ROWS, LANES), F32) if slab_in else base)


def _proj_kernel(x_ref, w_ref, o_ref):
    o_ref[...] = _dot(x_ref[...].astype(BF16), w_ref[...])


def _sample_proj(x, w_in, *, tn):
    m, d = x.shape
    n = w_in.shape[1]
    return pl.pallas_call(
        _proj_kernel,
        grid=(n // tn,),
        in_specs=[pl.BlockSpec((m, d), lambda j: (0, 0)), pl.BlockSpec((d, tn), lambda j: (0, j))],
        out_specs=pl.BlockSpec((m, tn), lambda j: (0, j)),
        out_shape=jax.ShapeDtypeStruct((m, n), F32),
        compiler_params=pltpu.CompilerParams(dimension_semantics=("arbitrary",), vmem_limit_bytes=VMEM_LIMIT),
        name="sample_proj",
    )(x, w_in)


def _sample_attn_kernel(q4_ref, knew_ref, vnew_ref, ck_ref, cv_ref, bias_ref, sink_ref, hmask_ref,
                        nk_ref, nv_ref, ag_ref):
    bt = ck_ref.shape[0]
    win = ck_ref.shape[1]
    scale = HEAD_DIM ** -0.5
    row = lax.broadcasted_iota(jnp.int32, (win, KV_DIM), 0)
    last = row == win - 1
    hmask = hmask_ref[...]
    for b in range(bt):
        kb = jnp.where(last, knew_ref[b:b + 1, :], pltpu.roll(ck_ref[b], win - 1, 0))
        vb = jnp.where(last, vnew_ref[b:b + 1, :], pltpu.roll(cv_ref[b], win - 1, 0))
        nk_ref[b] = kb
        nv_ref[b] = vb
        q4 = q4_ref[b]
        qm = (jnp.concatenate([q4] * N_KV_HEADS, axis=0) * hmask).astype(BF16)
        s = _dot_nt(qm, kb.astype(BF16)) * scale + bias_ref[...]
        p = _sink_softmax(s, sink_ref[...]).astype(BF16)
        o = _dot(p, vb.astype(BF16)) * hmask
        o4 = o[0:GROUP]
        for h in range(1, N_KV_HEADS):
            o4 = o4 + o[h * GROUP:(h + 1) * GROUP]
        ag_ref[b] = o4


def _sample_attn(q4, k_new, v_new, cache_k, cache_v, bias_s, sink_col, hmask, *, layer, bt):
    _, nb, win, kvd = cache_k.shape
    return pl.pallas_call(
        _sample_attn_kernel,
        grid=(nb // bt,),
        in_specs=[
            pl.BlockSpec((bt, GROUP, kvd), lambda i: (i, 0, 0)),
            pl.BlockSpec((bt, kvd), lambda i: (i, 0)),
            pl.BlockSpec((bt, kvd), lambda i: (i, 0)),
            pl.BlockSpec((None, bt, win, kvd), lambda i: (layer, i, 0, 0)),
            pl.BlockSpec((None, bt, win, kvd), lambda i: (layer, i, 0, 0)),
            pl.BlockSpec(bias_s.shape, lambda i: (0, 0)),
            pl.BlockSpec(sink_col.shape, lambda i: (0, 0)),
            pl.BlockSpec(hmask.shape, lambda i: (0, 0)),
        ],
        out_specs=[
            pl.BlockSpec((bt, win, kvd), lambda i: (i, 0, 0)),
            pl.BlockSpec((bt, win, kvd), lambda i: (i, 0, 0)),
            pl.BlockSpec((bt, GROUP, kvd), lambda i: (i, 0, 0)),
        ],
        out_shape=[
            jax.ShapeDtypeStruct((nb, win, kvd), F32),
            jax.ShapeDtypeStruct((nb, win, kvd), F32),
            jax.ShapeDtypeStruct((nb, GROUP, kvd), F32),
        ],
        compiler_params=pltpu.CompilerParams(dimension_semantics=("arbitrary",), vmem_limit_bytes=VMEM_LIMIT),
        name="sample_attn",
    )(q4, k_new, v_new, cache_k, cache_v, bias_s, sink_col, hmask)


def _sample_post_kernel(x_ref, att_ref, proj_ref, st_ref, convw_ref, w_ao_ref, w_co_ref, w_o_ref,
                        lng_ref, lnb_ref, x1_ref, u_ref, *, alpha):
    attn_o = _dot(att_ref[...].astype(BF16), w_ao_ref[...])
    u = proj_ref[:, OFF_C:OFF_H] * proj_ref[:, OFF_H:OFF_GA]
    cw = convw_ref[...]
    y = cw[0:1] * st_ref[0] + cw[1:2] * st_ref[1] + cw[2:3] * u
    u_ref[...] = u
    y_conv = proj_ref[:, OFF_B:OFF_C] * y
    x1_ref[...] = _merge_project(x_ref[...], attn_o, y_conv, proj_ref[:, OFF_GA:OFF_GB], proj_ref[:, OFF_GB:IN_DIM],
                                 w_co_ref, w_o_ref, lng_ref, lnb_ref, alpha)


def _sample_post(x, att, proj, state, conv_w, w_ao, w_co, w_o, ln_g, ln_b, *, alpha):
    m, d = x.shape
    kernel = functools.partial(_sample_post_kernel, alpha=alpha)
    return pl.pallas_call(
        kernel,
        out_shape=[jax.ShapeDtypeStruct((m, d), F32), jax.ShapeDtypeStruct((m, CONV_DIM), F32)],
        compiler_params=pltpu.CompilerParams(vmem_limit_bytes=VMEM_LIMIT),
        name="sample_post",
    )(x, att, proj, state, conv_w, w_ao, w_co, w_o, ln_g, ln_b)


def _first_max(cur, ids, axes, big):
    m = cur
    for ax in axes:
        m = jnp.max(m, axis=ax, keepdims=True)
    idx = jnp.where(cur == m, ids, big)
    for ax in axes:
        idx = jnp.min(idx, axis=ax, keepdims=True)
    return m, idx


def _router_kernel(x_ref, rwt_ref, rb_ref, eidx_ref, gate_ref, *, tm, slab_in):
    x = _slab_load(x_ref, tm) if slab_in else x_ref[...]
    logits_t = _dot_nt(rwt_ref[...], x.astype(BF16))
    scores = jax.nn.sigmoid(logits_t)
    sel = scores + rb_ref[...]
    shape3 = (N_GROUPS, GROUP_SIZE, tm)
    sel3 = sel.reshape(shape3)
    scores3 = scores.reshape(shape3)
    member = lax.broadcasted_iota(jnp.int32, shape3, 1)
    m1, i1 = _first_max(sel3, member, (1,), GROUP_SIZE)
    m2 = jnp.max(jnp.where(member == i1, -jnp.inf, sel3), axis=1, keepdims=True)
    gscore = m1 + m2
    gid = lax.broadcasted_iota(jnp.int32, gscore.shape, 0)
    gsel = jnp.zeros(gscore.shape, jnp.bool_)
    for _ in range(TOPK_GROUPS):
        _, gi = _first_max(gscore, gid, (0,), N_GROUPS)
        hit = gid == gi
        gsel = jnp.logical_or(gsel, hit)
        gscore = jnp.where(hit, -jnp.inf, gscore)
    eid = lax.broadcasted_iota(jnp.int32, shape3, 0) * GROUP_SIZE + member
    cur = jnp.where(gsel, sel3, -jnp.inf)
    ids, ws = [], []
    for _ in range(TOP_K):
        _, ei = _first_max(cur, eid, (1, 0), N_EXPERTS)
        hit = eid == ei
        sc = jnp.sum(jnp.sum(jnp.where(hit, scores3, 0.0), axis=1, keepdims=True), axis=0, keepdims=True)
        ids.append(ei[0])
        ws.append(sc[0])
        cur = jnp.where(hit, -jnp.inf, cur)
    w = jnp.concatenate(ws, axis=0)
    tot = jnp.sum(w, axis=0, keepdims=True)
    eidx_ref[...] = jnp.concatenate(ids, axis=0)
    gate_ref[...] = w / tot * ROUTED_SCALE


def _router(x, rw_t, rb_col, *, tm, tokens=None):
    slab_in = x.ndim == 3
    if slab_in:
        t = tokens
        steps_per_chunk = (t // x.shape[0]) // tm
        assert t % (x.shape[0] * tm) == 0
        x_spec = pl.BlockSpec((None, tm * SLAB_ROWS, LANES), lambda i: (i // steps_per_chunk, i % steps_per_chunk, 0))
    else:
        t = x.shape[0]
        x_spec = pl.BlockSpec((tm, D_MODEL), lambda i: (i, 0))
    return pl.pallas_call(
        functools.partial(_router_kernel, tm=tm, slab_in=slab_in),
        grid=(t // tm,),
        in_specs=[
            x_spec,
            pl.BlockSpec(rw_t.shape, lambda i: (0, 0)),
            pl.BlockSpec(rb_col.shape, lambda i: (0, 0)),
        ],
        out_specs=[pl.BlockSpec((TOP_K, tm), lambda i: (0, i)), pl.BlockSpec((TOP_K, tm), lambda i: (0, i))],
        out_shape=[jax.ShapeDtypeStruct((TOP_K, t), jnp.int32), jax.ShapeDtypeStruct((TOP_K, t), F32)],
        compiler_params=pltpu.CompilerParams(dimension_semantics=("arbitrary",), vmem_limit_bytes=VMEM_LIMIT),
        name="router",
    )(x, rw_t, rb_col)


ROW_TILE = 288
TILE_PITCH = ROW_TILE + 1
GATE_LANES = -(-ROW_TILE // LANES) * LANES
SPARE_TOKENS = 8
PAIR_BITS = 16
FLAG_FIRST, FLAG_LAST, FLAG_NEW_EXPERT, FLAG_VALID, FLAG_HAS_NEXT = 1, 2, 4, 8, 16
SCATTER_BATCH = 4


PREV, CUR, NEXT = 0, 1, 2


def _moe_routed_kernel(ce_ref, flags_ref, used_ref, nexte_ref, idx_ref, gate_ref, x_ref, wg_ref, wu_ref,
                       wd_ref, sg_ref, su_ref, sd_ref, lng_ref, lnb_ref, o_ref, wgu_ref, wdb_ref,
                       gat0_ref, gat1_ref, res0_ref, res1_ref, wsg_ref, wsu_ref, wsd_ref, wsem_ref, *,
                       layer, alpha, chunk_tokens, sub_tokens):
    del used_ref
    step = pl.program_id(0)
    flags = flags_ref[step]
    odd = (step & 1) == 1
    nchunk = D_MODEL // LANES
    ne, ff = wg_ref.shape[1], wg_ref.shape[3]

    def slab_row(which, r):
        return pl.multiple_of(idx_ref[0, which, r], 8)

    def gather_row(which, gat_ref, r):
        gat_ref[pl.ds(r, nchunk, stride=TILE_PITCH), :] = x_ref[pl.ds(slab_row(which, r), 8), :]

    def scatter_rows(which, res_ref, rows):
        dst = [slab_row(which, r) for r in rows]
        acc = [o_ref[pl.ds(d, 8), :] + res_ref[pl.ds(r, nchunk, stride=TILE_PITCH), :] for d, r in zip(dst, rows)]
        for d, a in zip(dst, acc):
            o_ref[pl.ds(d, 8), :] = a

    def expert_mlp(gat_ref, res_ref):
        lhs = jnp.concatenate([gat_ref[j * TILE_PITCH:j * TILE_PITCH + ROW_TILE, :] for j in range(nchunk)],
                              axis=1).astype(BF16)
        h = _dot(lhs, wgu_ref[...])
        gate_col = jnp.broadcast_to(gate_ref[0], (LANES, GATE_LANES)).T[0:ROW_TILE]
        hid = jax.nn.silu(h[:, 0:ff]) * h[:, ff:2 * ff] * jnp.concatenate([gate_col] * (ff // LANES), axis=1)
        y = _dot(hid.astype(BF16), wdb_ref[...])
        for j in range(nchunk):
            res_ref[j * TILE_PITCH:j * TILE_PITCH + ROW_TILE, :] = y[:, j * LANES:(j + 1) * LANES]

    def by_parity(fn):
        @pl.when(jnp.logical_not(odd))
        def _even():
            fn(gat0_ref, gat1_ref, res0_ref, res1_ref)

        @pl.when(odd)
        def _odd():
            fn(gat1_ref, gat0_ref, res1_ref, res0_ref)

    @pl.when((flags & FLAG_FIRST) != 0)
    def _start_chunk():
        o_ref[...] = jnp.zeros(o_ref.shape, F32)

        def start(gat_cur, gat_other, res_cur, res_other):
            res_other[...] = jnp.zeros(res_other.shape, F32)

            def body(r, carry):
                gather_row(CUR, gat_cur, r)
                return carry

            lax.fori_loop(0, ROW_TILE, body, 0)

        by_parity(start)

    def weight_copies(e):
        return [pltpu.make_async_copy(src.at[layer, e], dst, wsem_ref.at[k])
                for k, (src, dst) in enumerate(((wg_ref, wsg_ref), (wu_ref, wsu_ref), (wd_ref, wsd_ref)))]

    @pl.when(step == 0)
    def _first_fetch():
        for cp in weight_copies(ce_ref[0] % ne):
            cp.start()

    @pl.when((flags & FLAG_NEW_EXPERT) != 0)
    def _next_expert():
        for cp in weight_copies(0):
            cp.wait()
        wgu_ref[:, 0:ff] = wsg_ref[...].astype(BF16)
        wgu_ref[:, ff:2 * ff] = wsu_ref[...].astype(BF16)
        wdb_ref[...] = wsd_ref[...].astype(BF16)

        @pl.when((flags & FLAG_HAS_NEXT) != 0)
        def _prefetch():
            for cp in weight_copies(nexte_ref[step]):
                cp.start()

    @pl.when((flags & FLAG_VALID) != 0)
    def _tile():
        def main(gat_cur, gat_other, res_cur, res_other):
            for r0 in range(0, ROW_TILE, SCATTER_BATCH):
                scatter_rows(PREV, res_other, range(r0, r0 + SCATTER_BATCH))
            for r in range(ROW_TILE):
                gather_row(NEXT, gat_other, r)
            expert_mlp(gat_cur, res_cur)

        by_parity(main)

    @pl.when((flags & FLAG_LAST) != 0)
    def _finish():
        def flush(gat_cur, gat_other, res_cur, res_other):
            def body(r, carry):
                scatter_rows(CUR, res_cur, [r])
                return carry

            lax.fori_loop(0, ROW_TILE, body, 0)

        by_parity(flush)

        def body(s, carry):
            base = pl.multiple_of(s * (sub_tokens * 8), 8)

            def rows_2d(ref):
                return jnp.concatenate([ref[pl.ds(base + j, sub_tokens, stride=8), :] for j in range(nchunk)],
                                       axis=1)

            x2 = rows_2d(x_ref)
            xb = x2.astype(BF16)
            hs = jax.nn.silu(_dot(xb, sg_ref[...])) * _dot(xb, su_ref[...])
            ffn = rows_2d(o_ref) + _dot(hs.astype(BF16), sd_ref[...])
            res = _layer_norm(alpha * x2 + ffn, lng_ref[...], lnb_ref[...])
            for j in range(nchunk):
                o_ref[pl.ds(base + j, sub_tokens, stride=8), :] = res[:, j * LANES:(j + 1) * LANES]
            return carry

        lax.fori_loop(0, chunk_tokens // sub_tokens, body, 0)


def _moe_routed(tile_ce, tile_flags, n_used, next_expert, row_idx, gates, x_tm, wg, wu, wd, sg, su, sd, ln_g,
                ln_b, *, layer, alpha, chunk_tokens, sub_tokens):
    n_chunks, chunk_rows, _ = x_tm.shape
    n_tiles = row_idx.shape[0]
    _, ne, d, ff = wg.shape
    kernel = functools.partial(_moe_routed_kernel, layer=layer, alpha=alpha, chunk_tokens=chunk_tokens,
                               sub_tokens=sub_tokens)

    def tile_map(i, ce, fl, used, nxt):
        return (jnp.minimum(i, used[0] - 1), 0, 0)

    def chunk_map(i, ce, fl, used, nxt):
        return (ce[i] // ne, 0, 0)

    def const2(i, ce, fl, used, nxt):
        return (0, 0)

    grid_spec = pltpu.PrefetchScalarGridSpec(
        num_scalar_prefetch=4,
        grid=(n_tiles,),
        in_specs=[
            pl.BlockSpec((1, 3, ROW_TILE), tile_map, memory_space=pltpu.SMEM),
            pl.BlockSpec((1, 1, GATE_LANES), tile_map),
            pl.BlockSpec((None, chunk_rows, LANES), chunk_map, pipeline_mode=pl.Buffered(1)),
            pl.BlockSpec(memory_space=pl.ANY),
            pl.BlockSpec(memory_space=pl.ANY),
            pl.BlockSpec(memory_space=pl.ANY),
            pl.BlockSpec(sg.shape, const2, pipeline_mode=pl.Buffered(1)),
            pl.BlockSpec(su.shape, const2, pipeline_mode=pl.Buffered(1)),
            pl.BlockSpec(sd.shape, const2, pipeline_mode=pl.Buffered(1)),
            pl.BlockSpec(ln_g.shape, const2),
            pl.BlockSpec(ln_b.shape, const2),
        ],
        out_specs=pl.BlockSpec((None, chunk_rows, LANES), chunk_map, pipeline_mode=pl.Buffered(1)),
        scratch_shapes=[
            pltpu.VMEM((d, 2 * ff), BF16),
            pltpu.VMEM((ff, d), BF16),
            pltpu.VMEM((8 * TILE_PITCH, LANES), F32),
            pltpu.VMEM((8 * TILE_PITCH, LANES), F32),
            pltpu.VMEM((8 * TILE_PITCH, LANES), F32),
            pltpu.VMEM((8 * TILE_PITCH, LANES), F32),
            pltpu.VMEM((d, ff), F32),
            pltpu.VMEM((d, ff), F32),
            pltpu.VMEM((ff, d), F32),
            pltpu.SemaphoreType.DMA((3,)),
        ],
    )
    return pl.pallas_call(
        kernel,
        grid_spec=grid_spec,
        out_shape=jax.ShapeDtypeStruct(x_tm.shape, F32),
        compiler_params=pltpu.CompilerParams(dimension_semantics=("arbitrary",), vmem_limit_bytes=VMEM_LIMIT),
        name="moe_routed",
    )(tile_ce, tile_flags, n_used, next_expert, row_idx, gates, x_tm, wg, wu, wd, sg, su, sd, ln_g, ln_b)


MOE_CHUNKS = 4
MOE_SUB_MAX = 512


def _moe_tiling(n_prompt, n_sample):
    assert n_prompt % (MOE_CHUNKS * 8) == 0 and n_sample % (MOE_CHUNKS * 8) == 0
    chunk_tokens = (n_prompt + n_sample) // MOE_CHUNKS
    assert (chunk_tokens + 1) * TOP_K <= 1 << PAIR_BITS
    sub_tokens = max(s for s in range(8, MOE_SUB_MAX + 1, 8) if chunk_tokens % s == 0)
    return MOE_CHUNKS, chunk_tokens, sub_tokens


def _chunked(prompt, sample, n_chunks):
    return jnp.concatenate([prompt.reshape((n_chunks, -1) + prompt.shape[1:]),
                            sample.reshape((n_chunks, -1) + sample.shape[1:])], axis=1)


def _route_plan(eidx_p, eidx_s, gate_p, gate_s, *, chunk_tokens, n_chunks):
    n_seg = n_chunks * N_EXPERTS
    eidx = _chunked(eidx_p.T, eidx_s.T, n_chunks)
    gate = _chunked(gate_p.T, gate_s.T, n_chunks)
    seg = jnp.arange(n_chunks, dtype=jnp.int32)[:, None, None] * N_EXPERTS + eidx
    pair = jnp.arange(chunk_tokens * TOP_K, dtype=jnp.int32).reshape(1, chunk_tokens, TOP_K)
    keys_real = ((seg << PAIR_BITS) | pair).reshape(-1)
    counts = jnp.sum(eidx[..., None] == jnp.arange(N_EXPERTS, dtype=jnp.int32), axis=(1, 2),
                     dtype=jnp.int32).reshape(n_seg)
    n_pad = (-counts) % ROW_TILE
    slot = jnp.arange(ROW_TILE - 1, dtype=jnp.int32)
    int_max = jnp.iinfo(jnp.int32).max
    pad_pair = (1 << PAIR_BITS) - 1
    keys_pad = jnp.where(slot[None, :] < n_pad[:, None],
                         (jnp.arange(n_seg, dtype=jnp.int32)[:, None] << PAIR_BITS) | pad_pair, int_max)
    n_fill = (-(keys_real.size + keys_pad.size)) % ROW_TILE
    keys = jnp.concatenate([keys_real, keys_pad.reshape(-1), jnp.full((n_fill,), int_max, jnp.int32)])
    vals = jnp.concatenate([gate.reshape(-1), jnp.zeros((keys_pad.size + n_fill,), F32)])
    n_tiles = keys.size // ROW_TILE
    keys, vals = lax.sort((keys, vals), num_keys=1)
    keys = keys.reshape(n_tiles, ROW_TILE)
    head = keys[:, 0]
    valid = head != int_max
    tile_ce = jnp.minimum(head >> PAIR_BITS, n_seg - 1)
    tile_c = tile_ce // N_EXPERTS
    prev_ce = jnp.concatenate([jnp.full((1,), -N_EXPERTS, jnp.int32), tile_ce[:-1]])
    next_c = jnp.concatenate([tile_c[1:], jnp.full((1,), -1, jnp.int32)])
    next_valid = jnp.concatenate([valid[1:], jnp.zeros((1,), jnp.bool_)])
    first = valid & (tile_c != prev_ce // N_EXPERTS)
    last = valid & (~next_valid | (next_c != tile_c))
    new_expert = valid & (tile_ce != prev_ce)
    tile_id = jnp.arange(n_tiles, dtype=jnp.int32)
    change_at = jnp.where(new_expert, tile_id, n_tiles)
    next_change = jnp.concatenate([lax.cummin(change_at, reverse=True)[1:], jnp.full((1,), n_tiles, jnp.int32)])
    has_next = new_expert & (next_change < n_tiles)
    next_expert = tile_ce[jnp.minimum(next_change, n_tiles - 1)] % N_EXPERTS
    flags = (first * FLAG_FIRST + last * FLAG_LAST + new_expert * FLAG_NEW_EXPERT + valid * FLAG_VALID
             + has_next * FLAG_HAS_NEXT)
    row_idx = jnp.minimum((keys & pad_pair) // TOP_K, chunk_tokens) * 8
    row_idx3 = jnp.stack([jnp.concatenate([row_idx[:1], row_idx[:-1]]), row_idx,
                          jnp.concatenate([row_idx[1:], row_idx[-1:]])], axis=1)
    n_used = jnp.sum(valid, dtype=jnp.int32).reshape(1)
    gates = jnp.pad(vals.reshape(n_tiles, 1, ROW_TILE), ((0, 0), (0, 0), (0, GATE_LANES - ROW_TILE)))
    return tile_ce, flags.astype(jnp.int32), n_used, next_expert.astype(jnp.int32), row_idx3, gates


def _t5_bucket(dist):
    n = jnp.maximum(dist, 0)
    max_exact = N_BUCKETS // 2
    large = max_exact + (jnp.log(jnp.maximum(n, 1).astype(F32) / max_exact)
                         / math.log(MAX_DISTANCE / max_exact) * (N_BUCKETS - max_exact)).astype(jnp.int32)
    large = jnp.minimum(large, N_BUCKETS - 1)
    return jnp.where(n < max_exact, n, large)


def _bias_lookup(rel_bias, bucket):
    onehot = (bucket[..., None] == jnp.arange(N_BUCKETS, dtype=bucket.dtype)).astype(F32)
    return jnp.einsum("...b,bh->h...", onehot, rel_bias.astype(F32), precision=lax.Precision.HIGHEST)


def _bias_tables(rel_bias, win):
    qi = jnp.arange(ATTN_BLOCK)[:, None]
    ki = jnp.arange(2 * ATTN_BLOCK)[None, :]
    dist = qi + ATTN_BLOCK - ki
    valid = (dist >= 0) & (dist < WINDOW)
    bias = _bias_lookup(rel_bias, _t5_bucket(dist))
    bias = jnp.where(valid[None], bias, NEG).reshape(N_KV_HEADS, GROUP, ATTN_BLOCK, 2 * ATTN_BLOCK)
    bias_ab = jnp.stack([jnp.concatenate([bias[:, t], bias[:, t + 2]], axis=1) for t in range(2)], axis=1)
    dist_s = (win - 1) - jnp.arange(win)
    bias_s = _bias_lookup(rel_bias, _t5_bucket(dist_s))
    return bias_ab, bias_s


def _sink_tables(sink):
    s = sink.astype(F32).reshape(N_KV_HEADS, GROUP)
    rows = [jnp.concatenate([jnp.broadcast_to(s[:, t, None], (N_KV_HEADS, ATTN_BLOCK)),
                             jnp.broadcast_to(s[:, t + 2, None], (N_KV_HEADS, ATTN_BLOCK))], axis=1)
            for t in range(2)]
    return jnp.stack(rows, axis=1)[..., None], sink.astype(F32)[:, None]


def kernel(x_prompt, x_sample, cache_k_win, cache_v_win, state_conv, rel_bias, w_in, attn_sink, conv_w,
           w_attn_out, w_conv_out, w_out, ln1_g, ln1_b, router_w, router_bias, exp_w_gate, exp_w_up,
           exp_w_down, shared_w_gate, shared_w_up, shared_w_down, ln2_g, ln2_b):
    depth = w_in.shape[0]
    alpha = (2 * depth) ** 0.25
    nb, seq, d = x_prompt.shape
    nd = x_sample.shape[0]
    win = cache_k_win.shape[2]
    assert x_sample.shape[1] == 1 and win == WINDOW and seq % 512 == 0

    n_prompt = nb * seq
    n_chunks, chunk_tokens, sub_tokens = _moe_tiling(n_prompt, nd)
    prompt_rows = n_prompt // n_chunks * SLAB_ROWS
    slab_shape = (n_chunks, (chunk_tokens + SPARE_TOKENS) * SLAB_ROWS, LANES)

    bias_ab, bias_s = _bias_tables(rel_bias, win)
    hmask = (jnp.arange(KV_DIM)[None, :] // HEAD_DIM == jnp.arange(N_HEADS)[:, None] // GROUP).astype(F32)

    yp = x_prompt
    ys = x_sample.reshape(nd, d)
    outs = [[] for _ in range(6)]
    for l in range(depth):
        w_in_b = w_in[l].astype(BF16)
        w_ao_b = w_attn_out[l].astype(BF16)
        w_co_b = w_conv_out[l].astype(BF16)
        w_o_b = w_out[l].astype(BF16)
        g1, b1 = ln1_g[l][None, :], ln1_b[l][None, :]
        g2, b2 = ln2_g[l][None, :], ln2_b[l][None, :]
        sink_ab, sink_col = _sink_tables(attn_sink[l])

        slab, kp, vp, cp = _mixer_prompt(yp, w_in_b, w_attn_out[l].T.astype(BF16), w_co_b, w_o_b, conv_w[l],
                                         jnp.swapaxes(bias_ab, -1, -2), jnp.swapaxes(sink_ab, -1, -2), g1, b1,
                                         alpha=alpha, tq=512, batch=nb, seq=seq, slab_shape=slab_shape,
                                         base=jnp.zeros(slab_shape, F32) if l == 0 else None)

        proj = _sample_proj(ys, w_in_b, tn=IN_DIM // 4)
        q4 = proj[:, :Q_DIM].reshape(nd, N_KV_HEADS, GROUP, HEAD_DIM).transpose(0, 2, 1, 3).reshape(nd, GROUP, KV_DIM)
        ksn, vsn, ag = _sample_attn(q4, proj[:, OFF_K:OFF_V], proj[:, OFF_V:OFF_B],
                                    cache_k_win.reshape(depth, nd, win, KV_DIM),
                                    cache_v_win.reshape(depth, nd, win, KV_DIM),
                                    bias_s, sink_col, hmask, layer=l, bt=16)
        att = ag.reshape(nd, GROUP, N_KV_HEADS, HEAD_DIM).transpose(0, 2, 1, 3).reshape(nd, Q_DIM)
        state_t = jnp.swapaxes(state_conv[l], 0, 1)
        ys, us = _sample_post(ys, att, proj, state_t, conv_w[l], w_ao_b, w_co_b, w_o_b, g1, b1, alpha=alpha)

        outs[0].append(kp.reshape(nb, WINDOW, N_KV_HEADS, HEAD_DIM))
        outs[1].append(vp.reshape(nb, WINDOW, N_KV_HEADS, HEAD_DIM))
        outs[2].append(cp)
        outs[3].append(ksn.reshape(nd, win, N_KV_HEADS, HEAD_DIM))
        outs[4].append(vsn.reshape(nd, win, N_KV_HEADS, HEAD_DIM))
        outs[5].append(jnp.concatenate([state_conv[l][:, 1:], us[:, None, :]], axis=1))

        rw_t = router_w[l].T.astype(BF16)
        rb_col = router_bias[l].astype(F32)[:, None]
        sg, su, sd = (shared_w_gate[l].astype(BF16), shared_w_up[l].astype(BF16), shared_w_down[l].astype(BF16))
        tail = jnp.concatenate([ys.reshape(n_chunks, -1, LANES),
                                jnp.zeros((n_chunks, SPARE_TOKENS * SLAB_ROWS, LANES), F32)], axis=1)
        slab = lax.dynamic_update_slice(slab, tail, (0, prompt_rows, 0))
        ep, wp = _router(slab, rw_t, rb_col, tm=512, tokens=n_prompt)
        es, ws = _router(ys, rw_t, rb_col, tm=nd)
        plan = _route_plan(ep, es, wp, ws, chunk_tokens=chunk_tokens, n_chunks=n_chunks)
        yp = _moe_routed(*plan, slab, exp_w_gate, exp_w_up, exp_w_down, sg, su, sd, g2, b2,
                         layer=l, alpha=alpha, chunk_tokens=chunk_tokens, sub_tokens=sub_tokens)
        ys = yp[:, prompt_rows:chunk_tokens * SLAB_ROWS].reshape(nd, d)

    y_prompt = yp[:, :prompt_rows].reshape(nb, seq, d)
    return (y_prompt, ys.reshape(nd, 1, d)) + tuple(jnp.stack(o) for o in outs)
```

```python
import functools
import math

import jax
import jax.numpy as jnp
from jax import lax
from jax.experimental import pallas as pl
from jax.experimental.pallas import tpu as pltpu

D_MODEL = 1024
N_HEADS = 16
N_KV_HEADS = 4
HEAD_DIM = 64
GROUP = N_HEADS // N_KV_HEADS
WINDOW = 128
ATTN_BLOCK = 128
N_BUCKETS = 32
MAX_DISTANCE = 128
CONV_DIM = 1024
CONV_K = 3
N_EXPERTS = 64
TOP_K = 8
N_GROUPS = 8
TOPK_GROUPS = 4
GROUP_SIZE = N_EXPERTS // N_GROUPS
EXPERT_FF = 256
ROUTED_SCALE = 2.5
LN_EPS = 1e-5
NEG = -1e30

Q_DIM = N_HEADS * HEAD_DIM
KV_DIM = N_KV_HEADS * HEAD_DIM
OFF_K = Q_DIM
OFF_V = OFF_K + KV_DIM
OFF_B = OFF_V + KV_DIM
OFF_C = OFF_B + CONV_DIM
OFF_H = OFF_C + CONV_DIM
OFF_GA = OFF_H + CONV_DIM
OFF_GB = OFF_GA + D_MODEL
IN_DIM = OFF_GB + D_MODEL

LANES = 128
CONV_PAD = 8
VMEM_LIMIT = 60 * 1024 * 1024

BF16 = jnp.bfloat16
F32 = jnp.float32


def _dot(a, b):
    return jnp.dot(a, b, preferred_element_type=F32)


def _dot_nt(a, b):
    return lax.dot_general(a, b, (((1,), (1,)), ((), ())), preferred_element_type=F32)


def _layer_norm(z, g, b):
    mu = jnp.mean(z, axis=-1, keepdims=True)
    d = z - mu
    var = jnp.mean(d * d, axis=-1, keepdims=True)
    return d * lax.rsqrt(var + LN_EPS) * g + b


def _sink_softmax(s, sink, axis=-1):
    m = jnp.maximum(jnp.max(s, axis=axis, keepdims=True), sink)
    e = jnp.exp(s - m)
    den = jnp.sum(e, axis=axis, keepdims=True) + jnp.exp(sink - m)
    return e * (1.0 / den)


SLAB_ROWS = D_MODEL // LANES


def _slab_load(ref, tokens, base=0):
    return jnp.concatenate([ref[pl.ds(base + j, tokens, stride=SLAB_ROWS), :] for j in range(SLAB_ROWS)], axis=1)


def _slab_store(ref, val, base=0):
    tokens = val.shape[0]
    for j in range(SLAB_ROWS):
        ref[pl.ds(base + j, tokens, stride=SLAB_ROWS), :] = val[:, j * LANES:(j + 1) * LANES]


def _merge_project(x, attn_o, y_conv, g_a, g_b, w_co_ref, w_o_ref, lng_ref, lnb_ref, alpha):
    merged = jax.nn.sigmoid(g_a) * attn_o + jax.nn.sigmoid(g_b) * _dot(y_conv.astype(BF16), w_co_ref[...])
    out = _dot(merged.astype(BF16), w_o_ref[...])
    return _layer_norm(alpha * x + out, lng_ref[...], lnb_ref[...])


def _mixer_prompt_kernel(x_ref, w_in_ref, w_ao_ref, w_co_ref, w_o_ref, convw_ref, bias_ref, sink_ref,
                         lng_ref, lnb_ref, base_ref,
                         x1_ref, kwin_ref, vwin_ref, conv_ref,
                         ka_ref, kb_ref, vat_ref, vbt_ref, s_ref, p_ref, att_ref, ubuf_ref, *, alpha, tq, slab_in):
    del base_ref
    i = pl.program_id(1)
    nblk = tq // ATTN_BLOCK
    half = LANES // 2
    scale = HEAD_DIM ** -0.5
    assert math.frexp(scale)[0] == 0.5

    @pl.when(i == 0)
    def _init():
        for ref in (ka_ref, kb_ref):
            ref[:, 0:ATTN_BLOCK, :] = jnp.zeros((N_KV_HEADS, ATTN_BLOCK, LANES), BF16)
        for ref in (vat_ref, vbt_ref):
            ref[:, :, 0:ATTN_BLOCK] = jnp.zeros((N_KV_HEADS, LANES, ATTN_BLOCK), BF16)
        ubuf_ref[0:CONV_PAD, :] = jnp.zeros((CONV_PAD, CONV_DIM), F32)

    x = _slab_load(x_ref, tq) if slab_in else x_ref[0]
    xb = x.astype(BF16)
    qkv = _dot(xb, w_in_ref[:, 0:OFF_B])

    lo = lax.broadcasted_iota(jnp.int32, (tq, LANES), 1) < half
    zeros_t = jnp.zeros((half, tq), BF16)
    for c in range(N_KV_HEADS // 2):
        chunk = qkv[:, OFF_K + c * LANES: OFF_K + (c + 1) * LANES]
        c_lo = jnp.where(lo, chunk, 0.0)
        c_hi = jnp.where(lo, 0.0, chunk)
        ka_ref[2 * c, ATTN_BLOCK:, :] = c_lo.astype(BF16)
        kb_ref[2 * c, ATTN_BLOCK:, :] = pltpu.roll(c_lo, half, 1).astype(BF16)
        kb_ref[2 * c + 1, ATTN_BLOCK:, :] = c_hi.astype(BF16)
        ka_ref[2 * c + 1, ATTN_BLOCK:, :] = pltpu.roll(c_hi, half, 1).astype(BF16)
        vt = qkv[:, OFF_V + c * LANES: OFF_V + (c + 1) * LANES].T.astype(BF16)
        vat_ref[2 * c, :, ATTN_BLOCK:] = jnp.concatenate([vt[0:half], zeros_t], axis=0)
        vbt_ref[2 * c, :, ATTN_BLOCK:] = jnp.concatenate([zeros_t, vt[0:half]], axis=0)
        vbt_ref[2 * c + 1, :, ATTN_BLOCK:] = jnp.concatenate([zeros_t, vt[half:]], axis=0)
        vat_ref[2 * c + 1, :, ATTN_BLOCK:] = jnp.concatenate([vt[half:], zeros_t], axis=0)

    key_row = lax.broadcasted_iota(jnp.int32, (2 * ATTN_BLOCK, 2 * ATTN_BLOCK), 0)
    for j in range(nblk):
        rows = slice(j * ATTN_BLOCK, (j + 1) * ATTN_BLOCK)
        keys = slice(j * ATTN_BLOCK, (j + 2) * ATTN_BLOCK)
        for h in range(N_KV_HEADS):
            q0 = h * GROUP * HEAD_DIM
            q2 = (jnp.concatenate([qkv[rows, q0:q0 + LANES], qkv[rows, q0 + LANES:q0 + 2 * LANES]], axis=0)
                  * scale).astype(BF16)
            for t, k_ref in enumerate((ka_ref, kb_ref)):
                s = _dot_nt(k_ref[h, keys, :], q2)
                if j == 0:
                    s = jnp.where(jnp.logical_and(i == 0, key_row < ATTN_BLOCK), NEG, s)
                s_ref[j, h, t] = s
    for h in range(N_KV_HEADS):
        for t in range(2):
            s = s_ref[:, h, t] + bias_ref[h, t][None]
            p_ref[:, h, t] = _sink_softmax(s, sink_ref[h, t][None], axis=1).astype(BF16)
    for j in range(nblk):
        rows = slice(j * ATTN_BLOCK, (j + 1) * ATTN_BLOCK)
        keys = slice(j * ATTN_BLOCK, (j + 2) * ATTN_BLOCK)
        for h in range(N_KV_HEADS):
            q0 = h * GROUP * HEAD_DIM
            o_t = _dot(vat_ref[h, :, keys], p_ref[j, h, 0]) + _dot(vbt_ref[h, :, keys], p_ref[j, h, 1])
            att_ref[q0:q0 + LANES, rows] = o_t[:, 0:ATTN_BLOCK].astype(BF16)
            att_ref[q0 + LANES:q0 + 2 * LANES, rows] = o_t[:, ATTN_BLOCK:].astype(BF16)
    for ref in (ka_ref, kb_ref):
        ref[:, 0:ATTN_BLOCK, :] = ref[:, tq:tq + ATTN_BLOCK, :]
    for ref in (vat_ref, vbt_ref):
        ref[:, :, 0:ATTN_BLOCK] = ref[:, :, tq:tq + ATTN_BLOCK]

    kwin_ref[0] = qkv[tq - WINDOW:tq, OFF_K:OFF_V]
    vwin_ref[0] = qkv[tq - WINDOW:tq, OFF_V:OFF_B]

    attn_o = _dot(w_ao_ref[...], att_ref[...]).T

    bch = _dot(xb, w_in_ref[:, OFF_B:OFF_GA])
    u = bch[:, CONV_DIM:2 * CONV_DIM] * bch[:, 2 * CONV_DIM:3 * CONV_DIM]
    ubuf_ref[CONV_PAD:CONV_PAD + tq, :] = u
    cw = convw_ref[...]
    y = (cw[0:1] * ubuf_ref[CONV_PAD - 2:CONV_PAD - 2 + tq, :]
         + cw[1:2] * ubuf_ref[CONV_PAD - 1:CONV_PAD - 1 + tq, :]
         + cw[2:3] * u)
    conv_ref[0] = ubuf_ref[CONV_PAD + tq - (CONV_K - 1):CONV_PAD + tq, :]
    ubuf_ref[0:CONV_PAD, :] = ubuf_ref[tq:tq + CONV_PAD, :]
    y_conv = bch[:, 0:CONV_DIM] * y

    gab = _dot(xb, w_in_ref[:, OFF_GA:IN_DIM])
    _slab_store(x1_ref, _merge_project(x, attn_o, y_conv, gab[:, 0:D_MODEL], gab[:, D_MODEL:], w_co_ref, w_o_ref,
                                       lng_ref, lnb_ref, alpha))


def _const_spec(shape):
    nd = len(shape)
    return pl.BlockSpec(shape, lambda *_: (0,) * nd, pipeline_mode=pl.Buffered(1))


def _mixer_prompt(x, w_in, w_ao, w_co, w_o, conv_w, bias_ab, sink_ab, ln_g, ln_b, *, alpha, tq, batch, seq,
                  slab_shape, base=None):
    b, s = batch, seq
    n_chunks, chunk_rows, _ = slab_shape
    slab_in = x.ndim == 3 and x.shape == slab_shape
    steps_per_chunk = (b * s // n_chunks) // tq
    assert (b * s) % (n_chunks * tq) == 0 and s % tq == 0

    def slab_map(bi, i):
        g = bi * (s // tq) + i
        return (g // steps_per_chunk, g % steps_per_chunk, 0)

    slab_spec = pl.BlockSpec((None, tq * SLAB_ROWS, LANES), slab_map)
    kernel = functools.partial(_mixer_prompt_kernel, alpha=alpha, tq=tq, slab_in=slab_in)
    return pl.pallas_call(
        kernel,
        grid=(b, s // tq),
        in_specs=[
            slab_spec if slab_in else pl.BlockSpec((1, tq, D_MODEL), lambda bi, i: (bi, i, 0)),
            _const_spec(w_in.shape), _const_spec(w_ao.shape), _const_spec(w_co.shape), _const_spec(w_o.shape),
            _const_spec(conv_w.shape), _const_spec(bias_ab.shape), _const_spec(sink_ab.shape),
            _const_spec(ln_g.shape), _const_spec(ln_b.shape),
            pl.BlockSpec(memory_space=pl.ANY),
        ],
        out_specs=[
            slab_spec,
            pl.BlockSpec((1, WINDOW, KV_DIM), lambda bi, i: (bi, 0, 0)),
            pl.BlockSpec((1, WINDOW, KV_DIM), lambda bi, i: (bi, 0, 0)),
            pl.BlockSpec((1, CONV_K - 1, CONV_DIM), lambda bi, i: (bi, 0, 0)),
        ],
        out_shape=[
            jax.ShapeDtypeStruct(slab_shape, F32),
            jax.ShapeDtypeStruct((b, WINDOW, KV_DIM), F32),
            jax.ShapeDtypeStruct((b, WINDOW, KV_DIM), F32),
            jax.ShapeDtypeStruct((b, CONV_K - 1, CONV_DIM), F32),
        ],
        scratch_shapes=[
            pltpu.VMEM((N_KV_HEADS, ATTN_BLOCK + tq, LANES), BF16),
            pltpu.VMEM((N_KV_HEADS, ATTN_BLOCK + tq, LANES), BF16),
            pltpu.VMEM((N_KV_HEADS, LANES, ATTN_BLOCK + tq), BF16),
            pltpu.VMEM((N_KV_HEADS, LANES, ATTN_BLOCK + tq), BF16),
            pltpu.VMEM((tq // ATTN_BLOCK, N_KV_HEADS, 2, 2 * ATTN_BLOCK, 2 * ATTN_BLOCK), F32),
            pltpu.VMEM((tq // ATTN_BLOCK, N_KV_HEADS, 2, 2 * ATTN_BLOCK, 2 * ATTN_BLOCK), BF16),
            pltpu.VMEM((Q_DIM, tq), BF16),
            pltpu.VMEM((tq + CONV_PAD, CONV_DIM), F32),
        ],
        compiler_params=pltpu.CompilerParams(
            dimension_semantics=("arbitrary", "arbitrary"), vmem_limit_bytes=VMEM_LIMIT),
        input_output_aliases={0 if slab_in else 10: 0},
        name="mixer_prompt",
    )(x, w_in, w_ao, w_co, w_o, conv_w, bias_ab, sink_ab, ln_g, ln_b,
      jnp.zeros((1, SLAB_ROWS, LANES), F32) if slab_in else base)


def _proj_kernel(x_ref, w_ref, o_ref):
    o_ref[...] = _dot(x_ref[...].astype(BF16), w_ref[...])


def _sample_proj(x, w_in, *, tn):
    m, d = x.shape
    n = w_in.shape[1]
    return pl.pallas_call(
        _proj_kernel,
        grid=(n // tn,),
        in_specs=[pl.BlockSpec((m, d), lambda j: (0, 0)), pl.BlockSpec((d, tn), lambda j: (0, j))],
        out_specs=pl.BlockSpec((m, tn), lambda j: (0, j)),
        out_shape=jax.ShapeDtypeStruct((m, n), F32),
        compiler_params=pltpu.CompilerParams(dimension_semantics=("arbitrary",), vmem_limit_bytes=VMEM_LIMIT),
        name="sample_proj",
    )(x, w_in)


def _sample_attn_kernel(q4_ref, knew_ref, vnew_ref, ck_ref, cv_ref, bias_ref, sink_ref, hmask_ref,
                        nk_ref, nv_ref, ag_ref):
    bt = ck_ref.shape[0]
    win = ck_ref.shape[1]
    scale = HEAD_DIM ** -0.5
    row = lax.broadcasted_iota(jnp.int32, (win, KV_DIM), 0)
    last = row == win - 1
    hmask = hmask_ref[...]
    for b in range(bt):
        kb = jnp.where(last, knew_ref[b:b + 1, :], pltpu.roll(ck_ref[b], win - 1, 0))
        vb = jnp.where(last, vnew_ref[b:b + 1, :], pltpu.roll(cv_ref[b], win - 1, 0))
        nk_ref[b] = kb
        nv_ref[b] = vb
        q4 = q4_ref[b]
        qm = (jnp.concatenate([q4] * N_KV_HEADS, axis=0) * hmask).astype(BF16)
        s = _dot_nt(qm, kb.astype(BF16)) * scale + bias_ref[...]
        p = _sink_softmax(s, sink_ref[...]).astype(BF16)
        o = _dot(p, vb.astype(BF16)) * hmask
        o4 = o[0:GROUP]
        for h in range(1, N_KV_HEADS):
            o4 = o4 + o[h * GROUP:(h + 1) * GROUP]
        ag_ref[b] = o4


def _sample_attn(q4, k_new, v_new, cache_k, cache_v, bias_s, sink_col, hmask, *, layer, bt):
    _, nb, win, kvd = cache_k.shape
    return pl.pallas_call(
        _sample_attn_kernel,
        grid=(nb // bt,),
        in_specs=[
            pl.BlockSpec((bt, GROUP, kvd), lambda i: (i, 0, 0)),
            pl.BlockSpec((bt, kvd), lambda i: (i, 0)),
            pl.BlockSpec((bt, kvd), lambda i: (i, 0)),
            pl.BlockSpec((None, bt, win, kvd), lambda i: (layer, i, 0, 0)),
            pl.BlockSpec((None, bt, win, kvd), lambda i: (layer, i, 0, 0)),
            pl.BlockSpec(bias_s.shape, lambda i: (0, 0)),
            pl.BlockSpec(sink_col.shape, lambda i: (0, 0)),
            pl.BlockSpec(hmask.shape, lambda i: (0, 0)),
        ],
        out_specs=[
            pl.BlockSpec((bt, win, kvd), lambda i: (i, 0, 0)),
            pl.BlockSpec((bt, win, kvd), lambda i: (i, 0, 0)),
            pl.BlockSpec((bt, GROUP, kvd), lambda i: (i, 0, 0)),
        ],
        out_shape=[
            jax.ShapeDtypeStruct((nb, win, kvd), F32),
            jax.ShapeDtypeStruct((nb, win, kvd), F32),
            jax.ShapeDtypeStruct((nb, GROUP, kvd), F32),
        ],
        compiler_params=pltpu.CompilerParams(dimension_semantics=("arbitrary",), vmem_limit_bytes=VMEM_LIMIT),
        name="sample_attn",
    )(q4, k_new, v_new, cache_k, cache_v, bias_s, sink_col, hmask)


def _sample_post_kernel(x_ref, att_ref, proj_ref, st_ref, convw_ref, w_ao_ref, w_co_ref, w_o_ref,
                        lng_ref, lnb_ref, x1_ref, u_ref, *, alpha):
    attn_o = _dot(att_ref[...].astype(BF16), w_ao_ref[...])
    u = proj_ref[:, OFF_C:OFF_H] * proj_ref[:, OFF_H:OFF_GA]
    cw = convw_ref[...]
    y = cw[0:1] * st_ref[0] + cw[1:2] * st_ref[1] + cw[2:3] * u
    u_ref[...] = u
    y_conv = proj_ref[:, OFF_B:OFF_C] * y
    x1_ref[...] = _merge_project(x_ref[...], attn_o, y_conv, proj_ref[:, OFF_GA:OFF_GB], proj_ref[:, OFF_GB:IN_DIM],
                                 w_co_ref, w_o_ref, lng_ref, lnb_ref, alpha)


def _sample_post(x, att, proj, state, conv_w, w_ao, w_co, w_o, ln_g, ln_b, *, alpha):
    m, d = x.shape
    kernel = functools.partial(_sample_post_kernel, alpha=alpha)
    return pl.pallas_call(
        kernel,
        out_shape=[jax.ShapeDtypeStruct((m, d), F32), jax.ShapeDtypeStruct((m, CONV_DIM), F32)],
        compiler_params=pltpu.CompilerParams(vmem_limit_bytes=VMEM_LIMIT),
        name="sample_post",
    )(x, att, proj, state, conv_w, w_ao, w_co, w_o, ln_g, ln_b)


def _first_max(cur, ids, axes, big):
    m = cur
    for ax in axes:
        m = jnp.max(m, axis=ax, keepdims=True)
    idx = jnp.where(cur == m, ids, big)
    for ax in axes:
        idx = jnp.min(idx, axis=ax, keepdims=True)
    return m, idx


def _router_kernel(x_ref, rwt_ref, rb_ref, eidx_ref, gate_ref, *, tm, slab_in):
    x = _slab_load(x_ref, tm) if slab_in else x_ref[...]
    logits_t = _dot_nt(rwt_ref[...], x.astype(BF16))
    scores = jax.nn.sigmoid(logits_t)
    sel = scores + rb_ref[...]
    shape3 = (N_GROUPS, GROUP_SIZE, tm)
    sel3 = sel.reshape(shape3)
    scores3 = scores.reshape(shape3)
    member = lax.broadcasted_iota(jnp.int32, shape3, 1)
    m1, i1 = _first_max(sel3, member, (1,), GROUP_SIZE)
    m2 = jnp.max(jnp.where(member == i1, -jnp.inf, sel3), axis=1, keepdims=True)
    gscore = m1 + m2
    gid = lax.broadcasted_iota(jnp.int32, gscore.shape, 0)
    gsel = jnp.zeros(gscore.shape, jnp.bool_)
    for _ in range(TOPK_GROUPS):
        _, gi = _first_max(gscore, gid, (0,), N_GROUPS)
        hit = gid == gi
        gsel = jnp.logical_or(gsel, hit)
        gscore = jnp.where(hit, -jnp.inf, gscore)
    eid = lax.broadcasted_iota(jnp.int32, shape3, 0) * GROUP_SIZE + member
    cur = jnp.where(gsel, sel3, -jnp.inf)
    ids, ws = [], []
    for _ in range(TOP_K):
        _, ei = _first_max(cur, eid, (1, 0), N_EXPERTS)
        hit = eid == ei
        sc = jnp.sum(jnp.sum(jnp.where(hit, scores3, 0.0), axis=1, keepdims=True), axis=0, keepdims=True)
        ids.append(ei[0])
        ws.append(sc[0])
        cur = jnp.where(hit, -jnp.inf, cur)
    w = jnp.concatenate(ws, axis=0)
    tot = jnp.sum(w, axis=0, keepdims=True)
    eidx_ref[...] = jnp.concatenate(ids, axis=0)
    gate_ref[...] = w / tot * ROUTED_SCALE


def _router(x, rw_t, rb_col, *, tm, tokens=None):
    slab_in = x.ndim == 3
    if slab_in:
        t = tokens
        steps_per_chunk = (t // x.shape[0]) // tm
        assert t % (x.shape[0] * tm) == 0
        x_spec = pl.BlockSpec((None, tm * SLAB_ROWS, LANES), lambda i: (i // steps_per_chunk, i % steps_per_chunk, 0))
    else:
        t = x.shape[0]
        x_spec = pl.BlockSpec((tm, D_MODEL), lambda i: (i, 0))
    return pl.pallas_call(
        functools.partial(_router_kernel, tm=tm, slab_in=slab_in),
        grid=(t // tm,),
        in_specs=[
            x_spec,
            pl.BlockSpec(rw_t.shape, lambda i: (0, 0)),
            pl.BlockSpec(rb_col.shape, lambda i: (0, 0)),
        ],
        out_specs=[pl.BlockSpec((TOP_K, tm), lambda i: (0, i)), pl.BlockSpec((TOP_K, tm), lambda i: (0, i))],
        out_shape=[jax.ShapeDtypeStruct((TOP_K, t), jnp.int32), jax.ShapeDtypeStruct((TOP_K, t), F32)],
        compiler_params=pltpu.CompilerParams(dimension_semantics=("arbitrary",), vmem_limit_bytes=VMEM_LIMIT),
        name="router",
    )(x, rw_t, rb_col)


ROW_TILE = 288
TILE_PITCH = ROW_TILE + 1
GATE_LANES = -(-ROW_TILE // LANES) * LANES
SPARE_TOKENS = 8
PAIR_BITS = 16
FLAG_FIRST, FLAG_LAST, FLAG_NEW_EXPERT, FLAG_VALID, FLAG_HAS_NEXT = 1, 2, 4, 8, 16
SCATTER_BATCH = 2


PREV, CUR, NEXT = 0, 1, 2


def _moe_routed_kernel(ce_ref, flags_ref, used_ref, nexte_ref, idx_ref, gate_ref, x_ref, wg_ref, wu_ref,
                       wd_ref, sg_ref, su_ref, sd_ref, lng_ref, lnb_ref, o_ref, wgu_ref, wdb_ref,
                       gat0_ref, gat1_ref, res0_ref, res1_ref, wsg_ref, wsu_ref, wsd_ref, wsem_ref, *,
                       layer, alpha, chunk_tokens, sub_tokens):
    del used_ref
    step = pl.program_id(0)
    flags = flags_ref[step]
    odd = (step & 1) == 1
    nchunk = D_MODEL // LANES
    ne, ff = wg_ref.shape[1], wg_ref.shape[3]

    def slab_row(which, r):
        return pl.multiple_of(idx_ref[0, which, r], 8)

    def gather_row(which, gat_ref, r):
        gat_ref[pl.ds(r, nchunk, stride=TILE_PITCH), :] = x_ref[pl.ds(slab_row(which, r), 8), :]

    def scatter_rows(which, res_ref, rows):
        dst = [slab_row(which, r) for r in rows]
        acc = [o_ref[pl.ds(d, 8), :] + res_ref[pl.ds(r, nchunk, stride=TILE_PITCH), :] for d, r in zip(dst, rows)]
        for d, a in zip(dst, acc):
            o_ref[pl.ds(d, 8), :] = a

    def expert_mlp(gat_ref, res_ref):
        lhs = jnp.concatenate([gat_ref[j * TILE_PITCH:j * TILE_PITCH + ROW_TILE, :] for j in range(nchunk)],
                              axis=1).astype(BF16)
        h = _dot(lhs, wgu_ref[...])
        gate_col = jnp.broadcast_to(gate_ref[0], (LANES, GATE_LANES)).T[0:ROW_TILE]
        hid = jax.nn.silu(h[:, 0:ff]) * h[:, ff:2 * ff] * jnp.concatenate([gate_col] * (ff // LANES), axis=1)
        y = _dot(hid.astype(BF16), wdb_ref[...])
        for j in range(nchunk):
            res_ref[j * TILE_PITCH:j * TILE_PITCH + ROW_TILE, :] = y[:, j * LANES:(j + 1) * LANES]

    def by_parity(fn):
        @pl.when(jnp.logical_not(odd))
        def _even():
            fn(gat0_ref, gat1_ref, res0_ref, res1_ref)

        @pl.when(odd)
        def _odd():
            fn(gat1_ref, gat0_ref, res1_ref, res0_ref)

    @pl.when((flags & FLAG_FIRST) != 0)
    def _start_chunk():
        o_ref[...] = jnp.zeros(o_ref.shape, F32)

        def start(gat_cur, gat_other, res_cur, res_other):
            res_other[...] = jnp.zeros(res_other.shape, F32)

            def body(r, carry):
                gather_row(CUR, gat_cur, r)
                return carry

            lax.fori_loop(0, ROW_TILE, body, 0)

        by_parity(start)

    def weight_copies(e):
        return [pltpu.make_async_copy(src.at[layer, e], dst, wsem_ref.at[k])
                for k, (src, dst) in enumerate(((wg_ref, wsg_ref), (wu_ref, wsu_ref), (wd_ref, wsd_ref)))]

    @pl.when(step == 0)
    def _first_fetch():
        for cp in weight_copies(ce_ref[0] % ne):
            cp.start()

    @pl.when((flags & FLAG_NEW_EXPERT) != 0)
    def _next_expert():
        for cp in weight_copies(0):
            cp.wait()
        wgu_ref[:, 0:ff] = wsg_ref[...].astype(BF16)
        wgu_ref[:, ff:2 * ff] = wsu_ref[...].astype(BF16)
        wdb_ref[...] = wsd_ref[...].astype(BF16)

        @pl.when((flags & FLAG_HAS_NEXT) != 0)
        def _prefetch():
            for cp in weight_copies(nexte_ref[step]):
                cp.start()

    @pl.when((flags & FLAG_VALID) != 0)
    def _tile():
        def main(gat_cur, gat_other, res_cur, res_other):
            for r0 in range(0, ROW_TILE, SCATTER_BATCH):
                scatter_rows(PREV, res_other, range(r0, r0 + SCATTER_BATCH))
            for r in range(ROW_TILE):
                gather_row(NEXT, gat_other, r)
            expert_mlp(gat_cur, res_cur)

        by_parity(main)

    @pl.when((flags & FLAG_LAST) != 0)
    def _finish():
        def flush(gat_cur, gat_other, res_cur, res_other):
            def body(r, carry):
                scatter_rows(CUR, res_cur, [r])
                return carry

            lax.fori_loop(0, ROW_TILE, body, 0)

        by_parity(flush)

        def body(s, carry):
            base = pl.multiple_of(s * (sub_tokens * 8), 8)

            def rows_2d(ref):
                return jnp.concatenate([ref[pl.ds(base + j, sub_tokens, stride=8), :] for j in range(nchunk)],
                                       axis=1)

            x2 = rows_2d(x_ref)
            xb = x2.astype(BF16)
            hs = jax.nn.silu(_dot(xb, sg_ref[...])) * _dot(xb, su_ref[...])
            ffn = rows_2d(o_ref) + _dot(hs.astype(BF16), sd_ref[...])
            res = _layer_norm(alpha * x2 + ffn, lng_ref[...], lnb_ref[...])
            for j in range(nchunk):
                o_ref[pl.ds(base + j, sub_tokens, stride=8), :] = res[:, j * LANES:(j + 1) * LANES]
            return carry

        lax.fori_loop(0, chunk_tokens // sub_tokens, body, 0)


def _moe_routed(tile_ce, tile_flags, n_used, next_expert, row_idx, gates, x_tm, wg, wu, wd, sg, su, sd, ln_g,
                ln_b, *, layer, alpha, chunk_tokens, sub_tokens):
    n_chunks, chunk_rows, _ = x_tm.shape
    n_tiles = row_idx.shape[0]
    _, ne, d, ff = wg.shape
    kernel = functools.partial(_moe_routed_kernel, layer=layer, alpha=alpha, chunk_tokens=chunk_tokens,
                               sub_tokens=sub_tokens)

    def tile_map(i, ce, fl, used, nxt):
        return (jnp.minimum(i, used[0] - 1), 0, 0)

    def chunk_map(i, ce, fl, used, nxt):
        return (ce[i] // ne, 0, 0)

    def const2(i, ce, fl, used, nxt):
        return (0, 0)

    grid_spec = pltpu.PrefetchScalarGridSpec(
        num_scalar_prefetch=4,
        grid=(n_tiles,),
        in_specs=[
            pl.BlockSpec((1, 3, ROW_TILE), tile_map, memory_space=pltpu.SMEM),
            pl.BlockSpec((1, 1, GATE_LANES), tile_map),
            pl.BlockSpec((None, chunk_rows, LANES), chunk_map, pipeline_mode=pl.Buffered(1)),
            pl.BlockSpec(memory_space=pl.ANY),
            pl.BlockSpec(memory_space=pl.ANY),
            pl.BlockSpec(memory_space=pl.ANY),
            pl.BlockSpec(sg.shape, const2, pipeline_mode=pl.Buffered(1)),
            pl.BlockSpec(su.shape, const2, pipeline_mode=pl.Buffered(1)),
            pl.BlockSpec(sd.shape, const2, pipeline_mode=pl.Buffered(1)),
            pl.BlockSpec(ln_g.shape, const2),
            pl.BlockSpec(ln_b.shape, const2),
        ],
        out_specs=pl.BlockSpec((None, chunk_rows, LANES), chunk_map, pipeline_mode=pl.Buffered(1)),
        scratch_shapes=[
            pltpu.VMEM((d, 2 * ff), BF16),
            pltpu.VMEM((ff, d), BF16),
            pltpu.VMEM((8 * TILE_PITCH, LANES), F32),
            pltpu.VMEM((8 * TILE_PITCH, LANES), F32),
            pltpu.VMEM((8 * TILE_PITCH, LANES), F32),
            pltpu.VMEM((8 * TILE_PITCH, LANES), F32),
            pltpu.VMEM((d, ff), F32),
            pltpu.VMEM((d, ff), F32),
            pltpu.VMEM((ff, d), F32),
            pltpu.SemaphoreType.DMA((3,)),
        ],
    )
    return pl.pallas_call(
        kernel,
        grid_spec=grid_spec,
        out_shape=jax.ShapeDtypeStruct(x_tm.shape, F32),
        compiler_params=pltpu.CompilerParams(dimension_semantics=("arbitrary",), vmem_limit_bytes=VMEM_LIMIT),
        name="moe_routed",
    )(tile_ce, tile_flags, n_used, next_expert, row_idx, gates, x_tm, wg, wu, wd, sg, su, sd, ln_g, ln_b)


MOE_CHUNKS = 4
MOE_SUB_MAX = 512


def _moe_tiling(n_prompt, n_sample):
    assert n_prompt % (MOE_CHUNKS * 8) == 0 and n_sample % (MOE_CHUNKS * 8) == 0
    chunk_tokens = (n_prompt + n_sample) // MOE_CHUNKS
    assert (chunk_tokens + 1) * TOP_K <= 1 << PAIR_BITS
    sub_tokens = max(s for s in range(8, MOE_SUB_MAX + 1, 8) if chunk_tokens % s == 0)
    return MOE_CHUNKS, chunk_tokens, sub_tokens


def _chunked(prompt, sample, n_chunks):
    return jnp.concatenate([prompt.reshape((n_chunks, -1) + prompt.shape[1:]),
                            sample.reshape((n_chunks, -1) + sample.shape[1:])], axis=1)


def _route_plan(eidx_p, eidx_s, gate_p, gate_s, *, chunk_tokens, n_chunks):
    n_seg = n_chunks * N_EXPERTS
    eidx = _chunked(eidx_p.T, eidx_s.T, n_chunks)
    gate = _chunked(gate_p.T, gate_s.T, n_chunks)
    seg = jnp.arange(n_chunks, dtype=jnp.int32)[:, None, None] * N_EXPERTS + eidx
    pair = jnp.arange(chunk_tokens * TOP_K, dtype=jnp.int32).reshape(1, chunk_tokens, TOP_K)
    keys_real = ((seg << PAIR_BITS) | pair).reshape(-1)
    counts = jnp.sum(eidx[..., None] == jnp.arange(N_EXPERTS, dtype=jnp.int32), axis=(1, 2),
                     dtype=jnp.int32).reshape(n_seg)
    n_pad = (-counts) % ROW_TILE
    slot = jnp.arange(ROW_TILE - 1, dtype=jnp.int32)
    int_max = jnp.iinfo(jnp.int32).max
    pad_pair = (1 << PAIR_BITS) - 1
    keys_pad = jnp.where(slot[None, :] < n_pad[:, None],
                         (jnp.arange(n_seg, dtype=jnp.int32)[:, None] << PAIR_BITS) | pad_pair, int_max)
    n_fill = (-(keys_real.size + keys_pad.size)) % ROW_TILE
    keys = jnp.concatenate([keys_real, keys_pad.reshape(-1), jnp.full((n_fill,), int_max, jnp.int32)])
    vals = jnp.concatenate([gate.reshape(-1), jnp.zeros((keys_pad.size + n_fill,), F32)])
    n_tiles = keys.size // ROW_TILE
    keys, vals = lax.sort((keys, vals), num_keys=1)
    keys = keys.reshape(n_tiles, ROW_TILE)
    head = keys[:, 0]
    valid = head != int_max
    tile_ce = jnp.minimum(head >> PAIR_BITS, n_seg - 1)
    tile_c = tile_ce // N_EXPERTS
    prev_ce = jnp.concatenate([jnp.full((1,), -N_EXPERTS, jnp.int32), tile_ce[:-1]])
    next_c = jnp.concatenate([tile_c[1:], jnp.full((1,), -1, jnp.int32)])
    next_valid = jnp.concatenate([valid[1:], jnp.zeros((1,), jnp.bool_)])
    first = valid & (tile_c != prev_ce // N_EXPERTS)
    last = valid & (~next_valid | (next_c != tile_c))
    new_expert = valid & (tile_ce != prev_ce)
    tile_id = jnp.arange(n_tiles, dtype=jnp.int32)
    change_at = jnp.where(new_expert, tile_id, n_tiles)
    next_change = jnp.concatenate([lax.cummin(change_at, reverse=True)[1:], jnp.full((1,), n_tiles, jnp.int32)])
    has_next = new_expert & (next_change < n_tiles)
    next_expert = tile_ce[jnp.minimum(next_change, n_tiles - 1)] % N_EXPERTS
    flags = (first * FLAG_FIRST + last * FLAG_LAST + new_expert * FLAG_NEW_EXPERT + valid * FLAG_VALID
             + has_next * FLAG_HAS_NEXT)
    row_idx = jnp.minimum((keys & pad_pair) // TOP_K, chunk_tokens) * 8
    row_idx3 = jnp.stack([jnp.concatenate([row_idx[:1], row_idx[:-1]]), row_idx,
                          jnp.concatenate([row_idx[1:], row_idx[-1:]])], axis=1)
    n_used = jnp.sum(valid, dtype=jnp.int32).reshape(1)
    gates = jnp.pad(vals.reshape(n_tiles, 1, ROW_TILE), ((0, 0), (0, 0), (0, GATE_LANES - ROW_TILE)))
    return tile_ce, flags.astype(jnp.int32), n_used, next_expert.astype(jnp.int32), row_idx3, gates


def _t5_bucket(dist):
    n = jnp.maximum(dist, 0)
    max_exact = N_BUCKETS // 2
    large = max_exact + (jnp.log(jnp.maximum(n, 1).astype(F32) / max_exact)
                         / math.log(MAX_DISTANCE / max_exact) * (N_BUCKETS - max_exact)).astype(jnp.int32)
    large = jnp.minimum(large, N_BUCKETS - 1)
    return jnp.where(n < max_exact, n, large)


def _bias_lookup(rel_bias, bucket):
    onehot = (bucket[..., None] == jnp.arange(N_BUCKETS, dtype=bucket.dtype)).astype(F32)
    return jnp.einsum("...b,bh->h...", onehot, rel_bias.astype(F32), precision=lax.Precision.HIGHEST)


def _bias_tables(rel_bias, win):
    qi = jnp.arange(ATTN_BLOCK)[:, None]
    ki = jnp.arange(2 * ATTN_BLOCK)[None, :]
    dist = qi + ATTN_BLOCK - ki
    valid = (dist >= 0) & (dist < WINDOW)
    bias = _bias_lookup(rel_bias, _t5_bucket(dist))
    bias = jnp.where(valid[None], bias, NEG).reshape(N_KV_HEADS, GROUP, ATTN_BLOCK, 2 * ATTN_BLOCK)
    bias_ab = jnp.stack([jnp.concatenate([bias[:, t], bias[:, t + 2]], axis=1) for t in range(2)], axis=1)
    dist_s = (win - 1) - jnp.arange(win)
    bias_s = _bias_lookup(rel_bias, _t5_bucket(dist_s))
    return bias_ab, bias_s


def _sink_tables(sink):
    s = sink.astype(F32).reshape(N_KV_HEADS, GROUP)
    rows = [jnp.concatenate([jnp.broadcast_to(s[:, t, None], (N_KV_HEADS, ATTN_BLOCK)),
                             jnp.broadcast_to(s[:, t + 2, None], (N_KV_HEADS, ATTN_BLOCK))], axis=1)
            for t in range(2)]
    return jnp.stack(rows, axis=1)[..., None], sink.astype(F32)[:, None]


def kernel(x_prompt, x_sample, cache_k_win, cache_v_win, state_conv, rel_bias, w_in, attn_sink, conv_w,
           w_attn_out, w_conv_out, w_out, ln1_g, ln1_b, router_w, router_bias, exp_w_gate, exp_w_up,
           exp_w_down, shared_w_gate, shared_w_up, shared_w_down, ln2_g, ln2_b):
    depth = w_in.shape[0]
    alpha = (2 * depth) ** 0.25
    nb, seq, d = x_prompt.shape
    nd = x_sample.shape[0]
    win = cache_k_win.shape[2]
    assert x_sample.shape[1] == 1 and win == WINDOW and seq % 512 == 0

    n_prompt = nb * seq
    n_chunks, chunk_tokens, sub_tokens = _moe_tiling(n_prompt, nd)
    prompt_rows = n_prompt // n_chunks * SLAB_ROWS
    slab_shape = (n_chunks, (chunk_tokens + SPARE_TOKENS) * SLAB_ROWS, LANES)

    bias_ab, bias_s = _bias_tables(rel_bias, win)
    hmask = (jnp.arange(KV_DIM)[None, :] // HEAD_DIM == jnp.arange(N_HEADS)[:, None] // GROUP).astype(F32)

    yp = x_prompt
    ys = x_sample.reshape(nd, d)
    outs = [[] for _ in range(6)]
    for l in range(depth):
        w_in_b = w_in[l].astype(BF16)
        w_ao_b = w_attn_out[l].astype(BF16)
        w_co_b = w_conv_out[l].astype(BF16)
        w_o_b = w_out[l].astype(BF16)
        g1, b1 = ln1_g[l][None, :], ln1_b[l][None, :]
        g2, b2 = ln2_g[l][None, :], ln2_b[l][None, :]
        sink_ab, sink_col = _sink_tables(attn_sink[l])

        slab, kp, vp, cp = _mixer_prompt(yp, w_in_b, w_attn_out[l].T.astype(BF16), w_co_b, w_o_b, conv_w[l],
                                         jnp.swapaxes(bias_ab, -1, -2), jnp.swapaxes(sink_ab, -1, -2), g1, b1,
                                         alpha=alpha, tq=512, batch=nb, seq=seq, slab_shape=slab_shape,
                                         base=jnp.zeros(slab_shape, F32) if l == 0 else None)

        proj = _sample_proj(ys, w_in_b, tn=IN_DIM // 4)
        q4 = proj[:, :Q_DIM].reshape(nd, N_KV_HEADS, GROUP, HEAD_DIM).transpose(0, 2, 1, 3).reshape(nd, GROUP, KV_DIM)
        ksn, vsn, ag = _sample_attn(q4, proj[:, OFF_K:OFF_V], proj[:, OFF_V:OFF_B],
                                    cache_k_win.reshape(depth, nd, win, KV_DIM),
                                    cache_v_win.reshape(depth, nd, win, KV_DIM),
                                    bias_s, sink_col, hmask, layer=l, bt=16)
        att = ag.reshape(nd, GROUP, N_KV_HEADS, HEAD_DIM).transpose(0, 2, 1, 3).reshape(nd, Q_DIM)
        state_t = jnp.swapaxes(state_conv[l], 0, 1)
        ys, us = _sample_post(ys, att, proj, state_t, conv_w[l], w_ao_b, w_co_b, w_o_b, g1, b1, alpha=alpha)

        outs[0].append(kp.reshape(nb, WINDOW, N_KV_HEADS, HEAD_DIM))
        outs[1].append(vp.reshape(nb, WINDOW, N_KV_HEADS, HEAD_DIM))
        outs[2].append(cp)
        outs[3].append(ksn.reshape(nd, win, N_KV_HEADS, HEAD_DIM))
        outs[4].append(vsn.reshape(nd, win, N_KV_HEADS, HEAD_DIM))
        outs[5].append(jnp.concatenate([state_conv[l][:, 1:], us[:, None, :]], axis=1))

        rw_t = router_w[l].T.astype(BF16)
        rb_col = router_bias[l].astype(F32)[:, None]
        sg, su, sd = (shared_w_gate[l].astype(BF16), shared_w_up[l].astype(BF16), shared_w_down[l].astype(BF16))
        tail = jnp.concatenate([ys.reshape(n_chunks, -1, LANES),
                                jnp.zeros((n_chunks, SPARE_TOKENS * SLAB_ROWS, LANES), F32)], axis=1)
        slab = lax.dynamic_update_slice(slab, tail, (0, prompt_rows, 0))
        ep, wp = _router(slab, rw_t, rb_col, tm=512, tokens=n_prompt)
        es, ws = _router(ys, rw_t, rb_col, tm=nd)
        plan = _route_plan(ep, es, wp, ws, chunk_tokens=chunk_tokens, n_chunks=n_chunks)
        yp = _moe_routed(*plan, slab, exp_w_gate, exp_w_up, exp_w_down, sg, su, sd, g2, b2,
                         layer=l, alpha=alpha, chunk_tokens=chunk_tokens, sub_tokens=sub_tokens)
        ys = yp[:, prompt_rows:chunk_tokens * SLAB_ROWS].reshape(nd, d)

    y_prompt = yp[:, :prompt_rows].reshape(nb, seq, d)
    return (y_prompt, ys.reshape(nd, 1, d)) + tuple(jnp.stack(o) for o in outs)
```

```python
import functools
import math

import jax
import jax.numpy as jnp
from jax import lax
from jax.experimental import pallas as pl
from jax.experimental.pallas import tpu as pltpu

D_MODEL = 1024
N_HEADS = 16
N_KV_HEADS = 4
HEAD_DIM = 64
GROUP = N_HEADS // N_KV_HEADS
WINDOW = 128
ATTN_BLOCK = 128
N_BUCKETS = 32
MAX_DISTANCE = 128
CONV_DIM = 1024
CONV_K = 3
N_EXPERTS = 64
TOP_K = 8
N_GROUPS = 8
TOPK_GROUPS = 4
GROUP_SIZE = N_EXPERTS // N_GROUPS
EXPERT_FF = 256
ROUTED_SCALE = 2.5
LN_EPS = 1e-5
NEG = -1e30

Q_DIM = N_HEADS * HEAD_DIM
KV_DIM = N_KV_HEADS * HEAD_DIM
OFF_K = Q_DIM
OFF_V = OFF_K + KV_DIM
OFF_B = OFF_V + KV_DIM
OFF_C = OFF_B + CONV_DIM
OFF_H = OFF_C + CONV_DIM
OFF_GA = OFF_H + CONV_DIM
OFF_GB = OFF_GA + D_MODEL
IN_DIM = OFF_GB + D_MODEL

LANES = 128
CONV_PAD = 8
VMEM_LIMIT = 60 * 1024 * 1024

BF16 = jnp.bfloat16
F32 = jnp.float32


def _dot(a, b):
    return jnp.dot(a, b, preferred_element_type=F32)


def _dot_nt(a, b):
    return lax.dot_general(a, b, (((1,), (1,)), ((), ())), preferred_element_type=F32)


def _layer_norm(z, g, b):
    mu = jnp.mean(z, axis=-1, keepdims=True)
    d = z - mu
    var = jnp.mean(d * d, axis=-1, keepdims=True)
    return d * lax.rsqrt(var + LN_EPS) * g + b


def _sink_softmax(s, sink, axis=-1):
    m = jnp.maximum(jnp.max(s, axis=axis, keepdims=True), sink)
    e = jnp.exp(s - m)
    den = jnp.sum(e, axis=axis, keepdims=True) + jnp.exp(sink - m)
    return e * (1.0 / den)


SLAB_ROWS = D_MODEL // LANES


def _slab_load(ref, tokens, base=0):
    return jnp.concatenate([ref[pl.ds(base + j, tokens, stride=SLAB_ROWS), :] for j in range(SLAB_ROWS)], axis=1)


def _slab_store(ref, val, base=0):
    tokens = val.shape[0]
    for j in range(SLAB_ROWS):
        ref[pl.ds(base + j, tokens, stride=SLAB_ROWS), :] = val[:, j * LANES:(j + 1) * LANES]


def _merge_project(x, attn_o, y_conv, g_a, g_b, w_co_ref, w_o_ref, lng_ref, lnb_ref, alpha):
    merged = jax.nn.sigmoid(g_a) * attn_o + jax.nn.sigmoid(g_b) * _dot(y_conv.astype(BF16), w_co_ref[...])
    out = _dot(merged.astype(BF16), w_o_ref[...])
    return _layer_norm(alpha * x + out, lng_ref[...], lnb_ref[...])


def _mixer_prompt_kernel(x_ref, w_in_ref, w_ao_ref, w_co_ref, w_o_ref, convw_ref, bias_ref, sink_ref,
                         lng_ref, lnb_ref, base_ref,
                         x1_ref, kwin_ref, vwin_ref, conv_ref,
                         ka_ref, kb_ref, vat_ref, vbt_ref, s_ref, p_ref, att_ref, ubuf_ref, *, alpha, tq, slab_in):
    del base_ref
    i = pl.program_id(1)
    nblk = tq // ATTN_BLOCK
    half = LANES // 2
    scale = HEAD_DIM ** -0.5
    assert math.frexp(scale)[0] == 0.5

    @pl.when(i == 0)
    def _init():
        for ref in (ka_ref, kb_ref):
            ref[:, 0:ATTN_BLOCK, :] = jnp.zeros((N_KV_HEADS, ATTN_BLOCK, LANES), BF16)
        for ref in (vat_ref, vbt_ref):
            ref[:, :, 0:ATTN_BLOCK] = jnp.zeros((N_KV_HEADS, LANES, ATTN_BLOCK), BF16)
        ubuf_ref[0:CONV_PAD, :] = jnp.zeros((CONV_PAD, CONV_DIM), F32)

    x = _slab_load(x_ref, tq) if slab_in else x_ref[0]
    xb = x.astype(BF16)
    qkv = _dot(xb, w_in_ref[:, 0:OFF_B])

    lo = lax.broadcasted_iota(jnp.int32, (tq, LANES), 1) < half
    zeros_t = jnp.zeros((half, tq), BF16)
    for c in range(N_KV_HEADS // 2):
        chunk = qkv[:, OFF_K + c * LANES: OFF_K + (c + 1) * LANES]
        c_lo = jnp.where(lo, chunk, 0.0)
        c_hi = jnp.where(lo, 0.0, chunk)
        ka_ref[2 * c, ATTN_BLOCK:, :] = c_lo.astype(BF16)
        kb_ref[2 * c, ATTN_BLOCK:, :] = pltpu.roll(c_lo, half, 1).astype(BF16)
        kb_ref[2 * c + 1, ATTN_BLOCK:, :] = c_hi.astype(BF16)
        ka_ref[2 * c + 1, ATTN_BLOCK:, :] = pltpu.roll(c_hi, half, 1).astype(BF16)
        vt = qkv[:, OFF_V + c * LANES: OFF_V + (c + 1) * LANES].T.astype(BF16)
        vat_ref[2 * c, :, ATTN_BLOCK:] = jnp.concatenate([vt[0:half], zeros_t], axis=0)
        vbt_ref[2 * c, :, ATTN_BLOCK:] = jnp.concatenate([zeros_t, vt[0:half]], axis=0)
        vbt_ref[2 * c + 1, :, ATTN_BLOCK:] = jnp.concatenate([zeros_t, vt[half:]], axis=0)
        vat_ref[2 * c + 1, :, ATTN_BLOCK:] = jnp.concatenate([vt[half:], zeros_t], axis=0)

    key_row = lax.broadcasted_iota(jnp.int32, (2 * ATTN_BLOCK, 2 * ATTN_BLOCK), 0)
    for j in range(nblk):
        rows = slice(j * ATTN_BLOCK, (j + 1) * ATTN_BLOCK)
        keys = slice(j * ATTN_BLOCK, (j + 2) * ATTN_BLOCK)
        for h in range(N_KV_HEADS):
            q0 = h * GROUP * HEAD_DIM
            q2 = (jnp.concatenate([qkv[rows, q0:q0 + LANES], qkv[rows, q0 + LANES:q0 + 2 * LANES]], axis=0)
                  * scale).astype(BF16)
            for t, k_ref in enumerate((ka_ref, kb_ref)):
                s = _dot_nt(k_ref[h, keys, :], q2)
                if j == 0:
                    s = jnp.where(jnp.logical_and(i == 0, key_row < ATTN_BLOCK), NEG, s)
                s_ref[j, h, t] = s
    for h in range(N_KV_HEADS):
        for t in range(2):
            s = s_ref[:, h, t] + bias_ref[h, t][None]
            p_ref[:, h, t] = _sink_softmax(s, sink_ref[h, t][None], axis=1).astype(BF16)
    for j in range(nblk):
        rows = slice(j * ATTN_BLOCK, (j + 1) * ATTN_BLOCK)
        keys = slice(j * ATTN_BLOCK, (j + 2) * ATTN_BLOCK)
        for h in range(N_KV_HEADS):
            q0 = h * GROUP * HEAD_DIM
            o_t = _dot(vat_ref[h, :, keys], p_ref[j, h, 0]) + _dot(vbt_ref[h, :, keys], p_ref[j, h, 1])
            att_ref[q0:q0 + LANES, rows] = o_t[:, 0:ATTN_BLOCK].astype(BF16)
            att_ref[q0 + LANES:q0 + 2 * LANES, rows] = o_t[:, ATTN_BLOCK:].astype(BF16)
    for ref in (ka_ref, kb_ref):
        ref[:, 0:ATTN_BLOCK, :] = ref[:, tq:tq + ATTN_BLOCK, :]
    for ref in (vat_ref, vbt_ref):
        ref[:, :, 0:ATTN_BLOCK] = ref[:, :, tq:tq + ATTN_BLOCK]

    kwin_ref[0] = qkv[tq - WINDOW:tq, OFF_K:OFF_V]
    vwin_ref[0] = qkv[tq - WINDOW:tq, OFF_V:OFF_B]

    attn_o = _dot(w_ao_ref[...], att_ref[...]).T

    bch = _dot(xb, w_in_ref[:, OFF_B:OFF_GA])
    u = bch[:, CONV_DIM:2 * CONV_DIM] * bch[:, 2 * CONV_DIM:3 * CONV_DIM]
    ubuf_ref[CONV_PAD:CONV_PAD + tq, :] = u
    cw = convw_ref[...]
    y = (cw[0:1] * ubuf_ref[CONV_PAD - 2:CONV_PAD - 2 + tq, :]
         + cw[1:2] * ubuf_ref[CONV_PAD - 1:CONV_PAD - 1 + tq, :]
         + cw[2:3] * u)
    conv_ref[0] = ubuf_ref[CONV_PAD + tq - (CONV_K - 1):CONV_PAD + tq, :]
    ubuf_ref[0:CONV_PAD, :] = ubuf_ref[tq:tq + CONV_PAD, :]
    y_conv = bch[:, 0:CONV_DIM] * y

    gab = _dot(xb, w_in_ref[:, OFF_GA:IN_DIM])
    _slab_store(x1_ref, _merge_project(x, attn_o, y_conv, gab[:, 0:D_MODEL], gab[:, D_MODEL:], w_co_ref, w_o_ref,
                                       lng_ref, lnb_ref, alpha))


def _const_spec(shape):
    nd = len(shape)
    return pl.BlockSpec(shape, lambda *_: (0,) * nd, pipeline_mode=pl.Buffered(1))


def _mixer_prompt(x, w_in, w_ao, w_co, w_o, conv_w, bias_ab, sink_ab, ln_g, ln_b, *, alpha, tq, batch, seq,
                  slab_shape, base=None):
    b, s = batch, seq
    n_chunks, chunk_rows, _ = slab_shape
    slab_in = x.ndim == 3 and x.shape == slab_shape
    steps_per_chunk = (b * s // n_chunks) // tq
    assert (b * s) % (n_chunks * tq) == 0 and s % tq == 0

    def slab_map(bi, i):
        g = bi * (s // tq) + i
        return (g // steps_per_chunk, g % steps_per_chunk, 0)

    slab_spec = pl.BlockSpec((None, tq * SLAB_ROWS, LANES), slab_map)
    kernel = functools.partial(_mixer_prompt_kernel, alpha=alpha, tq=tq, slab_in=slab_in)
    return pl.pallas_call(
        kernel,
        grid=(b, s // tq),
        in_specs=[
            slab_spec if slab_in else pl.BlockSpec((1, tq, D_MODEL), lambda bi, i: (bi, i, 0)),
            _const_spec(w_in.shape), _const_spec(w_ao.shape), _const_spec(w_co.shape), _const_spec(w_o.shape),
            _const_spec(conv_w.shape), _const_spec(bias_ab.shape), _const_spec(sink_ab.shape),
            _const_spec(ln_g.shape), _const_spec(ln_b.shape),
            pl.BlockSpec(memory_space=pl.ANY),
        ],
        out_specs=[
            slab_spec,
            pl.BlockSpec((1, WINDOW, KV_DIM), lambda bi, i: (bi, 0, 0)),
            pl.BlockSpec((1, WINDOW, KV_DIM), lambda bi, i: (bi, 0, 0)),
            pl.BlockSpec((1, CONV_K - 1, CONV_DIM), lambda bi, i: (bi, 0, 0)),
        ],
        out_shape=[
            jax.ShapeDtypeStruct(slab_shape, F32),
            jax.ShapeDtypeStruct((b, WINDOW, KV_DIM), F32),
            jax.ShapeDtypeStruct((b, WINDOW, KV_DIM), F32),
            jax.ShapeDtypeStruct((b, CONV_K - 1, CONV_DIM), F32),
        ],
        scratch_shapes=[
            pltpu.VMEM((N_KV_HEADS, ATTN_BLOCK + tq, LANES), BF16),
            pltpu.VMEM((N_KV_HEADS, ATTN_BLOCK + tq, LANES), BF16),
            pltpu.VMEM((N_KV_HEADS, LANES, ATTN_BLOCK + tq), BF16),
            pltpu.VMEM((N_KV_HEADS, LANES, ATTN_BLOCK + tq), BF16),
            pltpu.VMEM((tq // ATTN_BLOCK, N_KV_HEADS, 2, 2 * ATTN_BLOCK, 2 * ATTN_BLOCK), F32),
            pltpu.VMEM((tq // ATTN_BLOCK, N_KV_HEADS, 2, 2 * ATTN_BLOCK, 2 * ATTN_BLOCK), BF16),
            pltpu.VMEM((Q_DIM, tq), BF16),
            pltpu.VMEM((tq + CONV_PAD, CONV_DIM), F32),
        ],
        compiler_params=pltpu.CompilerParams(
            dimension_semantics=("arbitrary", "arbitrary"), vmem_limit_bytes=VMEM_LIMIT),
        input_output_aliases={0 if slab_in else 10: 0},
        name="mixer_prompt",
    )(x, w_in, w_ao, w_co, w_o, conv_w, bias_ab, sink_ab, ln_g, ln_b,
      jnp.zeros((1, SLAB_ROWS, LANES), F32) if slab_in else base)


def _proj_kernel(x_ref, w_ref, o_ref):
    o_ref[...] = _dot(x_ref[...].astype(BF16), w_ref[...])


def _sample_proj(x, w_in, *, tn):
    m, d = x.shape
    n = w_in.shape[1]
    return pl.pallas_call(
        _proj_kernel,
        grid=(n // tn,),
        in_specs=[pl.BlockSpec((m, d), lambda j: (0, 0)), pl.BlockSpec((d, tn), lambda j: (0, j))],
        out_specs=pl.BlockSpec((m, tn), lambda j: (0, j)),
        out_shape=jax.ShapeDtypeStruct((m, n), F32),
        compiler_params=pltpu.CompilerParams(dimension_semantics=("arbitrary",), vmem_limit_bytes=VMEM_LIMIT),
        name="sample_proj",
    )(x, w_in)


def _sample_attn_kernel(q4_ref, knew_ref, vnew_ref, ck_ref, cv_ref, bias_ref, sink_ref, hmask_ref,
                        nk_ref, nv_ref, ag_ref):
    bt = ck_ref.shape[0]
    win = ck_ref.shape[1]
    scale = HEAD_DIM ** -0.5
    row = lax.broadcasted_iota(jnp.int32, (win, KV_DIM), 0)
    last = row == win - 1
    hmask = hmask_ref[...]
    for b in range(bt):
        kb = jnp.where(last, knew_ref[b:b + 1, :], pltpu.roll(ck_ref[b], win - 1, 0))
        vb = jnp.where(last, vnew_ref[b:b + 1, :], pltpu.roll(cv_ref[b], win - 1, 0))
        nk_ref[b] = kb
        nv_ref[b] = vb
        q4 = q4_ref[b]
        qm = (jnp.concatenate([q4] * N_KV_HEADS, axis=0) * hmask).astype(BF16)
        s = _dot_nt(qm, kb.astype(BF16)) * scale + bias_ref[...]
        p = _sink_softmax(s, sink_ref[...]).astype(BF16)
        o = _dot(p, vb.astype(BF16)) * hmask
        o4 = o[0:GROUP]
        for h in range(1, N_KV_HEADS):
            o4 = o4 + o[h * GROUP:(h + 1) * GROUP]
        ag_ref[b] = o4


def _sample_attn(q4, k_new, v_new, cache_k, cache_v, bias_s, sink_col, hmask, *, layer, bt):
    _, nb, win, kvd = cache_k.shape
    return pl.pallas_call(
        _sample_attn_kernel,
        grid=(nb // bt,),
        in_specs=[
            pl.BlockSpec((bt, GROUP, kvd), lambda i: (i, 0, 0)),
            pl.BlockSpec((bt, kvd), lambda i: (i, 0)),
            pl.BlockSpec((bt, kvd), lambda i: (i, 0)),
            pl.BlockSpec((None, bt, win, kvd), lambda i: (layer, i, 0, 0)),
            pl.BlockSpec((None, bt, win, kvd), lambda i: (layer, i, 0, 0)),
            pl.BlockSpec(bias_s.shape, lambda i: (0, 0)),
            pl.BlockSpec(sink_col.shape, lambda i: (0, 0)),
            pl.BlockSpec(hmask.shape, lambda i: (0, 0)),
        ],
        out_specs=[
            pl.BlockSpec((bt, win, kvd), lambda i: (i, 0, 0)),
            pl.BlockSpec((bt, win, kvd), lambda i: (i, 0, 0)),
            pl.BlockSpec((bt, GROUP, kvd), lambda i: (i, 0, 0)),
        ],
        out_shape=[
            jax.ShapeDtypeStruct((nb, win, kvd), F32),
            jax.ShapeDtypeStruct((nb, win, kvd), F32),
            jax.ShapeDtypeStruct((nb, GROUP, kvd), F32),
        ],
        compiler_params=pltpu.CompilerParams(dimension_semantics=("arbitrary",), vmem_limit_bytes=VMEM_LIMIT),
        name="sample_attn",
    )(q4, k_new, v_new, cache_k, cache_v, bias_s, sink_col, hmask)


def _sample_post_kernel(x_ref, att_ref, proj_ref, st_ref, convw_ref, w_ao_ref, w_co_ref, w_o_ref,
                        lng_ref, lnb_ref, x1_ref, u_ref, *, alpha):
    attn_o = _dot(att_ref[...].astype(BF16), w_ao_ref[...])
    u = proj_ref[:, OFF_C:OFF_H] * proj_ref[:, OFF_H:OFF_GA]
    cw = convw_ref[...]
    y = cw[0:1] * st_ref[0] + cw[1:2] * st_ref[1] + cw[2:3] * u
    u_ref[...] = u
    y_conv = proj_ref[:, OFF_B:OFF_C] * y
    x1_ref[...] = _merge_project(x_ref[...], attn_o, y_conv, proj_ref[:, OFF_GA:OFF_GB], proj_ref[:, OFF_GB:IN_DIM],
                                 w_co_ref, w_o_ref, lng_ref, lnb_ref, alpha)


def _sample_post(x, att, proj, state, conv_w, w_ao, w_co, w_o, ln_g, ln_b, *, alpha):
    m, d = x.shape
    kernel = functools.partial(_sample_post_kernel, alpha=alpha)
    return pl.pallas_call(
        kernel,
        out_shape=[jax.ShapeDtypeStruct((m, d), F32), jax.ShapeDtypeStruct((m, CONV_DIM), F32)],
        compiler_params=pltpu.CompilerParams(vmem_limit_bytes=VMEM_LIMIT),
        name="sample_post",
    )(x, att, proj, state, conv_w, w_ao, w_co, w_o, ln_g, ln_b)


def _first_max(cur, ids, axes, big):
    m = cur
    for ax in axes:
        m = jnp.max(m, axis=ax, keepdims=True)
    idx = jnp.where(cur == m, ids, big)
    for ax in axes:
        idx = jnp.min(idx, axis=ax, keepdims=True)
    return m, idx


def _router_kernel(x_ref, rwt_ref, rb_ref, eidx_ref, gate_ref, *, tm, slab_in):
    x = _slab_load(x_ref, tm) if slab_in else x_ref[...]
    logits_t = _dot_nt(rwt_ref[...], x.astype(BF16))
    scores = jax.nn.sigmoid(logits_t)
    sel = scores + rb_ref[...]
    shape3 = (N_GROUPS, GROUP_SIZE, tm)
    sel3 = sel.reshape(shape3)
    scores3 = scores.reshape(shape3)
    member = lax.broadcasted_iota(jnp.int32, shape3, 1)
    m1, i1 = _first_max(sel3, member, (1,), GROUP_SIZE)
    m2 = jnp.max(jnp.where(member == i1, -jnp.inf, sel3), axis=1, keepdims=True)
    gscore = m1 + m2
    gid = lax.broadcasted_iota(jnp.int32, gscore.shape, 0)
    gsel = jnp.zeros(gscore.shape, jnp.bool_)
    for _ in range(TOPK_GROUPS):
        _, gi = _first_max(gscore, gid, (0,), N_GROUPS)
        hit = gid == gi
        gsel = jnp.logical_or(gsel, hit)
        gscore = jnp.where(hit, -jnp.inf, gscore)
    eid = lax.broadcasted_iota(jnp.int32, shape3, 0) * GROUP_SIZE + member
    cur = jnp.where(gsel, sel3, -jnp.inf)
    ids, ws = [], []
    for _ in range(TOP_K):
        _, ei = _first_max(cur, eid, (1, 0), N_EXPERTS)
        hit = eid == ei
        sc = jnp.sum(jnp.sum(jnp.where(hit, scores3, 0.0), axis=1, keepdims=True), axis=0, keepdims=True)
        ids.append(ei[0])
        ws.append(sc[0])
        cur = jnp.where(hit, -jnp.inf, cur)
    w = jnp.concatenate(ws, axis=0)
    tot = jnp.sum(w, axis=0, keepdims=True)
    eidx_ref[...] = jnp.concatenate(ids, axis=0)
    gate_ref[...] = w / tot * ROUTED_SCALE


def _router(x, rw_t, rb_col, *, tm, tokens=None):
    slab_in = x.ndim == 3
    if slab_in:
        t = tokens
        steps_per_chunk = (t // x.shape[0]) // tm
        assert t % (x.shape[0] * tm) == 0
        x_spec = pl.BlockSpec((None, tm * SLAB_ROWS, LANES), lambda i: (i // steps_per_chunk, i % steps_per_chunk, 0))
    else:
        t = x.shape[0]
        x_spec = pl.BlockSpec((tm, D_MODEL), lambda i: (i, 0))
    return pl.pallas_call(
        functools.partial(_router_kernel, tm=tm, slab_in=slab_in),
        grid=(t // tm,),
        in_specs=[
            x_spec,
            pl.BlockSpec(rw_t.shape, lambda i: (0, 0)),
            pl.BlockSpec(rb_col.shape, lambda i: (0, 0)),
        ],
        out_specs=[pl.BlockSpec((TOP_K, tm), lambda i: (0, i)), pl.BlockSpec((TOP_K, tm), lambda i: (0, i))],
        out_shape=[jax.ShapeDtypeStruct((TOP_K, t), jnp.int32), jax.ShapeDtypeStruct((TOP_K, t), F32)],
        compiler_params=pltpu.CompilerParams(dimension_semantics=("arbitrary",), vmem_limit_bytes=VMEM_LIMIT),
        name="router",
    )(x, rw_t, rb_col)


ROW_TILE = 288
TILE_PITCH = ROW_TILE + 1
GATE_LANES = -(-ROW_TILE // LANES) * LANES
SPARE_TOKENS = 8
PAIR_BITS = 16
FLAG_FIRST, FLAG_LAST, FLAG_NEW_EXPERT, FLAG_VALID, FLAG_HAS_NEXT = 1, 2, 4, 8, 16
SCATTER_BATCH = 6


PREV, CUR, NEXT = 0, 1, 2


def _moe_routed_kernel(ce_ref, flags_ref, used_ref, nexte_ref, idx_ref, gate_ref, x_ref, wg_ref, wu_ref,
                       wd_ref, sg_ref, su_ref, sd_ref, lng_ref, lnb_ref, o_ref, wgu_ref, wdb_ref,
                       gat0_ref, gat1_ref, res0_ref, res1_ref, wsg_ref, wsu_ref, wsd_ref, wsem_ref, *,
                       layer, alpha, chunk_tokens, sub_tokens):
    del used_ref
    step = pl.program_id(0)
    flags = flags_ref[step]
    odd = (step & 1) == 1
    nchunk = D_MODEL // LANES
    ne, ff = wg_ref.shape[1], wg_ref.shape[3]

    def slab_row(which, r):
        return pl.multiple_of(idx_ref[0, which, r], 8)

    def gather_row(which, gat_ref, r):
        gat_ref[pl.ds(r, nchunk, stride=TILE_PITCH), :] = x_ref[pl.ds(slab_row(which, r), 8), :]

    def scatter_rows(which, res_ref, rows):
        dst = [slab_row(which, r) for r in rows]
        acc = [o_ref[pl.ds(d, 8), :] + res_ref[pl.ds(r, nchunk, stride=TILE_PITCH), :] for d, r in zip(dst, rows)]
        for d, a in zip(dst, acc):
            o_ref[pl.ds(d, 8), :] = a

    def expert_mlp(gat_ref, res_ref):
        lhs = jnp.concatenate([gat_ref[j * TILE_PITCH:j * TILE_PITCH + ROW_TILE, :] for j in range(nchunk)],
                              axis=1).astype(BF16)
        h = _dot(lhs, wgu_ref[...])
        gate_col = jnp.broadcast_to(gate_ref[0], (LANES, GATE_LANES)).T[0:ROW_TILE]
        hid = jax.nn.silu(h[:, 0:ff]) * h[:, ff:2 * ff] * jnp.concatenate([gate_col] * (ff // LANES), axis=1)
        y = _dot(hid.astype(BF16), wdb_ref[...])
        for j in range(nchunk):
            res_ref[j * TILE_PITCH:j * TILE_PITCH + ROW_TILE, :] = y[:, j * LANES:(j + 1) * LANES]

    def by_parity(fn):
        @pl.when(jnp.logical_not(odd))
        def _even():
            fn(gat0_ref, gat1_ref, res0_ref, res1_ref)

        @pl.when(odd)
        def _odd():
            fn(gat1_ref, gat0_ref, res1_ref, res0_ref)

    @pl.when((flags & FLAG_FIRST) != 0)
    def _start_chunk():
        o_ref[...] = jnp.zeros(o_ref.shape, F32)

        def start(gat_cur, gat_other, res_cur, res_other):
            res_other[...] = jnp.zeros(res_other.shape, F32)

            def body(r, carry):
                gather_row(CUR, gat_cur, r)
                return carry

            lax.fori_loop(0, ROW_TILE, body, 0)

        by_parity(start)

    def weight_copies(e):
        return [pltpu.make_async_copy(src.at[layer, e], dst, wsem_ref.at[k])
                for k, (src, dst) in enumerate(((wg_ref, wsg_ref), (wu_ref, wsu_ref), (wd_ref, wsd_ref)))]

    @pl.when(step == 0)
    def _first_fetch():
        for cp in weight_copies(ce_ref[0] % ne):
            cp.start()

    @pl.when((flags & FLAG_NEW_EXPERT) != 0)
    def _next_expert():
        for cp in weight_copies(0):
            cp.wait()
        wgu_ref[:, 0:ff] = wsg_ref[...].astype(BF16)
        wgu_ref[:, ff:2 * ff] = wsu_ref[...].astype(BF16)
        wdb_ref[...] = wsd_ref[...].astype(BF16)

        @pl.when((flags & FLAG_HAS_NEXT) != 0)
        def _prefetch():
            for cp in weight_copies(nexte_ref[step]):
                cp.start()

    @pl.when((flags & FLAG_VALID) != 0)
    def _tile():
        def main(gat_cur, gat_other, res_cur, res_other):
            for r0 in range(0, ROW_TILE, SCATTER_BATCH):
                scatter_rows(PREV, res_other, range(r0, r0 + SCATTER_BATCH))
            for r in range(ROW_TILE):
                gather_row(NEXT, gat_other, r)
            expert_mlp(gat_cur, res_cur)

        by_parity(main)

    @pl.when((flags & FLAG_LAST) != 0)
    def _finish():
        def flush(gat_cur, gat_other, res_cur, res_other):
            def body(r, carry):
                scatter_rows(CUR, res_cur, [r])
                return carry

            lax.fori_loop(0, ROW_TILE, body, 0)

        by_parity(flush)

        def body(s, carry):
            base = pl.multiple_of(s * (sub_tokens * 8), 8)

            def rows_2d(ref):
                return jnp.concatenate([ref[pl.ds(base + j, sub_tokens, stride=8), :] for j in range(nchunk)],
                                       axis=1)

            x2 = rows_2d(x_ref)
            xb = x2.astype(BF16)
            hs = jax.nn.silu(_dot(xb, sg_ref[...])) * _dot(xb, su_ref[...])
            ffn = rows_2d(o_ref) + _dot(hs.astype(BF16), sd_ref[...])
            res = _layer_norm(alpha * x2 + ffn, lng_ref[...], lnb_ref[...])
            for j in range(nchunk):
                o_ref[pl.ds(base + j, sub_tokens, stride=8), :] = res[:, j * LANES:(j + 1) * LANES]
            return carry

        lax.fori_loop(0, chunk_tokens // sub_tokens, body, 0)


def _moe_routed(tile_ce, tile_flags, n_used, next_expert, row_idx, gates, x_tm, wg, wu, wd, sg, su, sd, ln_g,
                ln_b, *, layer, alpha, chunk_tokens, sub_tokens):
    n_chunks, chunk_rows, _ = x_tm.shape
    n_tiles = row_idx.shape[0]
    _, ne, d, ff = wg.shape
    kernel = functools.partial(_moe_routed_kernel, layer=layer, alpha=alpha, chunk_tokens=chunk_tokens,
                               sub_tokens=sub_tokens)

    def tile_map(i, ce, fl, used, nxt):
        return (jnp.minimum(i, used[0] - 1), 0, 0)

    def chunk_map(i, ce, fl, used, nxt):
        return (ce[i] // ne, 0, 0)

    def const2(i, ce, fl, used, nxt):
        return (0, 0)

    grid_spec = pltpu.PrefetchScalarGridSpec(
        num_scalar_prefetch=4,
        grid=(n_tiles,),
        in_specs=[
            pl.BlockSpec((1, 3, ROW_TILE), tile_map, memory_space=pltpu.SMEM),
            pl.BlockSpec((1, 1, GATE_LANES), tile_map),
            pl.BlockSpec((None, chunk_rows, LANES), chunk_map, pipeline_mode=pl.Buffered(1)),
            pl.BlockSpec(memory_space=pl.ANY),
            pl.BlockSpec(memory_space=pl.ANY),
            pl.BlockSpec(memory_space=pl.ANY),
            pl.BlockSpec(sg.shape, const2, pipeline_mode=pl.Buffered(1)),
            pl.BlockSpec(su.shape, const2, pipeline_mode=pl.Buffered(1)),
            pl.BlockSpec(sd.shape, const2, pipeline_mode=pl.Buffered(1)),
            pl.BlockSpec(ln_g.shape, const2),
            pl.BlockSpec(ln_b.shape, const2),
        ],
        out_specs=pl.BlockSpec((None, chunk_rows, LANES), chunk_map, pipeline_mode=pl.Buffered(1)),
        scratch_shapes=[
            pltpu.VMEM((d, 2 * ff), BF16),
            pltpu.VMEM((ff, d), BF16),
            pltpu.VMEM((8 * TILE_PITCH, LANES), F32),
            pltpu.VMEM((8 * TILE_PITCH, LANES), F32),
            pltpu.VMEM((8 * TILE_PITCH, LANES), F32),
            pltpu.VMEM((8 * TILE_PITCH, LANES), F32),
            pltpu.VMEM((d, ff), F32),
            pltpu.VMEM((d, ff), F32),
            pltpu.VMEM((ff, d), F32),
            pltpu.SemaphoreType.DMA((3,)),
        ],
    )
    return pl.pallas_call(
        kernel,
        grid_spec=grid_spec,
        out_shape=jax.ShapeDtypeStruct(x_tm.shape, F32),
        compiler_params=pltpu.CompilerParams(dimension_semantics=("arbitrary",), vmem_limit_bytes=VMEM_LIMIT),
        name="moe_routed",
    )(tile_ce, tile_flags, n_used, next_expert, row_idx, gates, x_tm, wg, wu, wd, sg, su, sd, ln_g, ln_b)


MOE_CHUNKS = 4
MOE_SUB_MAX = 512


def _moe_tiling(n_prompt, n_sample):
    assert n_prompt % (MOE_CHUNKS * 8) == 0 and n_sample % (MOE_CHUNKS * 8) == 0
    chunk_tokens = (n_prompt + n_sample) // MOE_CHUNKS
    assert (chunk_tokens + 1) * TOP_K <= 1 << PAIR_BITS
    sub_tokens = max(s for s in range(8, MOE_SUB_MAX + 1, 8) if chunk_tokens % s == 0)
    return MOE_CHUNKS, chunk_tokens, sub_tokens


def _chunked(prompt, sample, n_chunks):
    return jnp.concatenate([prompt.reshape((n_chunks, -1) + prompt.shape[1:]),
                            sample.reshape((n_chunks, -1) + sample.shape[1:])], axis=1)


def _route_plan(eidx_p, eidx_s, gate_p, gate_s, *, chunk_tokens, n_chunks):
    n_seg = n_chunks * N_EXPERTS
    eidx = _chunked(eidx_p.T, eidx_s.T, n_chunks)
    gate = _chunked(gate_p.T, gate_s.T, n_chunks)
    seg = jnp.arange(n_chunks, dtype=jnp.int32)[:, None, None] * N_EXPERTS + eidx
    pair = jnp.arange(chunk_tokens * TOP_K, dtype=jnp.int32).reshape(1, chunk_tokens, TOP_K)
    keys_real = ((seg << PAIR_BITS) | pair).reshape(-1)
    counts = jnp.sum(eidx[..., None] == jnp.arange(N_EXPERTS, dtype=jnp.int32), axis=(1, 2),
                     dtype=jnp.int32).reshape(n_seg)
    n_pad = (-counts) % ROW_TILE
    slot = jnp.arange(ROW_TILE - 1, dtype=jnp.int32)
    int_max = jnp.iinfo(jnp.int32).max
    pad_pair = (1 << PAIR_BITS) - 1
    keys_pad = jnp.where(slot[None, :] < n_pad[:, None],
                         (jnp.arange(n_seg, dtype=jnp.int32)[:, None] << PAIR_BITS) | pad_pair, int_max)
    n_fill = (-(keys_real.size + keys_pad.size)) % ROW_TILE
    keys = jnp.concatenate([keys_real, keys_pad.reshape(-1), jnp.full((n_fill,), int_max, jnp.int32)])
    vals = jnp.concatenate([gate.reshape(-1), jnp.zeros((keys_pad.size + n_fill,), F32)])
    n_tiles = keys.size // ROW_TILE
    keys, vals = lax.sort((keys, vals), num_keys=1)
    keys = keys.reshape(n_tiles, ROW_TILE)
    head = keys[:, 0]
    valid = head != int_max
    tile_ce = jnp.minimum(head >> PAIR_BITS, n_seg - 1)
    tile_c = tile_ce // N_EXPERTS
    prev_ce = jnp.concatenate([jnp.full((1,), -N_EXPERTS, jnp.int32), tile_ce[:-1]])
    next_c = jnp.concatenate([tile_c[1:], jnp.full((1,), -1, jnp.int32)])
    next_valid = jnp.concatenate([valid[1:], jnp.zeros((1,), jnp.bool_)])
    first = valid & (tile_c != prev_ce // N_EXPERTS)
    last = valid & (~next_valid | (next_c != tile_c))
    new_expert = valid & (tile_ce != prev_ce)
    tile_id = jnp.arange(n_tiles, dtype=jnp.int32)
    change_at = jnp.where(new_expert, tile_id, n_tiles)
    next_change = jnp.concatenate([lax.cummin(change_at, reverse=True)[1:], jnp.full((1,), n_tiles, jnp.int32)])
    has_next = new_expert & (next_change < n_tiles)
    next_expert = tile_ce[jnp.minimum(next_change, n_tiles - 1)] % N_EXPERTS
    flags = (first * FLAG_FIRST + last * FLAG_LAST + new_expert * FLAG_NEW_EXPERT + valid * FLAG_VALID
             + has_next * FLAG_HAS_NEXT)
    row_idx = jnp.minimum((keys & pad_pair) // TOP_K, chunk_tokens) * 8
    row_idx3 = jnp.stack([jnp.concatenate([row_idx[:1], row_idx[:-1]]), row_idx,
                          jnp.concatenate([row_idx[1:], row_idx[-1:]])], axis=1)
    n_used = jnp.sum(valid, dtype=jnp.int32).reshape(1)
    gates = jnp.pad(vals.reshape(n_tiles, 1, ROW_TILE), ((0, 0), (0, 0), (0, GATE_LANES - ROW_TILE)))
    return tile_ce, flags.astype(jnp.int32), n_used, next_expert.astype(jnp.int32), row_idx3, gates


def _t5_bucket(dist):
    n = jnp.maximum(dist, 0)
    max_exact = N_BUCKETS // 2
    large = max_exact + (jnp.log(jnp.maximum(n, 1).astype(F32) / max_exact)
                         / math.log(MAX_DISTANCE / max_exact) * (N_BUCKETS - max_exact)).astype(jnp.int32)
    large = jnp.minimum(large, N_BUCKETS - 1)
    return jnp.where(n < max_exact, n, large)


def _bias_lookup(rel_bias, bucket):
    onehot = (bucket[..., None] == jnp.arange(N_BUCKETS, dtype=bucket.dtype)).astype(F32)
    return jnp.einsum("...b,bh->h...", onehot, rel_bias.astype(F32), precision=lax.Precision.HIGHEST)


def _bias_tables(rel_bias, win):
    qi = jnp.arange(ATTN_BLOCK)[:, None]
    ki = jnp.arange(2 * ATTN_BLOCK)[None, :]
    dist = qi + ATTN_BLOCK - ki
    valid = (dist >= 0) & (dist < WINDOW)
    bias = _bias_lookup(rel_bias, _t5_bucket(dist))
    bias = jnp.where(valid[None], bias, NEG).reshape(N_KV_HEADS, GROUP, ATTN_BLOCK, 2 * ATTN_BLOCK)
    bias_ab = jnp.stack([jnp.concatenate([bias[:, t], bias[:, t + 2]], axis=1) for t in range(2)], axis=1)
    dist_s = (win - 1) - jnp.arange(win)
    bias_s = _bias_lookup(rel_bias, _t5_bucket(dist_s))
    return bias_ab, bias_s


def _sink_tables(sink):
    s = sink.astype(F32).reshape(N_KV_HEADS, GROUP)
    rows = [jnp.concatenate([jnp.broadcast_to(s[:, t, None], (N_KV_HEADS, ATTN_BLOCK)),
                             jnp.broadcast_to(s[:, t + 2, None], (N_KV_HEADS, ATTN_BLOCK))], axis=1)
            for t in range(2)]
    return jnp.stack(rows, axis=1)[..., None], sink.astype(F32)[:, None]


def kernel(x_prompt, x_sample, cache_k_win, cache_v_win, state_conv, rel_bias, w_in, attn_sink, conv_w,
           w_attn_out, w_conv_out, w_out, ln1_g, ln1_b, router_w, router_bias, exp_w_gate, exp_w_up,
           exp_w_down, shared_w_gate, shared_w_up, shared_w_down, ln2_g, ln2_b):
    depth = w_in.shape[0]
    alpha = (2 * depth) ** 0.25
    nb, seq, d = x_prompt.shape
    nd = x_sample.shape[0]
    win = cache_k_win.shape[2]
    assert x_sample.shape[1] == 1 and win == WINDOW and seq % 512 == 0

    n_prompt = nb * seq
    n_chunks, chunk_tokens, sub_tokens = _moe_tiling(n_prompt, nd)
    prompt_rows = n_prompt // n_chunks * SLAB_ROWS
    slab_shape = (n_chunks, (chunk_tokens + SPARE_TOKENS) * SLAB_ROWS, LANES)

    bias_ab, bias_s = _bias_tables(rel_bias, win)
    hmask = (jnp.arange(KV_DIM)[None, :] // HEAD_DIM == jnp.arange(N_HEADS)[:, None] // GROUP).astype(F32)

    yp = x_prompt
    ys = x_sample.reshape(nd, d)
    outs = [[] for _ in range(6)]
    for l in range(depth):
        w_in_b = w_in[l].astype(BF16)
        w_ao_b = w_attn_out[l].astype(BF16)
        w_co_b = w_conv_out[l].astype(BF16)
        w_o_b = w_out[l].astype(BF16)
        g1, b1 = ln1_g[l][None, :], ln1_b[l][None, :]
        g2, b2 = ln2_g[l][None, :], ln2_b[l][None, :]
        sink_ab, sink_col = _sink_tables(attn_sink[l])

        slab, kp, vp, cp = _mixer_prompt(yp, w_in_b, w_attn_out[l].T.astype(BF16), w_co_b, w_o_b, conv_w[l],
                                         jnp.swapaxes(bias_ab, -1, -2), jnp.swapaxes(sink_ab, -1, -2), g1, b1,
                                         alpha=alpha, tq=512, batch=nb, seq=seq, slab_shape=slab_shape,
                                         base=jnp.zeros(slab_shape, F32) if l == 0 else None)

        proj = _sample_proj(ys, w_in_b, tn=IN_DIM // 4)
        q4 = proj[:, :Q_DIM].reshape(nd, N_KV_HEADS, GROUP, HEAD_DIM).transpose(0, 2, 1, 3).reshape(nd, GROUP, KV_DIM)
        ksn, vsn, ag = _sample_attn(q4, proj[:, OFF_K:OFF_V], proj[:, OFF_V:OFF_B],
                                    cache_k_win.reshape(depth, nd, win, KV_DIM),
                                    cache_v_win.reshape(depth, nd, win, KV_DIM),
                                    bias_s, sink_col, hmask, layer=l, bt=16)
        att = ag.reshape(nd, GROUP, N_KV_HEADS, HEAD_DIM).transpose(0, 2, 1, 3).reshape(nd, Q_DIM)
        state_t = jnp.swapaxes(state_conv[l], 0, 1)
        ys, us = _sample_post(ys, att, proj, state_t, conv_w[l], w_ao_b, w_co_b, w_o_b, g1, b1, alpha=alpha)

        outs[0].append(kp.reshape(nb, WINDOW, N_KV_HEADS, HEAD_DIM))
        outs[1].append(vp.reshape(nb, WINDOW, N_KV_HEADS, HEAD_DIM))
        outs[2].append(cp)
        outs[3].append(ksn.reshape(nd, win, N_KV_HEADS, HEAD_DIM))
        outs[4].append(vsn.reshape(nd, win, N_KV_HEADS, HEAD_DIM))
        outs[5].append(jnp.concatenate([state_conv[l][:, 1:], us[:, None, :]], axis=1))

        rw_t = router_w[l].T.astype(BF16)
        rb_col = router_bias[l].astype(F32)[:, None]
        sg, su, sd = (shared_w_gate[l].astype(BF16), shared_w_up[l].astype(BF16), shared_w_down[l].astype(BF16))
        tail = jnp.concatenate([ys.reshape(n_chunks, -1, LANES),
                                jnp.zeros((n_chunks, SPARE_TOKENS * SLAB_ROWS, LANES), F32)], axis=1)
        slab = lax.dynamic_update_slice(slab, tail, (0, prompt_rows, 0))
        ep, wp = _router(slab, rw_t, rb_col, tm=512, tokens=n_prompt)
        es, ws = _router(ys, rw_t, rb_col, tm=nd)
        plan = _route_plan(ep, es, wp, ws, chunk_tokens=chunk_tokens, n_chunks=n_chunks)
        yp = _moe_routed(*plan, slab, exp_w_gate, exp_w_up, exp_w_down, sg, su, sd, g2, b2,
                         layer=l, alpha=alpha, chunk_tokens=chunk_tokens, sub_tokens=sub_tokens)
        ys = yp[:, prompt_rows:chunk_tokens * SLAB_ROWS].reshape(nd, d)

    y_prompt = yp[:, :prompt_rows].reshape(nb, seq, d)
    return (y_prompt, ys.reshape(nd, 1, d)) + tuple(jnp.stack(o) for o in outs)
```

```python
import functools
import math

import jax
import jax.numpy as jnp
from jax import lax
from jax.experimental import pallas as pl
from jax.experimental.pallas import tpu as pltpu

D_MODEL = 1024
N_HEADS = 16
N_KV_HEADS = 4
HEAD_DIM = 64
GROUP = N_HEADS // N_KV_HEADS
WINDOW = 128
ATTN_BLOCK = 128
N_BUCKETS = 32
MAX_DISTANCE = 128
CONV_DIM = 1024
CONV_K = 3
N_EXPERTS = 64
TOP_K = 8
N_GROUPS = 8
TOPK_GROUPS = 4
GROUP_SIZE = N_EXPERTS // N_GROUPS
EXPERT_FF = 256
ROUTED_SCALE = 2.5
LN_EPS = 1e-5
NEG = -1e30

Q_DIM = N_HEADS * HEAD_DIM
KV_DIM = N_KV_HEADS * HEAD_DIM
OFF_K = Q_DIM
OFF_V = OFF_K + KV_DIM
OFF_B = OFF_V + KV_DIM
OFF_C = OFF_B + CONV_DIM
OFF_H = OFF_C + CONV_DIM
OFF_GA = OFF_H + CONV_DIM
OFF_GB = OFF_GA + D_MODEL
IN_DIM = OFF_GB + D_MODEL

LANES = 128
CONV_PAD = 8
VMEM_LIMIT = 60 * 1024 * 1024

BF16 = jnp.bfloat16
F32 = jnp.float32


def _dot(a, b):
    return jnp.dot(a, b, preferred_element_type=F32)


def _dot_nt(a, b):
    return lax.dot_general(a, b, (((1,), (1,)), ((), ())), preferred_element_type=F32)


def _layer_norm(z, g, b):
    mu = jnp.mean(z, axis=-1, keepdims=True)
    d = z - mu
    var = jnp.mean(d * d, axis=-1, keepdims=True)
    return d * lax.rsqrt(var + LN_EPS) * g + b


def _sink_softmax(s, sink, axis=-1):
    m = jnp.maximum(jnp.max(s, axis=axis, keepdims=True), sink)
    e = jnp.exp(s - m)
    den = jnp.sum(e, axis=axis, keepdims=True) + jnp.exp(sink - m)
    return e * (1.0 / den)


SLAB_ROWS = D_MODEL // LANES


def _slab_load(ref, tokens, base=0):
    return jnp.concatenate([ref[pl.ds(base + j, tokens, stride=SLAB_ROWS), :] for j in range(SLAB_ROWS)], axis=1)


def _slab_store(ref, val, base=0):
    tokens = val.shape[0]
    for j in range(SLAB_ROWS):
        ref[pl.ds(base + j, tokens, stride=SLAB_ROWS), :] = val[:, j * LANES:(j + 1) * LANES]


def _merge_project(x, attn_o, y_conv, g_a, g_b, w_co_ref, w_o_ref, lng_ref, lnb_ref, alpha):
    merged = jax.nn.sigmoid(g_a) * attn_o + jax.nn.sigmoid(g_b) * _dot(y_conv.astype(BF16), w_co_ref[...])
    out = _dot(merged.astype(BF16), w_o_ref[...])
    return _layer_norm(alpha * x + out, lng_ref[...], lnb_ref[...])


def _mixer_prompt_kernel(x_ref, w_in_ref, w_ao_ref, w_co_ref, w_o_ref, convw_ref, bias_ref, sink_ref,
                         lng_ref, lnb_ref, base_ref,
                         x1_ref, kwin_ref, vwin_ref, conv_ref,
                         ka_ref, kb_ref, vat_ref, vbt_ref, s_ref, p_ref, att_ref, ubuf_ref, *, alpha, tq, slab_in):
    del base_ref
    i = pl.program_id(1)
    nblk = tq // ATTN_BLOCK
    half = LANES // 2
    scale = HEAD_DIM ** -0.5
    assert math.frexp(scale)[0] == 0.5

    @pl.when(i == 0)
    def _init():
        for ref in (ka_ref, kb_ref):
            ref[:, 0:ATTN_BLOCK, :] = jnp.zeros((N_KV_HEADS, ATTN_BLOCK, LANES), BF16)
        for ref in (vat_ref, vbt_ref):
            ref[:, :, 0:ATTN_BLOCK] = jnp.zeros((N_KV_HEADS, LANES, ATTN_BLOCK), BF16)
        ubuf_ref[0:CONV_PAD, :] = jnp.zeros((CONV_PAD, CONV_DIM), F32)

    x = _slab_load(x_ref, tq) if slab_in else x_ref[0]
    xb = x.astype(BF16)
    qkv = _dot(xb, w_in_ref[:, 0:OFF_B])

    lo = lax.broadcasted_iota(jnp.int32, (tq, LANES), 1) < half
    zeros_t = jnp.zeros((half, tq), BF16)
    for c in range(N_KV_HEADS // 2):
        chunk = qkv[:, OFF_K + c * LANES: OFF_K + (c + 1) * LANES]
        c_lo = jnp.where(lo, chunk, 0.0)
        c_hi = jnp.where(lo, 0.0, chunk)
        ka_ref[2 * c, ATTN_BLOCK:, :] = c_lo.astype(BF16)
        kb_ref[2 * c, ATTN_BLOCK:, :] = pltpu.roll(c_lo, half, 1).astype(BF16)
        kb_ref[2 * c + 1, ATTN_BLOCK:, :] = c_hi.astype(BF16)
        ka_ref[2 * c + 1, ATTN_BLOCK:, :] = pltpu.roll(c_hi, half, 1).astype(BF16)
        vt = qkv[:, OFF_V + c * LANES: OFF_V + (c + 1) * LANES].T.astype(BF16)
        vat_ref[2 * c, :, ATTN_BLOCK:] = jnp.concatenate([vt[0:half], zeros_t], axis=0)
        vbt_ref[2 * c, :, ATTN_BLOCK:] = jnp.concatenate([zeros_t, vt[0:half]], axis=0)
        vbt_ref[2 * c + 1, :, ATTN_BLOCK:] = jnp.concatenate([zeros_t, vt[half:]], axis=0)
        vat_ref[2 * c + 1, :, ATTN_BLOCK:] = jnp.concatenate([vt[half:], zeros_t], axis=0)

    key_row = lax.broadcasted_iota(jnp.int32, (2 * ATTN_BLOCK, 2 * ATTN_BLOCK), 0)
    for j in range(nblk):
        rows = slice(j * ATTN_BLOCK, (j + 1) * ATTN_BLOCK)
        keys = slice(j * ATTN_BLOCK, (j + 2) * ATTN_BLOCK)
        for h in range(N_KV_HEADS):
            q0 = h * GROUP * HEAD_DIM
            q2 = (jnp.concatenate([qkv[rows, q0:q0 + LANES], qkv[rows, q0 + LANES:q0 + 2 * LANES]], axis=0)
                  * scale).astype(BF16)
            for t, k_ref in enumerate((ka_ref, kb_ref)):
                s = _dot_nt(k_ref[h, keys, :], q2)
                if j == 0:
                    s = jnp.where(jnp.logical_and(i == 0, key_row < ATTN_BLOCK), NEG, s)
                s_ref[j, h, t] = s
    for h in range(N_KV_HEADS):
        for t in range(2):
            s = s_ref[:, h, t] + bias_ref[h, t][None]
            p_ref[:, h, t] = _sink_softmax(s, sink_ref[h, t][None], axis=1).astype(BF16)
    for j in range(nblk):
        rows = slice(j * ATTN_BLOCK, (j + 1) * ATTN_BLOCK)
        keys = slice(j * ATTN_BLOCK, (j + 2) * ATTN_BLOCK)
        for h in range(N_KV_HEADS):
            q0 = h * GROUP * HEAD_DIM
            o_t = _dot(vat_ref[h, :, keys], p_ref[j, h, 0]) + _dot(vbt_ref[h, :, keys], p_ref[j, h, 1])
            att_ref[q0:q0 + LANES, rows] = o_t[:, 0:ATTN_BLOCK].astype(BF16)
            att_ref[q0 + LANES:q0 + 2 * LANES, rows] = o_t[:, ATTN_BLOCK:].astype(BF16)
    for ref in (ka_ref, kb_ref):
        ref[:, 0:ATTN_BLOCK, :] = ref[:, tq:tq + ATTN_BLOCK, :]
    for ref in (vat_ref, vbt_ref):
        ref[:, :, 0:ATTN_BLOCK] = ref[:, :, tq:tq + ATTN_BLOCK]

    kwin_ref[0] = qkv[tq - WINDOW:tq, OFF_K:OFF_V]
    vwin_ref[0] = qkv[tq - WINDOW:tq, OFF_V:OFF_B]

    attn_o = _dot(w_ao_ref[...], att_ref[...]).T

    bch = _dot(xb, w_in_ref[:, OFF_B:OFF_GA])
    u = bch[:, CONV_DIM:2 * CONV_DIM] * bch[:, 2 * CONV_DIM:3 * CONV_DIM]
    ubuf_ref[CONV_PAD:CONV_PAD + tq, :] = u
    cw = convw_ref[...]
    y = (cw[0:1] * ubuf_ref[CONV_PAD - 2:CONV_PAD - 2 + tq, :]
         + cw[1:2] * ubuf_ref[CONV_PAD - 1:CONV_PAD - 1 + tq, :]
         + cw[2:3] * u)
    conv_ref[0] = ubuf_ref[CONV_PAD + tq - (CONV_K - 1):CONV_PAD + tq, :]
    ubuf_ref[0:CONV_PAD, :] = ubuf_ref[tq:tq + CONV_PAD, :]
    y_conv = bch[:, 0:CONV_DIM] * y

    gab = _dot(xb, w_in_ref[:, OFF_GA:IN_DIM])
    _slab_store(x1_ref, _merge_project(x, attn_o, y_conv, gab[:, 0:D_MODEL], gab[:, D_MODEL:], w_co_ref, w_o_ref,
                                       lng_ref, lnb_ref, alpha))


def _const_spec(shape):
    nd = len(shape)
    return pl.BlockSpec(shape, lambda *_: (0,) * nd, pipeline_mode=pl.Buffered(1))


def _mixer_prompt(x, w_in, w_ao, w_co, w_o, conv_w, bias_ab, sink_ab, ln_g, ln_b, *, alpha, tq, batch, seq,
                  slab_shape, base=None):
    b, s = batch, seq
    n_chunks, chunk_rows, _ = slab_shape
    slab_in = x.ndim == 3 and x.shape == slab_shape
    steps_per_chunk = (b * s // n_chunks) // tq
    assert (b * s) % (n_chunks * tq) == 0 and s % tq == 0

    def slab_map(bi, i):
        g = bi * (s // tq) + i
        return (g // steps_per_chunk, g % steps_per_chunk, 0)

    slab_spec = pl.BlockSpec((None, tq * SLAB_ROWS, LANES), slab_map)
    kernel = functools.partial(_mixer_prompt_kernel, alpha=alpha, tq=tq, slab_in=slab_in)
    return pl.pallas_call(
        kernel,
        grid=(b, s // tq),
        in_specs=[
            slab_spec if slab_in else pl.BlockSpec((1, tq, D_MODEL), lambda bi, i: (bi, i, 0)),
            _const_spec(w_in.shape), _const_spec(w_ao.shape), _const_spec(w_co.shape), _const_spec(w_o.shape),
            _const_spec(conv_w.shape), _const_spec(bias_ab.shape), _const_spec(sink_ab.shape),
            _const_spec(ln_g.shape), _const_spec(ln_b.shape),
            pl.BlockSpec(memory_space=pl.ANY),
        ],
        out_specs=[
            slab_spec,
            pl.BlockSpec((1, WINDOW, KV_DIM), lambda bi, i: (bi, 0, 0)),
            pl.BlockSpec((1, WINDOW, KV_DIM), lambda bi, i: (bi, 0, 0)),
            pl.BlockSpec((1, CONV_K - 1, CONV_DIM), lambda bi, i: (bi, 0, 0)),
        ],
        out_shape=[
            jax.ShapeDtypeStruct(slab_shape, F32),
            jax.ShapeDtypeStruct((b, WINDOW, KV_DIM), F32),
            jax.ShapeDtypeStruct((b, WINDOW, KV_DIM), F32),
            jax.ShapeDtypeStruct((b, CONV_K - 1, CONV_DIM), F32),
        ],
        scratch_shapes=[
            pltpu.VMEM((N_KV_HEADS, ATTN_BLOCK + tq, LANES), BF16),
            pltpu.VMEM((N_KV_HEADS, ATTN_BLOCK + tq, LANES), BF16),
            pltpu.VMEM((N_KV_HEADS, LANES, ATTN_BLOCK + tq), BF16),
            pltpu.VMEM((N_KV_HEADS, LANES, ATTN_BLOCK + tq), BF16),
            pltpu.VMEM((tq // ATTN_BLOCK, N_KV_HEADS, 2, 2 * ATTN_BLOCK, 2 * ATTN_BLOCK), F32),
            pltpu.VMEM((tq // ATTN_BLOCK, N_KV_HEADS, 2, 2 * ATTN_BLOCK, 2 * ATTN_BLOCK), BF16),
            pltpu.VMEM((Q_DIM, tq), BF16),
            pltpu.VMEM((tq + CONV_PAD, CONV_DIM), F32),
        ],
        compiler_params=pltpu.CompilerParams(
            dimension_semantics=("arbitrary", "arbitrary"), vmem_limit_bytes=VMEM_LIMIT),
        input_output_aliases={0 if slab_in else 10: 0},
        name="mixer_prompt",
    )(x, w_in, w_ao, w_co, w_o, conv_w, bias_ab, sink_ab, ln_g, ln_b,
      jnp.zeros((1, SLAB_ROWS, LANES), F32) if slab_in else base)


def _proj_kernel(x_ref, w_ref, o_ref):
    o_ref[...] = _dot(x_ref[...].astype(BF16), w_ref[...])


def _sample_proj(x, w_in, *, tn):
    m, d = x.shape
    n = w_in.shape[1]
    return pl.pallas_call(
        _proj_kernel,
        grid=(n // tn,),
        in_specs=[pl.BlockSpec((m, d), lambda j: (0, 0)), pl.BlockSpec((d, tn), lambda j: (0, j))],
        out_specs=pl.BlockSpec((m, tn), lambda j: (0, j)),
        out_shape=jax.ShapeDtypeStruct((m, n), F32),
        compiler_params=pltpu.CompilerParams(dimension_semantics=("arbitrary",), vmem_limit_bytes=VMEM_LIMIT),
        name="sample_proj",
    )(x, w_in)


def _sample_attn_kernel(q4_ref, knew_ref, vnew_ref, ck_ref, cv_ref, bias_ref, sink_ref, hmask_ref,
                        nk_ref, nv_ref, ag_ref):
    bt = ck_ref.shape[0]
    win = ck_ref.shape[1]
    scale = HEAD_DIM ** -0.5
    row = lax.broadcasted_iota(jnp.int32, (win, KV_DIM), 0)
    last = row == win - 1
    hmask = hmask_ref[...]
    for b in range(bt):
        kb = jnp.where(last, knew_ref[b:b + 1, :], pltpu.roll(ck_ref[b], win - 1, 0))
        vb = jnp.where(last, vnew_ref[b:b + 1, :], pltpu.roll(cv_ref[b], win - 1, 0))
        nk_ref[b] = kb
        nv_ref[b] = vb
        q4 = q4_ref[b]
        qm = (jnp.concatenate([q4] * N_KV_HEADS, axis=0) * hmask).astype(BF16)
        s = _dot_nt(qm, kb.astype(BF16)) * scale + bias_ref[...]
        p = _sink_softmax(s, sink_ref[...]).astype(BF16)
        o = _dot(p, vb.astype(BF16)) * hmask
        o4 = o[0:GROUP]
        for h in range(1, N_KV_HEADS):
            o4 = o4 + o[h * GROUP:(h + 1) * GROUP]
        ag_ref[b] = o4


def _sample_attn(q4, k_new, v_new, cache_k, cache_v, bias_s, sink_col, hmask, *, layer, bt):
    _, nb, win, kvd = cache_k.shape
    return pl.pallas_call(
        _sample_attn_kernel,
        grid=(nb // bt,),
        in_specs=[
            pl.BlockSpec((bt, GROUP, kvd), lambda i: (i, 0, 0)),
            pl.BlockSpec((bt, kvd), lambda i: (i, 0)),
            pl.BlockSpec((bt, kvd), lambda i: (i, 0)),
            pl.BlockSpec((None, bt, win, kvd), lambda i: (layer, i, 0, 0)),
            pl.BlockSpec((None, bt, win, kvd), lambda i: (layer, i, 0, 0)),
            pl.BlockSpec(bias_s.shape, lambda i: (0, 0)),
            pl.BlockSpec(sink_col.shape, lambda i: (0, 0)),
            pl.BlockSpec(hmask.shape, lambda i: (0, 0)),
        ],
        out_specs=[
            pl.BlockSpec((bt, win, kvd), lambda i: (i, 0, 0)),
            pl.BlockSpec((bt, win, kvd), lambda i: (i, 0, 0)),
            pl.BlockSpec((bt, GROUP, kvd), lambda i: (i, 0, 0)),
        ],
        out_shape=[
            jax.ShapeDtypeStruct((nb, win, kvd), F32),
            jax.ShapeDtypeStruct((nb, win, kvd), F32),
            jax.ShapeDtypeStruct((nb, GROUP, kvd), F32),
        ],
        compiler_params=pltpu.CompilerParams(dimension_semantics=("arbitrary",), vmem_limit_bytes=VMEM_LIMIT),
        name="sample_attn",
    )(q4, k_new, v_new, cache_k, cache_v, bias_s, sink_col, hmask)


def _sample_post_kernel(x_ref, att_ref, proj_ref, st_ref, convw_ref, w_ao_ref, w_co_ref, w_o_ref,
                        lng_ref, lnb_ref, x1_ref, u_ref, *, alpha):
    attn_o = _dot(att_ref[...].astype(BF16), w_ao_ref[...])
    u = proj_ref[:, OFF_C:OFF_H] * proj_ref[:, OFF_H:OFF_GA]
    cw = convw_ref[...]
    y = cw[0:1] * st_ref[0] + cw[1:2] * st_ref[1] + cw[2:3] * u
    u_ref[...] = u
    y_conv = proj_ref[:, OFF_B:OFF_C] * y
    x1_ref[...] = _merge_project(x_ref[...], attn_o, y_conv, proj_ref[:, OFF_GA:OFF_GB], proj_ref[:, OFF_GB:IN_DIM],
                                 w_co_ref, w_o_ref, lng_ref, lnb_ref, alpha)


def _sample_post(x, att, proj, state, conv_w, w_ao, w_co, w_o, ln_g, ln_b, *, alpha):
    m, d = x.shape
    kernel = functools.partial(_sample_post_kernel, alpha=alpha)
    return pl.pallas_call(
        kernel,
        out_shape=[jax.ShapeDtypeStruct((m, d), F32), jax.ShapeDtypeStruct((m, CONV_DIM), F32)],
        compiler_params=pltpu.CompilerParams(vmem_limit_bytes=VMEM_LIMIT),
        name="sample_post",
    )(x, att, proj, state, conv_w, w_ao, w_co, w_o, ln_g, ln_b)


def _first_max(cur, ids, axes, big):
    m = cur
    for ax in axes:
        m = jnp.max(m, axis=ax, keepdims=True)
    idx = jnp.where(cur == m, ids, big)
    for ax in axes:
        idx = jnp.min(idx, axis=ax, keepdims=True)
    return m, idx


def _router_kernel(x_ref, rwt_ref, rb_ref, eidx_ref, gate_ref, *, tm, slab_in):
    x = _slab_load(x_ref, tm) if slab_in else x_ref[...]
    logits_t = _dot_nt(rwt_ref[...], x.astype(BF16))
    scores = jax.nn.sigmoid(logits_t)
    sel = scores + rb_ref[...]
    shape3 = (N_GROUPS, GROUP_SIZE, tm)
    sel3 = sel.reshape(shape3)
    scores3 = scores.reshape(shape3)
    member = lax.broadcasted_iota(jnp.int32, shape3, 1)
    m1, i1 = _first_max(sel3, member, (1,), GROUP_SIZE)
    m2 = jnp.max(jnp.where(member == i1, -jnp.inf, sel3), axis=1, keepdims=True)
    gscore = m1 + m2
    gid = lax.broadcasted_iota(jnp.int32, gscore.shape, 0)
    gsel = jnp.zeros(gscore.shape, jnp.bool_)
    for _ in range(TOPK_GROUPS):
        _, gi = _first_max(gscore, gid, (0,), N_GROUPS)
        hit = gid == gi
        gsel = jnp.logical_or(gsel, hit)
        gscore = jnp.where(hit, -jnp.inf, gscore)
    eid = lax.broadcasted_iota(jnp.int32, shape3, 0) * GROUP_SIZE + member
    cur = jnp.where(gsel, sel3, -jnp.inf)
    ids, ws = [], []
    for _ in range(TOP_K):
        _, ei = _first_max(cur, eid, (1, 0), N_EXPERTS)
        hit = eid == ei
        sc = jnp.sum(jnp.sum(jnp.where(hit, scores3, 0.0), axis=1, keepdims=True), axis=0, keepdims=True)
        ids.append(ei[0])
        ws.append(sc[0])
        cur = jnp.where(hit, -jnp.inf, cur)
    w = jnp.concatenate(ws, axis=0)
    tot = jnp.sum(w, axis=0, keepdims=True)
    eidx_ref[...] = jnp.concatenate(ids, axis=0)
    gate_ref[...] = w / tot * ROUTED_SCALE


def _router(x, rw_t, rb_col, *, tm, tokens=None):
    slab_in = x.ndim == 3
    if slab_in:
        t = tokens
        steps_per_chunk = (t // x.shape[0]) // tm
        assert t % (x.shape[0] * tm) == 0
        x_spec = pl.BlockSpec((None, tm * SLAB_ROWS, LANES), lambda i: (i // steps_per_chunk, i % steps_per_chunk, 0))
    else:
        t = x.shape[0]
        x_spec = pl.BlockSpec((tm, D_MODEL), lambda i: (i, 0))
    return pl.pallas_call(
        functools.partial(_router_kernel, tm=tm, slab_in=slab_in),
        grid=(t // tm,),
        in_specs=[
            x_spec,
            pl.BlockSpec(rw_t.shape, lambda i: (0, 0)),
            pl.BlockSpec(rb_col.shape, lambda i: (0, 0)),
        ],
        out_specs=[pl.BlockSpec((TOP_K, tm), lambda i: (0, i)), pl.BlockSpec((TOP_K, tm), lambda i: (0, i))],
        out_shape=[jax.ShapeDtypeStruct((TOP_K, t), jnp.int32), jax.ShapeDtypeStruct((TOP_K, t), F32)],
        compiler_params=pltpu.CompilerParams(dimension_semantics=("arbitrary",), vmem_limit_bytes=VMEM_LIMIT),
        name="router",
    )(x, rw_t, rb_col)


ROW_TILE = 288
TILE_PITCH = ROW_TILE + 1
GATE_LANES = -(-ROW_TILE // LANES) * LANES
SPARE_TOKENS = 8
PAIR_BITS = 16
FLAG_FIRST, FLAG_LAST, FLAG_NEW_EXPERT, FLAG_VALID, FLAG_HAS_NEXT = 1, 2, 4, 8, 16
SCATTER_BATCH = 4


PREV, CUR, NEXT = 0, 1, 2


def _moe_routed_kernel(ce_ref, flags_ref, used_ref, nexte_ref, idx_ref, gate_ref, x_ref, wg_ref, wu_ref,
                       wd_ref, sg_ref, su_ref, sd_ref, lng_ref, lnb_ref, o_ref, wgu_ref, wdb_ref,
                       gat0_ref, gat1_ref, res0_ref, res1_ref, wsg_ref, wsu_ref, wsd_ref, wsem_ref, *,
                       layer, alpha, chunk_tokens, sub_tokens):
    del used_ref
    step = pl.program_id(0)
    flags = flags_ref[step]
    odd = (step & 1) == 1
    nchunk = D_MODEL // LANES
    ne, ff = wg_ref.shape[1], wg_ref.shape[3]

    def slab_row(which, r):
        return pl.multiple_of(idx_ref[0, which, r], 8)

    def gather_row(which, gat_ref, r):
        gat_ref[pl.ds(r, nchunk, stride=TILE_PITCH), :] = x_ref[pl.ds(slab_row(which, r), 8), :]

    def scatter_rows(which, res_ref, rows):
        dst = [slab_row(which, r) for r in rows]
        acc = [o_ref[pl.ds(d, 8), :] + res_ref[pl.ds(r, nchunk, stride=TILE_PITCH), :] for d, r in zip(dst, rows)]
        for d, a in zip(dst, acc):
            o_ref[pl.ds(d, 8), :] = a

    def expert_mlp(gat_ref, res_ref):
        lhs = jnp.concatenate([gat_ref[j * TILE_PITCH:j * TILE_PITCH + ROW_TILE, :] for j in range(nchunk)],
                              axis=1).astype(BF16)
        h = _dot(lhs, wgu_ref[...])
        gate_col = jnp.broadcast_to(gate_ref[0], (LANES, GATE_LANES)).T[0:ROW_TILE]
        hid = jax.nn.silu(h[:, 0:ff]) * h[:, ff:2 * ff] * jnp.concatenate([gate_col] * (ff // LANES), axis=1)
        y = _dot(hid.astype(BF16), wdb_ref[...])
        for j in range(nchunk):
            res_ref[j * TILE_PITCH:j * TILE_PITCH + ROW_TILE, :] = y[:, j * LANES:(j + 1) * LANES]

    def by_parity(fn):
        @pl.when(jnp.logical_not(odd))
        def _even():
            fn(gat0_ref, gat1_ref, res0_ref, res1_ref)

        @pl.when(odd)
        def _odd():
            fn(gat1_ref, gat0_ref, res1_ref, res0_ref)

    @pl.when((flags & FLAG_FIRST) != 0)
    def _start_chunk():
        o_ref[...] = jnp.zeros(o_ref.shape, F32)

        def start(gat_cur, gat_other, res_cur, res_other):
            res_other[...] = jnp.zeros(res_other.shape, F32)

            def body(r, carry):
                gather_row(CUR, gat_cur, r)
                return carry

            lax.fori_loop(0, ROW_TILE, body, 0)

        by_parity(start)

    def weight_copies(e):
        return [pltpu.make_async_copy(src.at[layer, e], dst, wsem_ref.at[k])
                for k, (src, dst) in enumerate(((wg_ref, wsg_ref), (wu_ref, wsu_ref), (wd_ref, wsd_ref)))]

    @pl.when(step == 0)
    def _first_fetch():
        for cp in weight_copies(ce_ref[0] % ne):
            cp.start()

    @pl.when((flags & FLAG_NEW_EXPERT) != 0)
    def _next_expert():
        for cp in weight_copies(0):
            cp.wait()
        wgu_ref[:, 0:ff] = wsg_ref[...].astype(BF16)
        wgu_ref[:, ff:2 * ff] = wsu_ref[...].astype(BF16)
        wdb_ref[...] = wsd_ref[...].astype(BF16)

        @pl.when((flags & FLAG_HAS_NEXT) != 0)
        def _prefetch():
            for cp in weight_copies(nexte_ref[step]):
                cp.start()

    @pl.when((flags & FLAG_VALID) != 0)
    def _tile():
        def main(gat_cur, gat_other, res_cur, res_other):
            for r in range(ROW_TILE):
                gather_row(NEXT, gat_other, r)
            for r0 in range(0, ROW_TILE, SCATTER_BATCH):
                scatter_rows(PREV, res_other, range(r0, r0 + SCATTER_BATCH))
            expert_mlp(gat_cur, res_cur)

        by_parity(main)

    @pl.when((flags & FLAG_LAST) != 0)
    def _finish():
        def flush(gat_cur, gat_other, res_cur, res_other):
            def body(r, carry):
                scatter_rows(CUR, res_cur, [r])
                return carry

            lax.fori_loop(0, ROW_TILE, body, 0)

        by_parity(flush)

        def body(s, carry):
            base = pl.multiple_of(s * (sub_tokens * 8), 8)

            def rows_2d(ref):
                return jnp.concatenate([ref[pl.ds(base + j, sub_tokens, stride=8), :] for j in range(nchunk)],
                                       axis=1)

            x2 = rows_2d(x_ref)
            xb = x2.astype(BF16)
            hs = jax.nn.silu(_dot(xb, sg_ref[...])) * _dot(xb, su_ref[...])
            ffn = rows_2d(o_ref) + _dot(hs.astype(BF16), sd_ref[...])
            res = _layer_norm(alpha * x2 + ffn, lng_ref[...], lnb_ref[...])
            for j in range(nchunk):
                o_ref[pl.ds(base + j, sub_tokens, stride=8), :] = res[:, j * LANES:(j + 1) * LANES]
            return carry

        lax.fori_loop(0, chunk_tokens // sub_tokens, body, 0)


def _moe_routed(tile_ce, tile_flags, n_used, next_expert, row_idx, gates, x_tm, wg, wu, wd, sg, su, sd, ln_g,
                ln_b, *, layer, alpha, chunk_tokens, sub_tokens):
    n_chunks, chunk_rows, _ = x_tm.shape
    n_tiles = row_idx.shape[0]
    _, ne, d, ff = wg.shape
    kernel = functools.partial(_moe_routed_kernel, layer=layer, alpha=alpha, chunk_tokens=chunk_tokens,
                               sub_tokens=sub_tokens)

    def tile_map(i, ce, fl, used, nxt):
        return (jnp.minimum(i, used[0] - 1), 0, 0)

    def chunk_map(i, ce, fl, used, nxt):
        return (ce[i] // ne, 0, 0)

    def const2(i, ce, fl, used, nxt):
        return (0, 0)

    grid_spec = pltpu.PrefetchScalarGridSpec(
        num_scalar_prefetch=4,
        grid=(n_tiles,),
        in_specs=[
            pl.BlockSpec((1, 3, ROW_TILE), tile_map, memory_space=pltpu.SMEM),
            pl.BlockSpec((1, 1, GATE_LANES), tile_map),
            pl.BlockSpec((None, chunk_rows, LANES), chunk_map, pipeline_mode=pl.Buffered(1)),
            pl.BlockSpec(memory_space=pl.ANY),
            pl.BlockSpec(memory_space=pl.ANY),
            pl.BlockSpec(memory_space=pl.ANY),
            pl.BlockSpec(sg.shape, const2, pipeline_mode=pl.Buffered(1)),
            pl.BlockSpec(su.shape, const2, pipeline_mode=pl.Buffered(1)),
            pl.BlockSpec(sd.shape, const2, pipeline_mode=pl.Buffered(1)),
            pl.BlockSpec(ln_g.shape, const2),
            pl.BlockSpec(ln_b.shape, const2),
        ],
        out_specs=pl.BlockSpec((None, chunk_rows, LANES), chunk_map, pipeline_mode=pl.Buffered(1)),
        scratch_shapes=[
            pltpu.VMEM((d, 2 * ff), BF16),
            pltpu.VMEM((ff, d), BF16),
            pltpu.VMEM((8 * TILE_PITCH, LANES), F32),
            pltpu.VMEM((8 * TILE_PITCH, LANES), F32),
            pltpu.VMEM((8 * TILE_PITCH, LANES), F32),
            pltpu.VMEM((8 * TILE_PITCH, LANES), F32),
            pltpu.VMEM((d, ff), F32),
            pltpu.VMEM((d, ff), F32),
            pltpu.VMEM((ff, d), F32),
            pltpu.SemaphoreType.DMA((3,)),
        ],
    )
    return pl.pallas_call(
        kernel,
        grid_spec=grid_spec,
        out_shape=jax.ShapeDtypeStruct(x_tm.shape, F32),
        compiler_params=pltpu.CompilerParams(dimension_semantics=("arbitrary",), vmem_limit_bytes=VMEM_LIMIT),
        name="moe_routed",
    )(tile_ce, tile_flags, n_used, next_expert, row_idx, gates, x_tm, wg, wu, wd, sg, su, sd, ln_g, ln_b)


MOE_CHUNKS = 4
MOE_SUB_MAX = 512


def _moe_tiling(n_prompt, n_sample):
    assert n_prompt % (MOE_CHUNKS * 8) == 0 and n_sample % (MOE_CHUNKS * 8) == 0
    chunk_tokens = (n_prompt + n_sample) // MOE_CHUNKS
    assert (chunk_tokens + 1) * TOP_K <= 1 << PAIR_BITS
    sub_tokens = max(s for s in range(8, MOE_SUB_MAX + 1, 8) if chunk_tokens % s == 0)
    return MOE_CHUNKS, chunk_tokens, sub_tokens


def _chunked(prompt, sample, n_chunks):
    return jnp.concatenate([prompt.reshape((n_chunks, -1) + prompt.shape[1:]),
                            sample.reshape((n_chunks, -1) + sample.shape[1:])], axis=1)


def _route_plan(eidx_p, eidx_s, gate_p, gate_s, *, chunk_tokens, n_chunks):
    n_seg = n_chunks * N_EXPERTS
    eidx = _chunked(eidx_p.T, eidx_s.T, n_chunks)
    gate = _chunked(gate_p.T, gate_s.T, n_chunks)
    seg = jnp.arange(n_chunks, dtype=jnp.int32)[:, None, None] * N_EXPERTS + eidx
    pair = jnp.arange(chunk_tokens * TOP_K, dtype=jnp.int32).reshape(1, chunk_tokens, TOP_K)
    keys_real = ((seg << PAIR_BITS) | pair).reshape(-1)
    counts = jnp.sum(eidx[..., None] == jnp.arange(N_EXPERTS, dtype=jnp.int32), axis=(1, 2),
                     dtype=jnp.int32).reshape(n_seg)
    n_pad = (-counts) % ROW_TILE
    slot = jnp.arange(ROW_TILE - 1, dtype=jnp.int32)
    int_max = jnp.iinfo(jnp.int32).max
    pad_pair = (1 << PAIR_BITS) - 1
    keys_pad = jnp.where(slot[None, :] < n_pad[:, None],
                         (jnp.arange(n_seg, dtype=jnp.int32)[:, None] << PAIR_BITS) | pad_pair, int_max)
    n_fill = (-(keys_real.size + keys_pad.size)) % ROW_TILE
    keys = jnp.concatenate([keys_real, keys_pad.reshape(-1), jnp.full((n_fill,), int_max, jnp.int32)])
    vals = jnp.concatenate([gate.reshape(-1), jnp.zeros((keys_pad.size + n_fill,), F32)])
    n_tiles = keys.size // ROW_TILE
    keys, vals = lax.sort((keys, vals), num_keys=1)
    keys = keys.reshape(n_tiles, ROW_TILE)
    head = keys[:, 0]
    valid = head != int_max
    tile_ce = jnp.minimum(head >> PAIR_BITS, n_seg - 1)
    tile_c = tile_ce // N_EXPERTS
    prev_ce = jnp.concatenate([jnp.full((1,), -N_EXPERTS, jnp.int32), tile_ce[:-1]])
    next_c = jnp.concatenate([tile_c[1:], jnp.full((1,), -1, jnp.int32)])
    next_valid = jnp.concatenate([valid[1:], jnp.zeros((1,), jnp.bool_)])
    first = valid & (tile_c != prev_ce // N_EXPERTS)
    last = valid & (~next_valid | (next_c != tile_c))
    new_expert = valid & (tile_ce != prev_ce)
    tile_id = jnp.arange(n_tiles, dtype=jnp.int32)
    change_at = jnp.where(new_expert, tile_id, n_tiles)
    next_change = jnp.concatenate([lax.cummin(change_at, reverse=True)[1:], jnp.full((1,), n_tiles, jnp.int32)])
    has_next = new_expert & (next_change < n_tiles)
    next_expert = tile_ce[jnp.minimum(next_change, n_tiles - 1)] % N_EXPERTS
    flags = (first * FLAG_FIRST + last * FLAG_LAST + new_expert * FLAG_NEW_EXPERT + valid * FLAG_VALID
             + has_next * FLAG_HAS_NEXT)
    row_idx = jnp.minimum((keys & pad_pair) // TOP_K, chunk_tokens) * 8
    row_idx3 = jnp.stack([jnp.concatenate([row_idx[:1], row_idx[:-1]]), row_idx,
                          jnp.concatenate([row_idx[1:], row_idx[-1:]])], axis=1)
    n_used = jnp.sum(valid, dtype=jnp.int32).reshape(1)
    gates = jnp.pad(vals.reshape(n_tiles, 1, ROW_TILE), ((0, 0), (0, 0), (0, GATE_LANES - ROW_TILE)))
    return tile_ce, flags.astype(jnp.int32), n_used, next_expert.astype(jnp.int32), row_idx3, gates


def _t5_bucket(dist):
    n = jnp.maximum(dist, 0)
    max_exact = N_BUCKETS // 2
    large = max_exact + (jnp.log(jnp.maximum(n, 1).astype(F32) / max_exact)
                         / math.log(MAX_DISTANCE / max_exact) * (N_BUCKETS - max_exact)).astype(jnp.int32)
    large = jnp.minimum(large, N_BUCKETS - 1)
    return jnp.where(n < max_exact, n, large)


def _bias_lookup(rel_bias, bucket):
    onehot = (bucket[..., None] == jnp.arange(N_BUCKETS, dtype=bucket.dtype)).astype(F32)
    return jnp.einsum("...b,bh->h...", onehot, rel_bias.astype(F32), precision=lax.Precision.HIGHEST)


def _bias_tables(rel_bias, win):
    qi = jnp.arange(ATTN_BLOCK)[:, None]
    ki = jnp.arange(2 * ATTN_BLOCK)[None, :]
    dist = qi + ATTN_BLOCK - ki
    valid = (dist >= 0) & (dist < WINDOW)
    bias = _bias_lookup(rel_bias, _t5_bucket(dist))
    bias = jnp.where(valid[None], bias, NEG).reshape(N_KV_HEADS, GROUP, ATTN_BLOCK, 2 * ATTN_BLOCK)
    bias_ab = jnp.stack([jnp.concatenate([bias[:, t], bias[:, t + 2]], axis=1) for t in range(2)], axis=1)
    dist_s = (win - 1) - jnp.arange(win)
    bias_s = _bias_lookup(rel_bias, _t5_bucket(dist_s))
    return bias_ab, bias_s


def _sink_tables(sink):
    s = sink.astype(F32).reshape(N_KV_HEADS, GROUP)
    rows = [jnp.concatenate([jnp.broadcast_to(s[:, t, None], (N_KV_HEADS, ATTN_BLOCK)),
                             jnp.broadcast_to(s[:, t + 2, None], (N_KV_HEADS, ATTN_BLOCK))], axis=1)
            for t in range(2)]
    return jnp.stack(rows, axis=1)[..., None], sink.astype(F32)[:, None]


def kernel(x_prompt, x_sample, cache_k_win, cache_v_win, state_conv, rel_bias, w_in, attn_sink, conv_w,
           w_attn_out, w_conv_out, w_out, ln1_g, ln1_b, router_w, router_bias, exp_w_gate, exp_w_up,
           exp_w_down, shared_w_gate, shared_w_up, shared_w_down, ln2_g, ln2_b):
    depth = w_in.shape[0]
    alpha = (2 * depth) ** 0.25
    nb, seq, d = x_prompt.shape
    nd = x_sample.shape[0]
    win = cache_k_win.shape[2]
    assert x_sample.shape[1] == 1 and win == WINDOW and seq % 512 == 0

    n_prompt = nb * seq
    n_chunks, chunk_tokens, sub_tokens = _moe_tiling(n_prompt, nd)
    prompt_rows = n_prompt // n_chunks * SLAB_ROWS
    slab_shape = (n_chunks, (chunk_tokens + SPARE_TOKENS) * SLAB_ROWS, LANES)

    bias_ab, bias_s = _bias_tables(rel_bias, win)
    hmask = (jnp.arange(KV_DIM)[None, :] // HEAD_DIM == jnp.arange(N_HEADS)[:, None] // GROUP).astype(F32)

    yp = x_prompt
    ys = x_sample.reshape(nd, d)
    outs = [[] for _ in range(6)]
    for l in range(depth):
        w_in_b = w_in[l].astype(BF16)
        w_ao_b = w_attn_out[l].astype(BF16)
        w_co_b = w_conv_out[l].astype(BF16)
        w_o_b = w_out[l].astype(BF16)
        g1, b1 = ln1_g[l][None, :], ln1_b[l][None, :]
        g2, b2 = ln2_g[l][None, :], ln2_b[l][None, :]
        sink_ab, sink_col = _sink_tables(attn_sink[l])

        slab, kp, vp, cp = _mixer_prompt(yp, w_in_b, w_attn_out[l].T.astype(BF16), w_co_b, w_o_b, conv_w[l],
                                         jnp.swapaxes(bias_ab, -1, -2), jnp.swapaxes(sink_ab, -1, -2), g1, b1,
                                         alpha=alpha, tq=512, batch=nb, seq=seq, slab_shape=slab_shape,
                                         base=jnp.zeros(slab_shape, F32) if l == 0 else None)

        proj = _sample_proj(ys, w_in_b, tn=IN_DIM // 4)
        q4 = proj[:, :Q_DIM].reshape(nd, N_KV_HEADS, GROUP, HEAD_DIM).transpose(0, 2, 1, 3).reshape(nd, GROUP, KV_DIM)
        ksn, vsn, ag = _sample_attn(q4, proj[:, OFF_K:OFF_V], proj[:, OFF_V:OFF_B],
                                    cache_k_win.reshape(depth, nd, win, KV_DIM),
                                    cache_v_win.reshape(depth, nd, win, KV_DIM),
                                    bias_s, sink_col, hmask, layer=l, bt=16)
        att = ag.reshape(nd, GROUP, N_KV_HEADS, HEAD_DIM).transpose(0, 2, 1, 3).reshape(nd, Q_DIM)
        state_t = jnp.swapaxes(state_conv[l], 0, 1)
        ys, us = _sample_post(ys, att, proj, state_t, conv_w[l], w_ao_b, w_co_b, w_o_b, g1, b1, alpha=alpha)

        outs[0].append(kp.reshape(nb, WINDOW, N_KV_HEADS, HEAD_DIM))
        outs[1].append(vp.reshape(nb, WINDOW, N_KV_HEADS, HEAD_DIM))
        outs[2].append(cp)
        outs[3].append(ksn.reshape(nd, win, N_KV_HEADS, HEAD_DIM))
        outs[4].append(vsn.reshape(nd, win, N_KV_HEADS, HEAD_DIM))
        outs[5].append(jnp.concatenate([state_conv[l][:, 1:], us[:, None, :]], axis=1))

        rw_t = router_w[l].T.astype(BF16)
        rb_col = router_bias[l].astype(F32)[:, None]
        sg, su, sd = (shared_w_gate[l].astype(BF16), shared_w_up[l].astype(BF16), shared_w_down[l].astype(BF16))
        tail = jnp.concatenate([ys.reshape(n_chunks, -1, LANES),
                                jnp.zeros((n_chunks, SPARE_TOKENS * SLAB_ROWS, LANES), F32)], axis=1)
        slab = lax.dynamic_update_slice(slab, tail, (0, prompt_rows, 0))
        ep, wp = _router(slab, rw_t, rb_col, tm=512, tokens=n_prompt)
        es, ws = _router(ys, rw_t, rb_col, tm=nd)
        plan = _route_plan(ep, es, wp, ws, chunk_tokens=chunk_tokens, n_chunks=n_chunks)
        yp = _moe_routed(*plan, slab, exp_w_gate, exp_w_up, exp_w_down, sg, su, sd, g2, b2,
                         layer=l, alpha=alpha, chunk_tokens=chunk_tokens, sub_tokens=sub_tokens)
        ys = yp[:, prompt_rows:chunk_tokens * SLAB_ROWS].reshape(nd, d)

    y_prompt = yp[:, :prompt_rows].reshape(nb, seq, d)
    return (y_prompt, ys.reshape(nd, 1, d)) + tuple(jnp.stack(o) for o in outs)
```
